```python
import math
import jax, jax.numpy as jnp
from jax import lax
import numpy as np

D_MODEL = 2048
BATCH = 8
SEQ = 4096
DEPTH = 4

HEAD_DIM = 128
MIX_WIDTH = D_MODEL
A_HEADS = 4
A_NOPE = 128
A_ROPE = 64
A_V = 128
Q_LORA_RANK = 448
KV_LORA_RANK = 512
B_HEADS = 6
B_KV_HEADS = 2
B_GROUP = B_HEADS // B_KV_HEADS
GRID_W = 64
C_HEADS = 6
C_BRANCHES = ((128, 1), (512, 4), (2048, 16))
D_FF = 4 * D_MODEL
ROPE_THETA = 10000.0
Q_BLOCK = 128
EPS = 1e-6
NEG_INF = -1e30
IN_SIZES = (Q_LORA_RANK, KV_LORA_RANK, A_ROPE,
            B_HEADS * HEAD_DIM, B_KV_HEADS * HEAD_DIM, B_KV_HEADS * HEAD_DIM,
            C_HEADS * HEAD_DIM, C_HEADS * HEAD_DIM, C_HEADS * HEAD_DIM)
IN_WIDTH = sum(IN_SIZES)
OUT_SIZES = (A_HEADS * A_V, B_HEADS * HEAD_DIM, C_HEADS * HEAD_DIM)

kernel_name = "hymba_style_mla_gqa2d_dilated_encoder"


def _rms(x, g=None):
    xf = x.astype(jnp.float32)
    y = xf * lax.rsqrt(jnp.mean(xf * xf, axis=-1, keepdims=True) + EPS)
    if g is not None:
        y = y * g.astype(jnp.float32)
    return y.astype(x.dtype)


def _rope_angles(pos, dim):
    inv = jnp.power(ROPE_THETA, -jnp.arange(0, dim, 2, dtype=jnp.float32) / dim)
    ang = pos.astype(jnp.float32)[:, None] * inv[None, :]
    return jnp.cos(ang), jnp.sin(ang)


def _apply_rope(x, cs):
    cos, sin = cs
    cos = cos[:, None, :]
    sin = sin[:, None, :]
    xf = x.astype(jnp.float32)
    half = x.shape[-1] // 2
    x1, x2 = xf[..., :half], xf[..., half:]
    return jnp.concatenate([x1 * cos - x2 * sin, x1 * sin + x2 * cos], axis=-1).astype(x.dtype)


def _split_cols(a, sizes):
    out = []
    start = 0
    for s in sizes:
        out.append(a[..., start:start + s])
        start += s
    return out


def _dense_attn_blocked(q, k, v, scale):
    B, S, Hkv, G, Dk = q.shape
    nb = S // Q_BLOCK
    qb = q.reshape(B, nb, Q_BLOCK, Hkv, G, Dk).transpose(1, 0, 2, 3, 4, 5)

    def one_block(qblk):
        s = jnp.einsum('bqhgd,bkhd->bhgqk', qblk, k).astype(jnp.float32) * scale
        p = jax.nn.softmax(s, axis=-1)
        return jnp.einsum('bhgqk,bkhd->bqhgd', p.astype(v.dtype), v)

    ob = lax.map(one_block, qb)
    return ob.transpose(1, 0, 2, 3, 4, 5).reshape(B, S, Hkv, G, v.shape[-1])


def _band_attn(q, k, v, half, scale):
    N, L, H, D = q.shape
    blk = half
    nb = -(-L // blk)
    pad = nb * blk - L
    qp = jnp.pad(q, ((0, 0), (0, pad), (0, 0), (0, 0)))
    kp = jnp.pad(k, ((0, 0), (blk, pad + blk), (0, 0), (0, 0))).reshape(N, nb + 2, blk, H, D)
    vp = jnp.pad(v, ((0, 0), (blk, pad + blk), (0, 0), (0, 0))).reshape(N, nb + 2, blk, H, D)
    kwin = jnp.concatenate([kp[:, :-2], kp[:, 1:-1], kp[:, 2:]], axis=2)
    vwin = jnp.concatenate([vp[:, :-2], vp[:, 1:-1], vp[:, 2:]], axis=2)
    qb = qp.reshape(N, nb, blk, H, D)
    s = jnp.einsum('nbqhd,nbkhd->nbhqk', qb, kwin).astype(jnp.float32) * scale
    qpos = jnp.arange(nb)[:, None] * blk + jnp.arange(blk)[None, :]
    kpos = (jnp.arange(nb)[:, None] - 1) * blk + jnp.arange(3 * blk)[None, :]
    rel = kpos[:, None, :] - qpos[:, :, None]
    mask = (jnp.abs(rel) <= half) & (kpos[:, None, :] >= 0) & (kpos[:, None, :] < L)
    s = jnp.where(mask[None, :, None, :, :], s, NEG_INF)
    m = jnp.max(s, axis=-1, keepdims=True)
    e = jnp.exp(s - m)
    den = jnp.sum(e, axis=-1, keepdims=True)
    o = jnp.einsum('nbhqk,nbkhd->nbqhd', (e / den).astype(v.dtype), vwin)
    lse = (m + jnp.log(den))[..., 0]
    o = o.reshape(N, nb * blk, H, D)[:, :L]
    lse = lse.transpose(0, 1, 3, 2).reshape(N, nb * blk, H)[:, :L]
    return o, lse


def _dilated_mixture(q, k, v, scale):
    B, S, H, D = q.shape
    outs = []
    lses = []
    for window, dil in C_BRANCHES:
        half = window // (2 * dil)
        L = S // dil

        def to_sub(t):
            return t.reshape(B, L, dil, H, D).transpose(0, 2, 1, 3, 4).reshape(B * dil, L, H, D)

        o, lse = _band_attn(to_sub(q), to_sub(k), to_sub(v), half, scale)
        outs.append(o.reshape(B, dil, L, H, D).transpose(0, 2, 1, 3, 4).reshape(B, S, H, D))
        lses.append(lse.reshape(B, dil, L, H).transpose(0, 2, 1, 3).reshape(B, S, H))
    w = jax.nn.softmax(jnp.stack(lses, axis=0), axis=0)
    o = jnp.stack(outs, axis=0).astype(jnp.float32)
    return jnp.sum(w[..., None] * o, axis=0).astype(q.dtype)


def _fwd_setup_inputs(seed: int = 0) -> dict:
    key = jax.random.key(seed)
    ks = jax.random.split(key, 16)
    f32 = jnp.float32

    def nrm(k, shape, scale):
        return jax.random.normal(k, shape, f32) * scale

    def gain(k, shape):
        return 1.0 + 0.02 * jax.random.normal(k, shape, f32)

    return {
        "x": jax.random.normal(ks[0], (BATCH, SEQ, D_MODEL), f32),
        "ln1_g": gain(ks[1], (DEPTH, D_MODEL)),
        "w_in": nrm(ks[2], (DEPTH, D_MODEL, IN_WIDTH), D_MODEL ** -0.5),
        "g_q_a": gain(ks[3], (DEPTH, Q_LORA_RANK)),
        "w_uq": nrm(ks[4], (DEPTH, Q_LORA_RANK, A_HEADS * (A_NOPE + A_ROPE)), Q_LORA_RANK ** -0.5),
        "g_kv_a": gain(ks[5], (DEPTH, KV_LORA_RANK)),
        "w_ukv": nrm(ks[6], (DEPTH, KV_LORA_RANK, A_HEADS * (A_NOPE + A_V)), KV_LORA_RANK ** -0.5),
        "g_qn_b": gain(ks[7], (DEPTH, HEAD_DIM)),
        "g_kn_b": gain(ks[8], (DEPTH, HEAD_DIM)),
        "g_out": gain(ks[9], (DEPTH, MIX_WIDTH)),
        "w_out": nrm(ks[10], (DEPTH, MIX_WIDTH, D_MODEL), MIX_WIDTH ** -0.5),
        "ln2_g": gain(ks[11], (DEPTH, D_MODEL)),
        "w_ff1": nrm(ks[12], (DEPTH, D_MODEL, D_FF), D_MODEL ** -0.5),
        "w_ff2": nrm(ks[13], (DEPTH, D_FF, D_MODEL), D_FF ** -0.5),
        "ln_f_g": gain(ks[14], (D_MODEL,)),
    }


def _fwd_reference(x, ln1_g, w_in, g_q_a, w_uq, g_kv_a, w_ukv, g_qn_b, g_kn_b, g_out, w_out,
              ln2_g, w_ff1, w_ff2, ln_f_g):
    B, S, _ = x.shape
    ROWS = S // GRID_W
    pos = jnp.arange(S, dtype=jnp.float32)
    row = jnp.repeat(jnp.arange(ROWS, dtype=jnp.float32), GRID_W)
    col = jnp.tile(jnp.arange(GRID_W, dtype=jnp.float32), ROWS)
    cs_a = _rope_angles(pos, A_ROPE)
    cs_c = _rope_angles(pos, HEAD_DIM)
    cs_row = _rope_angles(row, HEAD_DIM // 2)
    cs_col = _rope_angles(col, HEAD_DIM // 2)
    scale_a = 1.0 / math.sqrt(A_NOPE + A_ROPE)
    scale_h = 1.0 / math.sqrt(HEAD_DIM)

    def axial(t):
        hd = HEAD_DIM // 2
        return jnp.concatenate([_apply_rope(t[..., :hd], cs_row), _apply_rope(t[..., hd:], cs_col)], axis=-1)

    for l in range(DEPTH):
        h = _rms(x, ln1_g[l])
        proj = h @ w_in[l]
        a_cq, a_ckv, a_kr, b_q, b_k, b_v, c_q, c_k, c_v = _split_cols(proj, IN_SIZES)

        qa = (_rms(a_cq, g_q_a[l]) @ w_uq[l]).reshape(B, S, A_HEADS, A_NOPE + A_ROPE)
        qa = jnp.concatenate([qa[..., :A_NOPE], _apply_rope(qa[..., A_NOPE:], cs_a)], axis=-1)
        kva = (_rms(a_ckv, g_kv_a[l]) @ w_ukv[l]).reshape(B, S, A_HEADS, A_NOPE + A_V)
        k_pe = _apply_rope(a_kr[:, :, None, :], cs_a)
        ka = jnp.concatenate([kva[..., :A_NOPE], jnp.broadcast_to(k_pe, (B, S, A_HEADS, A_ROPE))], axis=-1)
        va = kva[..., A_NOPE:]
        o_a = _dense_attn_blocked(qa[:, :, :, None, :], ka, va, scale_a).reshape(B, S, OUT_SIZES[0])

        qb = axial(_rms(b_q.reshape(B, S, B_HEADS, HEAD_DIM), g_qn_b[l]))
        kb = axial(_rms(b_k.reshape(B, S, B_KV_HEADS, HEAD_DIM), g_kn_b[l]))
        vb = b_v.reshape(B, S, B_KV_HEADS, HEAD_DIM)
        qb = qb.reshape(B, S, B_KV_HEADS, B_GROUP, HEAD_DIM)
        o_b = _dense_attn_blocked(qb, kb, vb, scale_h).reshape(B, S, OUT_SIZES[1])

        qc = _apply_rope(c_q.reshape(B, S, C_HEADS, HEAD_DIM), cs_c)
        kc = _apply_rope(c_k.reshape(B, S, C_HEADS, HEAD_DIM), cs_c)
        vc = c_v.reshape(B, S, C_HEADS, HEAD_DIM)
        o_c = _dilated_mixture(qc, kc, vc, scale_h).reshape(B, S, OUT_SIZES[2])

        mixed = jnp.concatenate([_rms(o_a), _rms(o_b), _rms(o_c)], axis=-1) * g_out[l]
        x = x + mixed @ w_out[l]

        u = jnp.square(jax.nn.relu(_rms(x, ln2_g[l]) @ w_ff1[l]))
        x = x + u @ w_ff2[l]

    return _rms(x, ln_f_g)


import jax as _jax
import jax.numpy as _jnp

TWIN_FORMAT = 'train_step'
FWD_PARAMS = ['x', 'ln1_g', 'w_in', 'g_q_a', 'w_uq', 'g_kv_a', 'w_ukv', 'g_qn_b', 'g_kn_b', 'g_out', 'w_out', 'ln2_g', 'w_ff1', 'w_ff2', 'ln_f_g']
TWIN_WEIGHTS = ['ln1_g', 'w_in', 'g_q_a', 'w_uq', 'g_kv_a', 'w_ukv', 'g_qn_b', 'g_kn_b', 'g_out', 'w_out', 'ln2_g', 'w_ff1', 'w_ff2', 'ln_f_g']
TWIN_DIFF_INPUT = 'x'
TWIN_INPUTS = ['x', 'ln1_g', 'w_in', 'g_q_a', 'w_uq', 'g_kv_a', 'w_ukv', 'g_qn_b', 'g_kn_b', 'g_out', 'w_out', 'ln2_g', 'w_ff1', 'w_ff2', 'ln_f_g', 'loss_target', 'm_ln1_g', 'm_w_in', 'm_g_q_a', 'm_w_uq', 'm_g_kv_a', 'm_w_ukv', 'm_g_qn_b', 'm_g_kn_b', 'm_g_out', 'm_w_out', 'm_ln2_g', 'm_w_ff1', 'm_w_ff2', 'm_ln_f_g', 'v_ln1_g', 'v_w_in', 'v_g_q_a', 'v_w_uq', 'v_g_kv_a', 'v_w_ukv', 'v_g_qn_b', 'v_g_kn_b', 'v_g_out', 'v_w_out', 'v_ln2_g', 'v_w_ff1', 'v_w_ff2', 'v_ln_f_g']
TWIN_OUTPUTS = ['loss', 'grad_x', 'grad_ln1_g', 'grad_w_in', 'grad_g_q_a', 'grad_w_uq', 'grad_g_kv_a', 'grad_w_ukv', 'grad_g_qn_b', 'grad_g_kn_b', 'grad_g_out', 'grad_w_out', 'grad_ln2_g', 'grad_w_ff1', 'grad_w_ff2', 'grad_ln_f_g', 'delta_ln1_g', 'delta_w_in', 'delta_g_q_a', 'delta_w_uq', 'delta_g_kv_a', 'delta_w_ukv', 'delta_g_qn_b', 'delta_g_kn_b', 'delta_g_out', 'delta_w_out', 'delta_ln2_g', 'delta_w_ff1', 'delta_w_ff2', 'delta_ln_f_g', 'new_m_ln1_g', 'new_m_w_in', 'new_m_g_q_a', 'new_m_w_uq', 'new_m_g_kv_a', 'new_m_w_ukv', 'new_m_g_qn_b', 'new_m_g_kn_b', 'new_m_g_out', 'new_m_w_out', 'new_m_ln2_g', 'new_m_w_ff1', 'new_m_w_ff2', 'new_m_ln_f_g', 'new_v_ln1_g', 'new_v_w_in', 'new_v_g_q_a', 'new_v_w_uq', 'new_v_g_kv_a', 'new_v_w_ukv', 'new_v_g_qn_b', 'new_v_g_kn_b', 'new_v_g_out', 'new_v_w_out', 'new_v_ln2_g', 'new_v_w_ff1', 'new_v_w_ff2', 'new_v_ln_f_g']
TWIN_LEAF_KINDS = {'loss': 'loss', 'grad_x': 'grad_x', 'grad_ln1_g': 'grad_w', 'grad_w_in': 'grad_w', 'grad_g_q_a': 'grad_w', 'grad_w_uq': 'grad_w', 'grad_g_kv_a': 'grad_w', 'grad_w_ukv': 'grad_w', 'grad_g_qn_b': 'grad_w', 'grad_g_kn_b': 'grad_w', 'grad_g_out': 'grad_w', 'grad_w_out': 'grad_w', 'grad_ln2_g': 'grad_w', 'grad_w_ff1': 'grad_w', 'grad_w_ff2': 'grad_w', 'grad_ln_f_g': 'grad_w', 'delta_ln1_g': 'delta_w', 'delta_w_in': 'delta_w', 'delta_g_q_a': 'delta_w', 'delta_w_uq': 'delta_w', 'delta_g_kv_a': 'delta_w', 'delta_w_ukv': 'delta_w', 'delta_g_qn_b': 'delta_w', 'delta_g_kn_b': 'delta_w', 'delta_g_out': 'delta_w', 'delta_w_out': 'delta_w', 'delta_ln2_g': 'delta_w', 'delta_w_ff1': 'delta_w', 'delta_w_ff2': 'delta_w', 'delta_ln_f_g': 'delta_w', 'new_m_ln1_g': 'new_m', 'new_m_w_in': 'new_m', 'new_m_g_q_a': 'new_m', 'new_m_w_uq': 'new_m', 'new_m_g_kv_a': 'new_m', 'new_m_w_ukv': 'new_m', 'new_m_g_qn_b': 'new_m', 'new_m_g_kn_b': 'new_m', 'new_m_g_out': 'new_m', 'new_m_w_out': 'new_m', 'new_m_ln2_g': 'new_m', 'new_m_w_ff1': 'new_m', 'new_m_w_ff2': 'new_m', 'new_m_ln_f_g': 'new_m', 'new_v_ln1_g': 'new_v', 'new_v_w_in': 'new_v', 'new_v_g_q_a': 'new_v', 'new_v_w_uq': 'new_v', 'new_v_g_kv_a': 'new_v', 'new_v_w_ukv': 'new_v', 'new_v_g_qn_b': 'new_v', 'new_v_g_kn_b': 'new_v', 'new_v_g_out': 'new_v', 'new_v_w_out': 'new_v', 'new_v_ln2_g': 'new_v', 'new_v_w_ff1': 'new_v', 'new_v_w_ff2': 'new_v', 'new_v_ln_f_g': 'new_v'}


def _forward(args):
    return _fwd_reference(*[args[k] for k in FWD_PARAMS])


def _output_shape():
    def fwd():
        inp = _fwd_setup_inputs(0)
        return _fwd_reference(*[inp[k] for k in FWD_PARAMS])
    out = _jax.eval_shape(fwd)
    return out.shape, out.dtype

N_MICROBATCH = 1
ADAM_LR = 0.001
ADAM_B1 = 0.9
ADAM_B2 = 0.999
ADAM_EPS = 1e-08
ADAM_WD = 0.01
ADAM_STEP = 10
PER_EXAMPLE_BATCH_AXIS = {'x': 0, 'loss_target': 0}
SHARED_INPUTS = []
_WEIGHT_DTYPES = {'ln1_g': _jnp.float32, 'w_in': _jnp.float32, 'g_q_a': _jnp.float32, 'w_uq': _jnp.float32, 'g_kv_a': _jnp.float32, 'w_ukv': _jnp.float32, 'g_qn_b': _jnp.float32, 'g_kn_b': _jnp.float32, 'g_out': _jnp.float32, 'w_out': _jnp.float32, 'ln2_g': _jnp.float32, 'w_ff1': _jnp.float32, 'w_ff2': _jnp.float32, 'ln_f_g': _jnp.float32}
MOMENT_SCALE = {'ln1_g': 1.022158e-01, 'w_in': 6.611890e-02, 'g_q_a': 3.315800e-02, 'w_uq': 2.366614e-02, 'g_kv_a': 1.060648e-01, 'w_ukv': 6.866599e-02, 'g_qn_b': 8.227310e-02, 'g_kn_b': 8.004315e-02, 'g_out': 9.714859e-02, 'w_out': 9.565670e-02, 'ln2_g': 6.438069e-02, 'w_ff1': 3.229023e-02, 'w_ff2': 1.001119e-01, 'ln_f_g': 1.732708e+01}


def _to_microbatches(a, axis):
    t = _jnp.moveaxis(a, axis, 0)
    t = t.reshape((N_MICROBATCH, t.shape[0] // N_MICROBATCH) + t.shape[1:])
    return _jnp.moveaxis(t, 1, axis + 1)


def setup_inputs(seed: int = 0) -> dict:
    inp = _fwd_setup_inputs(seed)
    key = _jax.random.fold_in(_jax.random.key(seed), 7919)
    shape, _ = _output_shape()
    out = dict(inp)
    out["loss_target"] = _jax.random.normal(_jax.random.fold_in(key, 0), shape, _jnp.float32)
    for i, name in enumerate(TWIN_WEIGHTS):
        w = inp[name].astype(_jnp.float32)
        if MOMENT_SCALE is None:
            s = _jnp.sqrt(_jnp.mean(_jnp.square(w)) + 1e-30)
        else:
            s = MOMENT_SCALE[name]
        km, kv = _jax.random.split(_jax.random.fold_in(key, i + 1))
        out[name] = w
        out["m_" + name] = s * _jax.random.normal(km, w.shape, _jnp.float32)
        out["v_" + name] = (s * s) * _jax.random.uniform(kv, w.shape, _jnp.float32, 0.5, 1.5)
    if N_MICROBATCH > 1:
        for name, axis in PER_EXAMPLE_BATCH_AXIS.items():
            out[name] = _to_microbatches(out[name], axis)
    return {'x': out['x'], 'ln1_g': out['ln1_g'], 'w_in': out['w_in'], 'g_q_a': out['g_q_a'], 'w_uq': out['w_uq'], 'g_kv_a': out['g_kv_a'], 'w_ukv': out['w_ukv'], 'g_qn_b': out['g_qn_b'], 'g_kn_b': out['g_kn_b'], 'g_out': out['g_out'], 'w_out': out['w_out'], 'ln2_g': out['ln2_g'], 'w_ff1': out['w_ff1'], 'w_ff2': out['w_ff2'], 'ln_f_g': out['ln_f_g'], 'loss_target': out['loss_target'], 'm_ln1_g': out['m_ln1_g'], 'm_w_in': out['m_w_in'], 'm_g_q_a': out['m_g_q_a'], 'm_w_uq': out['m_w_uq'], 'm_g_kv_a': out['m_g_kv_a'], 'm_w_ukv': out['m_w_ukv'], 'm_g_qn_b': out['m_g_qn_b'], 'm_g_kn_b': out['m_g_kn_b'], 'm_g_out': out['m_g_out'], 'm_w_out': out['m_w_out'], 'm_ln2_g': out['m_ln2_g'], 'm_w_ff1': out['m_w_ff1'], 'm_w_ff2': out['m_w_ff2'], 'm_ln_f_g': out['m_ln_f_g'], 'v_ln1_g': out['v_ln1_g'], 'v_w_in': out['v_w_in'], 'v_g_q_a': out['v_g_q_a'], 'v_w_uq': out['v_w_uq'], 'v_g_kv_a': out['v_g_kv_a'], 'v_w_ukv': out['v_w_ukv'], 'v_g_qn_b': out['v_g_qn_b'], 'v_g_kn_b': out['v_g_kn_b'], 'v_g_out': out['v_g_out'], 'v_w_out': out['v_w_out'], 'v_ln2_g': out['v_ln2_g'], 'v_w_ff1': out['v_w_ff1'], 'v_w_ff2': out['v_w_ff2'], 'v_ln_f_g': out['v_ln_f_g']}


def _loss(weights, diff, rest, loss_target):
    with _jax.named_scope("forward"):
        args = {**rest, TWIN_DIFF_INPUT: diff, **{k: w.astype(_WEIGHT_DTYPES[k]) for k, w in weights.items()}}
        y = _forward(args)
    with _jax.named_scope("loss_head"):
        err = _jnp.square(y.astype(_jnp.float32) - loss_target)
        return 0.5 * _jnp.sum(_jnp.mean(err, axis=-1)) if err.ndim else 0.5 * err


def _adamw(w, g, m, v):
    m = ADAM_B1 * m + (1.0 - ADAM_B1) * g
    v = ADAM_B2 * v + (1.0 - ADAM_B2) * _jnp.square(g)
    m_hat = m / (1.0 - ADAM_B1 ** ADAM_STEP)
    v_hat = v / (1.0 - ADAM_B2 ** ADAM_STEP)
    delta = -ADAM_LR * (m_hat / (_jnp.sqrt(v_hat) + ADAM_EPS) + ADAM_WD * w)
    return delta, m, v


def reference(x, ln1_g, w_in, g_q_a, w_uq, g_kv_a, w_ukv, g_qn_b, g_kn_b, g_out, w_out, ln2_g, w_ff1, w_ff2, ln_f_g, loss_target, m_ln1_g, m_w_in, m_g_q_a, m_w_uq, m_g_kv_a, m_w_ukv, m_g_qn_b, m_g_kn_b, m_g_out, m_w_out, m_ln2_g, m_w_ff1, m_w_ff2, m_ln_f_g, v_ln1_g, v_w_in, v_g_q_a, v_w_uq, v_g_kv_a, v_w_ukv, v_g_qn_b, v_g_kn_b, v_g_out, v_w_out, v_ln2_g, v_w_ff1, v_w_ff2, v_ln_f_g):
    given = dict(x=x, ln1_g=ln1_g, w_in=w_in, g_q_a=g_q_a, w_uq=w_uq, g_kv_a=g_kv_a, w_ukv=w_ukv, g_qn_b=g_qn_b, g_kn_b=g_kn_b, g_out=g_out, w_out=w_out, ln2_g=ln2_g, w_ff1=w_ff1, w_ff2=w_ff2, ln_f_g=ln_f_g, loss_target=loss_target, m_ln1_g=m_ln1_g, m_w_in=m_w_in, m_g_q_a=m_g_q_a, m_w_uq=m_w_uq, m_g_kv_a=m_g_kv_a, m_w_ukv=m_w_ukv, m_g_qn_b=m_g_qn_b, m_g_kn_b=m_g_kn_b, m_g_out=m_g_out, m_w_out=m_w_out, m_ln2_g=m_ln2_g, m_w_ff1=m_w_ff1, m_w_ff2=m_w_ff2, m_ln_f_g=m_ln_f_g, v_ln1_g=v_ln1_g, v_w_in=v_w_in, v_g_q_a=v_g_q_a, v_w_uq=v_w_uq, v_g_kv_a=v_g_kv_a, v_w_ukv=v_w_ukv, v_g_qn_b=v_g_qn_b, v_g_kn_b=v_g_kn_b, v_g_out=v_g_out, v_w_out=v_w_out, v_ln2_g=v_ln2_g, v_w_ff1=v_w_ff1, v_w_ff2=v_w_ff2, v_ln_f_g=v_ln_f_g)
    weights = {n: given[n] for n in TWIN_WEIGHTS}
    shared = {n: given[n] for n in SHARED_INPUTS}
    per_example = {n: given[n] for n in ['x']}
    grad_fn = _jax.value_and_grad(_loss, argnums=(0, 1))

    def one_microbatch(ex, loss_target):
        ex = dict(ex)
        diff = ex.pop(TWIN_DIFF_INPUT)
        return grad_fn(weights, diff, {**shared, **ex}, loss_target)

    if N_MICROBATCH == 1:
        loss, (grad_w, grad_x) = one_microbatch(per_example, given["loss_target"])
    else:
        def body(carry, xs):
            loss_sum, grad_sum = carry
            l_k, (gw_k, gx_k) = one_microbatch(xs[0], xs[1])
            with _jax.named_scope("update"):
                return (loss_sum + l_k, _jax.tree.map(_jnp.add, grad_sum, gw_k)), gx_k

        init = (_jnp.zeros((), _jnp.float32), _jax.tree.map(_jnp.zeros_like, weights))
        (loss, grad_w), grad_x = _jax.lax.scan(body, init, (per_example, given["loss_target"]))
    with _jax.named_scope("update"):
        delta_w, new_m, new_v = {}, {}, {}
        for n in TWIN_WEIGHTS:
            delta_w[n], new_m[n], new_v[n] = _adamw(weights[n], grad_w[n], given["m_" + n], given["v_" + n])
    return (loss, grad_x, *[grad_w[n] for n in TWIN_WEIGHTS], *[delta_w[n] for n in TWIN_WEIGHTS],
            *[new_m[n] for n in TWIN_WEIGHTS], *[new_v[n] for n in TWIN_WEIGHTS])
```

```python
import functools
import math

import jax
import jax.numpy as jnp
from jax import lax
from jax.experimental import pallas as pl
from jax.experimental.pallas import tpu as pltpu

F32 = jnp.float32
MM_DT = jnp.bfloat16
LANES = 128
SUBLANES_F32 = 8
SUBLANES_BF16 = 16
VMEM_LIMIT = 48 * 1024 * 1024
EPS = 1e-6
NEG = -1e30
ROPE_THETA = 10000.0
ADAM_LR, ADAM_B1, ADAM_B2, ADAM_EPS, ADAM_WD, ADAM_STEP = 0.001, 0.9, 0.999, 1e-08, 0.01, 10
MESH_AXES = ("x", "y", "c")
N_SHARD = 4
MESH = pl.DeviceIdType.MESH

NN = (((1,), (0,)), ((), ()))
NT = (((1,), (1,)), ((), ()))
TN = (((0,), (0,)), ((), ()))


def _pcall(body, **kw):
    return pl.pallas_call(body, **kw)


def _rup(n, m):
    return -(-n // m) * m


def _pick(n, pref, mult):
    best = None
    for t in range(mult, min(n, pref) + 1, mult):
        if n % t == 0:
            best = t
    return best if best is not None else n


class Cfg:
    def __init__(self, S=4096, D=2048, DEPTH=4, AH=4, QL=448, KVL=512, BH=6, BKV=2, CH=6,
                 BRANCHES=((128, 1), (512, 4), (2048, 16)), DFF=8192, GRID_W=64, TB=256):
        self.S, self.D, self.DEPTH, self.AH, self.QL, self.KVL = S, D, DEPTH, AH, QL, KVL
        self.BH, self.BKV, self.CH, self.DFF, self.GRID_W, self.TB = BH, BKV, CH, DFF, GRID_W, TB
        self.G = BH // BKV
        self.AW, self.BW, self.CW = AH * 128, BH * 128, CH * 128
        self.MIX = self.AW + self.BW + self.CW
        self.QLP = _rup(QL, LANES)
        self.KV0 = (QL // LANES) * LANES
        self.PW = QL + KVL + 64
        assert self.PW % LANES == 0
        self.KVW = self.PW - self.KV0
        self.KOFF = QL - self.KV0
        self.o_bq = self.PW
        self.o_bk = self.o_bq + self.BW
        self.o_bv = self.o_bk + BKV * 128
        self.o_cq = self.o_bv + BKV * 128
        self.o_ck = self.o_cq + self.CW
        self.o_cv = self.o_ck + self.CW
        self.IN = self.o_cv + self.CW
        self.UQ, self.UKV = AH * 192, AH * 256
        self.branches = tuple(((w // (2 * d)) * d, d) for w, d in BRANCHES)
        for _, d in self.branches:
            assert d & (d - 1) == 0
        self.W = -(-max(r for r, _ in self.branches) // TB)
        assert S % TB == 0 and DEPTH % 2 == 0
        self.HD = DEPTH // 2
        self.mats = (("w_in", D, self.IN // 4, "col"), ("w_uq", QL, self.UQ // 4, "col"),
                     ("w_ukv", KVL, self.UKV // 4, "col"), ("w_out", self.MIX // 4, D, "row"),
                     ("w_ff1", D, DFF // 4, "col"), ("w_ff2", DFF // 4, D, "row"))


def _mm_call(name, mode, operands, in_specs, out_shape, out_specs, grid, acc_shape, epi, n_extra, aliases=None):
    nk = grid[2]

    def body(*refs):
        a_ref, b_ref = refs[0], refs[1]
        ex = refs[2:2 + n_extra]
        outs = refs[2 + n_extra:-1]
        acc = refs[-1]
        k = pl.program_id(2)

        @pl.when(k == 0)
        def _():
            acc[...] = jnp.zeros_like(acc)

        acc[...] += lax.dot_general(a_ref[...].astype(MM_DT), b_ref[...].astype(MM_DT), mode,
                                    preferred_element_type=F32)

        @pl.when(k == nk - 1)
        def _():
            epi(acc[...], ex, outs)

    return _pcall(body, name=name, grid=grid, in_specs=in_specs, out_specs=out_specs, out_shape=out_shape,
                  scratch_shapes=[pltpu.VMEM(acc_shape, F32)], input_output_aliases=aliases or {},
                  compiler_params=pltpu.CompilerParams(dimension_semantics=("parallel", "parallel", "arbitrary"),
                                                       vmem_limit_bytes=VMEM_LIMIT))(*operands)


def _wspec(kind, l, Rs, Cs, br, bc, rfn, cfn):
    assert Rs % br == 0 and Cs % bc == 0
    if kind == "col":
        npc = Cs // bc
        return pl.BlockSpec((None, None, br, bc), lambda i, j, k: (cfn(i, j, k) // npc, l, rfn(i, j, k), cfn(i, j, k) % npc))
    npr = Rs // br
    return pl.BlockSpec((None, None, br, bc), lambda i, j, k: (rfn(i, j, k) // npr, l, rfn(i, j, k) % npr, cfn(i, j, k)))


def _epi_plain(acc, ex, outs):
    outs[0][...] = acc.astype(outs[0].dtype)


def _epi_residual(acc, ex, outs):
    outs[0][...] = ex[0][...] + acc


def _epi_relu2(acc, ex, outs):
    outs[0][...] = acc.astype(outs[0].dtype)
    r = jnp.maximum(acc, 0.0)
    outs[1][...] = (r * r).astype(outs[1].dtype)


def _epi_drelu2(acc, ex, outs):
    a = ex[0][...].astype(F32)
    outs[0][...] = (acc * (2.0 * jnp.maximum(a, 0.0))).astype(outs[0].dtype)


def _wdims(wd):
    Wg, kind, l = wd
    ns, _, Rs, Cs = Wg.shape
    K = Rs * ns if kind == "row" else Rs
    N = Cs * ns if kind == "col" else Cs
    return Wg, kind, l, Rs, Cs, K, N


def _mm_nn(name, a, wd, epi=_epi_plain, out_dtypes=(F32,), extra=None):
    Wg, kind, l, Rs, Cs, K, N = _wdims(wd)
    M = a.shape[0]
    tm, tk, tn = _pick(M, 1024, 16), (Rs if Rs <= 1024 else _pick(Rs, 512, LANES)), _pick(Cs, 1152, LANES)
    grid = (M // tm, N // tn, K // tk)
    in_specs = [pl.BlockSpec((tm, tk), lambda i, j, k: (i, k)),
                _wspec(kind, l, Rs, Cs, tk, tn, lambda i, j, k: k, lambda i, j, k: j)]
    ops = [a, Wg]
    if extra is not None:
        in_specs.append(pl.BlockSpec((tm, tn), lambda i, j, k: (i, j)))
        ops.append(extra)
    o_spec = pl.BlockSpec((tm, tn), lambda i, j, k: (i, j))
    outs = tuple(jax.ShapeDtypeStruct((M, N), dt) for dt in out_dtypes)
    res = _mm_call(name, NN, ops, in_specs, outs, tuple(o_spec for _ in outs), grid, (tm, tn), epi,
                   0 if extra is None else 1)
    return res[0] if len(res) == 1 else res


def _mm_nt(name, g, wd, epi=_epi_plain, out_dtype=F32, extra=None):
    Wg, kind, l, Rs, Cs, K, N = _wdims(wd)
    M = g.shape[0]
    tm, tn, tk = _pick(M, 1024, 16), _pick(Rs, 1024, LANES), _pick(Cs, 512, LANES)
    grid = (M // tm, K // tn, N // tk)
    in_specs = [pl.BlockSpec((tm, tk), lambda i, j, k: (i, k)),
                _wspec(kind, l, Rs, Cs, tn, tk, lambda i, j, k: j, lambda i, j, k: k)]
    ops = [g, Wg]
    if extra is not None:
        in_specs.append(pl.BlockSpec((tm, tn), lambda i, j, k: (i, j)))
        ops.append(extra)
    res = _mm_call(name, NT, ops, in_specs, (jax.ShapeDtypeStruct((M, K), out_dtype),),
                   (pl.BlockSpec((tm, tn), lambda i, j, k: (i, j)),), grid, (tm, tn), epi, 0 if extra is None else 1)
    return res[0]


def _mm_tn(name, a, g, kind, l, Rs, Cs, buf):
    M, K = a.shape
    N = g.shape[1]
    tm, tn, tk = _pick(Rs, 1024, LANES), _pick(Cs, 1152, LANES), _pick(M, 512, LANES)
    grid = (K // tm, N // tn, M // tk)
    in_specs = [pl.BlockSpec((tk, tm), lambda i, j, k: (k, i)),
                pl.BlockSpec((tk, tn), lambda i, j, k: (k, j)),
                pl.BlockSpec(memory_space=pl.ANY)]
    o_spec = _wspec(kind, l, Rs, Cs, tm, tn, lambda i, j, k: i, lambda i, j, k: j)
    res = _mm_call(name, TN, [a, g, buf], in_specs, (jax.ShapeDtypeStruct(buf.shape, F32),), (o_spec,), grid,
                   (tm, tn), _epi_plain, 1, aliases={2: 0})
    return res[0]


def _row_params():
    return pltpu.CompilerParams(dimension_semantics=("arbitrary",), vmem_limit_bytes=VMEM_LIMIT)


def _rms_fwd(name, x, g):
    S, D = x.shape
    tr = _pick(S, 256, 16)

    def body(x_ref, g_ref, o_ref):
        xv = x_ref[...]
        r = lax.rsqrt(jnp.mean(xv * xv, axis=-1, keepdims=True) + EPS)
        o_ref[...] = (xv * r * g_ref[...]).astype(o_ref.dtype)

    return _pcall(body, name=name, grid=(S // tr,),
                  in_specs=[pl.BlockSpec((tr, D), lambda i: (i, 0)), pl.BlockSpec((1, D), lambda i: (0, 0))],
                  out_specs=pl.BlockSpec((tr, D), lambda i: (i, 0)), out_shape=jax.ShapeDtypeStruct((S, D), MM_DT),
                  compiler_params=_row_params())(x, g)


def _acc_rows(ref, part, first):
    @pl.when(first)
    def _():
        ref[...] = jnp.zeros_like(ref)

    ref[...] += jnp.broadcast_to(part, ref.shape)


def _rms_bwd(name, x, g, dy, res):
    S, D = x.shape
    tr = _pick(S, 256, 16)

    def body(x_ref, g_ref, dy_ref, res_ref, dx_ref, dg_ref):
        xv = x_ref[...]
        r = lax.rsqrt(jnp.mean(xv * xv, axis=-1, keepdims=True) + EPS)
        xh = xv * r
        dyv = dy_ref[...]
        dn = dyv * g_ref[...]
        dx_ref[...] = res_ref[...] + r * (dn - xh * jnp.mean(dn * xh, axis=-1, keepdims=True))
        _acc_rows(dg_ref, jnp.sum(dyv * xh, axis=0, keepdims=True), pl.program_id(0) == 0)

    row = pl.BlockSpec((tr, D), lambda i: (i, 0))
    return _pcall(body, name=name, grid=(S // tr,),
                  in_specs=[row, pl.BlockSpec((1, D), lambda i: (0, 0)), row, row],
                  out_specs=(row, pl.BlockSpec((8, D), lambda i: (0, 0))),
                  out_shape=(jax.ShapeDtypeStruct((S, D), F32), jax.ShapeDtypeStruct((8, D), F32)),
                  compiler_params=_row_params())(x, g, dy, res)


def _final_loss(name, x, g, tgt):
    S, D = x.shape
    tr = _pick(S, 256, 16)

    def body(x_ref, g_ref, t_ref, dx_ref, dg_ref, loss_ref):
        xv = x_ref[...]
        r = lax.rsqrt(jnp.mean(xv * xv, axis=-1, keepdims=True) + EPS)
        xh = xv * r
        gv = g_ref[...]
        e = xh * gv - t_ref[...]
        part = 0.5 * jnp.sum(jnp.mean(e * e, axis=-1, keepdims=True), axis=0, keepdims=True)
        dy = e * (1.0 / D)
        dn = dy * gv
        dx_ref[...] = r * (dn - xh * jnp.mean(dn * xh, axis=-1, keepdims=True))
        first = pl.program_id(0) == 0
        _acc_rows(dg_ref, jnp.sum(dy * xh, axis=0, keepdims=True), first)
        _acc_rows(loss_ref, part, first)

    row = pl.BlockSpec((tr, D), lambda i: (i, 0))
    return _pcall(body, name=name, grid=(S // tr,),
                  in_specs=[row, pl.BlockSpec((1, D), lambda i: (0, 0)), row],
                  out_specs=(row, pl.BlockSpec((8, D), lambda i: (0, 0)), pl.BlockSpec((8, LANES), lambda i: (0, 0))),
                  out_shape=(jax.ShapeDtypeStruct((S, D), F32), jax.ShapeDtypeStruct((8, D), F32),
                             jax.ShapeDtypeStruct((8, LANES), F32)),
                  compiler_params=_row_params())(x, g, tgt)


def _rope_tables(cos, sin, off, w):
    S = cos.shape[0]
    h = w // 2
    z = lambda n: jnp.zeros((S, n), F32)
    C = jnp.concatenate([z(off), cos, cos, z(LANES - off - w)], axis=1)
    SP = jnp.concatenate([z(off + h), sin, z(LANES - off - w)], axis=1)
    SN = jnp.concatenate([z(off), -sin, z(LANES - off - h)], axis=1)
    return C, SP, SN


def _angles(pos, dim):
    inv = jnp.power(ROPE_THETA, -jnp.arange(0, dim, 2, dtype=F32) / dim)
    ang = pos.astype(F32)[:, None] * inv[None, :]
    return jnp.cos(ang), jnp.sin(ang)


def _all_tables(cfg):
    S = cfg.S
    pos = jnp.arange(S, dtype=F32)
    rows = S // cfg.GRID_W
    row = jnp.repeat(jnp.arange(rows, dtype=F32), cfg.GRID_W)
    col = jnp.tile(jnp.arange(cfg.GRID_W, dtype=F32), rows)
    ca, sa = _angles(pos, 64)
    cc, sc = _angles(pos, 128)
    cr, sr = _angles(row, 64)
    cl, sl = _angles(col, 64)
    t_b = tuple(a + b for a, b in zip(_rope_tables(cr, sr, 0, 64), _rope_tables(cl, sl, 64, 64)))
    return {"aq": (_rope_tables(ca, sa, 0, 64), 64), "akr": (_rope_tables(ca, sa, 64, 64), 64),
            "b": (t_b, 64), "c": (_rope_tables(cc, sc, 0, 128), 128)}


def _rope(x, C, SP, SN, w):
    h = w // 2
    if 2 * h == LANES:
        return x * C + pltpu.roll(x, h, 1) * (SP + SN)
    return x * C + pltpu.roll(x, h, 1) * SP + pltpu.roll(x, LANES - h, 1) * SN


def _rope_t(dy, C, SP, SN, w):
    h = w // 2
    if 2 * h == LANES:
        return dy * C + pltpu.roll(dy * (SP + SN), h, 1)
    return dy * C + pltpu.roll(dy * SP, LANES - h, 1) + pltpu.roll(dy * SN, h, 1)


def _grid2_params():
    return pltpu.CompilerParams(dimension_semantics=("arbitrary", "arbitrary"), vmem_limit_bytes=VMEM_LIMIT)


def _headprep_fwd(name, proj, col_off, nb, gain, tabs):
    S = proj.shape[0]
    tr = _pick(S, 1024, 16)
    cb = col_off // LANES
    norm, rope = gain is not None, tabs is not None
    w = tabs[1] if rope else 0

    def body(*refs):
        x_ref = refs[0]
        pos = 1
        xv = x_ref[...]
        if norm:
            r = lax.rsqrt(jnp.mean(xv * xv, axis=-1, keepdims=True) + EPS)
            xv = xv * r * refs[pos][...]
            pos += 1
        if rope:
            xv = _rope(xv, refs[pos][...], refs[pos + 1][...], refs[pos + 2][...], w)
            pos += 3
        refs[pos][...] = xv.astype(refs[pos].dtype)

    ops, in_specs = [proj], [pl.BlockSpec((tr, LANES), lambda i, j: (i, cb + j))]
    if norm:
        ops.append(gain)
        in_specs.append(pl.BlockSpec((1, LANES), lambda i, j: (0, 0)))
    if rope:
        ops += list(tabs[0])
        in_specs += [pl.BlockSpec((tr, LANES), lambda i, j: (i, 0))] * 3
    return _pcall(body, name=name, grid=(S // tr, nb), in_specs=in_specs,
                  out_specs=pl.BlockSpec((tr, LANES), lambda i, j: (i, j)),
                  out_shape=jax.ShapeDtypeStruct((S, nb * LANES), MM_DT), compiler_params=_grid2_params())(*ops)


def _headprep_bwd(name, dy, proj, col_off, nb, gain, tabs, dproj):
    S = proj.shape[0]
    tr = _pick(S, 1024, 16)
    cb = col_off // LANES
    norm, rope = gain is not None, tabs is not None
    w = tabs[1] if rope else 0

    def body(*refs):
        dz = refs[0][...]
        pos = 1
        if norm:
            x_ref, g_ref = refs[pos], refs[pos + 1]
            pos += 2
        if rope:
            dz = _rope_t(dz, refs[pos][...], refs[pos + 1][...], refs[pos + 2][...], w)
            pos += 3
        pos += 1
        o_ref = refs[pos]
        if norm:
            dg_ref = refs[pos + 1]
            xv = x_ref[...]
            r = lax.rsqrt(jnp.mean(xv * xv, axis=-1, keepdims=True) + EPS)
            n = xv * r
            first = (pl.program_id(0) == 0) & (pl.program_id(1) == 0)
            _acc_rows(dg_ref, jnp.sum(dz * n, axis=0, keepdims=True), first)
            dn = dz * g_ref[...]
            dz = r * (dn - n * jnp.mean(dn * n, axis=-1, keepdims=True))
        o_ref[...] = dz.astype(o_ref.dtype)

    ops, in_specs = [dy], [pl.BlockSpec((tr, LANES), lambda i, j: (i, j))]
    if norm:
        ops += [proj, gain]
        in_specs += [pl.BlockSpec((tr, LANES), lambda i, j: (i, cb + j)), pl.BlockSpec((1, LANES), lambda i, j: (0, 0))]
    if rope:
        ops += list(tabs[0])
        in_specs += [pl.BlockSpec((tr, LANES), lambda i, j: (i, 0))] * 3
    alias_idx = len(ops)
    ops.append(dproj)
    in_specs.append(pl.BlockSpec(memory_space=pl.ANY))
    out_specs = [pl.BlockSpec((tr, LANES), lambda i, j: (i, cb + j))]
    out_shape = [jax.ShapeDtypeStruct(dproj.shape, dproj.dtype)]
    if norm:
        out_specs.append(pl.BlockSpec((8, LANES), lambda i, j: (0, 0)))
        out_shape.append(jax.ShapeDtypeStruct((8, LANES), F32))
    res = _pcall(body, name=name, grid=(S // tr, nb), in_specs=in_specs, out_specs=tuple(out_specs),
                 out_shape=tuple(out_shape), input_output_aliases={alias_idx: 0}, compiler_params=_grid2_params())(*ops)
    return (res[0], res[1]) if norm else (res[0], None)


def _masked_rms(xv, lo, n):
    lane = lax.broadcasted_iota(jnp.int32, xv.shape, 1)
    xm = jnp.where((lane >= lo) & (lane < lo + n), xv, 0.0)
    r = lax.rsqrt(jnp.sum(xm * xm, axis=-1, keepdims=True) * (1.0 / n) + EPS)
    return xm * r, r


def _mla_prep_fwd(name, cfg, proj, gq, gkv, tabs):
    S = cfg.S
    tr = _pick(S, 256, 16)
    (C, SP, SN), w = tabs

    def body(p_ref, gq_ref, gkv_ref, c_ref, sp_ref, sn_ref, cq_ref, ckv_ref, kpe_ref):
        nq, _ = _masked_rms(p_ref[:, 0:cfg.QLP], 0, cfg.QL)
        cq_ref[...] = (nq * gq_ref[...]).astype(cq_ref.dtype)
        nk, _ = _masked_rms(p_ref[:, cfg.KV0:cfg.PW], cfg.KOFF, cfg.KVL)
        ckv_ref[...] = (nk * gkv_ref[...]).astype(ckv_ref.dtype)
        kr = _rope(p_ref[:, cfg.PW - LANES:cfg.PW], c_ref[...], sp_ref[...], sn_ref[...], w)
        kpe_ref[...] = pltpu.roll(kr, 64, 1).astype(kpe_ref.dtype)

    tab = pl.BlockSpec((tr, LANES), lambda i: (i, 0))
    return _pcall(body, name=name, grid=(S // tr,),
                  in_specs=[pl.BlockSpec((tr, cfg.PW), lambda i: (i, 0)), pl.BlockSpec((1, cfg.QLP), lambda i: (0, 0)),
                            pl.BlockSpec((1, cfg.KVW), lambda i: (0, 0)), tab, tab, tab],
                  out_specs=(pl.BlockSpec((tr, cfg.QLP), lambda i: (i, 0)), pl.BlockSpec((tr, cfg.KVW), lambda i: (i, 0)), tab),
                  out_shape=(jax.ShapeDtypeStruct((S, cfg.QLP), MM_DT), jax.ShapeDtypeStruct((S, cfg.KVW), MM_DT),
                             jax.ShapeDtypeStruct((S, LANES), MM_DT)),
                  compiler_params=_row_params())(proj, gq, gkv, C, SP, SN)


def _mla_prep_bwd(name, cfg, dcq, dckv, dkpe, proj, gq, gkv, tabs, dproj):
    S = cfg.S
    tr = _pick(S, 256, 16)
    (C, SP, SN), w = tabs

    def body(dcq_ref, dckv_ref, dkpe_ref, p_ref, gq_ref, gkv_ref, c_ref, sp_ref, sn_ref, buf_ref, o_ref, dgq_ref, dgkv_ref):
        first = pl.program_id(0) == 0

        def norm_bwd(xv, lo, n, dz, g_ref, dg_ref):
            nrm, r = _masked_rms(xv, lo, n)
            _acc_rows(dg_ref, jnp.sum(dz * nrm, axis=0, keepdims=True), first)
            dn = dz * g_ref[...]
            return r * (dn - nrm * (jnp.sum(dn * nrm, axis=-1, keepdims=True) * (1.0 / n)))

        dxq = norm_bwd(p_ref[:, 0:cfg.QLP], 0, cfg.QL, dcq_ref[...], gq_ref, dgq_ref)
        dxk = norm_bwd(p_ref[:, cfg.KV0:cfg.PW], cfg.KOFF, cfg.KVL, dckv_ref[...], gkv_ref, dgkv_ref)
        dxr = _rope_t(pltpu.roll(dkpe_ref[...], 64, 1), c_ref[...], sp_ref[...], sn_ref[...], w)
        for cidx in range(cfg.PW // LANES):
            lo = cidx * LANES
            parts = []
            if lo < cfg.QLP:
                parts.append(dxq[:, lo:lo + LANES])
            if lo >= cfg.KV0:
                parts.append(dxk[:, lo - cfg.KV0:lo - cfg.KV0 + LANES])
            if lo == cfg.PW - LANES:
                parts.append(dxr)
            o_ref[:, lo:lo + LANES] = functools.reduce(lambda a, b: a + b, parts).astype(o_ref.dtype)

    tab = pl.BlockSpec((tr, LANES), lambda i: (i, 0))
    res = _pcall(body, name=name, grid=(S // tr,),
                 in_specs=[pl.BlockSpec((tr, cfg.QLP), lambda i: (i, 0)), pl.BlockSpec((tr, cfg.KVW), lambda i: (i, 0)), tab,
                           pl.BlockSpec((tr, cfg.PW), lambda i: (i, 0)), pl.BlockSpec((1, cfg.QLP), lambda i: (0, 0)),
                           pl.BlockSpec((1, cfg.KVW), lambda i: (0, 0)), tab, tab, tab, pl.BlockSpec(memory_space=pl.ANY)],
                 out_specs=(pl.BlockSpec((tr, cfg.PW), lambda i: (i, 0)), pl.BlockSpec((8, cfg.QLP), lambda i: (0, 0)),
                            pl.BlockSpec((8, cfg.KVW), lambda i: (0, 0))),
                 out_shape=(jax.ShapeDtypeStruct(dproj.shape, dproj.dtype), jax.ShapeDtypeStruct((8, cfg.QLP), F32),
                            jax.ShapeDtypeStruct((8, cfg.KVW), F32)),
                 input_output_aliases={9: 0}, compiler_params=_row_params())(dcq, dckv, dkpe, proj, gq, gkv, C, SP, SN, dproj)
    return res


def _mla_build_fwd(name, cfg, qa, kva, kpe, tabs):
    S, AH = cfg.S, cfg.AH
    tr = _pick(S, 256, 16)
    (C, SP, SN), w = tabs

    def body(qa_ref, kva_ref, kpe_ref, c_ref, sp_ref, sn_ref, q_ref, k_ref, v_ref):
        for h in range(AH):
            a, b = 256 * h, 256 * h + LANES
            q_ref[:, a:b] = qa_ref[:, a:b].astype(q_ref.dtype)
            q_ref[:, b:b + LANES] = _rope(qa_ref[:, b:b + LANES], c_ref[...], sp_ref[...], sn_ref[...], w).astype(q_ref.dtype)
            k_ref[:, a:b] = kva_ref[:, a:b].astype(k_ref.dtype)
            k_ref[:, b:b + LANES] = kpe_ref[...]
            v_ref[:, LANES * h:LANES * (h + 1)] = kva_ref[:, b:b + LANES].astype(v_ref.dtype)

    tab = pl.BlockSpec((tr, LANES), lambda i: (i, 0))
    wide = pl.BlockSpec((tr, AH * 256), lambda i: (i, 0))
    return _pcall(body, name=name, grid=(S // tr,), in_specs=[wide, wide, tab, tab, tab, tab],
                  out_specs=(wide, wide, pl.BlockSpec((tr, AH * LANES), lambda i: (i, 0))),
                  out_shape=(jax.ShapeDtypeStruct((S, AH * 256), MM_DT), jax.ShapeDtypeStruct((S, AH * 256), MM_DT),
                             jax.ShapeDtypeStruct((S, AH * LANES), MM_DT)),
                  compiler_params=_row_params())(qa, kva, kpe, C, SP, SN)


def _mla_build_bwd(name, cfg, dq, dk, dv, tabs):
    S, AH = cfg.S, cfg.AH
    tr = _pick(S, 256, 16)
    (C, SP, SN), w = tabs

    def body(dq_ref, dk_ref, dv_ref, c_ref, sp_ref, sn_ref, dqa_ref, dkva_ref, dkpe_ref):
        dkpe = None
        for h in range(AH):
            a, b = 256 * h, 256 * h + LANES
            dqa_ref[:, a:b] = dq_ref[:, a:b].astype(dqa_ref.dtype)
            dqa_ref[:, b:b + LANES] = _rope_t(dq_ref[:, b:b + LANES], c_ref[...], sp_ref[...], sn_ref[...], w).astype(dqa_ref.dtype)
            dkva_ref[:, a:b] = dk_ref[:, a:b].astype(dkva_ref.dtype)
            dkva_ref[:, b:b + LANES] = dv_ref[:, LANES * h:LANES * (h + 1)].astype(dkva_ref.dtype)
            part = dk_ref[:, b:b + LANES]
            dkpe = part if dkpe is None else dkpe + part
        dkpe_ref[...] = dkpe

    tab = pl.BlockSpec((tr, LANES), lambda i: (i, 0))
    wide = pl.BlockSpec((tr, AH * 256), lambda i: (i, 0))
    return _pcall(body, name=name, grid=(S // tr,),
                  in_specs=[wide, wide, pl.BlockSpec((tr, AH * LANES), lambda i: (i, 0)), tab, tab, tab],
                  out_specs=(wide, wide, tab),
                  out_shape=(jax.ShapeDtypeStruct((S, AH * 256), MM_DT), jax.ShapeDtypeStruct((S, AH * 256), MM_DT),
                             jax.ShapeDtypeStruct((S, LANES), F32)),
                  compiler_params=_row_params())(dq, dk, dv, C, SP, SN)


def _outnorm_fwd(name, cfg, oa, ob, oc, g):
    S = cfg.S
    tr = _pick(S, 256, 16)
    widths = (cfg.AW, cfg.BW, cfg.CW)

    def body(a_ref, b_ref, c_ref, g_ref, o_ref):
        off = 0
        for ref, wd in zip((a_ref, b_ref, c_ref), widths):
            v = ref[...]
            r = lax.rsqrt(jnp.mean(v * v, axis=-1, keepdims=True) + EPS)
            o_ref[:, off:off + wd] = (v * r * g_ref[:, off:off + wd]).astype(o_ref.dtype)
            off += wd

    return _pcall(body, name=name, grid=(S // tr,),
                  in_specs=[pl.BlockSpec((tr, wd), lambda i: (i, 0)) for wd in widths] + [pl.BlockSpec((1, cfg.MIX), lambda i: (0, 0))],
                  out_specs=pl.BlockSpec((tr, cfg.MIX), lambda i: (i, 0)),
                  out_shape=jax.ShapeDtypeStruct((S, cfg.MIX), MM_DT), compiler_params=_row_params())(oa, ob, oc, g)


def _outnorm_bwd(name, cfg, dmix, oa, ob, oc, g):
    S = cfg.S
    tr = _pick(S, 256, 16)
    widths = (cfg.AW, cfg.BW, cfg.CW)

    def body(dm_ref, a_ref, b_ref, c_ref, g_ref, da_ref, db_ref, dc_ref, dg_ref):
        off = 0
        parts = []
        for ref, dref, wd in zip((a_ref, b_ref, c_ref), (da_ref, db_ref, dc_ref), widths):
            v = ref[...]
            r = lax.rsqrt(jnp.mean(v * v, axis=-1, keepdims=True) + EPS)
            n = v * r
            dm = dm_ref[:, off:off + wd]
            parts.append(jnp.sum(dm * n, axis=0, keepdims=True))
            dn = dm * g_ref[:, off:off + wd]
            dref[...] = r * (dn - n * jnp.mean(dn * n, axis=-1, keepdims=True))
            off += wd
        _acc_rows(dg_ref, jnp.concatenate(parts, axis=1), pl.program_id(0) == 0)

    segs = [pl.BlockSpec((tr, wd), lambda i: (i, 0)) for wd in widths]
    return _pcall(body, name=name, grid=(S // tr,),
                  in_specs=[pl.BlockSpec((tr, cfg.MIX), lambda i: (i, 0))] + segs + [pl.BlockSpec((1, cfg.MIX), lambda i: (0, 0))],
                  out_specs=tuple(segs) + (pl.BlockSpec((8, cfg.MIX), lambda i: (0, 0)),),
                  out_shape=tuple(jax.ShapeDtypeStruct((S, wd), F32) for wd in widths) + (jax.ShapeDtypeStruct((8, cfg.MIX), F32),),
                  compiler_params=_row_params())(dmix, oa, ob, oc, g)


def _mult(qi, kj, tq, tk, branches):
    row = lax.broadcasted_iota(jnp.int32, (tq, tk), 0)
    col = lax.broadcasted_iota(jnp.int32, (tq, tk), 1)
    d = (qi * tq - kj * tk) + row - col
    ad = jnp.abs(d)
    m = jnp.zeros((tq, tk), F32)
    for reach, dil in branches:
        ok = ad <= reach
        if dil > 1:
            ok = ok & ((d & (dil - 1)) == 0)
        m = m + ok.astype(F32)
    return m


def _attn_params():
    return pltpu.CompilerParams(dimension_semantics=("parallel", "parallel", "arbitrary"), vmem_limit_bytes=VMEM_LIMIT)


def _scores(q_ref, k_ref, scale, qi, kj, t, branches):
    s = lax.dot_general(q_ref[...], k_ref[...], NT, preferred_element_type=F32) * scale
    if branches is None:
        return s, None
    mult = _mult(qi, kj, t, t, branches)
    return jnp.where(mult > 0, s, NEG), mult


def _flash_fwd(name, q, k, v, H, G, dk, dv, scale, branches=None, t_band=None, W=None):
    S = q.shape[0]
    band = branches is not None
    t = t_band if band else _pick(S, 512, LANES)
    n = S // t
    nsteps = 2 * W + 1 if band else n

    def kblock(qi, st):
        return jnp.clip(qi - W + st, 0, n - 1) if band else st

    def body(q_ref, k_ref, v_ref, o_ref, lse_ref, m_sc, l_sc, acc_sc):
        qi, st = pl.program_id(1), pl.program_id(2)

        @pl.when(st == 0)
        def _():
            m_sc[...] = jnp.full_like(m_sc, NEG)
            l_sc[...] = jnp.zeros_like(l_sc)
            acc_sc[...] = jnp.zeros_like(acc_sc)

        kj = qi - W + st if band else st

        def step():
            s, mult = _scores(q_ref, k_ref, scale, qi, kj, t, branches)
            m_prev = m_sc[...]
            m_new = jnp.maximum(m_prev, jnp.max(s, axis=-1, keepdims=True))
            alpha = jnp.exp(m_prev - m_new)
            p = jnp.exp(s - m_new)
            if band:
                p = p * mult
            l_sc[...] = alpha * l_sc[...] + jnp.sum(p, axis=-1, keepdims=True)
            acc_sc[...] = alpha * acc_sc[...] + lax.dot_general(p.astype(MM_DT), v_ref[...], NN, preferred_element_type=F32)
            m_sc[...] = m_new

        if band:
            pl.when((kj >= 0) & (kj < n))(step)
        else:
            step()

        @pl.when(st == nsteps - 1)
        def _():
            l = l_sc[...]
            o_ref[...] = acc_sc[...] / l
            lse_ref[...] = jnp.broadcast_to(m_sc[...] + jnp.log(l), lse_ref.shape)

    return _pcall(body, name=name, grid=(H, n, nsteps),
                  in_specs=[pl.BlockSpec((t, dk), lambda h, qi, st: (qi, h)),
                            pl.BlockSpec((t, dk), lambda h, qi, st: (kblock(qi, st), h // G)),
                            pl.BlockSpec((t, dv), lambda h, qi, st: (kblock(qi, st), h // G))],
                  out_specs=(pl.BlockSpec((t, dv), lambda h, qi, st: (qi, h)),
                             pl.BlockSpec((None, t, LANES), lambda h, qi, st: (h, qi, 0))),
                  out_shape=(jax.ShapeDtypeStruct((S, H * dv), F32), jax.ShapeDtypeStruct((H, S, LANES), F32)),
                  scratch_shapes=[pltpu.VMEM((t, 1), F32), pltpu.VMEM((t, 1), F32), pltpu.VMEM((t, dv), F32)],
                  compiler_params=_attn_params())(q, k, v)


def _flash_dq(name, q, k, v, do, o, lse, H, G, dk, dv, scale, branches=None, t_band=None, W=None):
    S = q.shape[0]
    band = branches is not None
    t = t_band if band else _pick(S, 512, LANES)
    n = S // t
    nsteps = 2 * W + 1 if band else n

    def kblock(qi, st):
        return jnp.clip(qi - W + st, 0, n - 1) if band else st

    def body(q_ref, k_ref, v_ref, do_ref, o_ref, lse_ref, dq_ref, delta_sc, acc_sc):
        qi, st = pl.program_id(1), pl.program_id(2)

        @pl.when(st == 0)
        def _():
            delta_sc[...] = jnp.sum(do_ref[...] * o_ref[...], axis=-1, keepdims=True)
            acc_sc[...] = jnp.zeros_like(acc_sc)

        kj = qi - W + st if band else st

        def step():
            s, mult = _scores(q_ref, k_ref, scale, qi, kj, t, branches)
            p = jnp.exp(s - lse_ref[:, 0:1])
            if band:
                p = p * mult
            dp = lax.dot_general(do_ref[...].astype(MM_DT), v_ref[...], NT, preferred_element_type=F32)
            ds = p * (dp - delta_sc[...]) * scale
            acc_sc[...] += lax.dot_general(ds.astype(MM_DT), k_ref[...], NN, preferred_element_type=F32)

        if band:
            pl.when((kj >= 0) & (kj < n))(step)
        else:
            step()

        @pl.when(st == nsteps - 1)
        def _():
            dq_ref[...] = acc_sc[...]

    qspec = lambda wd: pl.BlockSpec((t, wd), lambda h, qi, st: (qi, h))
    return _pcall(body, name=name, grid=(H, n, nsteps),
                  in_specs=[qspec(dk),
                            pl.BlockSpec((t, dk), lambda h, qi, st: (kblock(qi, st), h // G)),
                            pl.BlockSpec((t, dv), lambda h, qi, st: (kblock(qi, st), h // G)),
                            qspec(dv), qspec(dv),
                            pl.BlockSpec((None, t, LANES), lambda h, qi, st: (h, qi, 0))],
                  out_specs=qspec(dk), out_shape=jax.ShapeDtypeStruct((S, H * dk), F32),
                  scratch_shapes=[pltpu.VMEM((t, 1), F32), pltpu.VMEM((t, dk), F32)],
                  compiler_params=_attn_params())(q, k, v, do, o, lse)


def _flash_dkv(name, q, k, v, do, o, lse, H, G, dk, dv, scale, branches=None, t_band=None, W=None):
    S = q.shape[0]
    band = branches is not None
    t = t_band if band else _pick(S, 512, LANES)
    n = S // t
    nq = 2 * W + 1 if band else n
    nsteps = G * nq
    Hkv = H // G

    def qhead(hk, st):
        return hk * G + st // nq

    def qblock(kj, st):
        return jnp.clip(kj - W + st % nq, 0, n - 1) if band else st % nq

    def body(q_ref, k_ref, v_ref, do_ref, o_ref, lse_ref, dk_ref, dv_ref, dk_sc, dv_sc):
        kj, st = pl.program_id(1), pl.program_id(2)

        @pl.when(st == 0)
        def _():
            dk_sc[...] = jnp.zeros_like(dk_sc)
            dv_sc[...] = jnp.zeros_like(dv_sc)

        qi = kj - W + st % nq if band else st % nq

        def step():
            s, mult = _scores(q_ref, k_ref, scale, qi, kj, t, branches)
            p = jnp.exp(s - lse_ref[:, 0:1])
            if band:
                p = p * mult
            dof = do_ref[...]
            dob = dof.astype(MM_DT)
            dv_sc[...] += lax.dot_general(p.astype(MM_DT), dob, TN, preferred_element_type=F32)
            dp = lax.dot_general(dob, v_ref[...], NT, preferred_element_type=F32)
            delta = jnp.sum(dof * o_ref[...], axis=-1, keepdims=True)
            ds = p * (dp - delta) * scale
            dk_sc[...] += lax.dot_general(ds.astype(MM_DT), q_ref[...], TN, preferred_element_type=F32)

        if band:
            pl.when((qi >= 0) & (qi < n))(step)
        else:
            step()

        @pl.when(st == nsteps - 1)
        def _():
            dk_ref[...] = dk_sc[...]
            dv_ref[...] = dv_sc[...]

    qspec = lambda wd: pl.BlockSpec((t, wd), lambda hk, kj, st: (qblock(kj, st), qhead(hk, st)))
    kspec = lambda wd: pl.BlockSpec((t, wd), lambda hk, kj, st: (kj, hk))
    return _pcall(body, name=name, grid=(Hkv, n, nsteps),
                  in_specs=[qspec(dk), kspec(dk), kspec(dv), qspec(dv), qspec(dv),
                            pl.BlockSpec((None, t, LANES), lambda hk, kj, st: (qhead(hk, st), qblock(kj, st), 0))],
                  out_specs=(kspec(dk), kspec(dv)),
                  out_shape=(jax.ShapeDtypeStruct((S, Hkv * dk), F32), jax.ShapeDtypeStruct((S, Hkv * dv), F32)),
                  scratch_shapes=[pltpu.VMEM((t, dk), F32), pltpu.VMEM((t, dv), F32)],
                  compiler_params=_attn_params())(q, k, v, do, o, lse)


def _rowtile(rows, cols):
    return _pick(rows, max(16, (512 * 1024) // cols // 16 * 16), 16)


def _cast_rows(name, w, dtype):
    R, C = w.shape
    tr = _rowtile(R, C)

    def body(w_ref, o_ref):
        o_ref[...] = w_ref[...].astype(o_ref.dtype)

    spec = pl.BlockSpec((tr, C), lambda i: (i, 0))
    return _pcall(body, name=name, grid=(R // tr,), in_specs=[spec], out_specs=spec,
                  out_shape=jax.ShapeDtypeStruct((R, C), dtype), compiler_params=_row_params())(w)


def _add_sibling(name, gw, ra, c_arr, hd):
    ns, depth, Ks, Ns = gw.shape
    rows = hd * Ks
    tr = _rowtile(rows, Ns)
    gw_v = gw.reshape(ns, 2, rows, Ns)
    ra_v = ra.reshape(ns, rows, Ns)

    def body(c_ref, g_ref, r_ref, o_ref):
        o_ref[...] = (g_ref[...] + r_ref[...]).astype(o_ref.dtype)

    grid_spec = pltpu.PrefetchScalarGridSpec(
        num_scalar_prefetch=1, grid=(ns, rows // tr),
        in_specs=[pl.BlockSpec((None, None, tr, Ns), lambda s, r, c_ref: (s, c_ref[0], r, 0)),
                  pl.BlockSpec((None, tr, Ns), lambda s, r, c_ref: (s, r, 0))],
        out_specs=pl.BlockSpec((None, tr, Ns), lambda s, r, c_ref: (s, r, 0)))
    return _pcall(body, name=name, grid_spec=grid_spec, out_shape=jax.ShapeDtypeStruct((ns, rows, Ns), MM_DT),
                  compiler_params=_grid2_params())(c_arr, gw_v, ra_v)


def _add_chips(name, p, rb, me_arr):
    ns, rows, Ns = p.shape
    tr = _rowtile(rows, Ns)

    def body(me_ref, p_ref, b0_ref, b1_ref, b2_ref, o_ref):
        o_ref[...] = ((p_ref[...].astype(F32) + b0_ref[...].astype(F32)) + b1_ref[...].astype(F32)) + b2_ref[...].astype(F32)

    grid_spec = pltpu.PrefetchScalarGridSpec(
        num_scalar_prefetch=1, grid=(rows // tr,),
        in_specs=[pl.BlockSpec((None, tr, Ns), lambda r, me_ref: (me_ref[0], r, 0))] +
                 [pl.BlockSpec((None, tr, Ns), functools.partial(lambda r, me_ref, j: (j, r, 0), j=j)) for j in range(3)],
        out_specs=pl.BlockSpec((tr, Ns), lambda r, me_ref: (r, 0)))
    return _pcall(body, name=name, grid_spec=grid_spec, out_shape=jax.ShapeDtypeStruct((rows, Ns), F32),
                  compiler_params=_row_params())(me_arr, p, rb, rb, rb)


def _adamw(name, w, g, m, v):
    R, C = w.shape
    tr = _rowtile(R, C)
    bc1 = 1.0 - ADAM_B1 ** ADAM_STEP
    bc2 = 1.0 - ADAM_B2 ** ADAM_STEP

    def body(w_ref, g_ref, m_ref, v_ref, d_ref, nm_ref, nv_ref):
        gv = g_ref[...]
        mn = ADAM_B1 * m_ref[...] + (1.0 - ADAM_B1) * gv
        vn = ADAM_B2 * v_ref[...] + (1.0 - ADAM_B2) * jnp.square(gv)
        m_hat = mn / bc1
        v_hat = vn / bc2
        d_ref[...] = -ADAM_LR * (m_hat / (jnp.sqrt(v_hat) + ADAM_EPS) + ADAM_WD * w_ref[...])
        nm_ref[...] = mn
        nv_ref[...] = vn

    spec = pl.BlockSpec((tr, C), lambda i: (i, 0))
    sds = jax.ShapeDtypeStruct((R, C), F32)
    return _pcall(body, name=name, grid=(R // tr,), in_specs=[spec] * 4, out_specs=(spec,) * 3,
                  out_shape=(sds, sds, sds), compiler_params=_row_params())(w, g, m, v)


HBM_SPEC = pl.BlockSpec(memory_space=pltpu.HBM)


def _place():
    x, y, c = lax.axis_index("x"), lax.axis_index("y"), lax.axis_index("c")
    chips = [(1 - x, y), (x, 1 - y), (1 - x, 1 - y)]
    return x, y, c, chips


def _allgather_weights(shards, hd):
    n = len(shards)

    def body(*refs):
        ins, outs = refs[:n], refs[n:2 * n]
        send, recv, loc = refs[2 * n:]
        x, y, c, chips = _place()
        me = 2 * x + y
        sib = (x, y, 1 - c)

        def rcopy(src, dst, k, to):
            return pltpu.make_async_remote_copy(src_ref=src, dst_ref=dst, send_sem=send.at[k], recv_sem=recv.at[k],
                                                device_id=to, device_id_type=MESH)

        started = []
        for t in range(n):
            lc = pltpu.make_async_copy(ins[t], outs[t].at[me], loc.at[t])
            lc.start()
            started.append(lc)
        sends = []
        for t in range(n):
            for j, chip in enumerate(chips):
                cp = rcopy(ins[t].at[pl.ds(c * hd, hd)], outs[t].at[me, pl.ds(c * hd, hd)], 6 * t + j, (chip[0], chip[1], c))
                cp.start()
                sends.append(cp)
        for t in range(n):
            for j, chip in enumerate(chips):
                slab = outs[t].at[2 * chip[0] + chip[1], pl.ds(c * hd, hd)]
                rcopy(slab, slab, 6 * t + j, (chip[0], chip[1], c)).wait_recv()
                fw = rcopy(slab, slab, 6 * t + 3 + j, sib)
                fw.start()
                sends.append(fw)
        for t in range(n):
            for j, chip in enumerate(chips):
                slab = outs[t].at[2 * chip[0] + chip[1], pl.ds((1 - c) * hd, hd)]
                rcopy(slab, slab, 6 * t + 3 + j, sib).wait_recv()
        for cp in sends:
            cp.wait_send()
        for lc in started:
            lc.wait()

    return _pcall(body, name="allgather_weights", in_specs=[HBM_SPEC] * n, out_specs=tuple([HBM_SPEC] * n),
                  out_shape=tuple(jax.ShapeDtypeStruct((N_SHARD,) + s.shape, s.dtype) for s in shards),
                  scratch_shapes=[pltpu.SemaphoreType.DMA((6 * n,)), pltpu.SemaphoreType.DMA((6 * n,)),
                                  pltpu.SemaphoreType.DMA((n,))])(*shards)


def _exchange_sibling_halves(gws, hd):
    n = len(gws)

    def body(*refs):
        ins, outs = refs[:n], refs[n:2 * n]
        send, recv = refs[2 * n:]
        x, y, c, _ = _place()
        cps = []
        for t in range(n):
            cp = pltpu.make_async_remote_copy(src_ref=ins[t].at[:, pl.ds((1 - c) * hd, hd)], dst_ref=outs[t],
                                              send_sem=send.at[t], recv_sem=recv.at[t],
                                              device_id=(x, y, 1 - c), device_id_type=MESH)
            cp.start()
            cps.append(cp)
        for cp in cps:
            cp.wait()

    return _pcall(body, name="rs_sibling_halves", in_specs=[HBM_SPEC] * n, out_specs=tuple([HBM_SPEC] * n),
                  out_shape=tuple(jax.ShapeDtypeStruct((g.shape[0], hd) + g.shape[2:], g.dtype) for g in gws),
                  scratch_shapes=[pltpu.SemaphoreType.DMA((n,)), pltpu.SemaphoreType.DMA((n,))])(*gws)


def _exchange_chips(ps):
    n = len(ps)

    def body(*refs):
        ins, outs = refs[:n], refs[n:2 * n]
        send, recv = refs[2 * n:]
        x, y, c, chips = _place()
        cps = []
        for t in range(n):
            for j, chip in enumerate(chips):
                cp = pltpu.make_async_remote_copy(src_ref=ins[t].at[2 * chip[0] + chip[1]], dst_ref=outs[t].at[j],
                                                  send_sem=send.at[3 * t + j], recv_sem=recv.at[3 * t + j],
                                                  device_id=(chip[0], chip[1], c), device_id_type=MESH)
                cp.start()
                cps.append(cp)
        for cp in cps:
            cp.wait()

    return _pcall(body, name="rs_chip_exchange", in_specs=[HBM_SPEC] * n, out_specs=tuple([HBM_SPEC] * n),
                  out_shape=tuple(jax.ShapeDtypeStruct((3,) + p.shape[1:], p.dtype) for p in ps),
                  scratch_shapes=[pltpu.SemaphoreType.DMA((3 * n,)), pltpu.SemaphoreType.DMA((3 * n,))])(*ps)


def _share_reduced(rs):
    n = len(rs)

    def body(*refs):
        ins, outs = refs[:n], refs[n:2 * n]
        send, recv, loc = refs[2 * n:]
        x, y, c, _ = _place()
        cps, lcs = [], []
        for t in range(n):
            lc = pltpu.make_async_copy(ins[t], outs[t].at[c], loc.at[t])
            lc.start()
            lcs.append(lc)
            cp = pltpu.make_async_remote_copy(src_ref=ins[t], dst_ref=outs[t].at[c], send_sem=send.at[t], recv_sem=recv.at[t],
                                              device_id=(x, y, 1 - c), device_id_type=MESH)
            cp.start()
            cps.append(cp)
        for t in range(n):
            pltpu.make_async_remote_copy(src_ref=ins[t], dst_ref=outs[t].at[1 - c], send_sem=send.at[t], recv_sem=recv.at[t],
                                         device_id=(x, y, 1 - c), device_id_type=MESH).wait_recv()
        for cp in cps:
            cp.wait_send()
        for lc in lcs:
            lc.wait()

    return _pcall(body, name="rs_share_reduced", in_specs=[HBM_SPEC] * n, out_specs=tuple([HBM_SPEC] * n),
                  out_shape=tuple(jax.ShapeDtypeStruct((2,) + r.shape, r.dtype) for r in rs),
                  scratch_shapes=[pltpu.SemaphoreType.DMA((n,)), pltpu.SemaphoreType.DMA((n,)),
                                  pltpu.SemaphoreType.DMA((n,))])(*rs)


def _allreduce_small(vec):
    R = vec.shape[0]

    def body(v_ref, o_ref, buf, send, recv):
        x, y, c, _ = _place()
        me = 4 * x + 2 * y + c
        buf[me] = v_ref[...]
        cps = []
        for r in range(1, 8):
            fx, fy, fc = (r >> 2) & 1, (r >> 1) & 1, r & 1
            to = (1 - x if fx else x, 1 - y if fy else y, 1 - c if fc else c)
            cp = pltpu.make_async_remote_copy(src_ref=v_ref, dst_ref=buf.at[me], send_sem=send.at[r - 1], recv_sem=recv.at[r - 1],
                                              device_id=to, device_id_type=MESH)
            cp.start()
            cps.append(cp)
        for r in range(1, 8):
            fx, fy, fc = (r >> 2) & 1, (r >> 1) & 1, r & 1
            frm = (1 - x if fx else x, 1 - y if fy else y, 1 - c if fc else c)
            src = 4 * frm[0] + 2 * frm[1] + frm[2]
            pltpu.make_async_remote_copy(src_ref=v_ref, dst_ref=buf.at[src], send_sem=send.at[r - 1], recv_sem=recv.at[r - 1],
                                         device_id=frm, device_id_type=MESH).wait_recv()
        for cp in cps:
            cp.wait_send()
        acc = buf[0]
        for i in range(1, 8):
            acc = acc + buf[i]
        o_ref[...] = acc

    vm = pl.BlockSpec(memory_space=pltpu.VMEM)
    return _pcall(body, name="allreduce_small", in_specs=[vm], out_specs=vm, out_shape=jax.ShapeDtypeStruct((R, LANES), F32),
                  scratch_shapes=[pltpu.VMEM((8, R, LANES), F32), pltpu.SemaphoreType.DMA((7,)), pltpu.SemaphoreType.DMA((7,))])(vec)


def _unshard_cols(wg):
    ns, depth, K, Ns = wg.shape
    return jnp.moveaxis(wg, 0, 2).reshape(depth, K, ns * Ns)


def _shard_cols(w):
    K, N = w.shape
    return jnp.moveaxis(w.reshape(K, N_SHARD, N // N_SHARD), 1, 0)


def _uq_padded(cfg, wuq_g):
    w = _unshard_cols(wuq_g).reshape(cfg.DEPTH, cfg.QL, cfg.AH, 192)
    w = jnp.pad(w, ((0, 0), (0, cfg.QLP - cfg.QL), (0, 0), (0, 64)))
    return w.reshape(1, cfg.DEPTH, cfg.QLP, cfg.AH * 256)


def _uq_grad_unpadded(cfg, dw):
    w = dw[:cfg.QL].reshape(cfg.QL, cfg.AH, 256)[:, :, :192].reshape(cfg.QL, cfg.UQ)
    return _shard_cols(w)


def _ukv_padded(cfg, wukv_g):
    w = _unshard_cols(wukv_g)
    w = jnp.pad(w, ((0, 0), (cfg.KOFF, cfg.KVW - cfg.KVL - cfg.KOFF), (0, 0)))
    return w[None]


def _ukv_grad_unpadded(cfg, dw):
    return _shard_cols(dw[cfg.KOFF:cfg.KOFF + cfg.KVL])


def _pad_lanes(v, lo, total):
    return jnp.pad(v, (lo, total - lo - v.shape[0]))[None]


def _layer_fwd(cfg, l, x, W, small, tabs):
    ln1, gq, gkv, gqn, gkn, gout, ln2 = small
    sc_a, sc_h = 1.0 / math.sqrt(192), 1.0 / math.sqrt(128)
    n = f"l{l}_"
    h = _rms_fwd(n + "ln1", x, ln1)
    proj = _mm_nn(n + "proj", h, (W["w_in"], "col", l))
    cqn, ckvn, kpe = _mla_prep_fwd(n + "mla_prep", cfg, proj, gq, gkv, tabs["akr"])
    qa = _mm_nn(n + "uq", cqn, (W["uq_p"], "col", l))
    kva = _mm_nn(n + "ukv", ckvn, (W["ukv_p"], "col", l))
    q_a, k_a, v_a = _mla_build_fwd(n + "mla_build", cfg, qa, kva, kpe, tabs["aq"])
    o_a, lse_a = _flash_fwd(n + "attn_a", q_a, k_a, v_a, cfg.AH, 1, 256, 128, sc_a)
    q_b = _headprep_fwd(n + "bq", proj, cfg.o_bq, cfg.BH, gqn, tabs["b"])
    k_b = _headprep_fwd(n + "bk", proj, cfg.o_bk, cfg.BKV, gkn, tabs["b"])
    v_b = _headprep_fwd(n + "bv", proj, cfg.o_bv, cfg.BKV, None, None)
    o_b, lse_b = _flash_fwd(n + "attn_b", q_b, k_b, v_b, cfg.BH, cfg.G, 128, 128, sc_h)
    q_c = _headprep_fwd(n + "cq", proj, cfg.o_cq, cfg.CH, None, tabs["c"])
    k_c = _headprep_fwd(n + "ck", proj, cfg.o_ck, cfg.CH, None, tabs["c"])
    v_c = _headprep_fwd(n + "cv", proj, cfg.o_cv, cfg.CH, None, None)
    o_c, lse_c = _flash_fwd(n + "attn_c", q_c, k_c, v_c, cfg.CH, 1, 128, 128, sc_h, cfg.branches, cfg.TB, cfg.W)
    mixed = _outnorm_fwd(n + "outnorm", cfg, o_a, o_b, o_c, gout)
    x1 = _mm_nn(n + "out", mixed, (W["w_out"], "row", l), epi=_epi_residual, extra=x)
    h2 = _rms_fwd(n + "ln2", x1, ln2)
    a, u = _mm_nn(n + "ff1", h2, (W["w_ff1"], "col", l), epi=_epi_relu2, out_dtypes=(MM_DT, MM_DT))
    x2 = _mm_nn(n + "ff2", u, (W["w_ff2"], "row", l), epi=_epi_residual, extra=x1)
    saved = dict(x=x, h=h, proj=proj, cqn=cqn, ckvn=ckvn, q_a=q_a, k_a=k_a, v_a=v_a, o_a=o_a, lse_a=lse_a,
                 q_b=q_b, k_b=k_b, v_b=v_b, o_b=o_b, lse_b=lse_b, q_c=q_c, k_c=k_c, v_c=v_c, o_c=o_c, lse_c=lse_c,
                 mixed=mixed, x1=x1, h2=h2, a=a, u=u)
    return x2, saved


def _layer_bwd(cfg, l, dx2, sv, W, small, tabs, GW):
    ln1, gq, gkv, gqn, gkn, gout, ln2 = small
    sc_a, sc_h = 1.0 / math.sqrt(192), 1.0 / math.sqrt(128)
    n = f"l{l}_b_"
    S = cfg.S
    mats = {m[0]: m for m in cfg.mats}

    def dw(name, a, g, key):
        _, Rs, Cs, kind = mats[key]
        GW[key] = _mm_tn(n + name, a, g, kind, l, Rs, Cs, GW[key])

    da = _mm_nt(n + "ff2_dx", dx2, (W["w_ff2"], "row", l), epi=_epi_drelu2, out_dtype=MM_DT, extra=sv["a"])
    dw("ff2_dw", sv["u"], dx2, "w_ff2")
    dh2 = _mm_nt(n + "ff1_dx", da, (W["w_ff1"], "col", l))
    dw("ff1_dw", sv["h2"], da, "w_ff1")
    dx1, dln2 = _rms_bwd(n + "ln2", sv["x1"], ln2, dh2, dx2)
    dmix = _mm_nt(n + "out_dx", dx1, (W["w_out"], "row", l))
    dw("out_dw", sv["mixed"], dx1, "w_out")
    do_a, do_b, do_c, dgout = _outnorm_bwd(n + "outnorm", cfg, dmix, sv["o_a"], sv["o_b"], sv["o_c"], gout)
    dproj = jnp.zeros((S, cfg.IN), MM_DT)
    args_c = (sv["q_c"], sv["k_c"], sv["v_c"], do_c, sv["o_c"], sv["lse_c"], cfg.CH, 1, 128, 128, sc_h, cfg.branches, cfg.TB, cfg.W)
    dq_c = _flash_dq(n + "attn_c_dq", *args_c)
    dk_c, dv_c = _flash_dkv(n + "attn_c_dkv", *args_c)
    dproj, _ = _headprep_bwd(n + "cq", dq_c, sv["proj"], cfg.o_cq, cfg.CH, None, tabs["c"], dproj)
    dproj, _ = _headprep_bwd(n + "ck", dk_c, sv["proj"], cfg.o_ck, cfg.CH, None, tabs["c"], dproj)
    dproj, _ = _headprep_bwd(n + "cv", dv_c, sv["proj"], cfg.o_cv, cfg.CH, None, None, dproj)
    args_b = (sv["q_b"], sv["k_b"], sv["v_b"], do_b, sv["o_b"], sv["lse_b"], cfg.BH, cfg.G, 128, 128, sc_h)
    dq_b = _flash_dq(n + "attn_b_dq", *args_b)
    dk_b, dv_b = _flash_dkv(n + "attn_b_dkv", *args_b)
    dproj, dgqn = _headprep_bwd(n + "bq", dq_b, sv["proj"], cfg.o_bq, cfg.BH, gqn, tabs["b"], dproj)
    dproj, dgkn = _headprep_bwd(n + "bk", dk_b, sv["proj"], cfg.o_bk, cfg.BKV, gkn, tabs["b"], dproj)
    dproj, _ = _headprep_bwd(n + "bv", dv_b, sv["proj"], cfg.o_bv, cfg.BKV, None, None, dproj)
    args_a = (sv["q_a"], sv["k_a"], sv["v_a"], do_a, sv["o_a"], sv["lse_a"], cfg.AH, 1, 256, 128, sc_a)
    dq_a = _flash_dq(n + "attn_a_dq", *args_a)
    dk_a, dv_a = _flash_dkv(n + "attn_a_dkv", *args_a)
    dqa, dkva, dkpe = _mla_build_bwd(n + "mla_build", cfg, dq_a, dk_a, dv_a, tabs["aq"])
    dcq = _mm_nt(n + "uq_dx", dqa, (W["uq_p"], "col", l))
    dwuq = _mm_tn(n + "uq_dw", sv["cqn"], dqa, "col", 0, cfg.QLP, cfg.AH * 256, jnp.zeros((1, 1, cfg.QLP, cfg.AH * 256), F32))
    dckv = _mm_nt(n + "ukv_dx", dkva, (W["ukv_p"], "col", l))
    dwukv = _mm_tn(n + "ukv_dw", sv["ckvn"], dkva, "col", 0, cfg.KVW, cfg.AH * 256, jnp.zeros((1, 1, cfg.KVW, cfg.AH * 256), F32))
    dproj, dgq, dgkv = _mla_prep_bwd(n + "mla_prep", cfg, dcq, dckv, dkpe, sv["proj"], gq, gkv, tabs["akr"], dproj)
    dh = _mm_nt(n + "proj_dx", dproj, (W["w_in"], "col", l))
    dw("proj_dw", sv["h"], dproj, "w_in")
    dx, dln1 = _rms_bwd(n + "ln1", sv["x"], ln1, dh, dx1)
    gains = dict(ln1_g=dln1[0], g_q_a=dgq[0, :cfg.QL], g_kv_a=dgkv[0, cfg.KOFF:cfg.KOFF + cfg.KVL], g_qn_b=dgqn[0],
                 g_kn_b=dgkn[0], g_out=dgout[0], ln2_g=dln2[0])
    return dx, gains, _uq_grad_unpadded(cfg, dwuq[0, 0]), _ukv_grad_unpadded(cfg, dwukv[0, 0])


SMALL_NAMES = ("ln1_g", "g_q_a", "g_kv_a", "g_qn_b", "g_kn_b", "g_out", "ln2_g")
MAT_NAMES = ("w_in", "w_uq", "w_ukv", "w_out", "w_ff1", "w_ff2")


def _pack_small(cfg, per_layer, final):
    flat = jnp.concatenate([per_layer[k].reshape(-1) for k in SMALL_NAMES] + [final.reshape(-1)])
    total = flat.shape[0]
    rows = _rup(-(-total // LANES), 8)
    return jnp.pad(flat, (0, rows * LANES - total)).reshape(rows, LANES)


def _unpack_small(cfg, packed, shapes):
    flat = packed.reshape(-1)
    out, off = {}, 0
    for k in SMALL_NAMES + ("ln_f_g",):
        n = math.prod(shapes[k])
        out[k] = flat[off:off + n].reshape(shapes[k])
        off += n
    return out


def _step(cfg, w, m, v, x, tgt):
    DEPTH, hd = cfg.DEPTH, cfg.HD
    c = lax.axis_index("c")
    me_chip = 2 * lax.axis_index("x") + lax.axis_index("y")
    c_arr = jnp.reshape(c, (1,)).astype(jnp.int32)
    me_arr = jnp.reshape(me_chip, (1,)).astype(jnp.int32)
    mats = {mt[0]: mt for mt in cfg.mats}

    shards = []
    for name in MAT_NAMES:
        _, Ks, Ns, _ = mats[name]
        shards.append(_cast_rows("cast_" + name, w[name].reshape(DEPTH * Ks, Ns), MM_DT).reshape(DEPTH, Ks, Ns))
    gathered = dict(zip(MAT_NAMES, _allgather_weights(shards, hd)))
    W = dict(w_in=gathered["w_in"], w_out=gathered["w_out"], w_ff1=gathered["w_ff1"], w_ff2=gathered["w_ff2"],
             uq_p=_uq_padded(cfg, gathered["w_uq"]), ukv_p=_ukv_padded(cfg, gathered["w_ukv"]))
    tabs = _all_tables(cfg)

    def small_of(l):
        return (w["ln1_g"][l][None], _pad_lanes(w["g_q_a"][l], 0, cfg.QLP), _pad_lanes(w["g_kv_a"][l], cfg.KOFF, cfg.KVW),
                w["g_qn_b"][l][None], w["g_kn_b"][l][None], w["g_out"][l][None], w["ln2_g"][l][None])

    saved = []
    xc = x
    for l in range(DEPTH):
        xc, sv = _layer_fwd(cfg, l, xc, W, small_of(l), tabs)
        saved.append(sv)
    dx, dlnf, loss_rows = _final_loss("final_loss", xc, w["ln_f_g"][None], tgt)
    loss = loss_rows[0, 0]
    GW = {name: jnp.zeros((N_SHARD, DEPTH, mats[name][1], mats[name][2]), F32) for name in ("w_in", "w_out", "w_ff1", "w_ff2")}
    gain_rows = [None] * DEPTH
    duq, dukv = [None] * DEPTH, [None] * DEPTH
    for l in reversed(range(DEPTH)):
        dx, gain_rows[l], duq[l], dukv[l] = _layer_bwd(cfg, l, dx, saved[l], W, small_of(l), tabs, GW)
    GW["w_uq"] = jnp.stack(duq, axis=1)
    GW["w_ukv"] = jnp.stack(dukv, axis=1)

    gws = [GW[name] for name in MAT_NAMES]
    ras = _exchange_sibling_halves(gws, hd)
    ps = [_add_sibling("rs_add_sib_" + name, g, r, c_arr, hd) for name, g, r in zip(MAT_NAMES, gws, ras)]
    rbs = _exchange_chips(ps)
    rs = [_add_chips("rs_add_chips_" + name, p, rb, me_arr) for name, p, rb in zip(MAT_NAMES, ps, rbs)]
    full = _share_reduced(rs)
    grad, delta, new_m, new_v = {}, {}, {}, {}
    for name, gfull in zip(MAT_NAMES, full):
        _, Ks, Ns, _ = mats[name]
        g2 = gfull.reshape(DEPTH * Ks, Ns)
        d2, m2, v2 = _adamw("adamw_" + name, w[name].reshape(DEPTH * Ks, Ns), g2, m[name].reshape(DEPTH * Ks, Ns),
                            v[name].reshape(DEPTH * Ks, Ns))
        shp = (DEPTH, Ks, Ns)
        grad[name], delta[name], new_m[name], new_v[name] = g2.reshape(shp), d2.reshape(shp), m2.reshape(shp), v2.reshape(shp)

    per_layer = {k: jnp.stack([gain_rows[l][k] for l in range(DEPTH)]) for k in SMALL_NAMES}
    shapes = {k: w[k].shape for k in SMALL_NAMES + ("ln_f_g",)}
    gsum = _allreduce_small(_pack_small(cfg, per_layer, dlnf[0]))
    pk = lambda d: _pack_small(cfg, {k: d[k] for k in SMALL_NAMES}, d["ln_f_g"])
    d_s, m_s, v_s = _adamw("adamw_small", pk(w), gsum, pk(m), pk(v))
    for res, packed in ((grad, gsum), (delta, d_s), (new_m, m_s), (new_v, v_s)):
        res.update(_unpack_small(cfg, packed, shapes))
    return loss, dx, grad, delta, new_m, new_v


WEIGHT_NAMES = ("ln1_g", "w_in", "g_q_a", "w_uq", "g_kv_a", "w_ukv", "g_qn_b", "g_kn_b", "g_out", "w_out", "ln2_g",
                "w_ff1", "w_ff2", "ln_f_g")


def _run(cfg, args):
    nw = len(WEIGHT_NAMES)
    x, tgt = args[0], args[1 + nw]
    w = dict(zip(WEIGHT_NAMES, args[1:1 + nw]))
    m = dict(zip(WEIGHT_NAMES, args[2 + nw:2 + 2 * nw]))
    v = dict(zip(WEIGHT_NAMES, args[2 + 2 * nw:2 + 3 * nw]))
    loss, dx, grad, delta, new_m, new_v = _step(cfg, w, m, v, x.reshape(cfg.S, cfg.D), tgt.reshape(cfg.S, cfg.D))
    loss = lax.psum(loss, MESH_AXES)
    return (loss, dx.reshape(x.shape), *[grad[k] for k in WEIGHT_NAMES], *[delta[k] for k in WEIGHT_NAMES],
            *[new_m[k] for k in WEIGHT_NAMES], *[new_v[k] for k in WEIGHT_NAMES])


def kernel(x, ln1_g, w_in, g_q_a, w_uq, g_kv_a, w_ukv, g_qn_b, g_kn_b, g_out, w_out, ln2_g, w_ff1, w_ff2, ln_f_g, loss_target, m_ln1_g, m_w_in, m_g_q_a, m_w_uq, m_g_kv_a, m_w_ukv, m_g_qn_b, m_g_kn_b, m_g_out, m_w_out, m_ln2_g, m_w_ff1, m_w_ff2, m_ln_f_g, v_ln1_g, v_w_in, v_g_q_a, v_w_uq, v_g_kv_a, v_w_ukv, v_g_qn_b, v_g_kn_b, v_g_out, v_w_out, v_ln2_g, v_w_ff1, v_w_ff2, v_ln_f_g):
    return _run(Cfg(), (x, ln1_g, w_in, g_q_a, w_uq, g_kv_a, w_ukv, g_qn_b, g_kn_b, g_out, w_out, ln2_g, w_ff1, w_ff2, ln_f_g, loss_target, m_ln1_g, m_w_in, m_g_q_a, m_w_uq, m_g_kv_a, m_w_ukv, m_g_qn_b, m_g_kn_b, m_g_out, m_w_out, m_ln2_g, m_w_ff1, m_w_ff2, m_ln_f_g, v_ln1_g, v_w_in, v_g_q_a, v_w_uq, v_g_kv_a, v_w_ukv, v_g_qn_b, v_g_kn_b, v_g_out, v_w_out, v_ln2_g, v_w_ff1, v_w_ff2, v_ln_f_g))
```

```python
import functools
import math

import jax
import jax.numpy as jnp
from jax import lax
from jax.experimental import pallas as pl
from jax.experimental.pallas import tpu as pltpu

F32 = jnp.float32
MM_DT = jnp.bfloat16
LANES = 128
SUBLANES_F32 = 8
SUBLANES_BF16 = 16
VMEM_LIMIT = 48 * 1024 * 1024
EPS = 1e-6
NEG = -1e30
ROPE_THETA = 10000.0
ADAM_LR, ADAM_B1, ADAM_B2, ADAM_EPS, ADAM_WD, ADAM_STEP = 0.001, 0.9, 0.999, 1e-08, 0.01, 10
MESH_AXES = ("x", "y", "c")
N_SHARD = 4
MESH = pl.DeviceIdType.MESH

NN = (((1,), (0,)), ((), ()))
NT = (((1,), (1,)), ((), ()))
TN = (((0,), (0,)), ((), ()))


def _pcall(body, **kw):
    return pl.pallas_call(body, **kw)


def _rup(n, m):
    return -(-n // m) * m


def _pick(n, pref, mult):
    best = None
    for t in range(mult, min(n, pref) + 1, mult):
        if n % t == 0:
            best = t
    return best if best is not None else n


class Cfg:
    def __init__(self, S=4096, D=2048, DEPTH=4, AH=4, QL=448, KVL=512, BH=6, BKV=2, CH=6,
                 BRANCHES=((128, 1), (512, 4), (2048, 16)), DFF=8192, GRID_W=64, TB=512):
        self.S, self.D, self.DEPTH, self.AH, self.QL, self.KVL = S, D, DEPTH, AH, QL, KVL
        self.BH, self.BKV, self.CH, self.DFF, self.GRID_W, self.TB = BH, BKV, CH, DFF, GRID_W, TB
        self.G = BH // BKV
        self.AW, self.BW, self.CW = AH * 128, BH * 128, CH * 128
        self.MIX = self.AW + self.BW + self.CW
        self.QLP = _rup(QL, LANES)
        self.KV0 = (QL // LANES) * LANES
        self.PW = QL + KVL + 64
        assert self.PW % LANES == 0
        self.KVW = self.PW - self.KV0
        self.KOFF = QL - self.KV0
        self.o_bq = self.PW
        self.o_bk = self.o_bq + self.BW
        self.o_bv = self.o_bk + BKV * 128
        self.o_cq = self.o_bv + BKV * 128
        self.o_ck = self.o_cq + self.CW
        self.o_cv = self.o_ck + self.CW
        self.IN = self.o_cv + self.CW
        self.UQ, self.UKV = AH * 192, AH * 256
        self.branches = tuple(((w // (2 * d)) * d, d) for w, d in BRANCHES)
        for _, d in self.branches:
            assert d & (d - 1) == 0
        self.W = -(-max(r for r, _ in self.branches) // TB)
        assert S % TB == 0 and DEPTH % 2 == 0
        self.HD = DEPTH // 2
        self.mats = (("w_in", D, self.IN // 4, "col"), ("w_uq", QL, self.UQ // 4, "col"),
                     ("w_ukv", KVL, self.UKV // 4, "col"), ("w_out", self.MIX // 4, D, "row"),
                     ("w_ff1", D, DFF // 4, "col"), ("w_ff2", DFF // 4, D, "row"))


def _mm_call(name, mode, operands, in_specs, out_shape, out_specs, grid, acc_shape, epi, n_extra, aliases=None):
    nk = grid[2]

    def body(*refs):
        a_ref, b_ref = refs[0], refs[1]
        ex = refs[2:2 + n_extra]
        outs = refs[2 + n_extra:-1]
        acc = refs[-1]
        k = pl.program_id(2)

        @pl.when(k == 0)
        def _():
            acc[...] = jnp.zeros_like(acc)

        acc[...] += lax.dot_general(a_ref[...].astype(MM_DT), b_ref[...].astype(MM_DT), mode,
                                    preferred_element_type=F32)

        @pl.when(k == nk - 1)
        def _():
            epi(acc[...], ex, outs)

    return _pcall(body, name=name, grid=grid, in_specs=in_specs, out_specs=out_specs, out_shape=out_shape,
                  scratch_shapes=[pltpu.VMEM(acc_shape, F32)], input_output_aliases=aliases or {},
                  compiler_params=pltpu.CompilerParams(dimension_semantics=("parallel", "parallel", "arbitrary"),
                                                       vmem_limit_bytes=VMEM_LIMIT))(*operands)


def _wspec(kind, l, Rs, Cs, br, bc, rfn, cfn):
    assert Rs % br == 0 and Cs % bc == 0
    if kind == "col":
        npc = Cs // bc
        return pl.BlockSpec((None, None, br, bc), lambda i, j, k: (cfn(i, j, k) // npc, l, rfn(i, j, k), cfn(i, j, k) % npc))
    npr = Rs // br
    return pl.BlockSpec((None, None, br, bc), lambda i, j, k: (rfn(i, j, k) // npr, l, rfn(i, j, k) % npr, cfn(i, j, k)))


def _epi_plain(acc, ex, outs):
    outs[0][...] = acc.astype(outs[0].dtype)


def _epi_residual(acc, ex, outs):
    outs[0][...] = ex[0][...] + acc


def _epi_relu2(acc, ex, outs):
    outs[0][...] = acc.astype(outs[0].dtype)
    r = jnp.maximum(acc, 0.0)
    outs[1][...] = (r * r).astype(outs[1].dtype)


def _epi_drelu2(acc, ex, outs):
    a = ex[0][...].astype(F32)
    outs[0][...] = (acc * (2.0 * jnp.maximum(a, 0.0))).astype(outs[0].dtype)


def _wdims(wd):
    Wg, kind, l = wd
    ns, _, Rs, Cs = Wg.shape
    K = Rs * ns if kind == "row" else Rs
    N = Cs * ns if kind == "col" else Cs
    return Wg, kind, l, Rs, Cs, K, N


def _mm_nn(name, a, wd, epi=_epi_plain, out_dtypes=(F32,), extra=None):
    Wg, kind, l, Rs, Cs, K, N = _wdims(wd)
    M = a.shape[0]
    tm, tk, tn = _pick(M, 1024, 16), (Rs if Rs <= 1024 else _pick(Rs, 512, LANES)), _pick(Cs, 1152, LANES)
    grid = (M // tm, N // tn, K // tk)
    in_specs = [pl.BlockSpec((tm, tk), lambda i, j, k: (i, k)),
                _wspec(kind, l, Rs, Cs, tk, tn, lambda i, j, k: k, lambda i, j, k: j)]
    ops = [a, Wg]
    if extra is not None:
        in_specs.append(pl.BlockSpec((tm, tn), lambda i, j, k: (i, j)))
        ops.append(extra)
    o_spec = pl.BlockSpec((tm, tn), lambda i, j, k: (i, j))
    outs = tuple(jax.ShapeDtypeStruct((M, N), dt) for dt in out_dtypes)
    res = _mm_call(name, NN, ops, in_specs, outs, tuple(o_spec for _ in outs), grid, (tm, tn), epi,
                   0 if extra is None else 1)
    return res[0] if len(res) == 1 else res


def _mm_nt(name, g, wd, epi=_epi_plain, out_dtype=F32, extra=None):
    Wg, kind, l, Rs, Cs, K, N = _wdims(wd)
    M = g.shape[0]
    tm, tn, tk = _pick(M, 1024, 16), _pick(Rs, 1024, LANES), _pick(Cs, 512, LANES)
    grid = (M // tm, K // tn, N // tk)
    in_specs = [pl.BlockSpec((tm, tk), lambda i, j, k: (i, k)),
                _wspec(kind, l, Rs, Cs, tn, tk, lambda i, j, k: j, lambda i, j, k: k)]
    ops = [g, Wg]
    if extra is not None:
        in_specs.append(pl.BlockSpec((tm, tn), lambda i, j, k: (i, j)))
        ops.append(extra)
    res = _mm_call(name, NT, ops, in_specs, (jax.ShapeDtypeStruct((M, K), out_dtype),),
                   (pl.BlockSpec((tm, tn), lambda i, j, k: (i, j)),), grid, (tm, tn), epi, 0 if extra is None else 1)
    return res[0]


def _mm_tn(name, a, g, kind, l, Rs, Cs, buf):
    M, K = a.shape
    N = g.shape[1]
    tm, tn, tk = _pick(Rs, 1024, LANES), _pick(Cs, 1152, LANES), _pick(M, 512, LANES)
    grid = (K // tm, N // tn, M // tk)
    in_specs = [pl.BlockSpec((tk, tm), lambda i, j, k: (k, i)),
                pl.BlockSpec((tk, tn), lambda i, j, k: (k, j)),
                pl.BlockSpec(memory_space=pl.ANY)]
    o_spec = _wspec(kind, l, Rs, Cs, tm, tn, lambda i, j, k: i, lambda i, j, k: j)
    res = _mm_call(name, TN, [a, g, buf], in_specs, (jax.ShapeDtypeStruct(buf.shape, F32),), (o_spec,), grid,
                   (tm, tn), _epi_plain, 1, aliases={2: 0})
    return res[0]


def _row_params():
    return pltpu.CompilerParams(dimension_semantics=("arbitrary",), vmem_limit_bytes=VMEM_LIMIT)


def _rms_fwd(name, x, g):
    S, D = x.shape
    tr = _pick(S, 256, 16)

    def body(x_ref, g_ref, o_ref):
        xv = x_ref[...]
        r = lax.rsqrt(jnp.mean(xv * xv, axis=-1, keepdims=True) + EPS)
        o_ref[...] = (xv * r * g_ref[...]).astype(o_ref.dtype)

    return _pcall(body, name=name, grid=(S // tr,),
                  in_specs=[pl.BlockSpec((tr, D), lambda i: (i, 0)), pl.BlockSpec((1, D), lambda i: (0, 0))],
                  out_specs=pl.BlockSpec((tr, D), lambda i: (i, 0)), out_shape=jax.ShapeDtypeStruct((S, D), MM_DT),
                  compiler_params=_row_params())(x, g)


def _acc_rows(ref, part, first):
    @pl.when(first)
    def _():
        ref[...] = jnp.zeros_like(ref)

    ref[...] += jnp.broadcast_to(part, ref.shape)


def _rms_bwd(name, x, g, dy, res):
    S, D = x.shape
    tr = _pick(S, 256, 16)

    def body(x_ref, g_ref, dy_ref, res_ref, dx_ref, dg_ref):
        xv = x_ref[...]
        r = lax.rsqrt(jnp.mean(xv * xv, axis=-1, keepdims=True) + EPS)
        xh = xv * r
        dyv = dy_ref[...]
        dn = dyv * g_ref[...]
        dx_ref[...] = res_ref[...] + r * (dn - xh * jnp.mean(dn * xh, axis=-1, keepdims=True))
        _acc_rows(dg_ref, jnp.sum(dyv * xh, axis=0, keepdims=True), pl.program_id(0) == 0)

    row = pl.BlockSpec((tr, D), lambda i: (i, 0))
    return _pcall(body, name=name, grid=(S // tr,),
                  in_specs=[row, pl.BlockSpec((1, D), lambda i: (0, 0)), row, row],
                  out_specs=(row, pl.BlockSpec((8, D), lambda i: (0, 0))),
                  out_shape=(jax.ShapeDtypeStruct((S, D), F32), jax.ShapeDtypeStruct((8, D), F32)),
                  compiler_params=_row_params())(x, g, dy, res)


def _final_loss(name, x, g, tgt):
    S, D = x.shape
    tr = _pick(S, 256, 16)

    def body(x_ref, g_ref, t_ref, dx_ref, dg_ref, loss_ref):
        xv = x_ref[...]
        r = lax.rsqrt(jnp.mean(xv * xv, axis=-1, keepdims=True) + EPS)
        xh = xv * r
        gv = g_ref[...]
        e = xh * gv - t_ref[...]
        part = 0.5 * jnp.sum(jnp.mean(e * e, axis=-1, keepdims=True), axis=0, keepdims=True)
        dy = e * (1.0 / D)
        dn = dy * gv
        dx_ref[...] = r * (dn - xh * jnp.mean(dn * xh, axis=-1, keepdims=True))
        first = pl.program_id(0) == 0
        _acc_rows(dg_ref, jnp.sum(dy * xh, axis=0, keepdims=True), first)
        _acc_rows(loss_ref, part, first)

    row = pl.BlockSpec((tr, D), lambda i: (i, 0))
    return _pcall(body, name=name, grid=(S // tr,),
                  in_specs=[row, pl.BlockSpec((1, D), lambda i: (0, 0)), row],
                  out_specs=(row, pl.BlockSpec((8, D), lambda i: (0, 0)), pl.BlockSpec((8, LANES), lambda i: (0, 0))),
                  out_shape=(jax.ShapeDtypeStruct((S, D), F32), jax.ShapeDtypeStruct((8, D), F32),
                             jax.ShapeDtypeStruct((8, LANES), F32)),
                  compiler_params=_row_params())(x, g, tgt)


def _rope_tables(cos, sin, off, w):
    S = cos.shape[0]
    h = w // 2
    z = lambda n: jnp.zeros((S, n), F32)
    C = jnp.concatenate([z(off), cos, cos, z(LANES - off - w)], axis=1)
    SP = jnp.concatenate([z(off + h), sin, z(LANES - off - w)], axis=1)
    SN = jnp.concatenate([z(off), -sin, z(LANES - off - h)], axis=1)
    return C, SP, SN


def _angles(pos, dim):
    inv = jnp.power(ROPE_THETA, -jnp.arange(0, dim, 2, dtype=F32) / dim)
    ang = pos.astype(F32)[:, None] * inv[None, :]
    return jnp.cos(ang), jnp.sin(ang)


def _all_tables(cfg):
    S = cfg.S
    pos = jnp.arange(S, dtype=F32)
    rows = S // cfg.GRID_W
    row = jnp.repeat(jnp.arange(rows, dtype=F32), cfg.GRID_W)
    col = jnp.tile(jnp.arange(cfg.GRID_W, dtype=F32), rows)
    ca, sa = _angles(pos, 64)
    cc, sc = _angles(pos, 128)
    cr, sr = _angles(row, 64)
    cl, sl = _angles(col, 64)
    t_b = tuple(a + b for a, b in zip(_rope_tables(cr, sr, 0, 64), _rope_tables(cl, sl, 64, 64)))
    return {"aq": (_rope_tables(ca, sa, 0, 64), 64), "akr": (_rope_tables(ca, sa, 64, 64), 64),
            "b": (t_b, 64), "c": (_rope_tables(cc, sc, 0, 128), 128)}


def _rope(x, C, SP, SN, w):
    h = w // 2
    if 2 * h == LANES:
        return x * C + pltpu.roll(x, h, 1) * (SP + SN)
    return x * C + pltpu.roll(x, h, 1) * SP + pltpu.roll(x, LANES - h, 1) * SN


def _rope_t(dy, C, SP, SN, w):
    h = w // 2
    if 2 * h == LANES:
        return dy * C + pltpu.roll(dy * (SP + SN), h, 1)
    return dy * C + pltpu.roll(dy * SP, LANES - h, 1) + pltpu.roll(dy * SN, h, 1)


def _grid2_params():
    return pltpu.CompilerParams(dimension_semantics=("arbitrary", "arbitrary"), vmem_limit_bytes=VMEM_LIMIT)


def _headprep_fwd(name, proj, col_off, nb, gain, tabs):
    S = proj.shape[0]
    tr = _pick(S, 1024, 16)
    cb = col_off // LANES
    norm, rope = gain is not None, tabs is not None
    w = tabs[1] if rope else 0

    def body(*refs):
        x_ref = refs[0]
        pos = 1
        xv = x_ref[...]
        if norm:
            r = lax.rsqrt(jnp.mean(xv * xv, axis=-1, keepdims=True) + EPS)
            xv = xv * r * refs[pos][...]
            pos += 1
        if rope:
            xv = _rope(xv, refs[pos][...], refs[pos + 1][...], refs[pos + 2][...], w)
            pos += 3
        refs[pos][...] = xv.astype(refs[pos].dtype)

    ops, in_specs = [proj], [pl.BlockSpec((tr, LANES), lambda i, j: (i, cb + j))]
    if norm:
        ops.append(gain)
        in_specs.append(pl.BlockSpec((1, LANES), lambda i, j: (0, 0)))
    if rope:
        ops += list(tabs[0])
        in_specs += [pl.BlockSpec((tr, LANES), lambda i, j: (i, 0))] * 3
    return _pcall(body, name=name, grid=(S // tr, nb), in_specs=in_specs,
                  out_specs=pl.BlockSpec((tr, LANES), lambda i, j: (i, j)),
                  out_shape=jax.ShapeDtypeStruct((S, nb * LANES), MM_DT), compiler_params=_grid2_params())(*ops)


def _headprep_bwd(name, dy, proj, col_off, nb, gain, tabs, dproj):
    S = proj.shape[0]
    tr = _pick(S, 1024, 16)
    cb = col_off // LANES
    norm, rope = gain is not None, tabs is not None
    w = tabs[1] if rope else 0

    def body(*refs):
        dz = refs[0][...]
        pos = 1
        if norm:
            x_ref, g_ref = refs[pos], refs[pos + 1]
            pos += 2
        if rope:
            dz = _rope_t(dz, refs[pos][...], refs[pos + 1][...], refs[pos + 2][...], w)
            pos += 3
        pos += 1
        o_ref = refs[pos]
        if norm:
            dg_ref = refs[pos + 1]
            xv = x_ref[...]
            r = lax.rsqrt(jnp.mean(xv * xv, axis=-1, keepdims=True) + EPS)
            n = xv * r
            first = (pl.program_id(0) == 0) & (pl.program_id(1) == 0)
            _acc_rows(dg_ref, jnp.sum(dz * n, axis=0, keepdims=True), first)
            dn = dz * g_ref[...]
            dz = r * (dn - n * jnp.mean(dn * n, axis=-1, keepdims=True))
        o_ref[...] = dz.astype(o_ref.dtype)

    ops, in_specs = [dy], [pl.BlockSpec((tr, LANES), lambda i, j: (i, j))]
    if norm:
        ops += [proj, gain]
        in_specs += [pl.BlockSpec((tr, LANES), lambda i, j: (i, cb + j)), pl.BlockSpec((1, LANES), lambda i, j: (0, 0))]
    if rope:
        ops += list(tabs[0])
        in_specs += [pl.BlockSpec((tr, LANES), lambda i, j: (i, 0))] * 3
    alias_idx = len(ops)
    ops.append(dproj)
    in_specs.append(pl.BlockSpec(memory_space=pl.ANY))
    out_specs = [pl.BlockSpec((tr, LANES), lambda i, j: (i, cb + j))]
    out_shape = [jax.ShapeDtypeStruct(dproj.shape, dproj.dtype)]
    if norm:
        out_specs.append(pl.BlockSpec((8, LANES), lambda i, j: (0, 0)))
        out_shape.append(jax.ShapeDtypeStruct((8, LANES), F32))
    res = _pcall(body, name=name, grid=(S // tr, nb), in_specs=in_specs, out_specs=tuple(out_specs),
                 out_shape=tuple(out_shape), input_output_aliases={alias_idx: 0}, compiler_params=_grid2_params())(*ops)
    return (res[0], res[1]) if norm else (res[0], None)


def _masked_rms(xv, lo, n):
    lane = lax.broadcasted_iota(jnp.int32, xv.shape, 1)
    xm = jnp.where((lane >= lo) & (lane < lo + n), xv, 0.0)
    r = lax.rsqrt(jnp.sum(xm * xm, axis=-1, keepdims=True) * (1.0 / n) + EPS)
    return xm * r, r


def _mla_prep_fwd(name, cfg, proj, gq, gkv, tabs):
    S = cfg.S
    tr = _pick(S, 256, 16)
    (C, SP, SN), w = tabs

    def body(p_ref, gq_ref, gkv_ref, c_ref, sp_ref, sn_ref, cq_ref, ckv_ref, kpe_ref):
        nq, _ = _masked_rms(p_ref[:, 0:cfg.QLP], 0, cfg.QL)
        cq_ref[...] = (nq * gq_ref[...]).astype(cq_ref.dtype)
        nk, _ = _masked_rms(p_ref[:, cfg.KV0:cfg.PW], cfg.KOFF, cfg.KVL)
        ckv_ref[...] = (nk * gkv_ref[...]).astype(ckv_ref.dtype)
        kr = _rope(p_ref[:, cfg.PW - LANES:cfg.PW], c_ref[...], sp_ref[...], sn_ref[...], w)
        kpe_ref[...] = pltpu.roll(kr, 64, 1).astype(kpe_ref.dtype)

    tab = pl.BlockSpec((tr, LANES), lambda i: (i, 0))
    return _pcall(body, name=name, grid=(S // tr,),
                  in_specs=[pl.BlockSpec((tr, cfg.PW), lambda i: (i, 0)), pl.BlockSpec((1, cfg.QLP), lambda i: (0, 0)),
                            pl.BlockSpec((1, cfg.KVW), lambda i: (0, 0)), tab, tab, tab],
                  out_specs=(pl.BlockSpec((tr, cfg.QLP), lambda i: (i, 0)), pl.BlockSpec((tr, cfg.KVW), lambda i: (i, 0)), tab),
                  out_shape=(jax.ShapeDtypeStruct((S, cfg.QLP), MM_DT), jax.ShapeDtypeStruct((S, cfg.KVW), MM_DT),
                             jax.ShapeDtypeStruct((S, LANES), MM_DT)),
                  compiler_params=_row_params())(proj, gq, gkv, C, SP, SN)


def _mla_prep_bwd(name, cfg, dcq, dckv, dkpe, proj, gq, gkv, tabs, dproj):
    S = cfg.S
    tr = _pick(S, 256, 16)
    (C, SP, SN), w = tabs

    def body(dcq_ref, dckv_ref, dkpe_ref, p_ref, gq_ref, gkv_ref, c_ref, sp_ref, sn_ref, buf_ref, o_ref, dgq_ref, dgkv_ref):
        first = pl.program_id(0) == 0

        def norm_bwd(xv, lo, n, dz, g_ref, dg_ref):
            nrm, r = _masked_rms(xv, lo, n)
            _acc_rows(dg_ref, jnp.sum(dz * nrm, axis=0, keepdims=True), first)
            dn = dz * g_ref[...]
            return r * (dn - nrm * (jnp.sum(dn * nrm, axis=-1, keepdims=True) * (1.0 / n)))

        dxq = norm_bwd(p_ref[:, 0:cfg.QLP], 0, cfg.QL, dcq_ref[...], gq_ref, dgq_ref)
        dxk = norm_bwd(p_ref[:, cfg.KV0:cfg.PW], cfg.KOFF, cfg.KVL, dckv_ref[...], gkv_ref, dgkv_ref)
        dxr = _rope_t(pltpu.roll(dkpe_ref[...], 64, 1), c_ref[...], sp_ref[...], sn_ref[...], w)
        for cidx in range(cfg.PW // LANES):
            lo = cidx * LANES
            parts = []
            if lo < cfg.QLP:
                parts.append(dxq[:, lo:lo + LANES])
            if lo >= cfg.KV0:
                parts.append(dxk[:, lo - cfg.KV0:lo - cfg.KV0 + LANES])
            if lo == cfg.PW - LANES:
                parts.append(dxr)
            o_ref[:, lo:lo + LANES] = functools.reduce(lambda a, b: a + b, parts).astype(o_ref.dtype)

    tab = pl.BlockSpec((tr, LANES), lambda i: (i, 0))
    res = _pcall(body, name=name, grid=(S // tr,),
                 in_specs=[pl.BlockSpec((tr, cfg.QLP), lambda i: (i, 0)), pl.BlockSpec((tr, cfg.KVW), lambda i: (i, 0)), tab,
                           pl.BlockSpec((tr, cfg.PW), lambda i: (i, 0)), pl.BlockSpec((1, cfg.QLP), lambda i: (0, 0)),
                           pl.BlockSpec((1, cfg.KVW), lambda i: (0, 0)), tab, tab, tab, pl.BlockSpec(memory_space=pl.ANY)],
                 out_specs=(pl.BlockSpec((tr, cfg.PW), lambda i: (i, 0)), pl.BlockSpec((8, cfg.QLP), lambda i: (0, 0)),
                            pl.BlockSpec((8, cfg.KVW), lambda i: (0, 0))),
                 out_shape=(jax.ShapeDtypeStruct(dproj.shape, dproj.dtype), jax.ShapeDtypeStruct((8, cfg.QLP), F32),
                            jax.ShapeDtypeStruct((8, cfg.KVW), F32)),
                 input_output_aliases={9: 0}, compiler_params=_row_params())(dcq, dckv, dkpe, proj, gq, gkv, C, SP, SN, dproj)
    return res


def _mla_build_fwd(name, cfg, qa, kva, kpe, tabs):
    S, AH = cfg.S, cfg.AH
    tr = _pick(S, 256, 16)
    (C, SP, SN), w = tabs

    def body(qa_ref, kva_ref, kpe_ref, c_ref, sp_ref, sn_ref, q_ref, k_ref, v_ref):
        for h in range(AH):
            a, b = 256 * h, 256 * h + LANES
            q_ref[:, a:b] = qa_ref[:, a:b].astype(q_ref.dtype)
            q_ref[:, b:b + LANES] = _rope(qa_ref[:, b:b + LANES], c_ref[...], sp_ref[...], sn_ref[...], w).astype(q_ref.dtype)
            k_ref[:, a:b] = kva_ref[:, a:b].astype(k_ref.dtype)
            k_ref[:, b:b + LANES] = kpe_ref[...]
            v_ref[:, LANES * h:LANES * (h + 1)] = kva_ref[:, b:b + LANES].astype(v_ref.dtype)

    tab = pl.BlockSpec((tr, LANES), lambda i: (i, 0))
    wide = pl.BlockSpec((tr, AH * 256), lambda i: (i, 0))
    return _pcall(body, name=name, grid=(S // tr,), in_specs=[wide, wide, tab, tab, tab, tab],
                  out_specs=(wide, wide, pl.BlockSpec((tr, AH * LANES), lambda i: (i, 0))),
                  out_shape=(jax.ShapeDtypeStruct((S, AH * 256), MM_DT), jax.ShapeDtypeStruct((S, AH * 256), MM_DT),
                             jax.ShapeDtypeStruct((S, AH * LANES), MM_DT)),
                  compiler_params=_row_params())(qa, kva, kpe, C, SP, SN)


def _mla_build_bwd(name, cfg, dq, dk, dv, tabs):
    S, AH = cfg.S, cfg.AH
    tr = _pick(S, 256, 16)
    (C, SP, SN), w = tabs

    def body(dq_ref, dk_ref, dv_ref, c_ref, sp_ref, sn_ref, dqa_ref, dkva_ref, dkpe_ref):
        dkpe = None
        for h in range(AH):
            a, b = 256 * h, 256 * h + LANES
            dqa_ref[:, a:b] = dq_ref[:, a:b].astype(dqa_ref.dtype)
            dqa_ref[:, b:b + LANES] = _rope_t(dq_ref[:, b:b + LANES], c_ref[...], sp_ref[...], sn_ref[...], w).astype(dqa_ref.dtype)
            dkva_ref[:, a:b] = dk_ref[:, a:b].astype(dkva_ref.dtype)
            dkva_ref[:, b:b + LANES] = dv_ref[:, LANES * h:LANES * (h + 1)].astype(dkva_ref.dtype)
            part = dk_ref[:, b:b + LANES]
            dkpe = part if dkpe is None else dkpe + part
        dkpe_ref[...] = dkpe

    tab = pl.BlockSpec((tr, LANES), lambda i: (i, 0))
    wide = pl.BlockSpec((tr, AH * 256), lambda i: (i, 0))
    return _pcall(body, name=name, grid=(S // tr,),
                  in_specs=[wide, wide, pl.BlockSpec((tr, AH * LANES), lambda i: (i, 0)), tab, tab, tab],
                  out_specs=(wide, wide, tab),
                  out_shape=(jax.ShapeDtypeStruct((S, AH * 256), MM_DT), jax.ShapeDtypeStruct((S, AH * 256), MM_DT),
                             jax.ShapeDtypeStruct((S, LANES), F32)),
                  compiler_params=_row_params())(dq, dk, dv, C, SP, SN)


def _outnorm_fwd(name, cfg, oa, ob, oc, g):
    S = cfg.S
    tr = _pick(S, 256, 16)
    widths = (cfg.AW, cfg.BW, cfg.CW)

    def body(a_ref, b_ref, c_ref, g_ref, o_ref):
        off = 0
        for ref, wd in zip((a_ref, b_ref, c_ref), widths):
            v = ref[...]
            r = lax.rsqrt(jnp.mean(v * v, axis=-1, keepdims=True) + EPS)
            o_ref[:, off:off + wd] = (v * r * g_ref[:, off:off + wd]).astype(o_ref.dtype)
            off += wd

    return _pcall(body, name=name, grid=(S // tr,),
                  in_specs=[pl.BlockSpec((tr, wd), lambda i: (i, 0)) for wd in widths] + [pl.BlockSpec((1, cfg.MIX), lambda i: (0, 0))],
                  out_specs=pl.BlockSpec((tr, cfg.MIX), lambda i: (i, 0)),
                  out_shape=jax.ShapeDtypeStruct((S, cfg.MIX), MM_DT), compiler_params=_row_params())(oa, ob, oc, g)


def _outnorm_bwd(name, cfg, dmix, oa, ob, oc, g):
    S = cfg.S
    tr = _pick(S, 256, 16)
    widths = (cfg.AW, cfg.BW, cfg.CW)

    def body(dm_ref, a_ref, b_ref, c_ref, g_ref, da_ref, db_ref, dc_ref, dg_ref):
        off = 0
        parts = []
        for ref, dref, wd in zip((a_ref, b_ref, c_ref), (da_ref, db_ref, dc_ref), widths):
            v = ref[...]
            r = lax.rsqrt(jnp.mean(v * v, axis=-1, keepdims=True) + EPS)
            n = v * r
            dm = dm_ref[:, off:off + wd]
            parts.append(jnp.sum(dm * n, axis=0, keepdims=True))
            dn = dm * g_ref[:, off:off + wd]
            dref[...] = r * (dn - n * jnp.mean(dn * n, axis=-1, keepdims=True))
            off += wd
        _acc_rows(dg_ref, jnp.concatenate(parts, axis=1), pl.program_id(0) == 0)

    segs = [pl.BlockSpec((tr, wd), lambda i: (i, 0)) for wd in widths]
    return _pcall(body, name=name, grid=(S // tr,),
                  in_specs=[pl.BlockSpec((tr, cfg.MIX), lambda i: (i, 0))] + segs + [pl.BlockSpec((1, cfg.MIX), lambda i: (0, 0))],
                  out_specs=tuple(segs) + (pl.BlockSpec((8, cfg.MIX), lambda i: (0, 0)),),
                  out_shape=tuple(jax.ShapeDtypeStruct((S, wd), F32) for wd in widths) + (jax.ShapeDtypeStruct((8, cfg.MIX), F32),),
                  compiler_params=_row_params())(dmix, oa, ob, oc, g)


def _band_bias(cfg):
    t, W = cfg.TB, cfg.W
    st = lax.broadcasted_iota(jnp.int32, (2 * W + 1, t, t), 0)
    row = lax.broadcasted_iota(jnp.int32, (2 * W + 1, t, t), 1)
    col = lax.broadcasted_iota(jnp.int32, (2 * W + 1, t, t), 2)
    d = (W - st) * t + row - col
    ad = jnp.abs(d)
    m = jnp.zeros(d.shape, F32)
    for reach, dil in cfg.branches:
        ok = ad <= reach
        if dil > 1:
            ok = ok & ((d & (dil - 1)) == 0)
        m = m + ok.astype(F32)
    return jnp.where(m > 0, jnp.log(jnp.maximum(m, 1.0)), NEG)


def _attn_params():
    return pltpu.CompilerParams(dimension_semantics=("parallel", "parallel", "arbitrary"), vmem_limit_bytes=VMEM_LIMIT)


def _scores(q_ref, k_ref, scale, bias_ref):
    s = lax.dot_general(q_ref[...], k_ref[...], NT, preferred_element_type=F32) * scale
    return s if bias_ref is None else s + bias_ref[...]


def _flash_fwd(name, q, k, v, H, G, dk, dv, scale, bias=None, W=None):
    S = q.shape[0]
    band = bias is not None
    t = bias.shape[1] if band else _pick(S, 512, LANES)
    n = S // t
    nsteps = 2 * W + 1 if band else n

    def kblock(qi, st):
        return jnp.clip(qi - W + st, 0, n - 1) if band else st

    def body(*refs):
        q_ref, k_ref, v_ref = refs[:3]
        bias_ref = refs[3] if band else None
        o_ref, lse_ref, m_sc, l_sc, acc_sc = refs[-5:]
        qi, st = pl.program_id(1), pl.program_id(2)

        @pl.when(st == 0)
        def _():
            m_sc[...] = jnp.full_like(m_sc, NEG)
            l_sc[...] = jnp.zeros_like(l_sc)
            acc_sc[...] = jnp.zeros_like(acc_sc)

        kj = qi - W + st if band else st

        def step():
            s = _scores(q_ref, k_ref, scale, bias_ref)
            m_prev = m_sc[...]
            m_new = jnp.maximum(m_prev, jnp.max(s, axis=-1, keepdims=True))
            alpha = jnp.exp(m_prev - m_new)
            p = jnp.exp(s - m_new)
            l_sc[...] = alpha * l_sc[...] + jnp.sum(p, axis=-1, keepdims=True)
            acc_sc[...] = alpha * acc_sc[...] + lax.dot_general(p.astype(MM_DT), v_ref[...], NN, preferred_element_type=F32)
            m_sc[...] = m_new

        if band:
            pl.when((kj >= 0) & (kj < n))(step)
        else:
            step()

        @pl.when(st == nsteps - 1)
        def _():
            l = l_sc[...]
            o_ref[...] = acc_sc[...] / l
            lse_ref[...] = jnp.broadcast_to(m_sc[...] + jnp.log(l), lse_ref.shape)

    in_specs = [pl.BlockSpec((t, dk), lambda h, qi, st: (qi, h)),
                pl.BlockSpec((t, dk), lambda h, qi, st: (kblock(qi, st), h // G)),
                pl.BlockSpec((t, dv), lambda h, qi, st: (kblock(qi, st), h // G))]
    ops = [q, k, v]
    if band:
        in_specs.append(pl.BlockSpec((None, t, t), lambda h, qi, st: (st, 0, 0)))
        ops.append(bias)
    return _pcall(body, name=name, grid=(H, n, nsteps), in_specs=in_specs,
                  out_specs=(pl.BlockSpec((t, dv), lambda h, qi, st: (qi, h)),
                             pl.BlockSpec((None, t, LANES), lambda h, qi, st: (h, qi, 0))),
                  out_shape=(jax.ShapeDtypeStruct((S, H * dv), F32), jax.ShapeDtypeStruct((H, S, LANES), F32)),
                  scratch_shapes=[pltpu.VMEM((t, 1), F32), pltpu.VMEM((t, 1), F32), pltpu.VMEM((t, dv), F32)],
                  compiler_params=_attn_params())(*ops)


def _flash_dq(name, q, k, v, do, o, lse, H, G, dk, dv, scale, bias=None, W=None):
    S = q.shape[0]
    band = bias is not None
    t = bias.shape[1] if band else _pick(S, 512, LANES)
    n = S // t
    nsteps = 2 * W + 1 if band else n

    def kblock(qi, st):
        return jnp.clip(qi - W + st, 0, n - 1) if band else st

    def body(*refs):
        q_ref, k_ref, v_ref, do_ref, o_ref, lse_ref = refs[:6]
        bias_ref = refs[6] if band else None
        dq_ref, delta_sc, acc_sc = refs[-3:]
        qi, st = pl.program_id(1), pl.program_id(2)

        @pl.when(st == 0)
        def _():
            delta_sc[...] = jnp.sum(do_ref[...] * o_ref[...], axis=-1, keepdims=True)
            acc_sc[...] = jnp.zeros_like(acc_sc)

        kj = qi - W + st if band else st

        def step():
            p = jnp.exp(_scores(q_ref, k_ref, scale, bias_ref) - lse_ref[:, 0:1])
            dp = lax.dot_general(do_ref[...].astype(MM_DT), v_ref[...], NT, preferred_element_type=F32)
            ds = p * (dp - delta_sc[...]) * scale
            acc_sc[...] += lax.dot_general(ds.astype(MM_DT), k_ref[...], NN, preferred_element_type=F32)

        if band:
            pl.when((kj >= 0) & (kj < n))(step)
        else:
            step()

        @pl.when(st == nsteps - 1)
        def _():
            dq_ref[...] = acc_sc[...]

    qspec = lambda wd: pl.BlockSpec((t, wd), lambda h, qi, st: (qi, h))
    in_specs = [qspec(dk),
                pl.BlockSpec((t, dk), lambda h, qi, st: (kblock(qi, st), h // G)),
                pl.BlockSpec((t, dv), lambda h, qi, st: (kblock(qi, st), h // G)),
                qspec(dv), qspec(dv),
                pl.BlockSpec((None, t, LANES), lambda h, qi, st: (h, qi, 0))]
    ops = [q, k, v, do, o, lse]
    if band:
        in_specs.append(pl.BlockSpec((None, t, t), lambda h, qi, st: (st, 0, 0)))
        ops.append(bias)
    return _pcall(body, name=name, grid=(H, n, nsteps), in_specs=in_specs,
                  out_specs=qspec(dk), out_shape=jax.ShapeDtypeStruct((S, H * dk), F32),
                  scratch_shapes=[pltpu.VMEM((t, 1), F32), pltpu.VMEM((t, dk), F32)],
                  compiler_params=_attn_params())(*ops)


def _flash_dkv(name, q, k, v, do, o, lse, H, G, dk, dv, scale, bias=None, W=None):
    S = q.shape[0]
    band = bias is not None
    t = bias.shape[1] if band else _pick(S, 512, LANES)
    n = S // t
    nq = 2 * W + 1 if band else n
    nsteps = G * nq
    Hkv = H // G

    def qhead(hk, st):
        return hk * G + st // nq

    def qblock(kj, st):
        return jnp.clip(kj - W + st % nq, 0, n - 1) if band else st % nq

    def body(*refs):
        q_ref, k_ref, v_ref, do_ref, o_ref, lse_ref = refs[:6]
        bias_ref = refs[6] if band else None
        dk_ref, dv_ref, dk_sc, dv_sc = refs[-4:]
        kj, st = pl.program_id(1), pl.program_id(2)

        @pl.when(st == 0)
        def _():
            dk_sc[...] = jnp.zeros_like(dk_sc)
            dv_sc[...] = jnp.zeros_like(dv_sc)

        qi = kj - W + st % nq if band else st % nq

        def step():
            p = jnp.exp(_scores(q_ref, k_ref, scale, bias_ref) - lse_ref[:, 0:1])
            dof = do_ref[...]
            dob = dof.astype(MM_DT)
            dv_sc[...] += lax.dot_general(p.astype(MM_DT), dob, TN, preferred_element_type=F32)
            dp = lax.dot_general(dob, v_ref[...], NT, preferred_element_type=F32)
            delta = jnp.sum(dof * o_ref[...], axis=-1, keepdims=True)
            ds = p * (dp - delta) * scale
            dk_sc[...] += lax.dot_general(ds.astype(MM_DT), q_ref[...], TN, preferred_element_type=F32)

        if band:
            pl.when((qi >= 0) & (qi < n))(step)
        else:
            step()

        @pl.when(st == nsteps - 1)
        def _():
            dk_ref[...] = dk_sc[...]
            dv_ref[...] = dv_sc[...]

    qspec = lambda wd: pl.BlockSpec((t, wd), lambda hk, kj, st: (qblock(kj, st), qhead(hk, st)))
    kspec = lambda wd: pl.BlockSpec((t, wd), lambda hk, kj, st: (kj, hk))
    in_specs = [qspec(dk), kspec(dk), kspec(dv), qspec(dv), qspec(dv),
                pl.BlockSpec((None, t, LANES), lambda hk, kj, st: (qhead(hk, st), qblock(kj, st), 0))]
    ops = [q, k, v, do, o, lse]
    if band:
        in_specs.append(pl.BlockSpec((None, t, t), lambda hk, kj, st: (2 * W - st % nq, 0, 0)))
        ops.append(bias)
    return _pcall(body, name=name, grid=(Hkv, n, nsteps), in_specs=in_specs,
                  out_specs=(kspec(dk), kspec(dv)),
                  out_shape=(jax.ShapeDtypeStruct((S, Hkv * dk), F32), jax.ShapeDtypeStruct((S, Hkv * dv), F32)),
                  scratch_shapes=[pltpu.VMEM((t, dk), F32), pltpu.VMEM((t, dv), F32)],
                  compiler_params=_attn_params())(*ops)


def _rowtile(rows, cols):
    return _pick(rows, max(16, (512 * 1024) // cols // 16 * 16), 16)


def _cast_rows(name, w, dtype):
    R, C = w.shape
    tr = _rowtile(R, C)

    def body(w_ref, o_ref):
        o_ref[...] = w_ref[...].astype(o_ref.dtype)

    spec = pl.BlockSpec((tr, C), lambda i: (i, 0))
    return _pcall(body, name=name, grid=(R // tr,), in_specs=[spec], out_specs=spec,
                  out_shape=jax.ShapeDtypeStruct((R, C), dtype), compiler_params=_row_params())(w)


def _add_sibling(name, gw, ra, c_arr, hd):
    ns, depth, Ks, Ns = gw.shape
    rows = hd * Ks
    tr = _rowtile(rows, Ns)
    gw_v = gw.reshape(ns, 2, rows, Ns)
    ra_v = ra.reshape(ns, rows, Ns)

    def body(c_ref, g_ref, r_ref, o_ref):
        o_ref[...] = (g_ref[...] + r_ref[...]).astype(o_ref.dtype)

    grid_spec = pltpu.PrefetchScalarGridSpec(
        num_scalar_prefetch=1, grid=(ns, rows // tr),
        in_specs=[pl.BlockSpec((None, None, tr, Ns), lambda s, r, c_ref: (s, c_ref[0], r, 0)),
                  pl.BlockSpec((None, tr, Ns), lambda s, r, c_ref: (s, r, 0))],
        out_specs=pl.BlockSpec((None, tr, Ns), lambda s, r, c_ref: (s, r, 0)))
    return _pcall(body, name=name, grid_spec=grid_spec, out_shape=jax.ShapeDtypeStruct((ns, rows, Ns), MM_DT),
                  compiler_params=_grid2_params())(c_arr, gw_v, ra_v)


def _add_chips(name, p, rb, me_arr):
    ns, rows, Ns = p.shape
    tr = _rowtile(rows, Ns)

    def body(me_ref, p_ref, b0_ref, b1_ref, b2_ref, o_ref):
        o_ref[...] = ((p_ref[...].astype(F32) + b0_ref[...].astype(F32)) + b1_ref[...].astype(F32)) + b2_ref[...].astype(F32)

    grid_spec = pltpu.PrefetchScalarGridSpec(
        num_scalar_prefetch=1, grid=(rows // tr,),
        in_specs=[pl.BlockSpec((None, tr, Ns), lambda r, me_ref: (me_ref[0], r, 0))] +
                 [pl.BlockSpec((None, tr, Ns), functools.partial(lambda r, me_ref, j: (j, r, 0), j=j)) for j in range(3)],
        out_specs=pl.BlockSpec((None, tr, Ns), lambda r, me_ref: (0, r, 0)))
    return _pcall(body, name=name, grid_spec=grid_spec, out_shape=jax.ShapeDtypeStruct((2, rows, Ns), F32),
                  compiler_params=_row_params())(me_arr, p, rb, rb, rb)


def _adamw_math(wv, gv, mv, vv):
    bc1 = 1.0 - ADAM_B1 ** ADAM_STEP
    bc2 = 1.0 - ADAM_B2 ** ADAM_STEP
    mn = ADAM_B1 * mv + (1.0 - ADAM_B1) * gv
    vn = ADAM_B2 * vv + (1.0 - ADAM_B2) * jnp.square(gv)
    m_hat = mn / bc1
    v_hat = vn / bc2
    return -ADAM_LR * (m_hat / (jnp.sqrt(v_hat) + ADAM_EPS) + ADAM_WD * wv), mn, vn


def _adamw_halves(name, w, g2, m, v, c_arr):
    _, R, C = w.shape
    tr = _rowtile(R, C)

    def body(c_ref, w_ref, g_ref, m_ref, v_ref, go_ref, d_ref, nm_ref, nv_ref):
        gv = g_ref[...]
        go_ref[...] = gv
        d_ref[...], nm_ref[...], nv_ref[...] = _adamw_math(w_ref[...], gv, m_ref[...], v_ref[...])

    spec = pl.BlockSpec((None, tr, C), lambda h, r, c_ref: (h, r, 0))
    gspec = pl.BlockSpec((None, tr, C), lambda h, r, c_ref: ((h + c_ref[0]) % 2, r, 0))
    sds = jax.ShapeDtypeStruct(w.shape, F32)
    grid_spec = pltpu.PrefetchScalarGridSpec(num_scalar_prefetch=1, grid=(2, R // tr), in_specs=[spec, gspec, spec, spec],
                                             out_specs=(spec,) * 4)
    return _pcall(body, name=name, grid_spec=grid_spec, out_shape=(sds,) * 4, compiler_params=_grid2_params())(c_arr, w, g2, m, v)


def _adamw(name, w, g, m, v):
    R, C = w.shape
    tr = _rowtile(R, C)

    def body(w_ref, g_ref, m_ref, v_ref, d_ref, nm_ref, nv_ref):
        d_ref[...], nm_ref[...], nv_ref[...] = _adamw_math(w_ref[...], g_ref[...], m_ref[...], v_ref[...])

    spec = pl.BlockSpec((tr, C), lambda i: (i, 0))
    sds = jax.ShapeDtypeStruct((R, C), F32)
    return _pcall(body, name=name, grid=(R // tr,), in_specs=[spec] * 4, out_specs=(spec,) * 3,
                  out_shape=(sds, sds, sds), compiler_params=_row_params())(w, g, m, v)


HBM_SPEC = pl.BlockSpec(memory_space=pltpu.HBM)


def _place():
    x, y, c = lax.axis_index("x"), lax.axis_index("y"), lax.axis_index("c")
    chips = [(1 - x, y), (x, 1 - y), (1 - x, 1 - y)]
    return x, y, c, chips


def _allgather_weights(shards, hd):
    n = len(shards)

    def body(*refs):
        ins, outs = refs[:n], refs[n:2 * n]
        send, recv = refs[2 * n:]
        x, y, c, chips = _place()
        me = 2 * x + y
        sib = (x, y, 1 - c)

        def rcopy(src, dst, k, to):
            return pltpu.make_async_remote_copy(src_ref=src, dst_ref=dst, send_sem=send.at[k], recv_sem=recv.at[k],
                                                device_id=to, device_id_type=MESH)

        sends = []
        for t in range(n):
            for j, chip in enumerate(chips):
                cp = rcopy(ins[t].at[pl.ds(c * hd, hd)], outs[t].at[me, pl.ds(c * hd, hd)], 7 * t + j, (chip[0], chip[1], c))
                cp.start()
                sends.append(cp)
        for t in range(n):
            cp = rcopy(ins[t], outs[t].at[me], 7 * t + 6, sib)
            cp.start()
            sends.append(cp)
        for t in range(n):
            for j, chip in enumerate(chips):
                slab = outs[t].at[2 * chip[0] + chip[1], pl.ds(c * hd, hd)]
                rcopy(slab, slab, 7 * t + j, (chip[0], chip[1], c)).wait_recv()
                fw = rcopy(slab, slab, 7 * t + 3 + j, sib)
                fw.start()
                sends.append(fw)
        for t in range(n):
            rcopy(ins[t], outs[t].at[me], 7 * t + 6, sib).wait_recv()
            for j, chip in enumerate(chips):
                slab = outs[t].at[2 * chip[0] + chip[1], pl.ds((1 - c) * hd, hd)]
                rcopy(slab, slab, 7 * t + 3 + j, sib).wait_recv()
        for cp in sends:
            cp.wait_send()

    return _pcall(body, name="allgather_weights", in_specs=[HBM_SPEC] * n, out_specs=tuple([HBM_SPEC] * n),
                  out_shape=tuple(jax.ShapeDtypeStruct((N_SHARD,) + s.shape, s.dtype) for s in shards),
                  scratch_shapes=[pltpu.SemaphoreType.DMA((7 * n,)), pltpu.SemaphoreType.DMA((7 * n,))])(*shards)


def _exchange_sibling_halves(gws, hd):
    n = len(gws)

    def body(*refs):
        ins, outs = refs[:n], refs[n:2 * n]
        send, recv = refs[2 * n:]
        x, y, c, _ = _place()
        cps = []
        for t in range(n):
            cp = pltpu.make_async_remote_copy(src_ref=ins[t].at[:, pl.ds((1 - c) * hd, hd)], dst_ref=outs[t],
                                              send_sem=send.at[t], recv_sem=recv.at[t],
                                              device_id=(x, y, 1 - c), device_id_type=MESH)
            cp.start()
            cps.append(cp)
        for cp in cps:
            cp.wait()

    return _pcall(body, name="rs_sibling_halves", in_specs=[HBM_SPEC] * n, out_specs=tuple([HBM_SPEC] * n),
                  out_shape=tuple(jax.ShapeDtypeStruct((g.shape[0], hd) + g.shape[2:], g.dtype) for g in gws),
                  scratch_shapes=[pltpu.SemaphoreType.DMA((n,)), pltpu.SemaphoreType.DMA((n,))])(*gws)


def _exchange_chips(ps):
    n = len(ps)

    def body(*refs):
        ins, outs = refs[:n], refs[n:2 * n]
        send, recv = refs[2 * n:]
        x, y, c, chips = _place()
        cps = []
        for t in range(n):
            for j, chip in enumerate(chips):
                cp = pltpu.make_async_remote_copy(src_ref=ins[t].at[2 * chip[0] + chip[1]], dst_ref=outs[t].at[j],
                                                  send_sem=send.at[3 * t + j], recv_sem=recv.at[3 * t + j],
                                                  device_id=(chip[0], chip[1], c), device_id_type=MESH)
                cp.start()
                cps.append(cp)
        for cp in cps:
            cp.wait()

    return _pcall(body, name="rs_chip_exchange", in_specs=[HBM_SPEC] * n, out_specs=tuple([HBM_SPEC] * n),
                  out_shape=tuple(jax.ShapeDtypeStruct((3,) + p.shape[1:], p.dtype) for p in ps),
                  scratch_shapes=[pltpu.SemaphoreType.DMA((3 * n,)), pltpu.SemaphoreType.DMA((3 * n,))])(*ps)


def _share_reduced(gs):
    n = len(gs)

    def body(*refs):
        ins, outs = refs[:n], refs[n:2 * n]
        send, recv = refs[2 * n:]
        x, y, c, _ = _place()
        cps = []
        for t in range(n):
            cp = pltpu.make_async_remote_copy(src_ref=ins[t].at[0], dst_ref=outs[t].at[1], send_sem=send.at[t], recv_sem=recv.at[t],
                                              device_id=(x, y, 1 - c), device_id_type=MESH)
            cp.start()
            cps.append(cp)
        for cp in cps:
            cp.wait()

    return _pcall(body, name="rs_share_reduced", in_specs=[HBM_SPEC] * n, out_specs=tuple([HBM_SPEC] * n),
                  out_shape=tuple(jax.ShapeDtypeStruct(g.shape, g.dtype) for g in gs),
                  input_output_aliases={t: t for t in range(n)},
                  scratch_shapes=[pltpu.SemaphoreType.DMA((n,)), pltpu.SemaphoreType.DMA((n,))])(*gs)


def _allreduce_small(vec):
    R = vec.shape[0]

    def body(v_ref, o_ref, buf, send, recv):
        x, y, c, _ = _place()
        me = 4 * x + 2 * y + c
        buf[me] = v_ref[...]
        cps = []
        for r in range(1, 8):
            fx, fy, fc = (r >> 2) & 1, (r >> 1) & 1, r & 1
            to = (1 - x if fx else x, 1 - y if fy else y, 1 - c if fc else c)
            cp = pltpu.make_async_remote_copy(src_ref=v_ref, dst_ref=buf.at[me], send_sem=send.at[r - 1], recv_sem=recv.at[r - 1],
                                              device_id=to, device_id_type=MESH)
            cp.start()
            cps.append(cp)
        for r in range(1, 8):
            fx, fy, fc = (r >> 2) & 1, (r >> 1) & 1, r & 1
            frm = (1 - x if fx else x, 1 - y if fy else y, 1 - c if fc else c)
            src = 4 * frm[0] + 2 * frm[1] + frm[2]
            pltpu.make_async_remote_copy(src_ref=v_ref, dst_ref=buf.at[src], send_sem=send.at[r - 1], recv_sem=recv.at[r - 1],
                                         device_id=frm, device_id_type=MESH).wait_recv()
        for cp in cps:
            cp.wait_send()
        acc = buf[0]
        for i in range(1, 8):
            acc = acc + buf[i]
        o_ref[...] = acc

    vm = pl.BlockSpec(memory_space=pltpu.VMEM)
    return _pcall(body, name="allreduce_small", in_specs=[vm], out_specs=vm, out_shape=jax.ShapeDtypeStruct((R, LANES), F32),
                  scratch_shapes=[pltpu.VMEM((8, R, LANES), F32), pltpu.SemaphoreType.DMA((7,)), pltpu.SemaphoreType.DMA((7,))])(vec)


def _unshard_cols(wg):
    ns, depth, K, Ns = wg.shape
    return jnp.moveaxis(wg, 0, 2).reshape(depth, K, ns * Ns)


def _shard_cols(w):
    K, N = w.shape
    return jnp.moveaxis(w.reshape(K, N_SHARD, N // N_SHARD), 1, 0)


def _uq_padded(cfg, wuq_g):
    w = _unshard_cols(wuq_g).reshape(cfg.DEPTH, cfg.QL, cfg.AH, 192)
    w = jnp.pad(w, ((0, 0), (0, cfg.QLP - cfg.QL), (0, 0), (0, 64)))
    return w.reshape(1, cfg.DEPTH, cfg.QLP, cfg.AH * 256)


def _uq_grad_unpadded(cfg, dw):
    w = dw[:cfg.QL].reshape(cfg.QL, cfg.AH, 256)[:, :, :192].reshape(cfg.QL, cfg.UQ)
    return _shard_cols(w)


def _ukv_padded(cfg, wukv_g):
    w = _unshard_cols(wukv_g)
    w = jnp.pad(w, ((0, 0), (cfg.KOFF, cfg.KVW - cfg.KVL - cfg.KOFF), (0, 0)))
    return w[None]


def _ukv_grad_unpadded(cfg, dw):
    return _shard_cols(dw[cfg.KOFF:cfg.KOFF + cfg.KVL])


def _pad_lanes(v, lo, total):
    return jnp.pad(v, (lo, total - lo - v.shape[0]))[None]


def _layer_fwd(cfg, l, x, W, small, tabs):
    ln1, gq, gkv, gqn, gkn, gout, ln2 = small
    sc_a, sc_h = 1.0 / math.sqrt(192), 1.0 / math.sqrt(128)
    n = f"l{l}_"
    h = _rms_fwd(n + "ln1", x, ln1)
    proj = _mm_nn(n + "proj", h, (W["w_in"], "col", l))
    cqn, ckvn, kpe = _mla_prep_fwd(n + "mla_prep", cfg, proj, gq, gkv, tabs["akr"])
    qa = _mm_nn(n + "uq", cqn, (W["uq_p"], "col", l))
    kva = _mm_nn(n + "ukv", ckvn, (W["ukv_p"], "col", l))
    q_a, k_a, v_a = _mla_build_fwd(n + "mla_build", cfg, qa, kva, kpe, tabs["aq"])
    o_a, lse_a = _flash_fwd(n + "attn_a", q_a, k_a, v_a, cfg.AH, 1, 256, 128, sc_a)
    q_b = _headprep_fwd(n + "bq", proj, cfg.o_bq, cfg.BH, gqn, tabs["b"])
    k_b = _headprep_fwd(n + "bk", proj, cfg.o_bk, cfg.BKV, gkn, tabs["b"])
    v_b = _headprep_fwd(n + "bv", proj, cfg.o_bv, cfg.BKV, None, None)
    o_b, lse_b = _flash_fwd(n + "attn_b", q_b, k_b, v_b, cfg.BH, cfg.G, 128, 128, sc_h)
    q_c = _headprep_fwd(n + "cq", proj, cfg.o_cq, cfg.CH, None, tabs["c"])
    k_c = _headprep_fwd(n + "ck", proj, cfg.o_ck, cfg.CH, None, tabs["c"])
    v_c = _headprep_fwd(n + "cv", proj, cfg.o_cv, cfg.CH, None, None)
    o_c, lse_c = _flash_fwd(n + "attn_c", q_c, k_c, v_c, cfg.CH, 1, 128, 128, sc_h, tabs["bias_c"], cfg.W)
    mixed = _outnorm_fwd(n + "outnorm", cfg, o_a, o_b, o_c, gout)
    x1 = _mm_nn(n + "out", mixed, (W["w_out"], "row", l), epi=_epi_residual, extra=x)
    h2 = _rms_fwd(n + "ln2", x1, ln2)
    a, u = _mm_nn(n + "ff1", h2, (W["w_ff1"], "col", l), epi=_epi_relu2, out_dtypes=(MM_DT, MM_DT))
    x2 = _mm_nn(n + "ff2", u, (W["w_ff2"], "row", l), epi=_epi_residual, extra=x1)
    saved = dict(x=x, h=h, proj=proj, cqn=cqn, ckvn=ckvn, q_a=q_a, k_a=k_a, v_a=v_a, o_a=o_a, lse_a=lse_a,
                 q_b=q_b, k_b=k_b, v_b=v_b, o_b=o_b, lse_b=lse_b, q_c=q_c, k_c=k_c, v_c=v_c, o_c=o_c, lse_c=lse_c,
                 mixed=mixed, x1=x1, h2=h2, a=a, u=u)
    return x2, saved


def _layer_bwd(cfg, l, dx2, sv, W, small, tabs, GW):
    ln1, gq, gkv, gqn, gkn, gout, ln2 = small
    sc_a, sc_h = 1.0 / math.sqrt(192), 1.0 / math.sqrt(128)
    n = f"l{l}_b_"
    S = cfg.S
    mats = {m[0]: m for m in cfg.mats}

    def dw(name, a, g, key):
        _, Rs, Cs, kind = mats[key]
        GW[key] = _mm_tn(n + name, a, g, kind, l, Rs, Cs, GW[key])

    da = _mm_nt(n + "ff2_dx", dx2, (W["w_ff2"], "row", l), epi=_epi_drelu2, out_dtype=MM_DT, extra=sv["a"])
    dw("ff2_dw", sv["u"], dx2, "w_ff2")
    dh2 = _mm_nt(n + "ff1_dx", da, (W["w_ff1"], "col", l))
    dw("ff1_dw", sv["h2"], da, "w_ff1")
    dx1, dln2 = _rms_bwd(n + "ln2", sv["x1"], ln2, dh2, dx2)
    dmix = _mm_nt(n + "out_dx", dx1, (W["w_out"], "row", l))
    dw("out_dw", sv["mixed"], dx1, "w_out")
    do_a, do_b, do_c, dgout = _outnorm_bwd(n + "outnorm", cfg, dmix, sv["o_a"], sv["o_b"], sv["o_c"], gout)
    dproj = jnp.zeros((S, cfg.IN), MM_DT)
    args_c = (sv["q_c"], sv["k_c"], sv["v_c"], do_c, sv["o_c"], sv["lse_c"], cfg.CH, 1, 128, 128, sc_h, tabs["bias_c"], cfg.W)
    dq_c = _flash_dq(n + "attn_c_dq", *args_c)
    dk_c, dv_c = _flash_dkv(n + "attn_c_dkv", *args_c)
    dproj, _ = _headprep_bwd(n + "cq", dq_c, sv["proj"], cfg.o_cq, cfg.CH, None, tabs["c"], dproj)
    dproj, _ = _headprep_bwd(n + "ck", dk_c, sv["proj"], cfg.o_ck, cfg.CH, None, tabs["c"], dproj)
    dproj, _ = _headprep_bwd(n + "cv", dv_c, sv["proj"], cfg.o_cv, cfg.CH, None, None, dproj)
    args_b = (sv["q_b"], sv["k_b"], sv["v_b"], do_b, sv["o_b"], sv["lse_b"], cfg.BH, cfg.G, 128, 128, sc_h)
    dq_b = _flash_dq(n + "attn_b_dq", *args_b)
    dk_b, dv_b = _flash_dkv(n + "attn_b_dkv", *args_b)
    dproj, dgqn = _headprep_bwd(n + "bq", dq_b, sv["proj"], cfg.o_bq, cfg.BH, gqn, tabs["b"], dproj)
    dproj, dgkn = _headprep_bwd(n + "bk", dk_b, sv["proj"], cfg.o_bk, cfg.BKV, gkn, tabs["b"], dproj)
    dproj, _ = _headprep_bwd(n + "bv", dv_b, sv["proj"], cfg.o_bv, cfg.BKV, None, None, dproj)
    args_a = (sv["q_a"], sv["k_a"], sv["v_a"], do_a, sv["o_a"], sv["lse_a"], cfg.AH, 1, 256, 128, sc_a)
    dq_a = _flash_dq(n + "attn_a_dq", *args_a)
    dk_a, dv_a = _flash_dkv(n + "attn_a_dkv", *args_a)
    dqa, dkva, dkpe = _mla_build_bwd(n + "mla_build", cfg, dq_a, dk_a, dv_a, tabs["aq"])
    dcq = _mm_nt(n + "uq_dx", dqa, (W["uq_p"], "col", l))
    dwuq = _mm_tn(n + "uq_dw", sv["cqn"], dqa, "col", 0, cfg.QLP, cfg.AH * 256, jnp.zeros((1, 1, cfg.QLP, cfg.AH * 256), F32))
    dckv = _mm_nt(n + "ukv_dx", dkva, (W["ukv_p"], "col", l))
    dwukv = _mm_tn(n + "ukv_dw", sv["ckvn"], dkva, "col", 0, cfg.KVW, cfg.AH * 256, jnp.zeros((1, 1, cfg.KVW, cfg.AH * 256), F32))
    dproj, dgq, dgkv = _mla_prep_bwd(n + "mla_prep", cfg, dcq, dckv, dkpe, sv["proj"], gq, gkv, tabs["akr"], dproj)
    dh = _mm_nt(n + "proj_dx", dproj, (W["w_in"], "col", l))
    dw("proj_dw", sv["h"], dproj, "w_in")
    dx, dln1 = _rms_bwd(n + "ln1", sv["x"], ln1, dh, dx1)
    gains = dict(ln1_g=dln1[0], g_q_a=dgq[0, :cfg.QL], g_kv_a=dgkv[0, cfg.KOFF:cfg.KOFF + cfg.KVL], g_qn_b=dgqn[0],
                 g_kn_b=dgkn[0], g_out=dgout[0], ln2_g=dln2[0])
    return dx, gains, _uq_grad_unpadded(cfg, dwuq[0, 0]), _ukv_grad_unpadded(cfg, dwukv[0, 0])


SMALL_NAMES = ("ln1_g", "g_q_a", "g_kv_a", "g_qn_b", "g_kn_b", "g_out", "ln2_g")
MAT_NAMES = ("w_in", "w_uq", "w_ukv", "w_out", "w_ff1", "w_ff2")


def _pack_small(cfg, per_layer, final):
    flat = jnp.concatenate([per_layer[k].reshape(-1) for k in SMALL_NAMES] + [final.reshape(-1)])
    total = flat.shape[0]
    rows = _rup(-(-total // LANES), 8)
    return jnp.pad(flat, (0, rows * LANES - total)).reshape(rows, LANES)


def _unpack_small(cfg, packed, shapes):
    flat = packed.reshape(-1)
    out, off = {}, 0
    for k in SMALL_NAMES + ("ln_f_g",):
        n = math.prod(shapes[k])
        out[k] = flat[off:off + n].reshape(shapes[k])
        off += n
    return out


def _step(cfg, w, m, v, x, tgt):
    DEPTH, hd = cfg.DEPTH, cfg.HD
    c = lax.axis_index("c")
    me_chip = 2 * lax.axis_index("x") + lax.axis_index("y")
    c_arr = jnp.reshape(c, (1,)).astype(jnp.int32)
    me_arr = jnp.reshape(me_chip, (1,)).astype(jnp.int32)
    mats = {mt[0]: mt for mt in cfg.mats}

    shards = []
    for name in MAT_NAMES:
        _, Ks, Ns, _ = mats[name]
        shards.append(_cast_rows("cast_" + name, w[name].reshape(DEPTH * Ks, Ns), MM_DT).reshape(DEPTH, Ks, Ns))
    gathered = dict(zip(MAT_NAMES, _allgather_weights(shards, hd)))
    W = dict(w_in=gathered["w_in"], w_out=gathered["w_out"], w_ff1=gathered["w_ff1"], w_ff2=gathered["w_ff2"],
             uq_p=_uq_padded(cfg, gathered["w_uq"]), ukv_p=_ukv_padded(cfg, gathered["w_ukv"]))
    tabs = _all_tables(cfg)
    tabs["bias_c"] = _band_bias(cfg)

    def small_of(l):
        return (w["ln1_g"][l][None], _pad_lanes(w["g_q_a"][l], 0, cfg.QLP), _pad_lanes(w["g_kv_a"][l], cfg.KOFF, cfg.KVW),
                w["g_qn_b"][l][None], w["g_kn_b"][l][None], w["g_out"][l][None], w["ln2_g"][l][None])

    saved = []
    xc = x
    for l in range(DEPTH):
        xc, sv = _layer_fwd(cfg, l, xc, W, small_of(l), tabs)
        saved.append(sv)
    dx, dlnf, loss_rows = _final_loss("final_loss", xc, w["ln_f_g"][None], tgt)
    loss = loss_rows[0, 0]
    GW = {name: jnp.zeros((N_SHARD, DEPTH, mats[name][1], mats[name][2]), F32) for name in ("w_in", "w_out", "w_ff1", "w_ff2")}
    gain_rows = [None] * DEPTH
    duq, dukv = [None] * DEPTH, [None] * DEPTH
    for l in reversed(range(DEPTH)):
        dx, gain_rows[l], duq[l], dukv[l] = _layer_bwd(cfg, l, dx, saved[l], W, small_of(l), tabs, GW)
    GW["w_uq"] = jnp.stack(duq, axis=1)
    GW["w_ukv"] = jnp.stack(dukv, axis=1)

    gws = [GW[name] for name in MAT_NAMES]
    ras = _exchange_sibling_halves(gws, hd)
    ps = [_add_sibling("rs_add_sib_" + name, g, r, c_arr, hd) for name, g, r in zip(MAT_NAMES, gws, ras)]
    rbs = _exchange_chips(ps)
    rs = [_add_chips("rs_add_chips_" + name, p, rb, me_arr) for name, p, rb in zip(MAT_NAMES, ps, rbs)]
    full = _share_reduced(rs)
    grad, delta, new_m, new_v = {}, {}, {}, {}
    for name, g2 in zip(MAT_NAMES, full):
        _, Ks, Ns, _ = mats[name]
        halves, shp = (2, hd * Ks, Ns), (DEPTH, Ks, Ns)
        res = _adamw_halves("adamw_" + name, w[name].reshape(halves), g2, m[name].reshape(halves), v[name].reshape(halves), c_arr)
        grad[name], delta[name], new_m[name], new_v[name] = (r.reshape(shp) for r in res)

    per_layer = {k: jnp.stack([gain_rows[l][k] for l in range(DEPTH)]) for k in SMALL_NAMES}
    shapes = {k: w[k].shape for k in SMALL_NAMES + ("ln_f_g",)}
    gsum = _allreduce_small(_pack_small(cfg, per_layer, dlnf[0]))
    pk = lambda d: _pack_small(cfg, {k: d[k] for k in SMALL_NAMES}, d["ln_f_g"])
    d_s, m_s, v_s = _adamw("adamw_small", pk(w), gsum, pk(m), pk(v))
    for res, packed in ((grad, gsum), (delta, d_s), (new_m, m_s), (new_v, v_s)):
        res.update(_unpack_small(cfg, packed, shapes))
    return loss, dx, grad, delta, new_m, new_v


WEIGHT_NAMES = ("ln1_g", "w_in", "g_q_a", "w_uq", "g_kv_a", "w_ukv", "g_qn_b", "g_kn_b", "g_out", "w_out", "ln2_g",
                "w_ff1", "w_ff2", "ln_f_g")


def _run(cfg, args):
    nw = len(WEIGHT_NAMES)
    x, tgt = args[0], args[1 + nw]
    w = dict(zip(WEIGHT_NAMES, args[1:1 + nw]))
    m = dict(zip(WEIGHT_NAMES, args[2 + nw:2 + 2 * nw]))
    v = dict(zip(WEIGHT_NAMES, args[2 + 2 * nw:2 + 3 * nw]))
    loss, dx, grad, delta, new_m, new_v = _step(cfg, w, m, v, x.reshape(cfg.S, cfg.D), tgt.reshape(cfg.S, cfg.D))
    loss = lax.psum(loss, MESH_AXES)
    return (loss, dx.reshape(x.shape), *[grad[k] for k in WEIGHT_NAMES], *[delta[k] for k in WEIGHT_NAMES],
            *[new_m[k] for k in WEIGHT_NAMES], *[new_v[k] for k in WEIGHT_NAMES])


def kernel(x, ln1_g, w_in, g_q_a, w_uq, g_kv_a, w_ukv, g_qn_b, g_kn_b, g_out, w_out, ln2_g, w_ff1, w_ff2, ln_f_g, loss_target, m_ln1_g, m_w_in, m_g_q_a, m_w_uq, m_g_kv_a, m_w_ukv, m_g_qn_b, m_g_kn_b, m_g_out, m_w_out, m_ln2_g, m_w_ff1, m_w_ff2, m_ln_f_g, v_ln1_g, v_w_in, v_g_q_a, v_w_uq, v_g_kv_a, v_w_ukv, v_g_qn_b, v_g_kn_b, v_g_out, v_w_out, v_ln2_g, v_w_ff1, v_w_ff2, v_ln_f_g):
    return _run(Cfg(), (x, ln1_g, w_in, g_q_a, w_uq, g_kv_a, w_ukv, g_qn_b, g_kn_b, g_out, w_out, ln2_g, w_ff1, w_ff2, ln_f_g, loss_target, m_ln1_g, m_w_in, m_g_q_a, m_w_uq, m_g_kv_a, m_w_ukv, m_g_qn_b, m_g_kn_b, m_g_out, m_w_out, m_ln2_g, m_w_ff1, m_w_ff2, m_ln_f_g, v_ln1_g, v_w_in, v_g_q_a, v_w_uq, v_g_kv_a, v_w_ukv, v_g_qn_b, v_g_kn_b, v_g_out, v_w_out, v_ln2_g, v_w_ff1, v_w_ff2, v_ln_f_g))
```

```python
import functools
import math

import jax
import jax.numpy as jnp
from jax import lax
from jax.experimental import pallas as pl
from jax.experimental.pallas import tpu as pltpu

F32 = jnp.float32
MM_DT = jnp.bfloat16
LANES = 128
SUBLANES_F32 = 8
SUBLANES_BF16 = 16
VMEM_LIMIT = 48 * 1024 * 1024
EPS = 1e-6
NEG = -1e30
ROPE_THETA = 10000.0
MM_TK = 1024
ATT_TQ, ATT_TK_FWD, ATT_TK = 512, 4096, 2048
ADAM_LR, ADAM_B1, ADAM_B2, ADAM_EPS, ADAM_WD, ADAM_STEP = 0.001, 0.9, 0.999, 1e-08, 0.01, 10
MESH_AXES = ("x", "y", "c")
N_SHARD = 4
MESH = pl.DeviceIdType.MESH

NN = (((1,), (0,)), ((), ()))
NT = (((1,), (1,)), ((), ()))
TN = (((0,), (0,)), ((), ()))


def _pcall(body, **kw):
    return pl.pallas_call(body, **kw)


def _rup(n, m):
    return -(-n // m) * m


def _pick(n, pref, mult):
    best = None
    for t in range(mult, min(n, pref) + 1, mult):
        if n % t == 0:
            best = t
    return best if best is not None else n


class Cfg:
    def __init__(self, S=4096, D=2048, DEPTH=4, AH=4, QL=448, KVL=512, BH=6, BKV=2, CH=6,
                 BRANCHES=((128, 1), (512, 4), (2048, 16)), DFF=8192, GRID_W=64, TB=512):
        self.S, self.D, self.DEPTH, self.AH, self.QL, self.KVL = S, D, DEPTH, AH, QL, KVL
        self.BH, self.BKV, self.CH, self.DFF, self.GRID_W, self.TB = BH, BKV, CH, DFF, GRID_W, TB
        self.G = BH // BKV
        self.AW, self.BW, self.CW = AH * 128, BH * 128, CH * 128
        self.MIX = self.AW + self.BW + self.CW
        self.QLP = _rup(QL, LANES)
        self.KV0 = (QL // LANES) * LANES
        self.PW = QL + KVL + 64
        assert self.PW % LANES == 0
        self.KVW = self.PW - self.KV0
        self.KOFF = QL - self.KV0
        self.o_bq = self.PW
        self.o_bk = self.o_bq + self.BW
        self.o_bv = self.o_bk + BKV * 128
        self.o_cq = self.o_bv + BKV * 128
        self.o_ck = self.o_cq + self.CW
        self.o_cv = self.o_ck + self.CW
        self.IN = self.o_cv + self.CW
        self.UQ, self.UKV = AH * 192, AH * 256
        self.branches = tuple(((w // (2 * d)) * d, d) for w, d in BRANCHES)
        for _, d in self.branches:
            assert d & (d - 1) == 0
        self.W = -(-max(r for r, _ in self.branches) // TB)
        assert S % TB == 0 and DEPTH % 2 == 0
        self.HD = DEPTH // 2
        self.mats = (("w_in", D, self.IN // 4, "col"), ("w_uq", QL, self.UQ // 4, "col"),
                     ("w_ukv", KVL, self.UKV // 4, "col"), ("w_out", self.MIX // 4, D, "row"),
                     ("w_ff1", D, DFF // 4, "col"), ("w_ff2", DFF // 4, D, "row"))


def _mm_call(name, mode, operands, in_specs, out_shape, out_specs, grid, acc_shape, epi, n_extra, aliases=None):
    nk = grid[2]

    def body(*refs):
        a_ref, b_ref = refs[0], refs[1]
        ex = refs[2:2 + n_extra]
        outs = refs[2 + n_extra:-1]
        acc = refs[-1]
        k = pl.program_id(2)

        @pl.when(k == 0)
        def _():
            acc[...] = jnp.zeros_like(acc)

        acc[...] += lax.dot_general(a_ref[...].astype(MM_DT), b_ref[...].astype(MM_DT), mode,
                                    preferred_element_type=F32)

        @pl.when(k == nk - 1)
        def _():
            epi(acc[...], ex, outs)

    return _pcall(body, name=name, grid=grid, in_specs=in_specs, out_specs=out_specs, out_shape=out_shape,
                  scratch_shapes=[pltpu.VMEM(acc_shape, F32)], input_output_aliases=aliases or {},
                  compiler_params=pltpu.CompilerParams(dimension_semantics=("parallel", "parallel", "arbitrary"),
                                                       vmem_limit_bytes=VMEM_LIMIT))(*operands)


def _wspec(kind, l, Rs, Cs, br, bc, rfn, cfn):
    assert Rs % br == 0 and Cs % bc == 0
    if kind == "col":
        npc = Cs // bc
        return pl.BlockSpec((None, None, br, bc), lambda i, j, k: (cfn(i, j, k) // npc, l, rfn(i, j, k), cfn(i, j, k) % npc))
    npr = Rs // br
    return pl.BlockSpec((None, None, br, bc), lambda i, j, k: (rfn(i, j, k) // npr, l, rfn(i, j, k) % npr, cfn(i, j, k)))


def _epi_plain(acc, ex, outs):
    outs[0][...] = acc.astype(outs[0].dtype)


def _epi_residual(acc, ex, outs):
    outs[0][...] = ex[0][...] + acc


def _epi_relu2(acc, ex, outs):
    outs[0][...] = acc.astype(outs[0].dtype)
    r = jnp.maximum(acc, 0.0)
    outs[1][...] = (r * r).astype(outs[1].dtype)


def _epi_drelu2(acc, ex, outs):
    a = ex[0][...].astype(F32)
    outs[0][...] = (acc * (2.0 * jnp.maximum(a, 0.0))).astype(outs[0].dtype)


def _wdims(wd):
    Wg, kind, l = wd
    ns, _, Rs, Cs = Wg.shape
    K = Rs * ns if kind == "row" else Rs
    N = Cs * ns if kind == "col" else Cs
    return Wg, kind, l, Rs, Cs, K, N


def _mm_nn(name, a, wd, epi=_epi_plain, out_dtypes=(F32,), extra=None):
    Wg, kind, l, Rs, Cs, K, N = _wdims(wd)
    M = a.shape[0]
    tm, tk, tn = _pick(M, 1024, 16), _pick(Rs, MM_TK, LANES), _pick(Cs, 1152, LANES)
    grid = (M // tm, N // tn, K // tk)
    in_specs = [pl.BlockSpec((tm, tk), lambda i, j, k: (i, k)),
                _wspec(kind, l, Rs, Cs, tk, tn, lambda i, j, k: k, lambda i, j, k: j)]
    ops = [a, Wg]
    if extra is not None:
        in_specs.append(pl.BlockSpec((tm, tn), lambda i, j, k: (i, j)))
        ops.append(extra)
    o_spec = pl.BlockSpec((tm, tn), lambda i, j, k: (i, j))
    outs = tuple(jax.ShapeDtypeStruct((M, N), dt) for dt in out_dtypes)
    res = _mm_call(name, NN, ops, in_specs, outs, tuple(o_spec for _ in outs), grid, (tm, tn), epi,
                   0 if extra is None else 1)
    return res[0] if len(res) == 1 else res


def _mm_nt(name, g, wd, epi=_epi_plain, out_dtype=F32, extra=None):
    Wg, kind, l, Rs, Cs, K, N = _wdims(wd)
    M = g.shape[0]
    tm, tn, tk = _pick(M, 1024, 16), _pick(Rs, 1024, LANES), _pick(Cs, 1152, LANES)
    grid = (M // tm, K // tn, N // tk)
    in_specs = [pl.BlockSpec((tm, tk), lambda i, j, k: (i, k)),
                _wspec(kind, l, Rs, Cs, tn, tk, lambda i, j, k: j, lambda i, j, k: k)]
    ops = [g, Wg]
    if extra is not None:
        in_specs.append(pl.BlockSpec((tm, tn), lambda i, j, k: (i, j)))
        ops.append(extra)
    res = _mm_call(name, NT, ops, in_specs, (jax.ShapeDtypeStruct((M, K), out_dtype),),
                   (pl.BlockSpec((tm, tn), lambda i, j, k: (i, j)),), grid, (tm, tn), epi, 0 if extra is None else 1)
    return res[0]


def _mm_tn(name, a, g, kind, l, Rs, Cs, buf):
    M, K = a.shape
    N = g.shape[1]
    tm, tn, tk = _pick(Rs, 1024, LANES), _pick(Cs, 1152, LANES), _pick(M, MM_TK, LANES)
    grid = (K // tm, N // tn, M // tk)
    in_specs = [pl.BlockSpec((tk, tm), lambda i, j, k: (k, i)),
                pl.BlockSpec((tk, tn), lambda i, j, k: (k, j))]
    o_spec = _wspec(kind, l, Rs, Cs, tm, tn, lambda i, j, k: i, lambda i, j, k: j)
    if isinstance(buf, tuple):
        res = _mm_call(name, TN, [a, g], in_specs, (jax.ShapeDtypeStruct(buf, F32),), (o_spec,), grid, (tm, tn), _epi_plain, 0)
    else:
        res = _mm_call(name, TN, [a, g, buf], in_specs + [pl.BlockSpec(memory_space=pl.ANY)],
                       (jax.ShapeDtypeStruct(buf.shape, F32),), (o_spec,), grid, (tm, tn), _epi_plain, 1, aliases={2: 0})
    return res[0]


def _row_params():
    return pltpu.CompilerParams(dimension_semantics=("arbitrary",), vmem_limit_bytes=VMEM_LIMIT)


def _rms_fwd(name, x, g):
    S, D = x.shape
    tr = _pick(S, 256, 16)

    def body(x_ref, g_ref, o_ref):
        xv = x_ref[...]
        r = lax.rsqrt(jnp.mean(xv * xv, axis=-1, keepdims=True) + EPS)
        o_ref[...] = (xv * r * g_ref[...]).astype(o_ref.dtype)

    return _pcall(body, name=name, grid=(S // tr,),
                  in_specs=[pl.BlockSpec((tr, D), lambda i: (i, 0)), pl.BlockSpec((1, D), lambda i: (0, 0))],
                  out_specs=pl.BlockSpec((tr, D), lambda i: (i, 0)), out_shape=jax.ShapeDtypeStruct((S, D), MM_DT),
                  compiler_params=_row_params())(x, g)


def _acc_rows(ref, part, first):
    @pl.when(first)
    def _():
        ref[...] = jnp.zeros_like(ref)

    ref[...] += jnp.broadcast_to(part, ref.shape)


def _rms_bwd(name, x, g, dy, res):
    S, D = x.shape
    tr = _pick(S, 256, 16)

    def body(x_ref, g_ref, dy_ref, res_ref, dx_ref, dg_ref):
        xv = x_ref[...]
        r = lax.rsqrt(jnp.mean(xv * xv, axis=-1, keepdims=True) + EPS)
        xh = xv * r
        dyv = dy_ref[...]
        dn = dyv * g_ref[...]
        dx_ref[...] = res_ref[...] + r * (dn - xh * jnp.mean(dn * xh, axis=-1, keepdims=True))
        _acc_rows(dg_ref, jnp.sum(dyv * xh, axis=0, keepdims=True), pl.program_id(0) == 0)

    row = pl.BlockSpec((tr, D), lambda i: (i, 0))
    return _pcall(body, name=name, grid=(S // tr,),
                  in_specs=[row, pl.BlockSpec((1, D), lambda i: (0, 0)), row, row],
                  out_specs=(row, pl.BlockSpec((8, D), lambda i: (0, 0))),
                  out_shape=(jax.ShapeDtypeStruct((S, D), F32), jax.ShapeDtypeStruct((8, D), F32)),
                  compiler_params=_row_params())(x, g, dy, res)


def _final_loss(name, x, g, tgt):
    S, D = x.shape
    tr = _pick(S, 256, 16)

    def body(x_ref, g_ref, t_ref, dx_ref, dg_ref, loss_ref):
        xv = x_ref[...]
        r = lax.rsqrt(jnp.mean(xv * xv, axis=-1, keepdims=True) + EPS)
        xh = xv * r
        gv = g_ref[...]
        e = xh * gv - t_ref[...]
        part = 0.5 * jnp.sum(jnp.mean(e * e, axis=-1, keepdims=True), axis=0, keepdims=True)
        dy = e * (1.0 / D)
        dn = dy * gv
        dx_ref[...] = r * (dn - xh * jnp.mean(dn * xh, axis=-1, keepdims=True))
        first = pl.program_id(0) == 0
        _acc_rows(dg_ref, jnp.sum(dy * xh, axis=0, keepdims=True), first)
        _acc_rows(loss_ref, part, first)

    row = pl.BlockSpec((tr, D), lambda i: (i, 0))
    return _pcall(body, name=name, grid=(S // tr,),
                  in_specs=[row, pl.BlockSpec((1, D), lambda i: (0, 0)), row],
                  out_specs=(row, pl.BlockSpec((8, D), lambda i: (0, 0)), pl.BlockSpec((8, LANES), lambda i: (0, 0))),
                  out_shape=(jax.ShapeDtypeStruct((S, D), F32), jax.ShapeDtypeStruct((8, D), F32),
                             jax.ShapeDtypeStruct((8, LANES), F32)),
                  compiler_params=_row_params())(x, g, tgt)


def _rope_tables(cos, sin, off, w):
    S = cos.shape[0]
    h = w // 2
    z = lambda n: jnp.zeros((S, n), F32)
    C = jnp.concatenate([z(off), cos, cos, z(LANES - off - w)], axis=1)
    SP = jnp.concatenate([z(off + h), sin, z(LANES - off - w)], axis=1)
    SN = jnp.concatenate([z(off), -sin, z(LANES - off - h)], axis=1)
    return C, SP, SN


def _angles(pos, dim):
    inv = jnp.power(ROPE_THETA, -jnp.arange(0, dim, 2, dtype=F32) / dim)
    ang = pos.astype(F32)[:, None] * inv[None, :]
    return jnp.cos(ang), jnp.sin(ang)


def _all_tables(cfg):
    S = cfg.S
    pos = jnp.arange(S, dtype=F32)
    rows = S // cfg.GRID_W
    row = jnp.repeat(jnp.arange(rows, dtype=F32), cfg.GRID_W)
    col = jnp.tile(jnp.arange(cfg.GRID_W, dtype=F32), rows)
    ca, sa = _angles(pos, 64)
    cc, sc = _angles(pos, 128)
    cr, sr = _angles(row, 64)
    cl, sl = _angles(col, 64)
    t_b = tuple(a + b for a, b in zip(_rope_tables(cr, sr, 0, 64), _rope_tables(cl, sl, 64, 64)))
    return {"aq": (_rope_tables(ca, sa, 0, 64), 64), "akr": (_rope_tables(ca, sa, 64, 64), 64),
            "b": (t_b, 64), "c": (_rope_tables(cc, sc, 0, 128), 128)}


def _rope(x, C, SP, SN, w):
    h = w // 2
    if 2 * h == LANES:
        return x * C + pltpu.roll(x, h, 1) * (SP + SN)
    return x * C + pltpu.roll(x, h, 1) * SP + pltpu.roll(x, LANES - h, 1) * SN


def _rope_t(dy, C, SP, SN, w):
    h = w // 2
    if 2 * h == LANES:
        return dy * C + pltpu.roll(dy * (SP + SN), h, 1)
    return dy * C + pltpu.roll(dy * SP, LANES - h, 1) + pltpu.roll(dy * SN, h, 1)


def _grid2_params():
    return pltpu.CompilerParams(dimension_semantics=("arbitrary", "arbitrary"), vmem_limit_bytes=VMEM_LIMIT)


def _headprep_fwd(name, proj, col_off, nb, gain, tabs):
    S = proj.shape[0]
    tr = _pick(S, 1024, 16)
    cb = col_off // LANES
    norm, rope = gain is not None, tabs is not None
    w = tabs[1] if rope else 0

    def body(*refs):
        x_ref = refs[0]
        pos = 1
        xv = x_ref[...]
        if norm:
            r = lax.rsqrt(jnp.mean(xv * xv, axis=-1, keepdims=True) + EPS)
            xv = xv * r * refs[pos][...]
            pos += 1
        if rope:
            xv = _rope(xv, refs[pos][...], refs[pos + 1][...], refs[pos + 2][...], w)
            pos += 3
        refs[pos][...] = xv.astype(refs[pos].dtype)

    ops, in_specs = [proj], [pl.BlockSpec((tr, LANES), lambda i, j: (i, cb + j))]
    if norm:
        ops.append(gain)
        in_specs.append(pl.BlockSpec((1, LANES), lambda i, j: (0, 0)))
    if rope:
        ops += list(tabs[0])
        in_specs += [pl.BlockSpec((tr, LANES), lambda i, j: (i, 0))] * 3
    return _pcall(body, name=name, grid=(S // tr, nb), in_specs=in_specs,
                  out_specs=pl.BlockSpec((tr, LANES), lambda i, j: (i, j)),
                  out_shape=jax.ShapeDtypeStruct((S, nb * LANES), MM_DT), compiler_params=_grid2_params())(*ops)


def _headprep_bwd(name, dy, proj, col_off, nb, gain, tabs, dproj):
    S = proj.shape[0]
    tr = _pick(S, 1024, 16)
    cb = col_off // LANES
    norm, rope = gain is not None, tabs is not None
    w = tabs[1] if rope else 0

    def body(*refs):
        dz = refs[0][...]
        pos = 1
        if norm:
            x_ref, g_ref = refs[pos], refs[pos + 1]
            pos += 2
        if rope:
            dz = _rope_t(dz, refs[pos][...], refs[pos + 1][...], refs[pos + 2][...], w)
            pos += 3
        pos += 1
        o_ref = refs[pos]
        if norm:
            dg_ref = refs[pos + 1]
            xv = x_ref[...]
            r = lax.rsqrt(jnp.mean(xv * xv, axis=-1, keepdims=True) + EPS)
            n = xv * r
            first = (pl.program_id(0) == 0) & (pl.program_id(1) == 0)
            _acc_rows(dg_ref, jnp.sum(dz * n, axis=0, keepdims=True), first)
            dn = dz * g_ref[...]
            dz = r * (dn - n * jnp.mean(dn * n, axis=-1, keepdims=True))
        o_ref[...] = dz.astype(o_ref.dtype)

    ops, in_specs = [dy], [pl.BlockSpec((tr, LANES), lambda i, j: (i, j))]
    if norm:
        ops += [proj, gain]
        in_specs += [pl.BlockSpec((tr, LANES), lambda i, j: (i, cb + j)), pl.BlockSpec((1, LANES), lambda i, j: (0, 0))]
    if rope:
        ops += list(tabs[0])
        in_specs += [pl.BlockSpec((tr, LANES), lambda i, j: (i, 0))] * 3
    alias_idx = len(ops)
    ops.append(dproj)
    in_specs.append(pl.BlockSpec(memory_space=pl.ANY))
    out_specs = [pl.BlockSpec((tr, LANES), lambda i, j: (i, cb + j))]
    out_shape = [jax.ShapeDtypeStruct(dproj.shape, dproj.dtype)]
    if norm:
        out_specs.append(pl.BlockSpec((8, LANES), lambda i, j: (0, 0)))
        out_shape.append(jax.ShapeDtypeStruct((8, LANES), F32))
    res = _pcall(body, name=name, grid=(S // tr, nb), in_specs=in_specs, out_specs=tuple(out_specs),
                 out_shape=tuple(out_shape), input_output_aliases={alias_idx: 0}, compiler_params=_grid2_params())(*ops)
    return (res[0], res[1]) if norm else (res[0], None)


def _masked_rms(xv, lo, n):
    lane = lax.broadcasted_iota(jnp.int32, xv.shape, 1)
    xm = jnp.where((lane >= lo) & (lane < lo + n), xv, 0.0)
    r = lax.rsqrt(jnp.sum(xm * xm, axis=-1, keepdims=True) * (1.0 / n) + EPS)
    return xm * r, r


def _mla_prep_fwd(name, cfg, proj, gq, gkv, tabs):
    S = cfg.S
    tr = _pick(S, 256, 16)
    (C, SP, SN), w = tabs

    def body(p_ref, gq_ref, gkv_ref, c_ref, sp_ref, sn_ref, cq_ref, ckv_ref, kpe_ref):
        nq, _ = _masked_rms(p_ref[:, 0:cfg.QLP], 0, cfg.QL)
        cq_ref[...] = (nq * gq_ref[...]).astype(cq_ref.dtype)
        nk, _ = _masked_rms(p_ref[:, cfg.KV0:cfg.PW], cfg.KOFF, cfg.KVL)
        ckv_ref[...] = (nk * gkv_ref[...]).astype(ckv_ref.dtype)
        kr = _rope(p_ref[:, cfg.PW - LANES:cfg.PW], c_ref[...], sp_ref[...], sn_ref[...], w)
        kpe_ref[...] = pltpu.roll(kr, 64, 1).astype(kpe_ref.dtype)

    tab = pl.BlockSpec((tr, LANES), lambda i: (i, 0))
    return _pcall(body, name=name, grid=(S // tr,),
                  in_specs=[pl.BlockSpec((tr, cfg.PW), lambda i: (i, 0)), pl.BlockSpec((1, cfg.QLP), lambda i: (0, 0)),
                            pl.BlockSpec((1, cfg.KVW), lambda i: (0, 0)), tab, tab, tab],
                  out_specs=(pl.BlockSpec((tr, cfg.QLP), lambda i: (i, 0)), pl.BlockSpec((tr, cfg.KVW), lambda i: (i, 0)), tab),
                  out_shape=(jax.ShapeDtypeStruct((S, cfg.QLP), MM_DT), jax.ShapeDtypeStruct((S, cfg.KVW), MM_DT),
                             jax.ShapeDtypeStruct((S, LANES), MM_DT)),
                  compiler_params=_row_params())(proj, gq, gkv, C, SP, SN)


def _mla_prep_bwd(name, cfg, dcq, dckv, dkpe, proj, gq, gkv, tabs, dproj):
    S = cfg.S
    tr = _pick(S, 256, 16)
    (C, SP, SN), w = tabs

    def body(dcq_ref, dckv_ref, dkpe_ref, p_ref, gq_ref, gkv_ref, c_ref, sp_ref, sn_ref, buf_ref, o_ref, dgq_ref, dgkv_ref):
        first = pl.program_id(0) == 0

        def norm_bwd(xv, lo, n, dz, g_ref, dg_ref):
            nrm, r = _masked_rms(xv, lo, n)
            _acc_rows(dg_ref, jnp.sum(dz * nrm, axis=0, keepdims=True), first)
            dn = dz * g_ref[...]
            return r * (dn - nrm * (jnp.sum(dn * nrm, axis=-1, keepdims=True) * (1.0 / n)))

        dxq = norm_bwd(p_ref[:, 0:cfg.QLP], 0, cfg.QL, dcq_ref[...], gq_ref, dgq_ref)
        dxk = norm_bwd(p_ref[:, cfg.KV0:cfg.PW], cfg.KOFF, cfg.KVL, dckv_ref[...], gkv_ref, dgkv_ref)
        dxr = _rope_t(pltpu.roll(dkpe_ref[...], 64, 1), c_ref[...], sp_ref[...], sn_ref[...], w)
        for cidx in range(cfg.PW // LANES):
            lo = cidx * LANES
            parts = []
            if lo < cfg.QLP:
                parts.append(dxq[:, lo:lo + LANES])
            if lo >= cfg.KV0:
                parts.append(dxk[:, lo - cfg.KV0:lo - cfg.KV0 + LANES])
            if lo == cfg.PW - LANES:
                parts.append(dxr)
            o_ref[:, lo:lo + LANES] = functools.reduce(lambda a, b: a + b, parts).astype(o_ref.dtype)

    tab = pl.BlockSpec((tr, LANES), lambda i: (i, 0))
    res = _pcall(body, name=name, grid=(S // tr,),
                 in_specs=[pl.BlockSpec((tr, cfg.QLP), lambda i: (i, 0)), pl.BlockSpec((tr, cfg.KVW), lambda i: (i, 0)), tab,
                           pl.BlockSpec((tr, cfg.PW), lambda i: (i, 0)), pl.BlockSpec((1, cfg.QLP), lambda i: (0, 0)),
                           pl.BlockSpec((1, cfg.KVW), lambda i: (0, 0)), tab, tab, tab, pl.BlockSpec(memory_space=pl.ANY)],
                 out_specs=(pl.BlockSpec((tr, cfg.PW), lambda i: (i, 0)), pl.BlockSpec((8, cfg.QLP), lambda i: (0, 0)),
                            pl.BlockSpec((8, cfg.KVW), lambda i: (0, 0))),
                 out_shape=(jax.ShapeDtypeStruct(dproj.shape, dproj.dtype), jax.ShapeDtypeStruct((8, cfg.QLP), F32),
                            jax.ShapeDtypeStruct((8, cfg.KVW), F32)),
                 input_output_aliases={9: 0}, compiler_params=_row_params())(dcq, dckv, dkpe, proj, gq, gkv, C, SP, SN, dproj)
    return res


def _mla_build_fwd(name, cfg, qa, kva, kpe, tabs):
    S, AH = cfg.S, cfg.AH
    tr = _pick(S, 256, 16)
    (C, SP, SN), w = tabs

    def body(qa_ref, kva_ref, kpe_ref, c_ref, sp_ref, sn_ref, q_ref, k_ref, v_ref):
        for h in range(AH):
            a, b = 256 * h, 256 * h + LANES
            q_ref[:, a:b] = qa_ref[:, a:b].astype(q_ref.dtype)
            q_ref[:, b:b + LANES] = _rope(qa_ref[:, b:b + LANES], c_ref[...], sp_ref[...], sn_ref[...], w).astype(q_ref.dtype)
            k_ref[:, a:b] = kva_ref[:, a:b].astype(k_ref.dtype)
            k_ref[:, b:b + LANES] = kpe_ref[...]
            v_ref[:, LANES * h:LANES * (h + 1)] = kva_ref[:, b:b + LANES].astype(v_ref.dtype)

    tab = pl.BlockSpec((tr, LANES), lambda i: (i, 0))
    wide = pl.BlockSpec((tr, AH * 256), lambda i: (i, 0))
    return _pcall(body, name=name, grid=(S // tr,), in_specs=[wide, wide, tab, tab, tab, tab],
                  out_specs=(wide, wide, pl.BlockSpec((tr, AH * LANES), lambda i: (i, 0))),
                  out_shape=(jax.ShapeDtypeStruct((S, AH * 256), MM_DT), jax.ShapeDtypeStruct((S, AH * 256), MM_DT),
                             jax.ShapeDtypeStruct((S, AH * LANES), MM_DT)),
                  compiler_params=_row_params())(qa, kva, kpe, C, SP, SN)


def _mla_build_bwd(name, cfg, dq, dk, dv, tabs):
    S, AH = cfg.S, cfg.AH
    tr = _pick(S, 256, 16)
    (C, SP, SN), w = tabs

    def body(dq_ref, dk_ref, dv_ref, c_ref, sp_ref, sn_ref, dqa_ref, dkva_ref, dkpe_ref):
        dkpe = None
        for h in range(AH):
            a, b = 256 * h, 256 * h + LANES
            dqa_ref[:, a:b] = dq_ref[:, a:b].astype(dqa_ref.dtype)
            dqa_ref[:, b:b + LANES] = _rope_t(dq_ref[:, b:b + LANES], c_ref[...], sp_ref[...], sn_ref[...], w).astype(dqa_ref.dtype)
            dkva_ref[:, a:b] = dk_ref[:, a:b].astype(dkva_ref.dtype)
            dkva_ref[:, b:b + LANES] = dv_ref[:, LANES * h:LANES * (h + 1)].astype(dkva_ref.dtype)
            part = dk_ref[:, b:b + LANES]
            dkpe = part if dkpe is None else dkpe + part
        dkpe_ref[...] = dkpe

    tab = pl.BlockSpec((tr, LANES), lambda i: (i, 0))
    wide = pl.BlockSpec((tr, AH * 256), lambda i: (i, 0))
    return _pcall(body, name=name, grid=(S // tr,),
                  in_specs=[wide, wide, pl.BlockSpec((tr, AH * LANES), lambda i: (i, 0)), tab, tab, tab],
                  out_specs=(wide, wide, tab),
                  out_shape=(jax.ShapeDtypeStruct((S, AH * 256), MM_DT), jax.ShapeDtypeStruct((S, AH * 256), MM_DT),
                             jax.ShapeDtypeStruct((S, LANES), F32)),
                  compiler_params=_row_params())(dq, dk, dv, C, SP, SN)


def _outnorm_fwd(name, cfg, oa, ob, oc, g):
    S = cfg.S
    tr = _pick(S, 256, 16)
    widths = (cfg.AW, cfg.BW, cfg.CW)

    def body(a_ref, b_ref, c_ref, g_ref, o_ref):
        off = 0
        for ref, wd in zip((a_ref, b_ref, c_ref), widths):
            v = ref[...]
            r = lax.rsqrt(jnp.mean(v * v, axis=-1, keepdims=True) + EPS)
            o_ref[:, off:off + wd] = (v * r * g_ref[:, off:off + wd]).astype(o_ref.dtype)
            off += wd

    return _pcall(body, name=name, grid=(S // tr,),
                  in_specs=[pl.BlockSpec((tr, wd), lambda i: (i, 0)) for wd in widths] + [pl.BlockSpec((1, cfg.MIX), lambda i: (0, 0))],
                  out_specs=pl.BlockSpec((tr, cfg.MIX), lambda i: (i, 0)),
                  out_shape=jax.ShapeDtypeStruct((S, cfg.MIX), MM_DT), compiler_params=_row_params())(oa, ob, oc, g)


def _outnorm_bwd(name, cfg, dmix, oa, ob, oc, g):
    S = cfg.S
    tr = _pick(S, 256, 16)
    widths = (cfg.AW, cfg.BW, cfg.CW)

    def body(dm_ref, a_ref, b_ref, c_ref, g_ref, da_ref, db_ref, dc_ref, dg_ref):
        off = 0
        parts = []
        for ref, dref, wd in zip((a_ref, b_ref, c_ref), (da_ref, db_ref, dc_ref), widths):
            v = ref[...]
            r = lax.rsqrt(jnp.mean(v * v, axis=-1, keepdims=True) + EPS)
            n = v * r
            dm = dm_ref[:, off:off + wd]
            parts.append(jnp.sum(dm * n, axis=0, keepdims=True))
            dn = dm * g_ref[:, off:off + wd]
            dref[...] = r * (dn - n * jnp.mean(dn * n, axis=-1, keepdims=True))
            off += wd
        _acc_rows(dg_ref, jnp.concatenate(parts, axis=1), pl.program_id(0) == 0)

    segs = [pl.BlockSpec((tr, wd), lambda i: (i, 0)) for wd in widths]
    return _pcall(body, name=name, grid=(S // tr,),
                  in_specs=[pl.BlockSpec((tr, cfg.MIX), lambda i: (i, 0))] + segs + [pl.BlockSpec((1, cfg.MIX), lambda i: (0, 0))],
                  out_specs=tuple(segs) + (pl.BlockSpec((8, cfg.MIX), lambda i: (0, 0)),),
                  out_shape=tuple(jax.ShapeDtypeStruct((S, wd), F32) for wd in widths) + (jax.ShapeDtypeStruct((8, cfg.MIX), F32),),
                  compiler_params=_row_params())(dmix, oa, ob, oc, g)


def _band_bias(cfg):
    t, W = cfg.TB, cfg.W
    st = lax.broadcasted_iota(jnp.int32, (2 * W + 1, t, t), 0)
    row = lax.broadcasted_iota(jnp.int32, (2 * W + 1, t, t), 1)
    col = lax.broadcasted_iota(jnp.int32, (2 * W + 1, t, t), 2)
    d = (W - st) * t + row - col
    ad = jnp.abs(d)
    m = jnp.zeros(d.shape, F32)
    for reach, dil in cfg.branches:
        ok = ad <= reach
        if dil > 1:
            ok = ok & ((d & (dil - 1)) == 0)
        m = m + ok.astype(F32)
    return jnp.where(m > 0, jnp.log(jnp.maximum(m, 1.0)), NEG)


def _attn_params():
    return pltpu.CompilerParams(dimension_semantics=("parallel", "parallel", "arbitrary"), vmem_limit_bytes=VMEM_LIMIT)


def _scores(q_ref, k_ref, scale, bias_ref):
    s = lax.dot_general(q_ref[...], k_ref[...], NT, preferred_element_type=F32) * scale
    return s if bias_ref is None else s + bias_ref[...]


def _flash_fwd(name, q, k, v, H, G, dk, dv, scale, bias=None, W=None):
    S = q.shape[0]
    band = bias is not None
    tq = bias.shape[1] if band else _pick(S, ATT_TQ, LANES)
    tk = bias.shape[1] if band else _pick(S, ATT_TK_FWD, LANES)
    n = S // tk
    nsteps = 2 * W + 1 if band else n

    def kblock(qi, st):
        return jnp.clip(qi - W + st, 0, n - 1) if band else st

    def body(*refs):
        q_ref, k_ref, v_ref = refs[:3]
        bias_ref = refs[3] if band else None
        o_ref, lse_ref, m_sc, l_sc, acc_sc = refs[-5:]
        qi, st = pl.program_id(1), pl.program_id(2)

        @pl.when(st == 0)
        def _():
            m_sc[...] = jnp.full_like(m_sc, NEG)
            l_sc[...] = jnp.zeros_like(l_sc)
            acc_sc[...] = jnp.zeros_like(acc_sc)

        kj = qi - W + st if band else st

        def step():
            s = _scores(q_ref, k_ref, scale, bias_ref)
            m_prev = m_sc[...]
            m_new = jnp.maximum(m_prev, jnp.max(s, axis=-1, keepdims=True))
            alpha = jnp.exp(m_prev - m_new)
            p = jnp.exp(s - m_new)
            l_sc[...] = alpha * l_sc[...] + jnp.sum(p, axis=-1, keepdims=True)
            acc_sc[...] = alpha * acc_sc[...] + lax.dot_general(p.astype(MM_DT), v_ref[...], NN, preferred_element_type=F32)
            m_sc[...] = m_new

        if band:
            pl.when((kj >= 0) & (kj < n))(step)
        else:
            step()

        @pl.when(st == nsteps - 1)
        def _():
            l = l_sc[...]
            o_ref[...] = acc_sc[...] / l
            lse_ref[...] = jnp.broadcast_to(m_sc[...] + jnp.log(l), lse_ref.shape)

    in_specs = [pl.BlockSpec((tq, dk), lambda h, qi, st: (qi, h)),
                pl.BlockSpec((tk, dk), lambda h, qi, st: (kblock(qi, st), h // G)),
                pl.BlockSpec((tk, dv), lambda h, qi, st: (kblock(qi, st), h // G))]
    ops = [q, k, v]
    if band:
        in_specs.append(pl.BlockSpec((None, tq, tk), lambda h, qi, st: (st, 0, 0)))
        ops.append(bias)
    return _pcall(body, name=name, grid=(H, S // tq, nsteps), in_specs=in_specs,
                  out_specs=(pl.BlockSpec((tq, dv), lambda h, qi, st: (qi, h)),
                             pl.BlockSpec((None, tq, LANES), lambda h, qi, st: (h, qi, 0))),
                  out_shape=(jax.ShapeDtypeStruct((S, H * dv), F32), jax.ShapeDtypeStruct((H, S, LANES), F32)),
                  scratch_shapes=[pltpu.VMEM((tq, 1), F32), pltpu.VMEM((tq, 1), F32), pltpu.VMEM((tq, dv), F32)],
                  compiler_params=_attn_params())(*ops)


def _flash_dq(name, q, k, v, do, o, lse, H, G, dk, dv, scale, bias=None, W=None):
    S = q.shape[0]
    band = bias is not None
    tq = bias.shape[1] if band else _pick(S, ATT_TQ, LANES)
    tk = bias.shape[1] if band else _pick(S, ATT_TK, LANES)
    n = S // tk
    nsteps = 2 * W + 1 if band else n

    def kblock(qi, st):
        return jnp.clip(qi - W + st, 0, n - 1) if band else st

    def body(*refs):
        q_ref, k_ref, v_ref, do_ref, o_ref, lse_ref = refs[:6]
        bias_ref = refs[6] if band else None
        dq_ref, delta_sc, acc_sc = refs[-3:]
        qi, st = pl.program_id(1), pl.program_id(2)

        @pl.when(st == 0)
        def _():
            delta_sc[...] = jnp.sum(do_ref[...] * o_ref[...], axis=-1, keepdims=True)
            acc_sc[...] = jnp.zeros_like(acc_sc)

        kj = qi - W + st if band else st

        def step():
            p = jnp.exp(_scores(q_ref, k_ref, scale, bias_ref) - lse_ref[:, 0:1])
            dp = lax.dot_general(do_ref[...].astype(MM_DT), v_ref[...], NT, preferred_element_type=F32)
            ds = p * (dp - delta_sc[...]) * scale
            acc_sc[...] += lax.dot_general(ds.astype(MM_DT), k_ref[...], NN, preferred_element_type=F32)

        if band:
            pl.when((kj >= 0) & (kj < n))(step)
        else:
            step()

        @pl.when(st == nsteps - 1)
        def _():
            dq_ref[...] = acc_sc[...]

    qspec = lambda wd: pl.BlockSpec((tq, wd), lambda h, qi, st: (qi, h))
    in_specs = [qspec(dk),
                pl.BlockSpec((tk, dk), lambda h, qi, st: (kblock(qi, st), h // G)),
                pl.BlockSpec((tk, dv), lambda h, qi, st: (kblock(qi, st), h // G)),
                qspec(dv), qspec(dv),
                pl.BlockSpec((None, tq, LANES), lambda h, qi, st: (h, qi, 0))]
    ops = [q, k, v, do, o, lse]
    if band:
        in_specs.append(pl.BlockSpec((None, tq, tk), lambda h, qi, st: (st, 0, 0)))
        ops.append(bias)
    return _pcall(body, name=name, grid=(H, S // tq, nsteps), in_specs=in_specs,
                  out_specs=qspec(dk), out_shape=jax.ShapeDtypeStruct((S, H * dk), F32),
                  scratch_shapes=[pltpu.VMEM((tq, 1), F32), pltpu.VMEM((tq, dk), F32)],
                  compiler_params=_attn_params())(*ops)


def _flash_dkv(name, q, k, v, do, o, lse, H, G, dk, dv, scale, bias=None, W=None):
    S = q.shape[0]
    band = bias is not None
    tq = bias.shape[1] if band else _pick(S, ATT_TQ, LANES)
    tk = bias.shape[1] if band else _pick(S, ATT_TK, LANES)
    n = S // tq
    nq = 2 * W + 1 if band else n
    nsteps = G * nq
    Hkv = H // G

    def qhead(hk, st):
        return hk * G + st // nq

    def qblock(kj, st):
        return jnp.clip(kj - W + st % nq, 0, n - 1) if band else st % nq

    def body(*refs):
        q_ref, k_ref, v_ref, do_ref, o_ref, lse_ref = refs[:6]
        bias_ref = refs[6] if band else None
        dk_ref, dv_ref, dk_sc, dv_sc = refs[-4:]
        kj, st = pl.program_id(1), pl.program_id(2)

        @pl.when(st == 0)
        def _():
            dk_sc[...] = jnp.zeros_like(dk_sc)
            dv_sc[...] = jnp.zeros_like(dv_sc)

        qi = kj - W + st % nq if band else st % nq

        def step():
            p = jnp.exp(_scores(q_ref, k_ref, scale, bias_ref) - lse_ref[:, 0:1])
            dof = do_ref[...]
            dob = dof.astype(MM_DT)
            dv_sc[...] += lax.dot_general(p.astype(MM_DT), dob, TN, preferred_element_type=F32)
            dp = lax.dot_general(dob, v_ref[...], NT, preferred_element_type=F32)
            delta = jnp.sum(dof * o_ref[...], axis=-1, keepdims=True)
            ds = p * (dp - delta) * scale
            dk_sc[...] += lax.dot_general(ds.astype(MM_DT), q_ref[...], TN, preferred_element_type=F32)

        if band:
            pl.when((qi >= 0) & (qi < n))(step)
        else:
            step()

        @pl.when(st == nsteps - 1)
        def _():
            dk_ref[...] = dk_sc[...]
            dv_ref[...] = dv_sc[...]

    qspec = lambda wd: pl.BlockSpec((tq, wd), lambda hk, kj, st: (qblock(kj, st), qhead(hk, st)))
    kspec = lambda wd: pl.BlockSpec((tk, wd), lambda hk, kj, st: (kj, hk))
    in_specs = [qspec(dk), kspec(dk), kspec(dv), qspec(dv), qspec(dv),
                pl.BlockSpec((None, tq, LANES), lambda hk, kj, st: (qhead(hk, st), qblock(kj, st), 0))]
    ops = [q, k, v, do, o, lse]
    if band:
        in_specs.append(pl.BlockSpec((None, tq, tk), lambda hk, kj, st: (2 * W - st % nq, 0, 0)))
        ops.append(bias)
    return _pcall(body, name=name, grid=(Hkv, S // tk, nsteps), in_specs=in_specs,
                  out_specs=(kspec(dk), kspec(dv)),
                  out_shape=(jax.ShapeDtypeStruct((S, Hkv * dk), F32), jax.ShapeDtypeStruct((S, Hkv * dv), F32)),
                  scratch_shapes=[pltpu.VMEM((tk, dk), F32), pltpu.VMEM((tk, dv), F32)],
                  compiler_params=_attn_params())(*ops)


def _rowtile(rows, cols):
    return _pick(rows, max(16, (512 * 1024) // cols // 16 * 16), 16)


def _cast_rows(name, w, dtype):
    R, C = w.shape
    tr = _rowtile(R, C)

    def body(w_ref, o_ref):
        o_ref[...] = w_ref[...].astype(o_ref.dtype)

    spec = pl.BlockSpec((tr, C), lambda i: (i, 0))
    return _pcall(body, name=name, grid=(R // tr,), in_specs=[spec], out_specs=spec,
                  out_shape=jax.ShapeDtypeStruct((R, C), dtype), compiler_params=_row_params())(w)


def _add_sibling(name, gw, ra, c_arr, hd):
    ns, depth, Ks, Ns = gw.shape
    rows = hd * Ks
    tr = _rowtile(rows, Ns)
    gw_v = gw.reshape(ns, 2, rows, Ns)
    ra_v = ra.reshape(ns, rows, Ns)

    def body(c_ref, g_ref, r_ref, o_ref):
        o_ref[...] = (g_ref[...] + r_ref[...]).astype(o_ref.dtype)

    grid_spec = pltpu.PrefetchScalarGridSpec(
        num_scalar_prefetch=1, grid=(ns, rows // tr),
        in_specs=[pl.BlockSpec((None, None, tr, Ns), lambda s, r, c_ref: (s, c_ref[0], r, 0)),
                  pl.BlockSpec((None, tr, Ns), lambda s, r, c_ref: (s, r, 0))],
        out_specs=pl.BlockSpec((None, tr, Ns), lambda s, r, c_ref: (s, r, 0)))
    return _pcall(body, name=name, grid_spec=grid_spec, out_shape=jax.ShapeDtypeStruct((ns, rows, Ns), MM_DT),
                  compiler_params=_grid2_params())(c_arr, gw_v, ra_v)


def _add_chips(name, p, rb, me_arr):
    ns, rows, Ns = p.shape
    tr = _rowtile(rows, Ns)

    def body(me_ref, p_ref, b0_ref, b1_ref, b2_ref, o_ref):
        o_ref[...] = ((p_ref[...].astype(F32) + b0_ref[...].astype(F32)) + b1_ref[...].astype(F32)) + b2_ref[...].astype(F32)

    grid_spec = pltpu.PrefetchScalarGridSpec(
        num_scalar_prefetch=1, grid=(rows // tr,),
        in_specs=[pl.BlockSpec((None, tr, Ns), lambda r, me_ref: (me_ref[0], r, 0))] +
                 [pl.BlockSpec((None, tr, Ns), functools.partial(lambda r, me_ref, j: (j, r, 0), j=j)) for j in range(3)],
        out_specs=pl.BlockSpec((None, tr, Ns), lambda r, me_ref: (0, r, 0)))
    return _pcall(body, name=name, grid_spec=grid_spec, out_shape=jax.ShapeDtypeStruct((2, rows, Ns), F32),
                  compiler_params=_row_params())(me_arr, p, rb, rb, rb)


def _adamw_math(wv, gv, mv, vv):
    bc1 = 1.0 - ADAM_B1 ** ADAM_STEP
    bc2 = 1.0 - ADAM_B2 ** ADAM_STEP
    mn = ADAM_B1 * mv + (1.0 - ADAM_B1) * gv
    vn = ADAM_B2 * vv + (1.0 - ADAM_B2) * jnp.square(gv)
    m_hat = mn / bc1
    v_hat = vn / bc2
    return -ADAM_LR * (m_hat / (jnp.sqrt(v_hat) + ADAM_EPS) + ADAM_WD * wv), mn, vn


def _adamw_halves(name, w, g2, m, v, c_arr):
    _, R, C = w.shape
    tr = _rowtile(R, C)

    def body(c_ref, w_ref, g_ref, m_ref, v_ref, go_ref, d_ref, nm_ref, nv_ref):
        gv = g_ref[...]
        go_ref[...] = gv
        d_ref[...], nm_ref[...], nv_ref[...] = _adamw_math(w_ref[...], gv, m_ref[...], v_ref[...])

    spec = pl.BlockSpec((None, tr, C), lambda h, r, c_ref: (h, r, 0))
    gspec = pl.BlockSpec((None, tr, C), lambda h, r, c_ref: ((h + c_ref[0]) % 2, r, 0))
    sds = jax.ShapeDtypeStruct(w.shape, F32)
    grid_spec = pltpu.PrefetchScalarGridSpec(num_scalar_prefetch=1, grid=(2, R // tr), in_specs=[spec, gspec, spec, spec],
                                             out_specs=(spec,) * 4)
    return _pcall(body, name=name, grid_spec=grid_spec, out_shape=(sds,) * 4, compiler_params=_grid2_params())(c_arr, w, g2, m, v)


def _adamw(name, w, g, m, v):
    R, C = w.shape
    tr = _rowtile(R, C)

    def body(w_ref, g_ref, m_ref, v_ref, d_ref, nm_ref, nv_ref):
        d_ref[...], nm_ref[...], nv_ref[...] = _adamw_math(w_ref[...], g_ref[...], m_ref[...], v_ref[...])

    spec = pl.BlockSpec((tr, C), lambda i: (i, 0))
    sds = jax.ShapeDtypeStruct((R, C), F32)
    return _pcall(body, name=name, grid=(R // tr,), in_specs=[spec] * 4, out_specs=(spec,) * 3,
                  out_shape=(sds, sds, sds), compiler_params=_row_params())(w, g, m, v)


HBM_SPEC = pl.BlockSpec(memory_space=pltpu.HBM)


def _place():
    x, y, c = lax.axis_index("x"), lax.axis_index("y"), lax.axis_index("c")
    chips = [(1 - x, y), (x, 1 - y), (1 - x, 1 - y)]
    return x, y, c, chips


def _allgather_weights(shards, hd):
    n = len(shards)

    def body(*refs):
        ins, outs = refs[:n], refs[n:2 * n]
        send, recv = refs[2 * n:]
        x, y, c, chips = _place()
        me = 2 * x + y
        sib = (x, y, 1 - c)

        def rcopy(src, dst, k, to):
            return pltpu.make_async_remote_copy(src_ref=src, dst_ref=dst, send_sem=send.at[k], recv_sem=recv.at[k],
                                                device_id=to, device_id_type=MESH)

        sends = []
        for t in range(n):
            for j, chip in enumerate(chips):
                cp = rcopy(ins[t].at[pl.ds(c * hd, hd)], outs[t].at[me, pl.ds(c * hd, hd)], 7 * t + j, (chip[0], chip[1], c))
                cp.start()
                sends.append(cp)
        for t in range(n):
            cp = rcopy(ins[t], outs[t].at[me], 7 * t + 6, sib)
            cp.start()
            sends.append(cp)
        for t in range(n):
            for j, chip in enumerate(chips):
                slab = outs[t].at[2 * chip[0] + chip[1], pl.ds(c * hd, hd)]
                rcopy(slab, slab, 7 * t + j, (chip[0], chip[1], c)).wait_recv()
                fw = rcopy(slab, slab, 7 * t + 3 + j, sib)
                fw.start()
                sends.append(fw)
        for t in range(n):
            rcopy(ins[t], outs[t].at[me], 7 * t + 6, sib).wait_recv()
            for j, chip in enumerate(chips):
                slab = outs[t].at[2 * chip[0] + chip[1], pl.ds((1 - c) * hd, hd)]
                rcopy(slab, slab, 7 * t + 3 + j, sib).wait_recv()
        for cp in sends:
            cp.wait_send()

    return _pcall(body, name="allgather_weights", in_specs=[HBM_SPEC] * n, out_specs=tuple([HBM_SPEC] * n),
                  out_shape=tuple(jax.ShapeDtypeStruct((N_SHARD,) + s.shape, s.dtype) for s in shards),
                  scratch_shapes=[pltpu.SemaphoreType.DMA((7 * n,)), pltpu.SemaphoreType.DMA((7 * n,))])(*shards)


def _exchange_sibling_halves(gws, hd):
    n = len(gws)

    def body(*refs):
        ins, outs = refs[:n], refs[n:2 * n]
        send, recv = refs[2 * n:]
        x, y, c, _ = _place()
        cps = []
        for t in range(n):
            cp = pltpu.make_async_remote_copy(src_ref=ins[t].at[:, pl.ds((1 - c) * hd, hd)], dst_ref=outs[t],
                                              send_sem=send.at[t], recv_sem=recv.at[t],
                                              device_id=(x, y, 1 - c), device_id_type=MESH)
            cp.start()
            cps.append(cp)
        for cp in cps:
            cp.wait()

    return _pcall(body, name="rs_sibling_halves", in_specs=[HBM_SPEC] * n, out_specs=tuple([HBM_SPEC] * n),
                  out_shape=tuple(jax.ShapeDtypeStruct((g.shape[0], hd) + g.shape[2:], g.dtype) for g in gws),
                  scratch_shapes=[pltpu.SemaphoreType.DMA((n,)), pltpu.SemaphoreType.DMA((n,))])(*gws)


def _exchange_chips(ps):
    n = len(ps)

    def body(*refs):
        ins, outs = refs[:n], refs[n:2 * n]
        send, recv = refs[2 * n:]
        x, y, c, chips = _place()
        cps = []
        for t in range(n):
            for j, chip in enumerate(chips):
                cp = pltpu.make_async_remote_copy(src_ref=ins[t].at[2 * chip[0] + chip[1]], dst_ref=outs[t].at[j],
                                                  send_sem=send.at[3 * t + j], recv_sem=recv.at[3 * t + j],
                                                  device_id=(chip[0], chip[1], c), device_id_type=MESH)
                cp.start()
                cps.append(cp)
        for cp in cps:
            cp.wait()

    return _pcall(body, name="rs_chip_exchange", in_specs=[HBM_SPEC] * n, out_specs=tuple([HBM_SPEC] * n),
                  out_shape=tuple(jax.ShapeDtypeStruct((3,) + p.shape[1:], p.dtype) for p in ps),
                  scratch_shapes=[pltpu.SemaphoreType.DMA((3 * n,)), pltpu.SemaphoreType.DMA((3 * n,))])(*ps)


def _share_reduced(gs):
    n = len(gs)

    def body(*refs):
        ins, outs = refs[:n], refs[n:2 * n]
        send, recv = refs[2 * n:]
        x, y, c, _ = _place()
        cps = []
        for t in range(n):
            cp = pltpu.make_async_remote_copy(src_ref=ins[t].at[0], dst_ref=outs[t].at[1], send_sem=send.at[t], recv_sem=recv.at[t],
                                              device_id=(x, y, 1 - c), device_id_type=MESH)
            cp.start()
            cps.append(cp)
        for cp in cps:
            cp.wait()

    return _pcall(body, name="rs_share_reduced", in_specs=[HBM_SPEC] * n, out_specs=tuple([HBM_SPEC] * n),
                  out_shape=tuple(jax.ShapeDtypeStruct(g.shape, g.dtype) for g in gs),
                  input_output_aliases={t: t for t in range(n)},
                  scratch_shapes=[pltpu.SemaphoreType.DMA((n,)), pltpu.SemaphoreType.DMA((n,))])(*gs)


def _allreduce_small(vec):
    R = vec.shape[0]

    def body(v_ref, o_ref, buf, send, recv):
        x, y, c, _ = _place()
        me = 4 * x + 2 * y + c
        buf[me] = v_ref[...]
        cps = []
        for r in range(1, 8):
            fx, fy, fc = (r >> 2) & 1, (r >> 1) & 1, r & 1
            to = (1 - x if fx else x, 1 - y if fy else y, 1 - c if fc else c)
            cp = pltpu.make_async_remote_copy(src_ref=v_ref, dst_ref=buf.at[me], send_sem=send.at[r - 1], recv_sem=recv.at[r - 1],
                                              device_id=to, device_id_type=MESH)
            cp.start()
            cps.append(cp)
        for r in range(1, 8):
            fx, fy, fc = (r >> 2) & 1, (r >> 1) & 1, r & 1
            frm = (1 - x if fx else x, 1 - y if fy else y, 1 - c if fc else c)
            src = 4 * frm[0] + 2 * frm[1] + frm[2]
            pltpu.make_async_remote_copy(src_ref=v_ref, dst_ref=buf.at[src], send_sem=send.at[r - 1], recv_sem=recv.at[r - 1],
                                         device_id=frm, device_id_type=MESH).wait_recv()
        for cp in cps:
            cp.wait_send()
        acc = buf[0]
        for i in range(1, 8):
            acc = acc + buf[i]
        o_ref[...] = acc

    vm = pl.BlockSpec(memory_space=pltpu.VMEM)
    return _pcall(body, name="allreduce_small", in_specs=[vm], out_specs=vm, out_shape=jax.ShapeDtypeStruct((R, LANES), F32),
                  scratch_shapes=[pltpu.VMEM((8, R, LANES), F32), pltpu.SemaphoreType.DMA((7,)), pltpu.SemaphoreType.DMA((7,))])(vec)


def _unshard_cols(wg):
    ns, depth, K, Ns = wg.shape
    return jnp.moveaxis(wg, 0, 2).reshape(depth, K, ns * Ns)


def _shard_cols(w):
    K, N = w.shape
    return jnp.moveaxis(w.reshape(K, N_SHARD, N // N_SHARD), 1, 0)


def _uq_padded(cfg, wuq_g):
    w = _unshard_cols(wuq_g).reshape(cfg.DEPTH, cfg.QL, cfg.AH, 192)
    w = jnp.pad(w, ((0, 0), (0, cfg.QLP - cfg.QL), (0, 0), (0, 64)))
    return w.reshape(1, cfg.DEPTH, cfg.QLP, cfg.AH * 256)


def _uq_grad_unpadded(cfg, dw):
    w = dw[:cfg.QL].reshape(cfg.QL, cfg.AH, 256)[:, :, :192].reshape(cfg.QL, cfg.UQ)
    return _shard_cols(w)


def _ukv_padded(cfg, wukv_g):
    w = _unshard_cols(wukv_g)
    w = jnp.pad(w, ((0, 0), (cfg.KOFF, cfg.KVW - cfg.KVL - cfg.KOFF), (0, 0)))
    return w[None]


def _ukv_grad_unpadded(cfg, dw):
    return _shard_cols(dw[cfg.KOFF:cfg.KOFF + cfg.KVL])


def _pad_lanes(v, lo, total):
    return jnp.pad(v, (lo, total - lo - v.shape[0]))[None]


def _layer_fwd(cfg, l, x, W, small, tabs):
    ln1, gq, gkv, gqn, gkn, gout, ln2 = small
    sc_a, sc_h = 1.0 / math.sqrt(192), 1.0 / math.sqrt(128)
    n = f"l{l}_"
    h = _rms_fwd(n + "ln1", x, ln1)
    proj = _mm_nn(n + "proj", h, (W["w_in"], "col", l))
    cqn, ckvn, kpe = _mla_prep_fwd(n + "mla_prep", cfg, proj, gq, gkv, tabs["akr"])
    qa = _mm_nn(n + "uq", cqn, (W["uq_p"], "col", l))
    kva = _mm_nn(n + "ukv", ckvn, (W["ukv_p"], "col", l))
    q_a, k_a, v_a = _mla_build_fwd(n + "mla_build", cfg, qa, kva, kpe, tabs["aq"])
    o_a, lse_a = _flash_fwd(n + "attn_a", q_a, k_a, v_a, cfg.AH, 1, 256, 128, sc_a)
    q_b = _headprep_fwd(n + "bq", proj, cfg.o_bq, cfg.BH, gqn, tabs["b"])
    k_b = _headprep_fwd(n + "bk", proj, cfg.o_bk, cfg.BKV, gkn, tabs["b"])
    v_b = _headprep_fwd(n + "bv", proj, cfg.o_bv, cfg.BKV, None, None)
    o_b, lse_b = _flash_fwd(n + "attn_b", q_b, k_b, v_b, cfg.BH, cfg.G, 128, 128, sc_h)
    q_c = _headprep_fwd(n + "cq", proj, cfg.o_cq, cfg.CH, None, tabs["c"])
    k_c = _headprep_fwd(n + "ck", proj, cfg.o_ck, cfg.CH, None, tabs["c"])
    v_c = _headprep_fwd(n + "cv", proj, cfg.o_cv, cfg.CH, None, None)
    o_c, lse_c = _flash_fwd(n + "attn_c", q_c, k_c, v_c, cfg.CH, 1, 128, 128, sc_h, tabs["bias_c"], cfg.W)
    mixed = _outnorm_fwd(n + "outnorm", cfg, o_a, o_b, o_c, gout)
    x1 = _mm_nn(n + "out", mixed, (W["w_out"], "row", l), epi=_epi_residual, extra=x)
    h2 = _rms_fwd(n + "ln2", x1, ln2)
    a, u = _mm_nn(n + "ff1", h2, (W["w_ff1"], "col", l), epi=_epi_relu2, out_dtypes=(MM_DT, MM_DT))
    x2 = _mm_nn(n + "ff2", u, (W["w_ff2"], "row", l), epi=_epi_residual, extra=x1)
    saved = dict(x=x, h=h, proj=proj, cqn=cqn, ckvn=ckvn, q_a=q_a, k_a=k_a, v_a=v_a, o_a=o_a, lse_a=lse_a,
                 q_b=q_b, k_b=k_b, v_b=v_b, o_b=o_b, lse_b=lse_b, q_c=q_c, k_c=k_c, v_c=v_c, o_c=o_c, lse_c=lse_c,
                 mixed=mixed, x1=x1, h2=h2, a=a, u=u)
    return x2, saved


def _layer_bwd(cfg, l, dx2, sv, W, small, tabs, GW):
    ln1, gq, gkv, gqn, gkn, gout, ln2 = small
    sc_a, sc_h = 1.0 / math.sqrt(192), 1.0 / math.sqrt(128)
    n = f"l{l}_b_"
    S = cfg.S
    mats = {m[0]: m for m in cfg.mats}

    def dw(name, a, g, key):
        _, Rs, Cs, kind = mats[key]
        GW[key] = _mm_tn(n + name, a, g, kind, l, Rs, Cs, GW[key])

    da = _mm_nt(n + "ff2_dx", dx2, (W["w_ff2"], "row", l), epi=_epi_drelu2, out_dtype=MM_DT, extra=sv["a"])
    dw("ff2_dw", sv["u"], dx2, "w_ff2")
    dh2 = _mm_nt(n + "ff1_dx", da, (W["w_ff1"], "col", l))
    dw("ff1_dw", sv["h2"], da, "w_ff1")
    dx1, dln2 = _rms_bwd(n + "ln2", sv["x1"], ln2, dh2, dx2)
    dmix = _mm_nt(n + "out_dx", dx1, (W["w_out"], "row", l))
    dw("out_dw", sv["mixed"], dx1, "w_out")
    do_a, do_b, do_c, dgout = _outnorm_bwd(n + "outnorm", cfg, dmix, sv["o_a"], sv["o_b"], sv["o_c"], gout)
    dproj = jnp.zeros((S, cfg.IN), MM_DT)
    args_c = (sv["q_c"], sv["k_c"], sv["v_c"], do_c, sv["o_c"], sv["lse_c"], cfg.CH, 1, 128, 128, sc_h, tabs["bias_c"], cfg.W)
    dq_c = _flash_dq(n + "attn_c_dq", *args_c)
    dk_c, dv_c = _flash_dkv(n + "attn_c_dkv", *args_c)
    dproj, _ = _headprep_bwd(n + "cq", dq_c, sv["proj"], cfg.o_cq, cfg.CH, None, tabs["c"], dproj)
    dproj, _ = _headprep_bwd(n + "ck", dk_c, sv["proj"], cfg.o_ck, cfg.CH, None, tabs["c"], dproj)
    dproj, _ = _headprep_bwd(n + "cv", dv_c, sv["proj"], cfg.o_cv, cfg.CH, None, None, dproj)
    args_b = (sv["q_b"], sv["k_b"], sv["v_b"], do_b, sv["o_b"], sv["lse_b"], cfg.BH, cfg.G, 128, 128, sc_h)
    dq_b = _flash_dq(n + "attn_b_dq", *args_b)
    dk_b, dv_b = _flash_dkv(n + "attn_b_dkv", *args_b)
    dproj, dgqn = _headprep_bwd(n + "bq", dq_b, sv["proj"], cfg.o_bq, cfg.BH, gqn, tabs["b"], dproj)
    dproj, dgkn = _headprep_bwd(n + "bk", dk_b, sv["proj"], cfg.o_bk, cfg.BKV, gkn, tabs["b"], dproj)
    dproj, _ = _headprep_bwd(n + "bv", dv_b, sv["proj"], cfg.o_bv, cfg.BKV, None, None, dproj)
    args_a = (sv["q_a"], sv["k_a"], sv["v_a"], do_a, sv["o_a"], sv["lse_a"], cfg.AH, 1, 256, 128, sc_a)
    dq_a = _flash_dq(n + "attn_a_dq", *args_a)
    dk_a, dv_a = _flash_dkv(n + "attn_a_dkv", *args_a)
    dqa, dkva, dkpe = _mla_build_bwd(n + "mla_build", cfg, dq_a, dk_a, dv_a, tabs["aq"])
    dcq = _mm_nt(n + "uq_dx", dqa, (W["uq_p"], "col", l))
    dwuq = _mm_tn(n + "uq_dw", sv["cqn"], dqa, "col", 0, cfg.QLP, cfg.AH * 256, (1, 1, cfg.QLP, cfg.AH * 256))
    dckv = _mm_nt(n + "ukv_dx", dkva, (W["ukv_p"], "col", l))
    dwukv = _mm_tn(n + "ukv_dw", sv["ckvn"], dkva, "col", 0, cfg.KVW, cfg.AH * 256, (1, 1, cfg.KVW, cfg.AH * 256))
    dproj, dgq, dgkv = _mla_prep_bwd(n + "mla_prep", cfg, dcq, dckv, dkpe, sv["proj"], gq, gkv, tabs["akr"], dproj)
    dh = _mm_nt(n + "proj_dx", dproj, (W["w_in"], "col", l))
    dw("proj_dw", sv["h"], dproj, "w_in")
    dx, dln1 = _rms_bwd(n + "ln1", sv["x"], ln1, dh, dx1)
    gains = dict(ln1_g=dln1[0], g_q_a=dgq[0, :cfg.QL], g_kv_a=dgkv[0, cfg.KOFF:cfg.KOFF + cfg.KVL], g_qn_b=dgqn[0],
                 g_kn_b=dgkn[0], g_out=dgout[0], ln2_g=dln2[0])
    return dx, gains, _uq_grad_unpadded(cfg, dwuq[0, 0]), _ukv_grad_unpadded(cfg, dwukv[0, 0])


SMALL_NAMES = ("ln1_g", "g_q_a", "g_kv_a", "g_qn_b", "g_kn_b", "g_out", "ln2_g")
MAT_NAMES = ("w_in", "w_uq", "w_ukv", "w_out", "w_ff1", "w_ff2")


def _pack_small(cfg, per_layer, final, scalar=None):
    last = jnp.zeros((1,), F32) if scalar is None else scalar.reshape(1)
    flat = jnp.concatenate([per_layer[k].reshape(-1) for k in SMALL_NAMES] + [final.reshape(-1), last])
    total = flat.shape[0]
    rows = _rup(-(-total // LANES), 8)
    return jnp.pad(flat, (0, rows * LANES - total)).reshape(rows, LANES)


def _unpack_small(cfg, packed, shapes):
    flat = packed.reshape(-1)
    out, off = {}, 0
    for k in SMALL_NAMES + ("ln_f_g",):
        n = math.prod(shapes[k])
        out[k] = flat[off:off + n].reshape(shapes[k])
        off += n
    return out, flat[off]


def _step(cfg, w, m, v, x, tgt):
    DEPTH, hd = cfg.DEPTH, cfg.HD
    c = lax.axis_index("c")
    me_chip = 2 * lax.axis_index("x") + lax.axis_index("y")
    c_arr = jnp.reshape(c, (1,)).astype(jnp.int32)
    me_arr = jnp.reshape(me_chip, (1,)).astype(jnp.int32)
    mats = {mt[0]: mt for mt in cfg.mats}

    shards = []
    for name in MAT_NAMES:
        _, Ks, Ns, _ = mats[name]
        shards.append(_cast_rows("cast_" + name, w[name].reshape(DEPTH * Ks, Ns), MM_DT).reshape(DEPTH, Ks, Ns))
    gathered = dict(zip(MAT_NAMES, _allgather_weights(shards, hd)))
    W = dict(w_in=gathered["w_in"], w_out=gathered["w_out"], w_ff1=gathered["w_ff1"], w_ff2=gathered["w_ff2"],
             uq_p=_uq_padded(cfg, gathered["w_uq"]), ukv_p=_ukv_padded(cfg, gathered["w_ukv"]))
    tabs = _all_tables(cfg)
    tabs["bias_c"] = _band_bias(cfg)

    def small_of(l):
        return (w["ln1_g"][l][None], _pad_lanes(w["g_q_a"][l], 0, cfg.QLP), _pad_lanes(w["g_kv_a"][l], cfg.KOFF, cfg.KVW),
                w["g_qn_b"][l][None], w["g_kn_b"][l][None], w["g_out"][l][None], w["ln2_g"][l][None])

    saved = []
    xc = x
    for l in range(DEPTH):
        xc, sv = _layer_fwd(cfg, l, xc, W, small_of(l), tabs)
        saved.append(sv)
    dx, dlnf, loss_rows = _final_loss("final_loss", xc, w["ln_f_g"][None], tgt)
    loss = loss_rows[0, 0]
    GW = {name: (N_SHARD, DEPTH, mats[name][1], mats[name][2]) for name in ("w_in", "w_out", "w_ff1", "w_ff2")}
    gain_rows = [None] * DEPTH
    duq, dukv = [None] * DEPTH, [None] * DEPTH
    for l in reversed(range(DEPTH)):
        dx, gain_rows[l], duq[l], dukv[l] = _layer_bwd(cfg, l, dx, saved[l], W, small_of(l), tabs, GW)
    GW["w_uq"] = jnp.stack(duq, axis=1)
    GW["w_ukv"] = jnp.stack(dukv, axis=1)

    gws = [GW[name] for name in MAT_NAMES]
    ras = _exchange_sibling_halves(gws, hd)
    ps = [_add_sibling("rs_add_sib_" + name, g, r, c_arr, hd) for name, g, r in zip(MAT_NAMES, gws, ras)]
    rbs = _exchange_chips(ps)
    rs = [_add_chips("rs_add_chips_" + name, p, rb, me_arr) for name, p, rb in zip(MAT_NAMES, ps, rbs)]
    full = _share_reduced(rs)
    grad, delta, new_m, new_v = {}, {}, {}, {}
    for name, g2 in zip(MAT_NAMES, full):
        _, Ks, Ns, _ = mats[name]
        halves, shp = (2, hd * Ks, Ns), (DEPTH, Ks, Ns)
        res = _adamw_halves("adamw_" + name, w[name].reshape(halves), g2, m[name].reshape(halves), v[name].reshape(halves), c_arr)
        grad[name], delta[name], new_m[name], new_v[name] = (r.reshape(shp) for r in res)

    per_layer = {k: jnp.stack([gain_rows[l][k] for l in range(DEPTH)]) for k in SMALL_NAMES}
    shapes = {k: w[k].shape for k in SMALL_NAMES + ("ln_f_g",)}
    gsum = _allreduce_small(_pack_small(cfg, per_layer, dlnf[0], loss))
    pk = lambda d: _pack_small(cfg, {k: d[k] for k in SMALL_NAMES}, d["ln_f_g"])
    d_s, m_s, v_s = _adamw("adamw_small", pk(w), gsum, pk(m), pk(v))
    for res, packed in ((grad, gsum), (delta, d_s), (new_m, m_s), (new_v, v_s)):
        res.update(_unpack_small(cfg, packed, shapes)[0])
    loss_total = _unpack_small(cfg, gsum, shapes)[1]
    return loss_total, dx, grad, delta, new_m, new_v


WEIGHT_NAMES = ("ln1_g", "w_in", "g_q_a", "w_uq", "g_kv_a", "w_ukv", "g_qn_b", "g_kn_b", "g_out", "w_out", "ln2_g",
                "w_ff1", "w_ff2", "ln_f_g")


def _run(cfg, args):
    nw = len(WEIGHT_NAMES)
    x, tgt = args[0], args[1 + nw]
    w = dict(zip(WEIGHT_NAMES, args[1:1 + nw]))
    m = dict(zip(WEIGHT_NAMES, args[2 + nw:2 + 2 * nw]))
    v = dict(zip(WEIGHT_NAMES, args[2 + 2 * nw:2 + 3 * nw]))
    loss, dx, grad, delta, new_m, new_v = _step(cfg, w, m, v, x.reshape(cfg.S, cfg.D), tgt.reshape(cfg.S, cfg.D))
    return (loss, dx.reshape(x.shape), *[grad[k] for k in WEIGHT_NAMES], *[delta[k] for k in WEIGHT_NAMES],
            *[new_m[k] for k in WEIGHT_NAMES], *[new_v[k] for k in WEIGHT_NAMES])


def kernel(x, ln1_g, w_in, g_q_a, w_uq, g_kv_a, w_ukv, g_qn_b, g_kn_b, g_out, w_out, ln2_g, w_ff1, w_ff2, ln_f_g, loss_target, m_ln1_g, m_w_in, m_g_q_a, m_w_uq, m_g_kv_a, m_w_ukv, m_g_qn_b, m_g_kn_b, m_g_out, m_w_out, m_ln2_g, m_w_ff1, m_w_ff2, m_ln_f_g, v_ln1_g, v_w_in, v_g_q_a, v_w_uq, v_g_kv_a, v_w_ukv, v_g_qn_b, v_g_kn_b, v_g_out, v_w_out, v_ln2_g, v_w_ff1, v_w_ff2, v_ln_f_g):
    return _run(Cfg(), (x, ln1_g, w_in, g_q_a, w_uq, g_kv_a, w_ukv, g_qn_b, g_kn_b, g_out, w_out, ln2_g, w_ff1, w_ff2, ln_f_g, loss_target, m_ln1_g, m_w_in, m_g_q_a, m_w_uq, m_g_kv_a, m_w_ukv, m_g_qn_b, m_g_kn_b, m_g_out, m_w_out, m_ln2_g, m_w_ff1, m_w_ff2, m_ln_f_g, v_ln1_g, v_w_in, v_g_q_a, v_w_uq, v_g_kv_a, v_w_ukv, v_g_qn_b, v_g_kn_b, v_g_out, v_w_out, v_ln2_g, v_w_ff1, v_w_ff2, v_ln_f_g))
```

```python
import functools
import math

import jax
import jax.numpy as jnp
from jax import lax
from jax.experimental import pallas as pl
from jax.experimental.pallas import tpu as pltpu

F32 = jnp.float32
MM_DT = jnp.bfloat16
LANES = 128
SUBLANES_F32 = 8
SUBLANES_BF16 = 16
VMEM_LIMIT = 48 * 1024 * 1024
EPS = 1e-6
NEG = -1e30
ROPE_THETA = 10000.0
MM_TK = 1024
ATT_TQ, ATT_TK_FWD, ATT_TK = 512, 4096, 2048
ADAM_LR, ADAM_B1, ADAM_B2, ADAM_EPS, ADAM_WD, ADAM_STEP = 0.001, 0.9, 0.999, 1e-08, 0.01, 10
MESH_AXES = ("x", "y", "c")
N_SHARD = 4
MESH = pl.DeviceIdType.MESH

NN = (((1,), (0,)), ((), ()))
NT = (((1,), (1,)), ((), ()))
TN = (((0,), (0,)), ((), ()))


def _pcall(body, **kw):
    return pl.pallas_call(body, **kw)


def _rup(n, m):
    return -(-n // m) * m


def _pick(n, pref, mult):
    best = None
    for t in range(mult, min(n, pref) + 1, mult):
        if n % t == 0:
            best = t
    return best if best is not None else n


class Cfg:
    def __init__(self, S=4096, D=2048, DEPTH=4, AH=4, QL=448, KVL=512, BH=6, BKV=2, CH=6,
                 BRANCHES=((128, 1), (512, 4), (2048, 16)), DFF=8192, GRID_W=64, TB=512, TBK=1024):
        self.S, self.D, self.DEPTH, self.AH, self.QL, self.KVL = S, D, DEPTH, AH, QL, KVL
        self.BH, self.BKV, self.CH, self.DFF, self.GRID_W, self.TB = BH, BKV, CH, DFF, GRID_W, TB
        self.G = BH // BKV
        self.AW, self.BW, self.CW = AH * 128, BH * 128, CH * 128
        self.MIX = self.AW + self.BW + self.CW
        self.QLP = _rup(QL, LANES)
        self.KV0 = (QL // LANES) * LANES
        self.PW = QL + KVL + 64
        assert self.PW % LANES == 0
        self.KVW = self.PW - self.KV0
        self.KOFF = QL - self.KV0
        self.o_bq = self.PW
        self.o_bk = self.o_bq + self.BW
        self.o_bv = self.o_bk + BKV * 128
        self.o_cq = self.o_bv + BKV * 128
        self.o_ck = self.o_cq + self.CW
        self.o_cv = self.o_ck + self.CW
        self.IN = self.o_cv + self.CW
        self.UQ, self.UKV = AH * 192, AH * 256
        self.branches = tuple(((w // (2 * d)) * d, d) for w, d in BRANCHES)
        for _, d in self.branches:
            assert d & (d - 1) == 0
        self.TBK = TBK
        self.W = -(-max(r for r, _ in self.branches) // TBK)
        assert S % TBK == 0 and TBK % TB == 0 and DEPTH % 2 == 0
        self.HD = DEPTH // 2
        self.mats = (("w_in", D, self.IN // 4, "col"), ("w_uq", QL, self.UQ // 4, "col"),
                     ("w_ukv", KVL, self.UKV // 4, "col"), ("w_out", self.MIX // 4, D, "row"),
                     ("w_ff1", D, DFF // 4, "col"), ("w_ff2", DFF // 4, D, "row"))


def _mm_call(name, mode, operands, in_specs, out_shape, out_specs, grid, acc_shape, epi, n_extra, aliases=None):
    nk = grid[2]

    def body(*refs):
        a_ref, b_ref = refs[0], refs[1]
        ex = refs[2:2 + n_extra]
        outs = refs[2 + n_extra:-1]
        acc = refs[-1]
        k = pl.program_id(2)

        @pl.when(k == 0)
        def _():
            acc[...] = jnp.zeros_like(acc)

        acc[...] += lax.dot_general(a_ref[...].astype(MM_DT), b_ref[...].astype(MM_DT), mode,
                                    preferred_element_type=F32)

        @pl.when(k == nk - 1)
        def _():
            epi(acc[...], ex, outs)

    return _pcall(body, name=name, grid=grid, in_specs=in_specs, out_specs=out_specs, out_shape=out_shape,
                  scratch_shapes=[pltpu.VMEM(acc_shape, F32)], input_output_aliases=aliases or {},
                  compiler_params=pltpu.CompilerParams(dimension_semantics=("parallel", "parallel", "arbitrary"),
                                                       vmem_limit_bytes=VMEM_LIMIT))(*operands)


def _wspec(kind, l, Rs, Cs, br, bc, rfn, cfn):
    assert Rs % br == 0 and Cs % bc == 0
    if kind == "col":
        npc = Cs // bc
        return pl.BlockSpec((None, None, br, bc), lambda i, j, k: (cfn(i, j, k) // npc, l, rfn(i, j, k), cfn(i, j, k) % npc))
    npr = Rs // br
    return pl.BlockSpec((None, None, br, bc), lambda i, j, k: (rfn(i, j, k) // npr, l, rfn(i, j, k) % npr, cfn(i, j, k)))


def _epi_plain(acc, ex, outs):
    outs[0][...] = acc.astype(outs[0].dtype)


def _epi_residual(acc, ex, outs):
    outs[0][...] = ex[0][...] + acc


def _epi_relu2(acc, ex, outs):
    outs[0][...] = acc.astype(outs[0].dtype)
    r = jnp.maximum(acc, 0.0)
    outs[1][...] = (r * r).astype(outs[1].dtype)


def _epi_drelu2(acc, ex, outs):
    a = ex[0][...].astype(F32)
    outs[0][...] = (acc * (2.0 * jnp.maximum(a, 0.0))).astype(outs[0].dtype)


def _wdims(wd):
    Wg, kind, l = wd
    ns, _, Rs, Cs = Wg.shape
    K = Rs * ns if kind == "row" else Rs
    N = Cs * ns if kind == "col" else Cs
    return Wg, kind, l, Rs, Cs, K, N


def _mm_nn(name, a, wd, epi=_epi_plain, out_dtypes=(F32,), extra=None):
    Wg, kind, l, Rs, Cs, K, N = _wdims(wd)
    M = a.shape[0]
    tm, tk, tn = _pick(M, 1024, 16), _pick(Rs, MM_TK, LANES), _pick(Cs, 1152, LANES)
    grid = (M // tm, N // tn, K // tk)
    in_specs = [pl.BlockSpec((tm, tk), lambda i, j, k: (i, k)),
                _wspec(kind, l, Rs, Cs, tk, tn, lambda i, j, k: k, lambda i, j, k: j)]
    ops = [a, Wg]
    if extra is not None:
        in_specs.append(pl.BlockSpec((tm, tn), lambda i, j, k: (i, j)))
        ops.append(extra)
    o_spec = pl.BlockSpec((tm, tn), lambda i, j, k: (i, j))
    outs = tuple(jax.ShapeDtypeStruct((M, N), dt) for dt in out_dtypes)
    res = _mm_call(name, NN, ops, in_specs, outs, tuple(o_spec for _ in outs), grid, (tm, tn), epi,
                   0 if extra is None else 1)
    return res[0] if len(res) == 1 else res


def _mm_nt(name, g, wd, epi=_epi_plain, out_dtype=F32, extra=None):
    Wg, kind, l, Rs, Cs, K, N = _wdims(wd)
    M = g.shape[0]
    tm, tn, tk = _pick(M, 1024, 16), _pick(Rs, 1024, LANES), _pick(Cs, 1152, LANES)
    grid = (M // tm, K // tn, N // tk)
    in_specs = [pl.BlockSpec((tm, tk), lambda i, j, k: (i, k)),
                _wspec(kind, l, Rs, Cs, tn, tk, lambda i, j, k: j, lambda i, j, k: k)]
    ops = [g, Wg]
    if extra is not None:
        in_specs.append(pl.BlockSpec((tm, tn), lambda i, j, k: (i, j)))
        ops.append(extra)
    res = _mm_call(name, NT, ops, in_specs, (jax.ShapeDtypeStruct((M, K), out_dtype),),
                   (pl.BlockSpec((tm, tn), lambda i, j, k: (i, j)),), grid, (tm, tn), epi, 0 if extra is None else 1)
    return res[0]


def _mm_tn(name, a, g, kind, l, Rs, Cs, buf, out_dtype=F32):
    M, K = a.shape
    N = g.shape[1]
    tm, tn, tk = _pick(Rs, 1024, LANES), _pick(Cs, 1152, LANES), _pick(M, MM_TK, LANES)
    grid = (K // tm, N // tn, M // tk)
    in_specs = [pl.BlockSpec((tk, tm), lambda i, j, k: (k, i)),
                pl.BlockSpec((tk, tn), lambda i, j, k: (k, j))]
    o_spec = _wspec(kind, l, Rs, Cs, tm, tn, lambda i, j, k: i, lambda i, j, k: j)
    if isinstance(buf, tuple):
        res = _mm_call(name, TN, [a, g], in_specs, (jax.ShapeDtypeStruct(buf, out_dtype),), (o_spec,), grid, (tm, tn), _epi_plain, 0)
    else:
        res = _mm_call(name, TN, [a, g, buf], in_specs + [pl.BlockSpec(memory_space=pl.ANY)],
                       (jax.ShapeDtypeStruct(buf.shape, buf.dtype),), (o_spec,), grid, (tm, tn), _epi_plain, 1, aliases={2: 0})
    return res[0]


def _row_params():
    return pltpu.CompilerParams(dimension_semantics=("arbitrary",), vmem_limit_bytes=VMEM_LIMIT)


def _rms_fwd(name, x, g):
    S, D = x.shape
    tr = _pick(S, 256, 16)

    def body(x_ref, g_ref, o_ref):
        xv = x_ref[...]
        r = lax.rsqrt(jnp.mean(xv * xv, axis=-1, keepdims=True) + EPS)
        o_ref[...] = (xv * r * g_ref[...]).astype(o_ref.dtype)

    return _pcall(body, name=name, grid=(S // tr,),
                  in_specs=[pl.BlockSpec((tr, D), lambda i: (i, 0)), pl.BlockSpec((1, D), lambda i: (0, 0))],
                  out_specs=pl.BlockSpec((tr, D), lambda i: (i, 0)), out_shape=jax.ShapeDtypeStruct((S, D), MM_DT),
                  compiler_params=_row_params())(x, g)


def _acc_rows(ref, part, first):
    @pl.when(first)
    def _():
        ref[...] = jnp.zeros_like(ref)

    ref[...] += jnp.broadcast_to(part, ref.shape)


def _rms_bwd(name, x, g, dy, res):
    S, D = x.shape
    tr = _pick(S, 256, 16)

    def body(x_ref, g_ref, dy_ref, res_ref, dx_ref, dg_ref):
        xv = x_ref[...]
        r = lax.rsqrt(jnp.mean(xv * xv, axis=-1, keepdims=True) + EPS)
        xh = xv * r
        dyv = dy_ref[...]
        dn = dyv * g_ref[...]
        dx_ref[...] = res_ref[...] + r * (dn - xh * jnp.mean(dn * xh, axis=-1, keepdims=True))
        _acc_rows(dg_ref, jnp.sum(dyv * xh, axis=0, keepdims=True), pl.program_id(0) == 0)

    row = pl.BlockSpec((tr, D), lambda i: (i, 0))
    return _pcall(body, name=name, grid=(S // tr,),
                  in_specs=[row, pl.BlockSpec((1, D), lambda i: (0, 0)), row, row],
                  out_specs=(row, pl.BlockSpec((8, D), lambda i: (0, 0))),
                  out_shape=(jax.ShapeDtypeStruct((S, D), F32), jax.ShapeDtypeStruct((8, D), F32)),
                  compiler_params=_row_params())(x, g, dy, res)


def _final_loss(name, x, g, tgt):
    S, D = x.shape
    tr = _pick(S, 256, 16)

    def body(x_ref, g_ref, t_ref, dx_ref, dg_ref, loss_ref):
        xv = x_ref[...]
        r = lax.rsqrt(jnp.mean(xv * xv, axis=-1, keepdims=True) + EPS)
        xh = xv * r
        gv = g_ref[...]
        e = xh * gv - t_ref[...]
        part = 0.5 * jnp.sum(jnp.mean(e * e, axis=-1, keepdims=True), axis=0, keepdims=True)
        dy = e * (1.0 / D)
        dn = dy * gv
        dx_ref[...] = r * (dn - xh * jnp.mean(dn * xh, axis=-1, keepdims=True))
        first = pl.program_id(0) == 0
        _acc_rows(dg_ref, jnp.sum(dy * xh, axis=0, keepdims=True), first)
        _acc_rows(loss_ref, part, first)

    row = pl.BlockSpec((tr, D), lambda i: (i, 0))
    return _pcall(body, name=name, grid=(S // tr,),
                  in_specs=[row, pl.BlockSpec((1, D), lambda i: (0, 0)), row],
                  out_specs=(row, pl.BlockSpec((8, D), lambda i: (0, 0)), pl.BlockSpec((8, LANES), lambda i: (0, 0))),
                  out_shape=(jax.ShapeDtypeStruct((S, D), F32), jax.ShapeDtypeStruct((8, D), F32),
                             jax.ShapeDtypeStruct((8, LANES), F32)),
                  compiler_params=_row_params())(x, g, tgt)


def _rope_tables(cos, sin, off, w):
    S = cos.shape[0]
    h = w // 2
    z = lambda n: jnp.zeros((S, n), F32)
    C = jnp.concatenate([z(off), cos, cos, z(LANES - off - w)], axis=1)
    SP = jnp.concatenate([z(off + h), sin, z(LANES - off - w)], axis=1)
    SN = jnp.concatenate([z(off), -sin, z(LANES - off - h)], axis=1)
    return C, SP, SN


def _angles(pos, dim):
    inv = jnp.power(ROPE_THETA, -jnp.arange(0, dim, 2, dtype=F32) / dim)
    ang = pos.astype(F32)[:, None] * inv[None, :]
    return jnp.cos(ang), jnp.sin(ang)


def _all_tables(cfg):
    S = cfg.S
    pos = jnp.arange(S, dtype=F32)
    rows = S // cfg.GRID_W
    row = jnp.repeat(jnp.arange(rows, dtype=F32), cfg.GRID_W)
    col = jnp.tile(jnp.arange(cfg.GRID_W, dtype=F32), rows)
    ca, sa = _angles(pos, 64)
    cc, sc = _angles(pos, 128)
    cr, sr = _angles(row, 64)
    cl, sl = _angles(col, 64)
    t_b = tuple(a + b for a, b in zip(_rope_tables(cr, sr, 0, 64), _rope_tables(cl, sl, 64, 64)))
    return {"aq": (_rope_tables(ca, sa, 0, 64), 64), "akr": (_rope_tables(ca, sa, 64, 64), 64),
            "b": (t_b, 64), "c": (_rope_tables(cc, sc, 0, 128), 128)}


def _rope(x, C, SP, SN, w):
    h = w // 2
    if 2 * h == LANES:
        return x * C + pltpu.roll(x, h, 1) * (SP + SN)
    return x * C + pltpu.roll(x, h, 1) * SP + pltpu.roll(x, LANES - h, 1) * SN


def _rope_t(dy, C, SP, SN, w):
    h = w // 2
    if 2 * h == LANES:
        return dy * C + pltpu.roll(dy * (SP + SN), h, 1)
    return dy * C + pltpu.roll(dy * SP, LANES - h, 1) + pltpu.roll(dy * SN, h, 1)


def _grid2_params():
    return pltpu.CompilerParams(dimension_semantics=("arbitrary", "arbitrary"), vmem_limit_bytes=VMEM_LIMIT)


def _headprep_fwd(name, proj, col_off, nb, gain, tabs):
    S = proj.shape[0]
    tr = _pick(S, 1024, 16)
    cb = col_off // LANES
    norm, rope = gain is not None, tabs is not None
    w = tabs[1] if rope else 0

    def body(*refs):
        x_ref = refs[0]
        pos = 1
        xv = x_ref[...]
        if norm:
            r = lax.rsqrt(jnp.mean(xv * xv, axis=-1, keepdims=True) + EPS)
            xv = xv * r * refs[pos][...]
            pos += 1
        if rope:
            xv = _rope(xv, refs[pos][...], refs[pos + 1][...], refs[pos + 2][...], w)
            pos += 3
        refs[pos][...] = xv.astype(refs[pos].dtype)

    ops, in_specs = [proj], [pl.BlockSpec((tr, LANES), lambda i, j: (i, cb + j))]
    if norm:
        ops.append(gain)
        in_specs.append(pl.BlockSpec((1, LANES), lambda i, j: (0, 0)))
    if rope:
        ops += list(tabs[0])
        in_specs += [pl.BlockSpec((tr, LANES), lambda i, j: (i, 0))] * 3
    return _pcall(body, name=name, grid=(S // tr, nb), in_specs=in_specs,
                  out_specs=pl.BlockSpec((tr, LANES), lambda i, j: (i, j)),
                  out_shape=jax.ShapeDtypeStruct((S, nb * LANES), MM_DT), compiler_params=_grid2_params())(*ops)


def _headprep_bwd(name, dy, proj, col_off, nb, gain, tabs, dproj):
    S = proj.shape[0]
    tr = _pick(S, 1024, 16)
    cb = col_off // LANES
    norm, rope = gain is not None, tabs is not None
    w = tabs[1] if rope else 0

    def body(*refs):
        dz = refs[0][...]
        pos = 1
        if norm:
            x_ref, g_ref = refs[pos], refs[pos + 1]
            pos += 2
        if rope:
            dz = _rope_t(dz, refs[pos][...], refs[pos + 1][...], refs[pos + 2][...], w)
            pos += 3
        pos += 1
        o_ref = refs[pos]
        if norm:
            dg_ref = refs[pos + 1]
            xv = x_ref[...]
            r = lax.rsqrt(jnp.mean(xv * xv, axis=-1, keepdims=True) + EPS)
            n = xv * r
            first = (pl.program_id(0) == 0) & (pl.program_id(1) == 0)
            _acc_rows(dg_ref, jnp.sum(dz * n, axis=0, keepdims=True), first)
            dn = dz * g_ref[...]
            dz = r * (dn - n * jnp.mean(dn * n, axis=-1, keepdims=True))
        o_ref[...] = dz.astype(o_ref.dtype)

    ops, in_specs = [dy], [pl.BlockSpec((tr, LANES), lambda i, j: (i, j))]
    if norm:
        ops += [proj, gain]
        in_specs += [pl.BlockSpec((tr, LANES), lambda i, j: (i, cb + j)), pl.BlockSpec((1, LANES), lambda i, j: (0, 0))]
    if rope:
        ops += list(tabs[0])
        in_specs += [pl.BlockSpec((tr, LANES), lambda i, j: (i, 0))] * 3
    alias_idx = len(ops)
    ops.append(dproj)
    in_specs.append(pl.BlockSpec(memory_space=pl.ANY))
    out_specs = [pl.BlockSpec((tr, LANES), lambda i, j: (i, cb + j))]
    out_shape = [jax.ShapeDtypeStruct(dproj.shape, dproj.dtype)]
    if norm:
        out_specs.append(pl.BlockSpec((8, LANES), lambda i, j: (0, 0)))
        out_shape.append(jax.ShapeDtypeStruct((8, LANES), F32))
    res = _pcall(body, name=name, grid=(S // tr, nb), in_specs=in_specs, out_specs=tuple(out_specs),
                 out_shape=tuple(out_shape), input_output_aliases={alias_idx: 0}, compiler_params=_grid2_params())(*ops)
    return (res[0], res[1]) if norm else (res[0], None)


def _masked_rms(xv, lo, n):
    lane = lax.broadcasted_iota(jnp.int32, xv.shape, 1)
    xm = jnp.where((lane >= lo) & (lane < lo + n), xv, 0.0)
    r = lax.rsqrt(jnp.sum(xm * xm, axis=-1, keepdims=True) * (1.0 / n) + EPS)
    return xm * r, r


def _mla_prep_fwd(name, cfg, proj, gq, gkv, tabs):
    S = cfg.S
    tr = _pick(S, 256, 16)
    (C, SP, SN), w = tabs

    def body(p_ref, gq_ref, gkv_ref, c_ref, sp_ref, sn_ref, cq_ref, ckv_ref, kpe_ref):
        nq, _ = _masked_rms(p_ref[:, 0:cfg.QLP], 0, cfg.QL)
        cq_ref[...] = (nq * gq_ref[...]).astype(cq_ref.dtype)
        nk, _ = _masked_rms(p_ref[:, cfg.KV0:cfg.PW], cfg.KOFF, cfg.KVL)
        ckv_ref[...] = (nk * gkv_ref[...]).astype(ckv_ref.dtype)
        kr = _rope(p_ref[:, cfg.PW - LANES:cfg.PW], c_ref[...], sp_ref[...], sn_ref[...], w)
        kpe_ref[...] = pltpu.roll(kr, 64, 1).astype(kpe_ref.dtype)

    tab = pl.BlockSpec((tr, LANES), lambda i: (i, 0))
    return _pcall(body, name=name, grid=(S // tr,),
                  in_specs=[pl.BlockSpec((tr, cfg.PW), lambda i: (i, 0)), pl.BlockSpec((1, cfg.QLP), lambda i: (0, 0)),
                            pl.BlockSpec((1, cfg.KVW), lambda i: (0, 0)), tab, tab, tab],
                  out_specs=(pl.BlockSpec((tr, cfg.QLP), lambda i: (i, 0)), pl.BlockSpec((tr, cfg.KVW), lambda i: (i, 0)), tab),
                  out_shape=(jax.ShapeDtypeStruct((S, cfg.QLP), MM_DT), jax.ShapeDtypeStruct((S, cfg.KVW), MM_DT),
                             jax.ShapeDtypeStruct((S, LANES), MM_DT)),
                  compiler_params=_row_params())(proj, gq, gkv, C, SP, SN)


def _mla_prep_bwd(name, cfg, dcq, dckv, dkpe, proj, gq, gkv, tabs, dproj):
    S = cfg.S
    tr = _pick(S, 256, 16)
    (C, SP, SN), w = tabs

    def body(dcq_ref, dckv_ref, dkpe_ref, p_ref, gq_ref, gkv_ref, c_ref, sp_ref, sn_ref, buf_ref, o_ref, dgq_ref, dgkv_ref):
        first = pl.program_id(0) == 0

        def norm_bwd(xv, lo, n, dz, g_ref, dg_ref):
            nrm, r = _masked_rms(xv, lo, n)
            _acc_rows(dg_ref, jnp.sum(dz * nrm, axis=0, keepdims=True), first)
            dn = dz * g_ref[...]
            return r * (dn - nrm * (jnp.sum(dn * nrm, axis=-1, keepdims=True) * (1.0 / n)))

        dxq = norm_bwd(p_ref[:, 0:cfg.QLP], 0, cfg.QL, dcq_ref[...], gq_ref, dgq_ref)
        dxk = norm_bwd(p_ref[:, cfg.KV0:cfg.PW], cfg.KOFF, cfg.KVL, dckv_ref[...], gkv_ref, dgkv_ref)
        dxr = _rope_t(pltpu.roll(dkpe_ref[...], 64, 1), c_ref[...], sp_ref[...], sn_ref[...], w)
        for cidx in range(cfg.PW // LANES):
            lo = cidx * LANES
            parts = []
            if lo < cfg.QLP:
                parts.append(dxq[:, lo:lo + LANES])
            if lo >= cfg.KV0:
                parts.append(dxk[:, lo - cfg.KV0:lo - cfg.KV0 + LANES])
            if lo == cfg.PW - LANES:
                parts.append(dxr)
            o_ref[:, lo:lo + LANES] = functools.reduce(lambda a, b: a + b, parts).astype(o_ref.dtype)

    tab = pl.BlockSpec((tr, LANES), lambda i: (i, 0))
    res = _pcall(body, name=name, grid=(S // tr,),
                 in_specs=[pl.BlockSpec((tr, cfg.QLP), lambda i: (i, 0)), pl.BlockSpec((tr, cfg.KVW), lambda i: (i, 0)), tab,
                           pl.BlockSpec((tr, cfg.PW), lambda i: (i, 0)), pl.BlockSpec((1, cfg.QLP), lambda i: (0, 0)),
                           pl.BlockSpec((1, cfg.KVW), lambda i: (0, 0)), tab, tab, tab, pl.BlockSpec(memory_space=pl.ANY)],
                 out_specs=(pl.BlockSpec((tr, cfg.PW), lambda i: (i, 0)), pl.BlockSpec((8, cfg.QLP), lambda i: (0, 0)),
                            pl.BlockSpec((8, cfg.KVW), lambda i: (0, 0))),
                 out_shape=(jax.ShapeDtypeStruct(dproj.shape, dproj.dtype), jax.ShapeDtypeStruct((8, cfg.QLP), F32),
                            jax.ShapeDtypeStruct((8, cfg.KVW), F32)),
                 input_output_aliases={9: 0}, compiler_params=_row_params())(dcq, dckv, dkpe, proj, gq, gkv, C, SP, SN, dproj)
    return res


def _mla_build_fwd(name, cfg, qa, kva, kpe, tabs):
    S, AH = cfg.S, cfg.AH
    tr = _pick(S, 256, 16)
    (C, SP, SN), w = tabs

    def body(qa_ref, kva_ref, kpe_ref, c_ref, sp_ref, sn_ref, q_ref, k_ref, v_ref):
        for h in range(AH):
            a, b = 256 * h, 256 * h + LANES
            q_ref[:, a:b] = qa_ref[:, a:b].astype(q_ref.dtype)
            q_ref[:, b:b + LANES] = _rope(qa_ref[:, b:b + LANES], c_ref[...], sp_ref[...], sn_ref[...], w).astype(q_ref.dtype)
            k_ref[:, a:b] = kva_ref[:, a:b].astype(k_ref.dtype)
            k_ref[:, b:b + LANES] = kpe_ref[...]
            v_ref[:, LANES * h:LANES * (h + 1)] = kva_ref[:, b:b + LANES].astype(v_ref.dtype)

    tab = pl.BlockSpec((tr, LANES), lambda i: (i, 0))
    wide = pl.BlockSpec((tr, AH * 256), lambda i: (i, 0))
    return _pcall(body, name=name, grid=(S // tr,), in_specs=[wide, wide, tab, tab, tab, tab],
                  out_specs=(wide, wide, pl.BlockSpec((tr, AH * LANES), lambda i: (i, 0))),
                  out_shape=(jax.ShapeDtypeStruct((S, AH * 256), MM_DT), jax.ShapeDtypeStruct((S, AH * 256), MM_DT),
                             jax.ShapeDtypeStruct((S, AH * LANES), MM_DT)),
                  compiler_params=_row_params())(qa, kva, kpe, C, SP, SN)


def _mla_build_bwd(name, cfg, dq, dk, dv, tabs):
    S, AH = cfg.S, cfg.AH
    tr = _pick(S, 256, 16)
    (C, SP, SN), w = tabs

    def body(dq_ref, dk_ref, dv_ref, c_ref, sp_ref, sn_ref, dqa_ref, dkva_ref, dkpe_ref):
        dkpe = None
        for h in range(AH):
            a, b = 256 * h, 256 * h + LANES
            dqa_ref[:, a:b] = dq_ref[:, a:b].astype(dqa_ref.dtype)
            dqa_ref[:, b:b + LANES] = _rope_t(dq_ref[:, b:b + LANES], c_ref[...], sp_ref[...], sn_ref[...], w).astype(dqa_ref.dtype)
            dkva_ref[:, a:b] = dk_ref[:, a:b].astype(dkva_ref.dtype)
            dkva_ref[:, b:b + LANES] = dv_ref[:, LANES * h:LANES * (h + 1)].astype(dkva_ref.dtype)
            part = dk_ref[:, b:b + LANES]
            dkpe = part if dkpe is None else dkpe + part
        dkpe_ref[...] = dkpe

    tab = pl.BlockSpec((tr, LANES), lambda i: (i, 0))
    wide = pl.BlockSpec((tr, AH * 256), lambda i: (i, 0))
    return _pcall(body, name=name, grid=(S // tr,),
                  in_specs=[wide, wide, pl.BlockSpec((tr, AH * LANES), lambda i: (i, 0)), tab, tab, tab],
                  out_specs=(wide, wide, tab),
                  out_shape=(jax.ShapeDtypeStruct((S, AH * 256), MM_DT), jax.ShapeDtypeStruct((S, AH * 256), MM_DT),
                             jax.ShapeDtypeStruct((S, LANES), F32)),
                  compiler_params=_row_params())(dq, dk, dv, C, SP, SN)


def _outnorm_fwd(name, cfg, oa, ob, oc, g):
    S = cfg.S
    tr = _pick(S, 256, 16)
    widths = (cfg.AW, cfg.BW, cfg.CW)

    def body(a_ref, b_ref, c_ref, g_ref, o_ref):
        off = 0
        for ref, wd in zip((a_ref, b_ref, c_ref), widths):
            v = ref[...]
            r = lax.rsqrt(jnp.mean(v * v, axis=-1, keepdims=True) + EPS)
            o_ref[:, off:off + wd] = (v * r * g_ref[:, off:off + wd]).astype(o_ref.dtype)
            off += wd

    return _pcall(body, name=name, grid=(S // tr,),
                  in_specs=[pl.BlockSpec((tr, wd), lambda i: (i, 0)) for wd in widths] + [pl.BlockSpec((1, cfg.MIX), lambda i: (0, 0))],
                  out_specs=pl.BlockSpec((tr, cfg.MIX), lambda i: (i, 0)),
                  out_shape=jax.ShapeDtypeStruct((S, cfg.MIX), MM_DT), compiler_params=_row_params())(oa, ob, oc, g)


def _outnorm_bwd(name, cfg, dmix, oa, ob, oc, g):
    S = cfg.S
    tr = _pick(S, 256, 16)
    widths = (cfg.AW, cfg.BW, cfg.CW)

    def body(dm_ref, a_ref, b_ref, c_ref, g_ref, da_ref, db_ref, dc_ref, dg_ref):
        off = 0
        parts = []
        for ref, dref, wd in zip((a_ref, b_ref, c_ref), (da_ref, db_ref, dc_ref), widths):
            v = ref[...]
            r = lax.rsqrt(jnp.mean(v * v, axis=-1, keepdims=True) + EPS)
            n = v * r
            dm = dm_ref[:, off:off + wd]
            parts.append(jnp.sum(dm * n, axis=0, keepdims=True))
            dn = dm * g_ref[:, off:off + wd]
            dref[...] = r * (dn - n * jnp.mean(dn * n, axis=-1, keepdims=True))
            off += wd
        _acc_rows(dg_ref, jnp.concatenate(parts, axis=1), pl.program_id(0) == 0)

    segs = [pl.BlockSpec((tr, wd), lambda i: (i, 0)) for wd in widths]
    return _pcall(body, name=name, grid=(S // tr,),
                  in_specs=[pl.BlockSpec((tr, cfg.MIX), lambda i: (i, 0))] + segs + [pl.BlockSpec((1, cfg.MIX), lambda i: (0, 0))],
                  out_specs=tuple(segs) + (pl.BlockSpec((8, cfg.MIX), lambda i: (0, 0)),),
                  out_shape=tuple(jax.ShapeDtypeStruct((S, wd), F32) for wd in widths) + (jax.ShapeDtypeStruct((8, cfg.MIX), F32),),
                  compiler_params=_row_params())(dmix, oa, ob, oc, g)


def _band_bias(cfg):
    tq, tk, W = cfg.TB, cfg.TBK, cfg.W
    ns = 2 * W + 1
    shape = ((tk // tq) * ns, tq, tk)
    slab = lax.broadcasted_iota(jnp.int32, shape, 0)
    row = lax.broadcasted_iota(jnp.int32, shape, 1)
    col = lax.broadcasted_iota(jnp.int32, shape, 2)
    d = (slab // ns) * tq + (W - slab % ns) * tk + row - col
    ad = jnp.abs(d)
    m = jnp.zeros(d.shape, F32)
    for reach, dil in cfg.branches:
        ok = ad <= reach
        if dil > 1:
            ok = ok & ((d & (dil - 1)) == 0)
        m = m + ok.astype(F32)
    return jnp.where(m > 0, jnp.log(jnp.maximum(m, 1.0)), NEG)


def _attn_params():
    return pltpu.CompilerParams(dimension_semantics=("parallel", "parallel", "arbitrary"), vmem_limit_bytes=VMEM_LIMIT)


def _scores(q_ref, k_ref, scale, bias_ref):
    s = lax.dot_general(q_ref[...], k_ref[...], NT, preferred_element_type=F32) * scale
    return s if bias_ref is None else s + bias_ref[...]


def _flash_fwd(name, q, k, v, H, G, dk, dv, scale, bias=None, W=None):
    S = q.shape[0]
    band = bias is not None
    tq = bias.shape[1] if band else _pick(S, ATT_TQ, LANES)
    tk = bias.shape[2] if band else _pick(S, ATT_TK_FWD, LANES)
    n = S // tk
    nsteps = 2 * W + 1 if band else n
    R = tk // tq

    def kblock(qi, st):
        return jnp.clip(qi // R - W + st, 0, n - 1) if band else st

    def body(*refs):
        q_ref, k_ref, v_ref = refs[:3]
        bias_ref = refs[3] if band else None
        o_ref, lse_ref, m_sc, l_sc, acc_sc = refs[-5:]
        qi, st = pl.program_id(1), pl.program_id(2)

        @pl.when(st == 0)
        def _():
            m_sc[...] = jnp.full_like(m_sc, NEG)
            l_sc[...] = jnp.zeros_like(l_sc)
            acc_sc[...] = jnp.zeros_like(acc_sc)

        kj = qi // R - W + st if band else st

        def step():
            s = _scores(q_ref, k_ref, scale, bias_ref)
            m_prev = m_sc[...]
            m_new = jnp.maximum(m_prev, jnp.max(s, axis=-1, keepdims=True))
            alpha = jnp.exp(m_prev - m_new)
            p = jnp.exp(s - m_new)
            l_sc[...] = alpha * l_sc[...] + jnp.sum(p, axis=-1, keepdims=True)
            acc_sc[...] = alpha * acc_sc[...] + lax.dot_general(p.astype(MM_DT), v_ref[...], NN, preferred_element_type=F32)
            m_sc[...] = m_new

        if band:
            pl.when((kj >= 0) & (kj < n))(step)
        else:
            step()

        @pl.when(st == nsteps - 1)
        def _():
            l = l_sc[...]
            o_ref[...] = acc_sc[...] / l
            lse_ref[...] = jnp.broadcast_to(m_sc[...] + jnp.log(l), lse_ref.shape)

    in_specs = [pl.BlockSpec((tq, dk), lambda h, qi, st: (qi, h)),
                pl.BlockSpec((tk, dk), lambda h, qi, st: (kblock(qi, st), h // G)),
                pl.BlockSpec((tk, dv), lambda h, qi, st: (kblock(qi, st), h // G))]
    ops = [q, k, v]
    if band:
        in_specs.append(pl.BlockSpec((None, tq, tk), lambda h, qi, st: ((qi % R) * nsteps + st, 0, 0)))
        ops.append(bias)
    return _pcall(body, name=name, grid=(H, S // tq, nsteps), in_specs=in_specs,
                  out_specs=(pl.BlockSpec((tq, dv), lambda h, qi, st: (qi, h)),
                             pl.BlockSpec((None, tq, LANES), lambda h, qi, st: (h, qi, 0))),
                  out_shape=(jax.ShapeDtypeStruct((S, H * dv), F32), jax.ShapeDtypeStruct((H, S, LANES), F32)),
                  scratch_shapes=[pltpu.VMEM((tq, 1), F32), pltpu.VMEM((tq, 1), F32), pltpu.VMEM((tq, dv), F32)],
                  compiler_params=_attn_params())(*ops)


def _flash_dq(name, q, k, v, do, o, lse, H, G, dk, dv, scale, bias=None, W=None):
    S = q.shape[0]
    band = bias is not None
    tq = bias.shape[1] if band else _pick(S, ATT_TQ, LANES)
    tk = bias.shape[2] if band else _pick(S, ATT_TK, LANES)
    n = S // tk
    nsteps = 2 * W + 1 if band else n
    R = tk // tq

    def kblock(qi, st):
        return jnp.clip(qi // R - W + st, 0, n - 1) if band else st

    def body(*refs):
        q_ref, k_ref, v_ref, do_ref, o_ref, lse_ref = refs[:6]
        bias_ref = refs[6] if band else None
        dq_ref, delta_sc, acc_sc = refs[-3:]
        qi, st = pl.program_id(1), pl.program_id(2)

        @pl.when(st == 0)
        def _():
            delta_sc[...] = jnp.sum(do_ref[...] * o_ref[...], axis=-1, keepdims=True)
            acc_sc[...] = jnp.zeros_like(acc_sc)

        kj = qi // R - W + st if band else st

        def step():
            p = jnp.exp(_scores(q_ref, k_ref, scale, bias_ref) - lse_ref[:, 0:1])
            dp = lax.dot_general(do_ref[...].astype(MM_DT), v_ref[...], NT, preferred_element_type=F32)
            ds = p * (dp - delta_sc[...]) * scale
            acc_sc[...] += lax.dot_general(ds.astype(MM_DT), k_ref[...], NN, preferred_element_type=F32)

        if band:
            pl.when((kj >= 0) & (kj < n))(step)
        else:
            step()

        @pl.when(st == nsteps - 1)
        def _():
            dq_ref[...] = acc_sc[...]

    qspec = lambda wd: pl.BlockSpec((tq, wd), lambda h, qi, st: (qi, h))
    in_specs = [qspec(dk),
                pl.BlockSpec((tk, dk), lambda h, qi, st: (kblock(qi, st), h // G)),
                pl.BlockSpec((tk, dv), lambda h, qi, st: (kblock(qi, st), h // G)),
                qspec(dv), qspec(dv),
                pl.BlockSpec((None, tq, LANES), lambda h, qi, st: (h, qi, 0))]
    ops = [q, k, v, do, o, lse]
    if band:
        in_specs.append(pl.BlockSpec((None, tq, tk), lambda h, qi, st: ((qi % R) * nsteps + st, 0, 0)))
        ops.append(bias)
    return _pcall(body, name=name, grid=(H, S // tq, nsteps), in_specs=in_specs,
                  out_specs=qspec(dk), out_shape=jax.ShapeDtypeStruct((S, H * dk), F32),
                  scratch_shapes=[pltpu.VMEM((tq, 1), F32), pltpu.VMEM((tq, dk), F32)],
                  compiler_params=_attn_params())(*ops)


def _flash_dkv(name, q, k, v, do, o, lse, H, G, dk, dv, scale, bias=None, W=None):
    S = q.shape[0]
    band = bias is not None
    tq = bias.shape[1] if band else _pick(S, ATT_TQ, LANES)
    tk = bias.shape[2] if band else _pick(S, ATT_TK, LANES)
    n = S // tq
    R = tk // tq
    nq = R * (2 * W + 1) if band else n
    nsteps = G * nq
    Hkv = H // G

    def qhead(hk, st):
        return hk * G + st // nq

    def qblock(kj, st):
        return jnp.clip(R * (kj - W) + st % nq, 0, n - 1) if band else st % nq

    def body(*refs):
        q_ref, k_ref, v_ref, do_ref, o_ref, lse_ref = refs[:6]
        bias_ref = refs[6] if band else None
        dk_ref, dv_ref, dk_sc, dv_sc = refs[-4:]
        kj, st = pl.program_id(1), pl.program_id(2)

        @pl.when(st == 0)
        def _():
            dk_sc[...] = jnp.zeros_like(dk_sc)
            dv_sc[...] = jnp.zeros_like(dv_sc)

        qi = R * (kj - W) + st % nq if band else st % nq

        def step():
            p = jnp.exp(_scores(q_ref, k_ref, scale, bias_ref) - lse_ref[:, 0:1])
            dof = do_ref[...]
            dob = dof.astype(MM_DT)
            dv_sc[...] += lax.dot_general(p.astype(MM_DT), dob, TN, preferred_element_type=F32)
            dp = lax.dot_general(dob, v_ref[...], NT, preferred_element_type=F32)
            delta = jnp.sum(dof * o_ref[...], axis=-1, keepdims=True)
            ds = p * (dp - delta) * scale
            dk_sc[...] += lax.dot_general(ds.astype(MM_DT), q_ref[...], TN, preferred_element_type=F32)

        if band:
            pl.when((qi >= 0) & (qi < n))(step)
        else:
            step()

        @pl.when(st == nsteps - 1)
        def _():
            dk_ref[...] = dk_sc[...]
            dv_ref[...] = dv_sc[...]

    qspec = lambda wd: pl.BlockSpec((tq, wd), lambda hk, kj, st: (qblock(kj, st), qhead(hk, st)))
    kspec = lambda wd: pl.BlockSpec((tk, wd), lambda hk, kj, st: (kj, hk))
    in_specs = [qspec(dk), kspec(dk), kspec(dv), qspec(dv), qspec(dv),
                pl.BlockSpec((None, tq, LANES), lambda hk, kj, st: (qhead(hk, st), qblock(kj, st), 0))]
    ops = [q, k, v, do, o, lse]
    if band:
        in_specs.append(pl.BlockSpec((None, tq, tk),
                                     lambda hk, kj, st: ((st % nq % R) * (2 * W + 1) + 2 * W - (st % nq) // R, 0, 0)))
        ops.append(bias)
    return _pcall(body, name=name, grid=(Hkv, S // tk, nsteps), in_specs=in_specs,
                  out_specs=(kspec(dk), kspec(dv)),
                  out_shape=(jax.ShapeDtypeStruct((S, Hkv * dk), F32), jax.ShapeDtypeStruct((S, Hkv * dv), F32)),
                  scratch_shapes=[pltpu.VMEM((tk, dk), F32), pltpu.VMEM((tk, dv), F32)],
                  compiler_params=_attn_params())(*ops)


def _rowtile(rows, cols):
    return _pick(rows, max(16, (512 * 1024) // cols // 16 * 16), 16)


def _cast_rows(name, w, dtype):
    R, C = w.shape
    tr = _rowtile(R, C)

    def body(w_ref, o_ref):
        o_ref[...] = w_ref[...].astype(o_ref.dtype)

    spec = pl.BlockSpec((tr, C), lambda i: (i, 0))
    return _pcall(body, name=name, grid=(R // tr,), in_specs=[spec], out_specs=spec,
                  out_shape=jax.ShapeDtypeStruct((R, C), dtype), compiler_params=_row_params())(w)


def _add_sibling(name, gw, ra, c_arr, hd):
    ns, depth, Ks, Ns = gw.shape
    rows = hd * Ks
    tr = _rowtile(rows, Ns)
    gw_v = gw.reshape(ns, 2, rows, Ns)
    ra_v = ra.reshape(ns, rows, Ns)

    def body(c_ref, g_ref, r_ref, o_ref):
        o_ref[...] = (g_ref[...].astype(F32) + r_ref[...].astype(F32)).astype(o_ref.dtype)

    grid_spec = pltpu.PrefetchScalarGridSpec(
        num_scalar_prefetch=1, grid=(ns, rows // tr),
        in_specs=[pl.BlockSpec((None, None, tr, Ns), lambda s, r, c_ref: (s, c_ref[0], r, 0)),
                  pl.BlockSpec((None, tr, Ns), lambda s, r, c_ref: (s, r, 0))],
        out_specs=pl.BlockSpec((None, tr, Ns), lambda s, r, c_ref: (s, r, 0)))
    return _pcall(body, name=name, grid_spec=grid_spec, out_shape=jax.ShapeDtypeStruct((ns, rows, Ns), MM_DT),
                  compiler_params=_grid2_params())(c_arr, gw_v, ra_v)


def _add_chips(name, p, rb, me_arr):
    ns, rows, Ns = p.shape
    tr = _rowtile(rows, Ns)

    def body(me_ref, p_ref, b0_ref, b1_ref, b2_ref, o_ref):
        o_ref[...] = ((p_ref[...].astype(F32) + b0_ref[...].astype(F32)) + b1_ref[...].astype(F32)) + b2_ref[...].astype(F32)

    grid_spec = pltpu.PrefetchScalarGridSpec(
        num_scalar_prefetch=1, grid=(rows // tr,),
        in_specs=[pl.BlockSpec((None, tr, Ns), lambda r, me_ref: (me_ref[0], r, 0))] +
                 [pl.BlockSpec((None, tr, Ns), functools.partial(lambda r, me_ref, j: (j, r, 0), j=j)) for j in range(3)],
        out_specs=pl.BlockSpec((None, tr, Ns), lambda r, me_ref: (0, r, 0)))
    return _pcall(body, name=name, grid_spec=grid_spec, out_shape=jax.ShapeDtypeStruct((2, rows, Ns), F32),
                  compiler_params=_row_params())(me_arr, p, rb, rb, rb)


def _adamw_math(wv, gv, mv, vv):
    bc1 = 1.0 - ADAM_B1 ** ADAM_STEP
    bc2 = 1.0 - ADAM_B2 ** ADAM_STEP
    mn = ADAM_B1 * mv + (1.0 - ADAM_B1) * gv
    vn = ADAM_B2 * vv + (1.0 - ADAM_B2) * jnp.square(gv)
    m_hat = mn / bc1
    v_hat = vn / bc2
    return -ADAM_LR * (m_hat / (jnp.sqrt(v_hat) + ADAM_EPS) + ADAM_WD * wv), mn, vn


def _adamw_halves(name, w, g2, m, v, c_arr):
    _, R, C = w.shape
    tr = _rowtile(R, C)

    def body(c_ref, w_ref, g_ref, m_ref, v_ref, go_ref, d_ref, nm_ref, nv_ref):
        gv = g_ref[...]
        go_ref[...] = gv
        d_ref[...], nm_ref[...], nv_ref[...] = _adamw_math(w_ref[...], gv, m_ref[...], v_ref[...])

    spec = pl.BlockSpec((None, tr, C), lambda h, r, c_ref: (h, r, 0))
    gspec = pl.BlockSpec((None, tr, C), lambda h, r, c_ref: ((h + c_ref[0]) % 2, r, 0))
    sds = jax.ShapeDtypeStruct(w.shape, F32)
    grid_spec = pltpu.PrefetchScalarGridSpec(num_scalar_prefetch=1, grid=(2, R // tr), in_specs=[spec, gspec, spec, spec],
                                             out_specs=(spec,) * 4)
    return _pcall(body, name=name, grid_spec=grid_spec, out_shape=(sds,) * 4, compiler_params=_grid2_params())(c_arr, w, g2, m, v)


def _adamw(name, w, g, m, v):
    R, C = w.shape
    tr = _rowtile(R, C)

    def body(w_ref, g_ref, m_ref, v_ref, d_ref, nm_ref, nv_ref):
        d_ref[...], nm_ref[...], nv_ref[...] = _adamw_math(w_ref[...], g_ref[...], m_ref[...], v_ref[...])

    spec = pl.BlockSpec((tr, C), lambda i: (i, 0))
    sds = jax.ShapeDtypeStruct((R, C), F32)
    return _pcall(body, name=name, grid=(R // tr,), in_specs=[spec] * 4, out_specs=(spec,) * 3,
                  out_shape=(sds, sds, sds), compiler_params=_row_params())(w, g, m, v)


HBM_SPEC = pl.BlockSpec(memory_space=pltpu.HBM)


def _place():
    x, y, c = lax.axis_index("x"), lax.axis_index("y"), lax.axis_index("c")
    chips = [(1 - x, y), (x, 1 - y), (1 - x, 1 - y)]
    return x, y, c, chips


def _allgather_weights(shards, hd):
    n = len(shards)

    def body(*refs):
        ins, outs = refs[:n], refs[n:2 * n]
        send, recv = refs[2 * n:]
        x, y, c, chips = _place()
        me = 2 * x + y
        sib = (x, y, 1 - c)

        def rcopy(src, dst, k, to):
            return pltpu.make_async_remote_copy(src_ref=src, dst_ref=dst, send_sem=send.at[k], recv_sem=recv.at[k],
                                                device_id=to, device_id_type=MESH)

        sends = []
        for t in range(n):
            for j, chip in enumerate(chips):
                cp = rcopy(ins[t].at[pl.ds(c * hd, hd)], outs[t].at[me, pl.ds(c * hd, hd)], 7 * t + j, (chip[0], chip[1], c))
                cp.start()
                sends.append(cp)
        for t in range(n):
            cp = rcopy(ins[t], outs[t].at[me], 7 * t + 6, sib)
            cp.start()
            sends.append(cp)
        for t in range(n):
            for j, chip in enumerate(chips):
                slab = outs[t].at[2 * chip[0] + chip[1], pl.ds(c * hd, hd)]
                rcopy(slab, slab, 7 * t + j, (chip[0], chip[1], c)).wait_recv()
                fw = rcopy(slab, slab, 7 * t + 3 + j, sib)
                fw.start()
                sends.append(fw)
        for t in range(n):
            rcopy(ins[t], outs[t].at[me], 7 * t + 6, sib).wait_recv()
            for j, chip in enumerate(chips):
                slab = outs[t].at[2 * chip[0] + chip[1], pl.ds((1 - c) * hd, hd)]
                rcopy(slab, slab, 7 * t + 3 + j, sib).wait_recv()
        for cp in sends:
            cp.wait_send()

    return _pcall(body, name="allgather_weights", in_specs=[HBM_SPEC] * n, out_specs=tuple([HBM_SPEC] * n),
                  out_shape=tuple(jax.ShapeDtypeStruct((N_SHARD,) + s.shape, s.dtype) for s in shards),
                  scratch_shapes=[pltpu.SemaphoreType.DMA((7 * n,)), pltpu.SemaphoreType.DMA((7 * n,))])(*shards)


def _exchange_sibling_halves(gws, hd):
    n = len(gws)

    def body(*refs):
        ins, outs = refs[:n], refs[n:2 * n]
        send, recv = refs[2 * n:]
        x, y, c, _ = _place()
        cps = []
        for t in range(n):
            cp = pltpu.make_async_remote_copy(src_ref=ins[t].at[:, pl.ds((1 - c) * hd, hd)], dst_ref=outs[t],
                                              send_sem=send.at[t], recv_sem=recv.at[t],
                                              device_id=(x, y, 1 - c), device_id_type=MESH)
            cp.start()
            cps.append(cp)
        for cp in cps:
            cp.wait()

    return _pcall(body, name="rs_sibling_halves", in_specs=[HBM_SPEC] * n, out_specs=tuple([HBM_SPEC] * n),
                  out_shape=tuple(jax.ShapeDtypeStruct((g.shape[0], hd) + g.shape[2:], g.dtype) for g in gws),
                  scratch_shapes=[pltpu.SemaphoreType.DMA((n,)), pltpu.SemaphoreType.DMA((n,))])(*gws)


def _exchange_chips(ps):
    n = len(ps)

    def body(*refs):
        ins, outs = refs[:n], refs[n:2 * n]
        send, recv = refs[2 * n:]
        x, y, c, chips = _place()
        cps = []
        for t in range(n):
            for j, chip in enumerate(chips):
                cp = pltpu.make_async_remote_copy(src_ref=ins[t].at[2 * chip[0] + chip[1]], dst_ref=outs[t].at[j],
                                                  send_sem=send.at[3 * t + j], recv_sem=recv.at[3 * t + j],
                                                  device_id=(chip[0], chip[1], c), device_id_type=MESH)
                cp.start()
                cps.append(cp)
        for cp in cps:
            cp.wait()

    return _pcall(body, name="rs_chip_exchange", in_specs=[HBM_SPEC] * n, out_specs=tuple([HBM_SPEC] * n),
                  out_shape=tuple(jax.ShapeDtypeStruct((3,) + p.shape[1:], p.dtype) for p in ps),
                  scratch_shapes=[pltpu.SemaphoreType.DMA((3 * n,)), pltpu.SemaphoreType.DMA((3 * n,))])(*ps)


def _share_reduced(gs):
    n = len(gs)

    def body(*refs):
        ins, outs = refs[:n], refs[n:2 * n]
        send, recv = refs[2 * n:]
        x, y, c, _ = _place()
        cps = []
        for t in range(n):
            cp = pltpu.make_async_remote_copy(src_ref=ins[t].at[0], dst_ref=outs[t].at[1], send_sem=send.at[t], recv_sem=recv.at[t],
                                              device_id=(x, y, 1 - c), device_id_type=MESH)
            cp.start()
            cps.append(cp)
        for cp in cps:
            cp.wait()

    return _pcall(body, name="rs_share_reduced", in_specs=[HBM_SPEC] * n, out_specs=tuple([HBM_SPEC] * n),
                  out_shape=tuple(jax.ShapeDtypeStruct(g.shape, g.dtype) for g in gs),
                  input_output_aliases={t: t for t in range(n)},
                  scratch_shapes=[pltpu.SemaphoreType.DMA((n,)), pltpu.SemaphoreType.DMA((n,))])(*gs)


def _allreduce_small(vec):
    R = vec.shape[0]

    def body(v_ref, o_ref, buf, send, recv):
        x, y, c, _ = _place()
        me = 4 * x + 2 * y + c
        buf[me] = v_ref[...]
        cps = []
        for r in range(1, 8):
            fx, fy, fc = (r >> 2) & 1, (r >> 1) & 1, r & 1
            to = (1 - x if fx else x, 1 - y if fy else y, 1 - c if fc else c)
            cp = pltpu.make_async_remote_copy(src_ref=v_ref, dst_ref=buf.at[me], send_sem=send.at[r - 1], recv_sem=recv.at[r - 1],
                                              device_id=to, device_id_type=MESH)
            cp.start()
            cps.append(cp)
        for r in range(1, 8):
            fx, fy, fc = (r >> 2) & 1, (r >> 1) & 1, r & 1
            frm = (1 - x if fx else x, 1 - y if fy else y, 1 - c if fc else c)
            src = 4 * frm[0] + 2 * frm[1] + frm[2]
            pltpu.make_async_remote_copy(src_ref=v_ref, dst_ref=buf.at[src], send_sem=send.at[r - 1], recv_sem=recv.at[r - 1],
                                         device_id=frm, device_id_type=MESH).wait_recv()
        for cp in cps:
            cp.wait_send()
        acc = buf[0]
        for i in range(1, 8):
            acc = acc + buf[i]
        o_ref[...] = acc

    vm = pl.BlockSpec(memory_space=pltpu.VMEM)
    return _pcall(body, name="allreduce_small", in_specs=[vm], out_specs=vm, out_shape=jax.ShapeDtypeStruct((R, LANES), F32),
                  scratch_shapes=[pltpu.VMEM((8, R, LANES), F32), pltpu.SemaphoreType.DMA((7,)), pltpu.SemaphoreType.DMA((7,))])(vec)


def _unshard_cols(wg):
    ns, depth, K, Ns = wg.shape
    return jnp.moveaxis(wg, 0, 2).reshape(depth, K, ns * Ns)


def _shard_cols(w):
    K, N = w.shape
    return jnp.moveaxis(w.reshape(K, N_SHARD, N // N_SHARD), 1, 0)


def _uq_padded(cfg, wuq_g):
    w = _unshard_cols(wuq_g).reshape(cfg.DEPTH, cfg.QL, cfg.AH, 192)
    w = jnp.pad(w, ((0, 0), (0, cfg.QLP - cfg.QL), (0, 0), (0, 64)))
    return w.reshape(1, cfg.DEPTH, cfg.QLP, cfg.AH * 256)


def _uq_grad_unpadded(cfg, dw):
    w = dw[:cfg.QL].reshape(cfg.QL, cfg.AH, 256)[:, :, :192].reshape(cfg.QL, cfg.UQ)
    return _shard_cols(w)


def _ukv_padded(cfg, wukv_g):
    w = _unshard_cols(wukv_g)
    w = jnp.pad(w, ((0, 0), (cfg.KOFF, cfg.KVW - cfg.KVL - cfg.KOFF), (0, 0)))
    return w[None]


def _ukv_grad_unpadded(cfg, dw):
    return _shard_cols(dw[cfg.KOFF:cfg.KOFF + cfg.KVL])


def _pad_lanes(v, lo, total):
    return jnp.pad(v, (lo, total - lo - v.shape[0]))[None]


def _layer_fwd(cfg, l, x, W, small, tabs):
    ln1, gq, gkv, gqn, gkn, gout, ln2 = small
    sc_a, sc_h = 1.0 / math.sqrt(192), 1.0 / math.sqrt(128)
    n = f"l{l}_"
    h = _rms_fwd(n + "ln1", x, ln1)
    proj = _mm_nn(n + "proj", h, (W["w_in"], "col", l))
    cqn, ckvn, kpe = _mla_prep_fwd(n + "mla_prep", cfg, proj, gq, gkv, tabs["akr"])
    qa = _mm_nn(n + "uq", cqn, (W["uq_p"], "col", l))
    kva = _mm_nn(n + "ukv", ckvn, (W["ukv_p"], "col", l))
    q_a, k_a, v_a = _mla_build_fwd(n + "mla_build", cfg, qa, kva, kpe, tabs["aq"])
    o_a, lse_a = _flash_fwd(n + "attn_a", q_a, k_a, v_a, cfg.AH, 1, 256, 128, sc_a)
    q_b = _headprep_fwd(n + "bq", proj, cfg.o_bq, cfg.BH, gqn, tabs["b"])
    k_b = _headprep_fwd(n + "bk", proj, cfg.o_bk, cfg.BKV, gkn, tabs["b"])
    v_b = _headprep_fwd(n + "bv", proj, cfg.o_bv, cfg.BKV, None, None)
    o_b, lse_b = _flash_fwd(n + "attn_b", q_b, k_b, v_b, cfg.BH, cfg.G, 128, 128, sc_h)
    q_c = _headprep_fwd(n + "cq", proj, cfg.o_cq, cfg.CH, None, tabs["c"])
    k_c = _headprep_fwd(n + "ck", proj, cfg.o_ck, cfg.CH, None, tabs["c"])
    v_c = _headprep_fwd(n + "cv", proj, cfg.o_cv, cfg.CH, None, None)
    o_c, lse_c = _flash_fwd(n + "attn_c", q_c, k_c, v_c, cfg.CH, 1, 128, 128, sc_h, tabs["bias_c"], cfg.W)
    mixed = _outnorm_fwd(n + "outnorm", cfg, o_a, o_b, o_c, gout)
    x1 = _mm_nn(n + "out", mixed, (W["w_out"], "row", l), epi=_epi_residual, extra=x)
    h2 = _rms_fwd(n + "ln2", x1, ln2)
    a, u = _mm_nn(n + "ff1", h2, (W["w_ff1"], "col", l), epi=_epi_relu2, out_dtypes=(MM_DT, MM_DT))
    x2 = _mm_nn(n + "ff2", u, (W["w_ff2"], "row", l), epi=_epi_residual, extra=x1)
    saved = dict(x=x, h=h, proj=proj, cqn=cqn, ckvn=ckvn, q_a=q_a, k_a=k_a, v_a=v_a, o_a=o_a, lse_a=lse_a,
                 q_b=q_b, k_b=k_b, v_b=v_b, o_b=o_b, lse_b=lse_b, q_c=q_c, k_c=k_c, v_c=v_c, o_c=o_c, lse_c=lse_c,
                 mixed=mixed, x1=x1, h2=h2, a=a, u=u)
    return x2, saved


def _layer_bwd(cfg, l, dx2, sv, W, small, tabs, GW):
    ln1, gq, gkv, gqn, gkn, gout, ln2 = small
    sc_a, sc_h = 1.0 / math.sqrt(192), 1.0 / math.sqrt(128)
    n = f"l{l}_b_"
    S = cfg.S
    mats = {m[0]: m for m in cfg.mats}

    def dw(name, a, g, key):
        _, Rs, Cs, kind = mats[key]
        GW[key] = _mm_tn(n + name, a, g, kind, l, Rs, Cs, GW[key], out_dtype=MM_DT)

    da = _mm_nt(n + "ff2_dx", dx2, (W["w_ff2"], "row", l), epi=_epi_drelu2, out_dtype=MM_DT, extra=sv["a"])
    dw("ff2_dw", sv["u"], dx2, "w_ff2")
    dh2 = _mm_nt(n + "ff1_dx", da, (W["w_ff1"], "col", l))
    dw("ff1_dw", sv["h2"], da, "w_ff1")
    dx1, dln2 = _rms_bwd(n + "ln2", sv["x1"], ln2, dh2, dx2)
    dmix = _mm_nt(n + "out_dx", dx1, (W["w_out"], "row", l))
    dw("out_dw", sv["mixed"], dx1, "w_out")
    do_a, do_b, do_c, dgout = _outnorm_bwd(n + "outnorm", cfg, dmix, sv["o_a"], sv["o_b"], sv["o_c"], gout)
    dproj = jnp.zeros((S, cfg.IN), MM_DT)
    args_c = (sv["q_c"], sv["k_c"], sv["v_c"], do_c, sv["o_c"], sv["lse_c"], cfg.CH, 1, 128, 128, sc_h, tabs["bias_c"], cfg.W)
    dq_c = _flash_dq(n + "attn_c_dq", *args_c)
    dk_c, dv_c = _flash_dkv(n + "attn_c_dkv", *args_c)
    dproj, _ = _headprep_bwd(n + "cq", dq_c, sv["proj"], cfg.o_cq, cfg.CH, None, tabs["c"], dproj)
    dproj, _ = _headprep_bwd(n + "ck", dk_c, sv["proj"], cfg.o_ck, cfg.CH, None, tabs["c"], dproj)
    dproj, _ = _headprep_bwd(n + "cv", dv_c, sv["proj"], cfg.o_cv, cfg.CH, None, None, dproj)
    args_b = (sv["q_b"], sv["k_b"], sv["v_b"], do_b, sv["o_b"], sv["lse_b"], cfg.BH, cfg.G, 128, 128, sc_h)
    dq_b = _flash_dq(n + "attn_b_dq", *args_b)
    dk_b, dv_b = _flash_dkv(n + "attn_b_dkv", *args_b)
    dproj, dgqn = _headprep_bwd(n + "bq", dq_b, sv["proj"], cfg.o_bq, cfg.BH, gqn, tabs["b"], dproj)
    dproj, dgkn = _headprep_bwd(n + "bk", dk_b, sv["proj"], cfg.o_bk, cfg.BKV, gkn, tabs["b"], dproj)
    dproj, _ = _headprep_bwd(n + "bv", dv_b, sv["proj"], cfg.o_bv, cfg.BKV, None, None, dproj)
    args_a = (sv["q_a"], sv["k_a"], sv["v_a"], do_a, sv["o_a"], sv["lse_a"], cfg.AH, 1, 256, 128, sc_a)
    dq_a = _flash_dq(n + "attn_a_dq", *args_a)
    dk_a, dv_a = _flash_dkv(n + "attn_a_dkv", *args_a)
    dqa, dkva, dkpe = _mla_build_bwd(n + "mla_build", cfg, dq_a, dk_a, dv_a, tabs["aq"])
    dcq = _mm_nt(n + "uq_dx", dqa, (W["uq_p"], "col", l))
    dwuq = _mm_tn(n + "uq_dw", sv["cqn"], dqa, "col", 0, cfg.QLP, cfg.AH * 256, (1, 1, cfg.QLP, cfg.AH * 256))
    dckv = _mm_nt(n + "ukv_dx", dkva, (W["ukv_p"], "col", l))
    dwukv = _mm_tn(n + "ukv_dw", sv["ckvn"], dkva, "col", 0, cfg.KVW, cfg.AH * 256, (1, 1, cfg.KVW, cfg.AH * 256))
    dproj, dgq, dgkv = _mla_prep_bwd(n + "mla_prep", cfg, dcq, dckv, dkpe, sv["proj"], gq, gkv, tabs["akr"], dproj)
    dh = _mm_nt(n + "proj_dx", dproj, (W["w_in"], "col", l))
    dw("proj_dw", sv["h"], dproj, "w_in")
    dx, dln1 = _rms_bwd(n + "ln1", sv["x"], ln1, dh, dx1)
    gains = dict(ln1_g=dln1[0], g_q_a=dgq[0, :cfg.QL], g_kv_a=dgkv[0, cfg.KOFF:cfg.KOFF + cfg.KVL], g_qn_b=dgqn[0],
                 g_kn_b=dgkn[0], g_out=dgout[0], ln2_g=dln2[0])
    return dx, gains, _uq_grad_unpadded(cfg, dwuq[0, 0]), _ukv_grad_unpadded(cfg, dwukv[0, 0])


SMALL_NAMES = ("ln1_g", "g_q_a", "g_kv_a", "g_qn_b", "g_kn_b", "g_out", "ln2_g")
MAT_NAMES = ("w_in", "w_uq", "w_ukv", "w_out", "w_ff1", "w_ff2")


def _pack_small(cfg, per_layer, final, scalar=None):
    last = jnp.zeros((1,), F32) if scalar is None else scalar.reshape(1)
    flat = jnp.concatenate([per_layer[k].reshape(-1) for k in SMALL_NAMES] + [final.reshape(-1), last])
    total = flat.shape[0]
    rows = _rup(-(-total // LANES), 8)
    return jnp.pad(flat, (0, rows * LANES - total)).reshape(rows, LANES)


def _unpack_small(cfg, packed, shapes):
    flat = packed.reshape(-1)
    out, off = {}, 0
    for k in SMALL_NAMES + ("ln_f_g",):
        n = math.prod(shapes[k])
        out[k] = flat[off:off + n].reshape(shapes[k])
        off += n
    return out, flat[off]


def _step(cfg, w, m, v, x, tgt):
    DEPTH, hd = cfg.DEPTH, cfg.HD
    c = lax.axis_index("c")
    me_chip = 2 * lax.axis_index("x") + lax.axis_index("y")
    c_arr = jnp.reshape(c, (1,)).astype(jnp.int32)
    me_arr = jnp.reshape(me_chip, (1,)).astype(jnp.int32)
    mats = {mt[0]: mt for mt in cfg.mats}

    shards = []
    for name in MAT_NAMES:
        _, Ks, Ns, _ = mats[name]
        shards.append(_cast_rows("cast_" + name, w[name].reshape(DEPTH * Ks, Ns), MM_DT).reshape(DEPTH, Ks, Ns))
    gathered = dict(zip(MAT_NAMES, _allgather_weights(shards, hd)))
    W = dict(w_in=gathered["w_in"], w_out=gathered["w_out"], w_ff1=gathered["w_ff1"], w_ff2=gathered["w_ff2"],
             uq_p=_uq_padded(cfg, gathered["w_uq"]), ukv_p=_ukv_padded(cfg, gathered["w_ukv"]))
    tabs = _all_tables(cfg)
    tabs["bias_c"] = _band_bias(cfg)

    def small_of(l):
        return (w["ln1_g"][l][None], _pad_lanes(w["g_q_a"][l], 0, cfg.QLP), _pad_lanes(w["g_kv_a"][l], cfg.KOFF, cfg.KVW),
                w["g_qn_b"][l][None], w["g_kn_b"][l][None], w["g_out"][l][None], w["ln2_g"][l][None])

    saved = []
    xc = x
    for l in range(DEPTH):
        xc, sv = _layer_fwd(cfg, l, xc, W, small_of(l), tabs)
        saved.append(sv)
    dx, dlnf, loss_rows = _final_loss("final_loss", xc, w["ln_f_g"][None], tgt)
    loss = loss_rows[0, 0]
    GW = {name: (N_SHARD, DEPTH, mats[name][1], mats[name][2]) for name in ("w_in", "w_out", "w_ff1", "w_ff2")}
    gain_rows = [None] * DEPTH
    duq, dukv = [None] * DEPTH, [None] * DEPTH
    for l in reversed(range(DEPTH)):
        dx, gain_rows[l], duq[l], dukv[l] = _layer_bwd(cfg, l, dx, saved[l], W, small_of(l), tabs, GW)
    GW["w_uq"] = jnp.stack(duq, axis=1).astype(MM_DT)
    GW["w_ukv"] = jnp.stack(dukv, axis=1).astype(MM_DT)

    gws = [GW[name] for name in MAT_NAMES]
    ras = _exchange_sibling_halves(gws, hd)
    ps = [_add_sibling("rs_add_sib_" + name, g, r, c_arr, hd) for name, g, r in zip(MAT_NAMES, gws, ras)]
    rbs = _exchange_chips(ps)
    rs = [_add_chips("rs_add_chips_" + name, p, rb, me_arr) for name, p, rb in zip(MAT_NAMES, ps, rbs)]
    full = _share_reduced(rs)
    grad, delta, new_m, new_v = {}, {}, {}, {}
    for name, g2 in zip(MAT_NAMES, full):
        _, Ks, Ns, _ = mats[name]
        halves, shp = (2, hd * Ks, Ns), (DEPTH, Ks, Ns)
        res = _adamw_halves("adamw_" + name, w[name].reshape(halves), g2, m[name].reshape(halves), v[name].reshape(halves), c_arr)
        grad[name], delta[name], new_m[name], new_v[name] = (r.reshape(shp) for r in res)

    per_layer = {k: jnp.stack([gain_rows[l][k] for l in range(DEPTH)]) for k in SMALL_NAMES}
    shapes = {k: w[k].shape for k in SMALL_NAMES + ("ln_f_g",)}
    gsum = _allreduce_small(_pack_small(cfg, per_layer, dlnf[0], loss))
    pk = lambda d: _pack_small(cfg, {k: d[k] for k in SMALL_NAMES}, d["ln_f_g"])
    d_s, m_s, v_s = _adamw("adamw_small", pk(w), gsum, pk(m), pk(v))
    for res, packed in ((grad, gsum), (delta, d_s), (new_m, m_s), (new_v, v_s)):
        res.update(_unpack_small(cfg, packed, shapes)[0])
    loss_total = _unpack_small(cfg, gsum, shapes)[1]
    return loss_total, dx, grad, delta, new_m, new_v


WEIGHT_NAMES = ("ln1_g", "w_in", "g_q_a", "w_uq", "g_kv_a", "w_ukv", "g_qn_b", "g_kn_b", "g_out", "w_out", "ln2_g",
                "w_ff1", "w_ff2", "ln_f_g")


def _run(cfg, args):
    nw = len(WEIGHT_NAMES)
    x, tgt = args[0], args[1 + nw]
    w = dict(zip(WEIGHT_NAMES, args[1:1 + nw]))
    m = dict(zip(WEIGHT_NAMES, args[2 + nw:2 + 2 * nw]))
    v = dict(zip(WEIGHT_NAMES, args[2 + 2 * nw:2 + 3 * nw]))
    loss, dx, grad, delta, new_m, new_v = _step(cfg, w, m, v, x.reshape(cfg.S, cfg.D), tgt.reshape(cfg.S, cfg.D))
    return (loss, dx.reshape(x.shape), *[grad[k] for k in WEIGHT_NAMES], *[delta[k] for k in WEIGHT_NAMES],
            *[new_m[k] for k in WEIGHT_NAMES], *[new_v[k] for k in WEIGHT_NAMES])


def kernel(x, ln1_g, w_in, g_q_a, w_uq, g_kv_a, w_ukv, g_qn_b, g_kn_b, g_out, w_out, ln2_g, w_ff1, w_ff2, ln_f_g, loss_target, m_ln1_g, m_w_in, m_g_q_a, m_w_uq, m_g_kv_a, m_w_ukv, m_g_qn_b, m_g_kn_b, m_g_out, m_w_out, m_ln2_g, m_w_ff1, m_w_ff2, m_ln_f_g, v_ln1_g, v_w_in, v_g_q_a, v_w_uq, v_g_kv_a, v_w_ukv, v_g_qn_b, v_g_kn_b, v_g_out, v_w_out, v_ln2_g, v_w_ff1, v_w_ff2, v_ln_f_g):
    return _run(Cfg(), (x, ln1_g, w_in, g_q_a, w_uq, g_kv_a, w_ukv, g_qn_b, g_kn_b, g_out, w_out, ln2_g, w_ff1, w_ff2, ln_f_g, loss_target, m_ln1_g, m_w_in, m_g_q_a, m_w_uq, m_g_kv_a, m_w_ukv, m_g_qn_b, m_g_kn_b, m_g_out, m_w_out, m_ln2_g, m_w_ff1, m_w_ff2, m_ln_f_g, v_ln1_g, v_w_in, v_g_q_a, v_w_uq, v_g_kv_a, v_w_ukv, v_g_qn_b, v_g_kn_b, v_g_out, v_w_out, v_ln2_g, v_w_ff1, v_w_ff2, v_ln_f_g))
```

```python
import functools
import math

import jax
import jax.numpy as jnp
from jax import lax
from jax.experimental import pallas as pl
from jax.experimental.pallas import tpu as pltpu
from jax.experimental.pallas import tpu_sc as plsc

F32 = jnp.float32
MM_DT = jnp.bfloat16
LANES = 128
SUBLANES_F32 = 8
SUBLANES_BF16 = 16
VMEM_LIMIT = 48 * 1024 * 1024
EPS = 1e-6
NEG = -1e30
ROPE_THETA = 10000.0
MM_TK = 1024
ATT_TQ, ATT_TK_FWD, ATT_TK = 512, 4096, 2048
ADAM_LR, ADAM_B1, ADAM_B2, ADAM_EPS, ADAM_WD, ADAM_STEP = 0.001, 0.9, 0.999, 1e-08, 0.01, 10
MESH_AXES = ("x", "y", "c")
N_SHARD = 4
MESH = pl.DeviceIdType.MESH

NN = (((1,), (0,)), ((), ()))
NT = (((1,), (1,)), ((), ()))
TN = (((0,), (0,)), ((), ()))


def _pcall(body, **kw):
    return pl.pallas_call(body, **kw)


def _rup(n, m):
    return -(-n // m) * m


def _pick(n, pref, mult):
    best = None
    for t in range(mult, min(n, pref) + 1, mult):
        if n % t == 0:
            best = t
    return best if best is not None else n


class Cfg:
    def __init__(self, S=4096, D=2048, DEPTH=4, AH=4, QL=448, KVL=512, BH=6, BKV=2, CH=6,
                 BRANCHES=((128, 1), (512, 4), (2048, 16)), DFF=8192, GRID_W=64, TB=512, TBK=1024):
        self.S, self.D, self.DEPTH, self.AH, self.QL, self.KVL = S, D, DEPTH, AH, QL, KVL
        self.BH, self.BKV, self.CH, self.DFF, self.GRID_W, self.TB = BH, BKV, CH, DFF, GRID_W, TB
        self.G = BH // BKV
        self.AW, self.BW, self.CW = AH * 128, BH * 128, CH * 128
        self.MIX = self.AW + self.BW + self.CW
        self.QLP = _rup(QL, LANES)
        self.KV0 = (QL // LANES) * LANES
        self.PW = QL + KVL + 64
        assert self.PW % LANES == 0
        self.KVW = self.PW - self.KV0
        self.KOFF = QL - self.KV0
        self.o_bq = self.PW
        self.o_bk = self.o_bq + self.BW
        self.o_bv = self.o_bk + BKV * 128
        self.o_cq = self.o_bv + BKV * 128
        self.o_ck = self.o_cq + self.CW
        self.o_cv = self.o_ck + self.CW
        self.IN = self.o_cv + self.CW
        self.UQ, self.UKV = AH * 192, AH * 256
        self.branches = tuple(((w // (2 * d)) * d, d) for w, d in BRANCHES)
        for _, d in self.branches:
            assert d & (d - 1) == 0
        self.TBK = TBK
        self.W = -(-max(r for r, _ in self.branches) // TBK)
        assert S % TBK == 0 and TBK % TB == 0 and DEPTH % 2 == 0
        self.HD = DEPTH // 2
        self.mats = (("w_in", D, self.IN // 4, "col"), ("w_uq", QL, self.UQ // 4, "col"),
                     ("w_ukv", KVL, self.UKV // 4, "col"), ("w_out", self.MIX // 4, D, "row"),
                     ("w_ff1", D, DFF // 4, "col"), ("w_ff2", DFF // 4, D, "row"))


def _mm_call(name, mode, operands, in_specs, out_shape, out_specs, grid, acc_shape, epi, n_extra, aliases=None):
    nk = grid[2]

    def body(*refs):
        a_ref, b_ref = refs[0], refs[1]
        ex = refs[2:2 + n_extra]
        outs = refs[2 + n_extra:-1]
        acc = refs[-1]
        k = pl.program_id(2)

        @pl.when(k == 0)
        def _():
            acc[...] = jnp.zeros_like(acc)

        acc[...] += lax.dot_general(a_ref[...].astype(MM_DT), b_ref[...].astype(MM_DT), mode,
                                    preferred_element_type=F32)

        @pl.when(k == nk - 1)
        def _():
            epi(acc[...], ex, outs)

    return _pcall(body, name=name, grid=grid, in_specs=in_specs, out_specs=out_specs, out_shape=out_shape,
                  scratch_shapes=[pltpu.VMEM(acc_shape, F32)], input_output_aliases=aliases or {},
                  compiler_params=pltpu.CompilerParams(dimension_semantics=("parallel", "parallel", "arbitrary"),
                                                       vmem_limit_bytes=VMEM_LIMIT))(*operands)


def _wspec(kind, l, Rs, Cs, br, bc, rfn, cfn):
    assert Rs % br == 0 and Cs % bc == 0
    if kind == "col":
        npc = Cs // bc
        return pl.BlockSpec((None, None, br, bc), lambda i, j, k: (cfn(i, j, k) // npc, l, rfn(i, j, k), cfn(i, j, k) % npc))
    npr = Rs // br
    return pl.BlockSpec((None, None, br, bc), lambda i, j, k: (rfn(i, j, k) // npr, l, rfn(i, j, k) % npr, cfn(i, j, k)))


def _epi_plain(acc, ex, outs):
    outs[0][...] = acc.astype(outs[0].dtype)


def _epi_residual(acc, ex, outs):
    outs[0][...] = ex[0][...] + acc


def _epi_relu2(acc, ex, outs):
    outs[0][...] = acc.astype(outs[0].dtype)
    r = jnp.maximum(acc, 0.0)
    outs[1][...] = (r * r).astype(outs[1].dtype)


def _epi_drelu2(acc, ex, outs):
    a = ex[0][...].astype(F32)
    outs[0][...] = (acc * (2.0 * jnp.maximum(a, 0.0))).astype(outs[0].dtype)


def _wdims(wd):
    Wg, kind, l = wd
    ns, _, Rs, Cs = Wg.shape
    K = Rs * ns if kind == "row" else Rs
    N = Cs * ns if kind == "col" else Cs
    return Wg, kind, l, Rs, Cs, K, N


def _mm_nn(name, a, wd, epi=_epi_plain, out_dtypes=(F32,), extra=None):
    Wg, kind, l, Rs, Cs, K, N = _wdims(wd)
    M = a.shape[0]
    tm, tk, tn = _pick(M, 1024, 16), _pick(Rs, MM_TK, LANES), _pick(Cs, 1152, LANES)
    grid = (M // tm, N // tn, K // tk)
    in_specs = [pl.BlockSpec((tm, tk), lambda i, j, k: (i, k)),
                _wspec(kind, l, Rs, Cs, tk, tn, lambda i, j, k: k, lambda i, j, k: j)]
    ops = [a, Wg]
    if extra is not None:
        in_specs.append(pl.BlockSpec((tm, tn), lambda i, j, k: (i, j)))
        ops.append(extra)
    o_spec = pl.BlockSpec((tm, tn), lambda i, j, k: (i, j))
    outs = tuple(jax.ShapeDtypeStruct((M, N), dt) for dt in out_dtypes)
    res = _mm_call(name, NN, ops, in_specs, outs, tuple(o_spec for _ in outs), grid, (tm, tn), epi,
                   0 if extra is None else 1)
    return res[0] if len(res) == 1 else res


def _mm_nt(name, g, wd, epi=_epi_plain, out_dtype=F32, extra=None):
    Wg, kind, l, Rs, Cs, K, N = _wdims(wd)
    M = g.shape[0]
    tm, tn, tk = _pick(M, 1024, 16), _pick(Rs, 1024, LANES), _pick(Cs, 1152, LANES)
    grid = (M // tm, K // tn, N // tk)
    in_specs = [pl.BlockSpec((tm, tk), lambda i, j, k: (i, k)),
                _wspec(kind, l, Rs, Cs, tn, tk, lambda i, j, k: j, lambda i, j, k: k)]
    ops = [g, Wg]
    if extra is not None:
        in_specs.append(pl.BlockSpec((tm, tn), lambda i, j, k: (i, j)))
        ops.append(extra)
    res = _mm_call(name, NT, ops, in_specs, (jax.ShapeDtypeStruct((M, K), out_dtype),),
                   (pl.BlockSpec((tm, tn), lambda i, j, k: (i, j)),), grid, (tm, tn), epi, 0 if extra is None else 1)
    return res[0]


def _mm_tn(name, a, g, kind, l, Rs, Cs, buf, out_dtype=F32):
    M, K = a.shape
    N = g.shape[1]
    tm, tn, tk = _pick(Rs, 1024, LANES), _pick(Cs, 1152, LANES), _pick(M, MM_TK, LANES)
    grid = (K // tm, N // tn, M // tk)
    in_specs = [pl.BlockSpec((tk, tm), lambda i, j, k: (k, i)),
                pl.BlockSpec((tk, tn), lambda i, j, k: (k, j))]
    o_spec = _wspec(kind, l, Rs, Cs, tm, tn, lambda i, j, k: i, lambda i, j, k: j)
    if isinstance(buf, tuple):
        res = _mm_call(name, TN, [a, g], in_specs, (jax.ShapeDtypeStruct(buf, out_dtype),), (o_spec,), grid, (tm, tn), _epi_plain, 0)
    else:
        res = _mm_call(name, TN, [a, g, buf], in_specs + [pl.BlockSpec(memory_space=pl.ANY)],
                       (jax.ShapeDtypeStruct(buf.shape, buf.dtype),), (o_spec,), grid, (tm, tn), _epi_plain, 1, aliases={2: 0})
    return res[0]


def _row_params():
    return pltpu.CompilerParams(dimension_semantics=("arbitrary",), vmem_limit_bytes=VMEM_LIMIT)


def _rms_fwd(name, x, g):
    S, D = x.shape
    tr = _pick(S, 256, 16)

    def body(x_ref, g_ref, o_ref):
        xv = x_ref[...]
        r = lax.rsqrt(jnp.mean(xv * xv, axis=-1, keepdims=True) + EPS)
        o_ref[...] = (xv * r * g_ref[...]).astype(o_ref.dtype)

    return _pcall(body, name=name, grid=(S // tr,),
                  in_specs=[pl.BlockSpec((tr, D), lambda i: (i, 0)), pl.BlockSpec((1, D), lambda i: (0, 0))],
                  out_specs=pl.BlockSpec((tr, D), lambda i: (i, 0)), out_shape=jax.ShapeDtypeStruct((S, D), MM_DT),
                  compiler_params=_row_params())(x, g)


def _acc_rows(ref, part, first):
    @pl.when(first)
    def _():
        ref[...] = jnp.zeros_like(ref)

    ref[...] += jnp.broadcast_to(part, ref.shape)


def _rms_bwd(name, x, g, dy, res):
    S, D = x.shape
    tr = _pick(S, 256, 16)

    def body(x_ref, g_ref, dy_ref, res_ref, dx_ref, dg_ref):
        xv = x_ref[...]
        r = lax.rsqrt(jnp.mean(xv * xv, axis=-1, keepdims=True) + EPS)
        xh = xv * r
        dyv = dy_ref[...]
        dn = dyv * g_ref[...]
        dx_ref[...] = res_ref[...] + r * (dn - xh * jnp.mean(dn * xh, axis=-1, keepdims=True))
        _acc_rows(dg_ref, jnp.sum(dyv * xh, axis=0, keepdims=True), pl.program_id(0) == 0)

    row = pl.BlockSpec((tr, D), lambda i: (i, 0))
    return _pcall(body, name=name, grid=(S // tr,),
                  in_specs=[row, pl.BlockSpec((1, D), lambda i: (0, 0)), row, row],
                  out_specs=(row, pl.BlockSpec((8, D), lambda i: (0, 0))),
                  out_shape=(jax.ShapeDtypeStruct((S, D), F32), jax.ShapeDtypeStruct((8, D), F32)),
                  compiler_params=_row_params())(x, g, dy, res)


def _final_loss(name, x, g, tgt):
    S, D = x.shape
    tr = _pick(S, 256, 16)

    def body(x_ref, g_ref, t_ref, dx_ref, dg_ref, loss_ref):
        xv = x_ref[...]
        r = lax.rsqrt(jnp.mean(xv * xv, axis=-1, keepdims=True) + EPS)
        xh = xv * r
        gv = g_ref[...]
        e = xh * gv - t_ref[...]
        part = 0.5 * jnp.sum(jnp.mean(e * e, axis=-1, keepdims=True), axis=0, keepdims=True)
        dy = e * (1.0 / D)
        dn = dy * gv
        dx_ref[...] = r * (dn - xh * jnp.mean(dn * xh, axis=-1, keepdims=True))
        first = pl.program_id(0) == 0
        _acc_rows(dg_ref, jnp.sum(dy * xh, axis=0, keepdims=True), first)
        _acc_rows(loss_ref, part, first)

    row = pl.BlockSpec((tr, D), lambda i: (i, 0))
    return _pcall(body, name=name, grid=(S // tr,),
                  in_specs=[row, pl.BlockSpec((1, D), lambda i: (0, 0)), row],
                  out_specs=(row, pl.BlockSpec((8, D), lambda i: (0, 0)), pl.BlockSpec((8, LANES), lambda i: (0, 0))),
                  out_shape=(jax.ShapeDtypeStruct((S, D), F32), jax.ShapeDtypeStruct((8, D), F32),
                             jax.ShapeDtypeStruct((8, LANES), F32)),
                  compiler_params=_row_params())(x, g, tgt)


def _rope_tables(cos, sin, off, w):
    S = cos.shape[0]
    h = w // 2
    z = lambda n: jnp.zeros((S, n), F32)
    C = jnp.concatenate([z(off), cos, cos, z(LANES - off - w)], axis=1)
    SP = jnp.concatenate([z(off + h), sin, z(LANES - off - w)], axis=1)
    SN = jnp.concatenate([z(off), -sin, z(LANES - off - h)], axis=1)
    return C, SP, SN


def _angles(pos, dim):
    inv = jnp.power(ROPE_THETA, -jnp.arange(0, dim, 2, dtype=F32) / dim)
    ang = pos.astype(F32)[:, None] * inv[None, :]
    return jnp.cos(ang), jnp.sin(ang)


def _all_tables(cfg):
    S = cfg.S
    pos = jnp.arange(S, dtype=F32)
    rows = S // cfg.GRID_W
    row = jnp.repeat(jnp.arange(rows, dtype=F32), cfg.GRID_W)
    col = jnp.tile(jnp.arange(cfg.GRID_W, dtype=F32), rows)
    ca, sa = _angles(pos, 64)
    cc, sc = _angles(pos, 128)
    cr, sr = _angles(row, 64)
    cl, sl = _angles(col, 64)
    t_b = tuple(a + b for a, b in zip(_rope_tables(cr, sr, 0, 64), _rope_tables(cl, sl, 64, 64)))
    return {"aq": (_rope_tables(ca, sa, 0, 64), 64), "akr": (_rope_tables(ca, sa, 64, 64), 64),
            "b": (t_b, 64), "c": (_rope_tables(cc, sc, 0, 128), 128)}


def _rope(x, C, SP, SN, w):
    h = w // 2
    if 2 * h == LANES:
        return x * C + pltpu.roll(x, h, 1) * (SP + SN)
    return x * C + pltpu.roll(x, h, 1) * SP + pltpu.roll(x, LANES - h, 1) * SN


def _rope_t(dy, C, SP, SN, w):
    h = w // 2
    if 2 * h == LANES:
        return dy * C + pltpu.roll(dy * (SP + SN), h, 1)
    return dy * C + pltpu.roll(dy * SP, LANES - h, 1) + pltpu.roll(dy * SN, h, 1)


def _grid2_params():
    return pltpu.CompilerParams(dimension_semantics=("arbitrary", "arbitrary"), vmem_limit_bytes=VMEM_LIMIT)


def _headprep_fwd(name, proj, col_off, nb, gain, tabs):
    S = proj.shape[0]
    tr = _pick(S, 1024, 16)
    cb = col_off // LANES
    norm, rope = gain is not None, tabs is not None
    w = tabs[1] if rope else 0

    def body(*refs):
        x_ref = refs[0]
        pos = 1
        xv = x_ref[...]
        if norm:
            r = lax.rsqrt(jnp.mean(xv * xv, axis=-1, keepdims=True) + EPS)
            xv = xv * r * refs[pos][...]
            pos += 1
        if rope:
            xv = _rope(xv, refs[pos][...], refs[pos + 1][...], refs[pos + 2][...], w)
            pos += 3
        refs[pos][...] = xv.astype(refs[pos].dtype)

    ops, in_specs = [proj], [pl.BlockSpec((tr, LANES), lambda i, j: (i, cb + j))]
    if norm:
        ops.append(gain)
        in_specs.append(pl.BlockSpec((1, LANES), lambda i, j: (0, 0)))
    if rope:
        ops += list(tabs[0])
        in_specs += [pl.BlockSpec((tr, LANES), lambda i, j: (i, 0))] * 3
    return _pcall(body, name=name, grid=(S // tr, nb), in_specs=in_specs,
                  out_specs=pl.BlockSpec((tr, LANES), lambda i, j: (i, j)),
                  out_shape=jax.ShapeDtypeStruct((S, nb * LANES), MM_DT), compiler_params=_grid2_params())(*ops)


def _headprep_bwd(name, dy, proj, col_off, nb, gain, tabs, dproj):
    S = proj.shape[0]
    tr = _pick(S, 1024, 16)
    cb = col_off // LANES
    norm, rope = gain is not None, tabs is not None
    w = tabs[1] if rope else 0

    def body(*refs):
        dz = refs[0][...]
        pos = 1
        if norm:
            x_ref, g_ref = refs[pos], refs[pos + 1]
            pos += 2
        if rope:
            dz = _rope_t(dz, refs[pos][...], refs[pos + 1][...], refs[pos + 2][...], w)
            pos += 3
        pos += 1
        o_ref = refs[pos]
        if norm:
            dg_ref = refs[pos + 1]
            xv = x_ref[...]
            r = lax.rsqrt(jnp.mean(xv * xv, axis=-1, keepdims=True) + EPS)
            n = xv * r
            first = (pl.program_id(0) == 0) & (pl.program_id(1) == 0)
            _acc_rows(dg_ref, jnp.sum(dz * n, axis=0, keepdims=True), first)
            dn = dz * g_ref[...]
            dz = r * (dn - n * jnp.mean(dn * n, axis=-1, keepdims=True))
        o_ref[...] = dz.astype(o_ref.dtype)

    ops, in_specs = [dy], [pl.BlockSpec((tr, LANES), lambda i, j: (i, j))]
    if norm:
        ops += [proj, gain]
        in_specs += [pl.BlockSpec((tr, LANES), lambda i, j: (i, cb + j)), pl.BlockSpec((1, LANES), lambda i, j: (0, 0))]
    if rope:
        ops += list(tabs[0])
        in_specs += [pl.BlockSpec((tr, LANES), lambda i, j: (i, 0))] * 3
    alias_idx = len(ops)
    ops.append(dproj)
    in_specs.append(pl.BlockSpec(memory_space=pl.ANY))
    out_specs = [pl.BlockSpec((tr, LANES), lambda i, j: (i, cb + j))]
    out_shape = [jax.ShapeDtypeStruct(dproj.shape, dproj.dtype)]
    if norm:
        out_specs.append(pl.BlockSpec((8, LANES), lambda i, j: (0, 0)))
        out_shape.append(jax.ShapeDtypeStruct((8, LANES), F32))
    res = _pcall(body, name=name, grid=(S // tr, nb), in_specs=in_specs, out_specs=tuple(out_specs),
                 out_shape=tuple(out_shape), input_output_aliases={alias_idx: 0}, compiler_params=_grid2_params())(*ops)
    return (res[0], res[1]) if norm else (res[0], None)


def _masked_rms(xv, lo, n):
    lane = lax.broadcasted_iota(jnp.int32, xv.shape, 1)
    xm = jnp.where((lane >= lo) & (lane < lo + n), xv, 0.0)
    r = lax.rsqrt(jnp.sum(xm * xm, axis=-1, keepdims=True) * (1.0 / n) + EPS)
    return xm * r, r


def _mla_prep_fwd(name, cfg, proj, gq, gkv, tabs):
    S = cfg.S
    tr = _pick(S, 256, 16)
    (C, SP, SN), w = tabs

    def body(p_ref, gq_ref, gkv_ref, c_ref, sp_ref, sn_ref, cq_ref, ckv_ref, kpe_ref):
        nq, _ = _masked_rms(p_ref[:, 0:cfg.QLP], 0, cfg.QL)
        cq_ref[...] = (nq * gq_ref[...]).astype(cq_ref.dtype)
        nk, _ = _masked_rms(p_ref[:, cfg.KV0:cfg.PW], cfg.KOFF, cfg.KVL)
        ckv_ref[...] = (nk * gkv_ref[...]).astype(ckv_ref.dtype)
        kr = _rope(p_ref[:, cfg.PW - LANES:cfg.PW], c_ref[...], sp_ref[...], sn_ref[...], w)
        kpe_ref[...] = pltpu.roll(kr, 64, 1).astype(kpe_ref.dtype)

    tab = pl.BlockSpec((tr, LANES), lambda i: (i, 0))
    return _pcall(body, name=name, grid=(S // tr,),
                  in_specs=[pl.BlockSpec((tr, cfg.PW), lambda i: (i, 0)), pl.BlockSpec((1, cfg.QLP), lambda i: (0, 0)),
                            pl.BlockSpec((1, cfg.KVW), lambda i: (0, 0)), tab, tab, tab],
                  out_specs=(pl.BlockSpec((tr, cfg.QLP), lambda i: (i, 0)), pl.BlockSpec((tr, cfg.KVW), lambda i: (i, 0)), tab),
                  out_shape=(jax.ShapeDtypeStruct((S, cfg.QLP), MM_DT), jax.ShapeDtypeStruct((S, cfg.KVW), MM_DT),
                             jax.ShapeDtypeStruct((S, LANES), MM_DT)),
                  compiler_params=_row_params())(proj, gq, gkv, C, SP, SN)


def _mla_prep_bwd(name, cfg, dcq, dckv, dkpe, proj, gq, gkv, tabs, dproj):
    S = cfg.S
    tr = _pick(S, 256, 16)
    (C, SP, SN), w = tabs

    def body(dcq_ref, dckv_ref, dkpe_ref, p_ref, gq_ref, gkv_ref, c_ref, sp_ref, sn_ref, buf_ref, o_ref, dgq_ref, dgkv_ref):
        first = pl.program_id(0) == 0

        def norm_bwd(xv, lo, n, dz, g_ref, dg_ref):
            nrm, r = _masked_rms(xv, lo, n)
            _acc_rows(dg_ref, jnp.sum(dz * nrm, axis=0, keepdims=True), first)
            dn = dz * g_ref[...]
            return r * (dn - nrm * (jnp.sum(dn * nrm, axis=-1, keepdims=True) * (1.0 / n)))

        dxq = norm_bwd(p_ref[:, 0:cfg.QLP], 0, cfg.QL, dcq_ref[...], gq_ref, dgq_ref)
        dxk = norm_bwd(p_ref[:, cfg.KV0:cfg.PW], cfg.KOFF, cfg.KVL, dckv_ref[...], gkv_ref, dgkv_ref)
        dxr = _rope_t(pltpu.roll(dkpe_ref[...], 64, 1), c_ref[...], sp_ref[...], sn_ref[...], w)
        for cidx in range(cfg.PW // LANES):
            lo = cidx * LANES
            parts = []
            if lo < cfg.QLP:
                parts.append(dxq[:, lo:lo + LANES])
            if lo >= cfg.KV0:
                parts.append(dxk[:, lo - cfg.KV0:lo - cfg.KV0 + LANES])
            if lo == cfg.PW - LANES:
                parts.append(dxr)
            o_ref[:, lo:lo + LANES] = functools.reduce(lambda a, b: a + b, parts).astype(o_ref.dtype)

    tab = pl.BlockSpec((tr, LANES), lambda i: (i, 0))
    res = _pcall(body, name=name, grid=(S // tr,),
                 in_specs=[pl.BlockSpec((tr, cfg.QLP), lambda i: (i, 0)), pl.BlockSpec((tr, cfg.KVW), lambda i: (i, 0)), tab,
                           pl.BlockSpec((tr, cfg.PW), lambda i: (i, 0)), pl.BlockSpec((1, cfg.QLP), lambda i: (0, 0)),
                           pl.BlockSpec((1, cfg.KVW), lambda i: (0, 0)), tab, tab, tab, pl.BlockSpec(memory_space=pl.ANY)],
                 out_specs=(pl.BlockSpec((tr, cfg.PW), lambda i: (i, 0)), pl.BlockSpec((8, cfg.QLP), lambda i: (0, 0)),
                            pl.BlockSpec((8, cfg.KVW), lambda i: (0, 0))),
                 out_shape=(jax.ShapeDtypeStruct(dproj.shape, dproj.dtype), jax.ShapeDtypeStruct((8, cfg.QLP), F32),
                            jax.ShapeDtypeStruct((8, cfg.KVW), F32)),
                 input_output_aliases={9: 0}, compiler_params=_row_params())(dcq, dckv, dkpe, proj, gq, gkv, C, SP, SN, dproj)
    return res


def _mla_build_fwd(name, cfg, qa, kva, kpe, tabs):
    S, AH = cfg.S, cfg.AH
    tr = _pick(S, 256, 16)
    (C, SP, SN), w = tabs

    def body(qa_ref, kva_ref, kpe_ref, c_ref, sp_ref, sn_ref, q_ref, k_ref, v_ref):
        for h in range(AH):
            a, b = 256 * h, 256 * h + LANES
            q_ref[:, a:b] = qa_ref[:, a:b].astype(q_ref.dtype)
            q_ref[:, b:b + LANES] = _rope(qa_ref[:, b:b + LANES], c_ref[...], sp_ref[...], sn_ref[...], w).astype(q_ref.dtype)
            k_ref[:, a:b] = kva_ref[:, a:b].astype(k_ref.dtype)
            k_ref[:, b:b + LANES] = kpe_ref[...]
            v_ref[:, LANES * h:LANES * (h + 1)] = kva_ref[:, b:b + LANES].astype(v_ref.dtype)

    tab = pl.BlockSpec((tr, LANES), lambda i: (i, 0))
    wide = pl.BlockSpec((tr, AH * 256), lambda i: (i, 0))
    return _pcall(body, name=name, grid=(S // tr,), in_specs=[wide, wide, tab, tab, tab, tab],
                  out_specs=(wide, wide, pl.BlockSpec((tr, AH * LANES), lambda i: (i, 0))),
                  out_shape=(jax.ShapeDtypeStruct((S, AH * 256), MM_DT), jax.ShapeDtypeStruct((S, AH * 256), MM_DT),
                             jax.ShapeDtypeStruct((S, AH * LANES), MM_DT)),
                  compiler_params=_row_params())(qa, kva, kpe, C, SP, SN)


def _mla_build_bwd(name, cfg, dq, dk, dv, tabs):
    S, AH = cfg.S, cfg.AH
    tr = _pick(S, 256, 16)
    (C, SP, SN), w = tabs

    def body(dq_ref, dk_ref, dv_ref, c_ref, sp_ref, sn_ref, dqa_ref, dkva_ref, dkpe_ref):
        dkpe = None
        for h in range(AH):
            a, b = 256 * h, 256 * h + LANES
            dqa_ref[:, a:b] = dq_ref[:, a:b].astype(dqa_ref.dtype)
            dqa_ref[:, b:b + LANES] = _rope_t(dq_ref[:, b:b + LANES], c_ref[...], sp_ref[...], sn_ref[...], w).astype(dqa_ref.dtype)
            dkva_ref[:, a:b] = dk_ref[:, a:b].astype(dkva_ref.dtype)
            dkva_ref[:, b:b + LANES] = dv_ref[:, LANES * h:LANES * (h + 1)].astype(dkva_ref.dtype)
            part = dk_ref[:, b:b + LANES]
            dkpe = part if dkpe is None else dkpe + part
        dkpe_ref[...] = dkpe

    tab = pl.BlockSpec((tr, LANES), lambda i: (i, 0))
    wide = pl.BlockSpec((tr, AH * 256), lambda i: (i, 0))
    return _pcall(body, name=name, grid=(S // tr,),
                  in_specs=[wide, wide, pl.BlockSpec((tr, AH * LANES), lambda i: (i, 0)), tab, tab, tab],
                  out_specs=(wide, wide, tab),
                  out_shape=(jax.ShapeDtypeStruct((S, AH * 256), MM_DT), jax.ShapeDtypeStruct((S, AH * 256), MM_DT),
                             jax.ShapeDtypeStruct((S, LANES), F32)),
                  compiler_params=_row_params())(dq, dk, dv, C, SP, SN)


def _outnorm_fwd(name, cfg, oa, ob, oc, g):
    S = cfg.S
    tr = _pick(S, 256, 16)
    widths = (cfg.AW, cfg.BW, cfg.CW)

    def body(a_ref, b_ref, c_ref, g_ref, o_ref):
        off = 0
        for ref, wd in zip((a_ref, b_ref, c_ref), widths):
            v = ref[...]
            r = lax.rsqrt(jnp.mean(v * v, axis=-1, keepdims=True) + EPS)
            o_ref[:, off:off + wd] = (v * r * g_ref[:, off:off + wd]).astype(o_ref.dtype)
            off += wd

    return _pcall(body, name=name, grid=(S // tr,),
                  in_specs=[pl.BlockSpec((tr, wd), lambda i: (i, 0)) for wd in widths] + [pl.BlockSpec((1, cfg.MIX), lambda i: (0, 0))],
                  out_specs=pl.BlockSpec((tr, cfg.MIX), lambda i: (i, 0)),
                  out_shape=jax.ShapeDtypeStruct((S, cfg.MIX), MM_DT), compiler_params=_row_params())(oa, ob, oc, g)


def _outnorm_bwd(name, cfg, dmix, oa, ob, oc, g):
    S = cfg.S
    tr = _pick(S, 256, 16)
    widths = (cfg.AW, cfg.BW, cfg.CW)

    def body(dm_ref, a_ref, b_ref, c_ref, g_ref, da_ref, db_ref, dc_ref, dg_ref):
        off = 0
        parts = []
        for ref, dref, wd in zip((a_ref, b_ref, c_ref), (da_ref, db_ref, dc_ref), widths):
            v = ref[...]
            r = lax.rsqrt(jnp.mean(v * v, axis=-1, keepdims=True) + EPS)
            n = v * r
            dm = dm_ref[:, off:off + wd]
            parts.append(jnp.sum(dm * n, axis=0, keepdims=True))
            dn = dm * g_ref[:, off:off + wd]
            dref[...] = r * (dn - n * jnp.mean(dn * n, axis=-1, keepdims=True))
            off += wd
        _acc_rows(dg_ref, jnp.concatenate(parts, axis=1), pl.program_id(0) == 0)

    segs = [pl.BlockSpec((tr, wd), lambda i: (i, 0)) for wd in widths]
    return _pcall(body, name=name, grid=(S // tr,),
                  in_specs=[pl.BlockSpec((tr, cfg.MIX), lambda i: (i, 0))] + segs + [pl.BlockSpec((1, cfg.MIX), lambda i: (0, 0))],
                  out_specs=tuple(segs) + (pl.BlockSpec((8, cfg.MIX), lambda i: (0, 0)),),
                  out_shape=tuple(jax.ShapeDtypeStruct((S, wd), F32) for wd in widths) + (jax.ShapeDtypeStruct((8, cfg.MIX), F32),),
                  compiler_params=_row_params())(dmix, oa, ob, oc, g)


def _band_bias(cfg):
    tq, tk, W = cfg.TB, cfg.TBK, cfg.W
    ns = 2 * W + 1
    shape = ((tk // tq) * ns, tq, tk)
    slab = lax.broadcasted_iota(jnp.int32, shape, 0)
    row = lax.broadcasted_iota(jnp.int32, shape, 1)
    col = lax.broadcasted_iota(jnp.int32, shape, 2)
    d = (slab // ns) * tq + (W - slab % ns) * tk + row - col
    ad = jnp.abs(d)
    m = jnp.zeros(d.shape, F32)
    for reach, dil in cfg.branches:
        ok = ad <= reach
        if dil > 1:
            ok = ok & ((d & (dil - 1)) == 0)
        m = m + ok.astype(F32)
    return jnp.where(m > 0, jnp.log(jnp.maximum(m, 1.0)), NEG)


def _attn_params():
    return pltpu.CompilerParams(dimension_semantics=("parallel", "parallel", "arbitrary"), vmem_limit_bytes=VMEM_LIMIT)


def _scores(q_ref, k_ref, scale, bias_ref):
    s = lax.dot_general(q_ref[...], k_ref[...], NT, preferred_element_type=F32) * scale
    return s if bias_ref is None else s + bias_ref[...]


def _flash_fwd(name, q, k, v, H, G, dk, dv, scale, bias=None, W=None):
    S = q.shape[0]
    band = bias is not None
    tq = bias.shape[1] if band else _pick(S, ATT_TQ, LANES)
    tk = bias.shape[2] if band else _pick(S, ATT_TK_FWD, LANES)
    n = S // tk
    nsteps = 2 * W + 1 if band else n
    R = tk // tq

    def kblock(qi, st):
        return jnp.clip(qi // R - W + st, 0, n - 1) if band else st

    def body(*refs):
        q_ref, k_ref, v_ref = refs[:3]
        bias_ref = refs[3] if band else None
        o_ref, lse_ref, m_sc, l_sc, acc_sc = refs[-5:]
        qi, st = pl.program_id(1), pl.program_id(2)

        @pl.when(st == 0)
        def _():
            m_sc[...] = jnp.full_like(m_sc, NEG)
            l_sc[...] = jnp.zeros_like(l_sc)
            acc_sc[...] = jnp.zeros_like(acc_sc)

        kj = qi // R - W + st if band else st

        def step():
            s = _scores(q_ref, k_ref, scale, bias_ref)
            m_prev = m_sc[...]
            m_new = jnp.maximum(m_prev, jnp.max(s, axis=-1, keepdims=True))
            alpha = jnp.exp(m_prev - m_new)
            p = jnp.exp(s - m_new)
            l_sc[...] = alpha * l_sc[...] + jnp.sum(p, axis=-1, keepdims=True)
            acc_sc[...] = alpha * acc_sc[...] + lax.dot_general(p.astype(MM_DT), v_ref[...], NN, preferred_element_type=F32)
            m_sc[...] = m_new

        if band:
            pl.when((kj >= 0) & (kj < n))(step)
        else:
            step()

        @pl.when(st == nsteps - 1)
        def _():
            l = l_sc[...]
            o_ref[...] = acc_sc[...] / l
            lse_ref[...] = jnp.broadcast_to(m_sc[...] + jnp.log(l), lse_ref.shape)

    in_specs = [pl.BlockSpec((tq, dk), lambda h, qi, st: (qi, h)),
                pl.BlockSpec((tk, dk), lambda h, qi, st: (kblock(qi, st), h // G)),
                pl.BlockSpec((tk, dv), lambda h, qi, st: (kblock(qi, st), h // G))]
    ops = [q, k, v]
    if band:
        in_specs.append(pl.BlockSpec((None, tq, tk), lambda h, qi, st: ((qi % R) * nsteps + st, 0, 0)))
        ops.append(bias)
    return _pcall(body, name=name, grid=(H, S // tq, nsteps), in_specs=in_specs,
                  out_specs=(pl.BlockSpec((tq, dv), lambda h, qi, st: (qi, h)),
                             pl.BlockSpec((None, tq, LANES), lambda h, qi, st: (h, qi, 0))),
                  out_shape=(jax.ShapeDtypeStruct((S, H * dv), F32), jax.ShapeDtypeStruct((H, S, LANES), F32)),
                  scratch_shapes=[pltpu.VMEM((tq, 1), F32), pltpu.VMEM((tq, 1), F32), pltpu.VMEM((tq, dv), F32)],
                  compiler_params=_attn_params())(*ops)


def _flash_dq(name, q, k, v, do, o, lse, H, G, dk, dv, scale, bias=None, W=None):
    S = q.shape[0]
    band = bias is not None
    tq = bias.shape[1] if band else _pick(S, ATT_TQ, LANES)
    tk = bias.shape[2] if band else _pick(S, ATT_TK, LANES)
    n = S // tk
    nsteps = 2 * W + 1 if band else n
    R = tk // tq

    def kblock(qi, st):
        return jnp.clip(qi // R - W + st, 0, n - 1) if band else st

    def body(*refs):
        q_ref, k_ref, v_ref, do_ref, o_ref, lse_ref = refs[:6]
        bias_ref = refs[6] if band else None
        dq_ref, delta_sc, acc_sc = refs[-3:]
        qi, st = pl.program_id(1), pl.program_id(2)

        @pl.when(st == 0)
        def _():
            delta_sc[...] = jnp.sum(do_ref[...] * o_ref[...], axis=-1, keepdims=True)
            acc_sc[...] = jnp.zeros_like(acc_sc)

        kj = qi // R - W + st if band else st

        def step():
            p = jnp.exp(_scores(q_ref, k_ref, scale, bias_ref) - lse_ref[:, 0:1])
            dp = lax.dot_general(do_ref[...].astype(MM_DT), v_ref[...], NT, preferred_element_type=F32)
            ds = p * (dp - delta_sc[...]) * scale
            acc_sc[...] += lax.dot_general(ds.astype(MM_DT), k_ref[...], NN, preferred_element_type=F32)

        if band:
            pl.when((kj >= 0) & (kj < n))(step)
        else:
            step()

        @pl.when(st == nsteps - 1)
        def _():
            dq_ref[...] = acc_sc[...]

    qspec = lambda wd: pl.BlockSpec((tq, wd), lambda h, qi, st: (qi, h))
    in_specs = [qspec(dk),
                pl.BlockSpec((tk, dk), lambda h, qi, st: (kblock(qi, st), h // G)),
                pl.BlockSpec((tk, dv), lambda h, qi, st: (kblock(qi, st), h // G)),
                qspec(dv), qspec(dv),
                pl.BlockSpec((None, tq, LANES), lambda h, qi, st: (h, qi, 0))]
    ops = [q, k, v, do, o, lse]
    if band:
        in_specs.append(pl.BlockSpec((None, tq, tk), lambda h, qi, st: ((qi % R) * nsteps + st, 0, 0)))
        ops.append(bias)
    return _pcall(body, name=name, grid=(H, S // tq, nsteps), in_specs=in_specs,
                  out_specs=qspec(dk), out_shape=jax.ShapeDtypeStruct((S, H * dk), F32),
                  scratch_shapes=[pltpu.VMEM((tq, 1), F32), pltpu.VMEM((tq, dk), F32)],
                  compiler_params=_attn_params())(*ops)


def _flash_dkv(name, q, k, v, do, o, lse, H, G, dk, dv, scale, bias=None, W=None):
    S = q.shape[0]
    band = bias is not None
    tq = bias.shape[1] if band else _pick(S, ATT_TQ, LANES)
    tk = bias.shape[2] if band else _pick(S, ATT_TK, LANES)
    n = S // tq
    R = tk // tq
    nq = R * (2 * W + 1) if band else n
    nsteps = G * nq
    Hkv = H // G

    def qhead(hk, st):
        return hk * G + st // nq

    def qblock(kj, st):
        return jnp.clip(R * (kj - W) + st % nq, 0, n - 1) if band else st % nq

    def body(*refs):
        q_ref, k_ref, v_ref, do_ref, o_ref, lse_ref = refs[:6]
        bias_ref = refs[6] if band else None
        dk_ref, dv_ref, dk_sc, dv_sc = refs[-4:]
        kj, st = pl.program_id(1), pl.program_id(2)

        @pl.when(st == 0)
        def _():
            dk_sc[...] = jnp.zeros_like(dk_sc)
            dv_sc[...] = jnp.zeros_like(dv_sc)

        qi = R * (kj - W) + st % nq if band else st % nq

        def step():
            p = jnp.exp(_scores(q_ref, k_ref, scale, bias_ref) - lse_ref[:, 0:1])
            dof = do_ref[...]
            dob = dof.astype(MM_DT)
            dv_sc[...] += lax.dot_general(p.astype(MM_DT), dob, TN, preferred_element_type=F32)
            dp = lax.dot_general(dob, v_ref[...], NT, preferred_element_type=F32)
            delta = jnp.sum(dof * o_ref[...], axis=-1, keepdims=True)
            ds = p * (dp - delta) * scale
            dk_sc[...] += lax.dot_general(ds.astype(MM_DT), q_ref[...], TN, preferred_element_type=F32)

        if band:
            pl.when((qi >= 0) & (qi < n))(step)
        else:
            step()

        @pl.when(st == nsteps - 1)
        def _():
            dk_ref[...] = dk_sc[...]
            dv_ref[...] = dv_sc[...]

    qspec = lambda wd: pl.BlockSpec((tq, wd), lambda hk, kj, st: (qblock(kj, st), qhead(hk, st)))
    kspec = lambda wd: pl.BlockSpec((tk, wd), lambda hk, kj, st: (kj, hk))
    in_specs = [qspec(dk), kspec(dk), kspec(dv), qspec(dv), qspec(dv),
                pl.BlockSpec((None, tq, LANES), lambda hk, kj, st: (qhead(hk, st), qblock(kj, st), 0))]
    ops = [q, k, v, do, o, lse]
    if band:
        in_specs.append(pl.BlockSpec((None, tq, tk),
                                     lambda hk, kj, st: ((st % nq % R) * (2 * W + 1) + 2 * W - (st % nq) // R, 0, 0)))
        ops.append(bias)
    return _pcall(body, name=name, grid=(Hkv, S // tk, nsteps), in_specs=in_specs,
                  out_specs=(kspec(dk), kspec(dv)),
                  out_shape=(jax.ShapeDtypeStruct((S, Hkv * dk), F32), jax.ShapeDtypeStruct((S, Hkv * dv), F32)),
                  scratch_shapes=[pltpu.VMEM((tk, dk), F32), pltpu.VMEM((tk, dv), F32)],
                  compiler_params=_attn_params())(*ops)


def _rowtile(rows, cols):
    return _pick(rows, max(16, (512 * 1024) // cols // 16 * 16), 16)


def _cast_rows(name, w, dtype):
    R, C = w.shape
    tr = _rowtile(R, C)

    def body(w_ref, o_ref):
        o_ref[...] = w_ref[...].astype(o_ref.dtype)

    spec = pl.BlockSpec((tr, C), lambda i: (i, 0))
    return _pcall(body, name=name, grid=(R // tr,), in_specs=[spec], out_specs=spec,
                  out_shape=jax.ShapeDtypeStruct((R, C), dtype), compiler_params=_row_params())(w)


def _add_sibling(name, gw, ra, c_arr, hd):
    ns, depth, Ks, Ns = gw.shape
    rows = hd * Ks
    tr = _rowtile(rows, Ns)
    gw_v = gw.reshape(ns, 2, rows, Ns)
    ra_v = ra.reshape(ns, rows, Ns)

    def body(c_ref, g_ref, r_ref, o_ref):
        o_ref[...] = (g_ref[...].astype(F32) + r_ref[...].astype(F32)).astype(o_ref.dtype)

    grid_spec = pltpu.PrefetchScalarGridSpec(
        num_scalar_prefetch=1, grid=(ns, rows // tr),
        in_specs=[pl.BlockSpec((None, None, tr, Ns), lambda s, r, c_ref: (s, c_ref[0], r, 0)),
                  pl.BlockSpec((None, tr, Ns), lambda s, r, c_ref: (s, r, 0))],
        out_specs=pl.BlockSpec((None, tr, Ns), lambda s, r, c_ref: (s, r, 0)))
    return _pcall(body, name=name, grid_spec=grid_spec, out_shape=jax.ShapeDtypeStruct((ns, rows, Ns), MM_DT),
                  compiler_params=_grid2_params())(c_arr, gw_v, ra_v)


def _add_chips(name, p, rb, me_arr):
    ns, rows, Ns = p.shape
    tr = _rowtile(rows, Ns)

    def body(me_ref, p_ref, b0_ref, b1_ref, b2_ref, o_ref):
        o_ref[...] = ((p_ref[...].astype(F32) + b0_ref[...].astype(F32)) + b1_ref[...].astype(F32)) + b2_ref[...].astype(F32)

    grid_spec = pltpu.PrefetchScalarGridSpec(
        num_scalar_prefetch=1, grid=(rows // tr,),
        in_specs=[pl.BlockSpec((None, tr, Ns), lambda r, me_ref: (me_ref[0], r, 0))] +
                 [pl.BlockSpec((None, tr, Ns), functools.partial(lambda r, me_ref, j: (j, r, 0), j=j)) for j in range(3)],
        out_specs=pl.BlockSpec((None, tr, Ns), lambda r, me_ref: (0, r, 0)))
    return _pcall(body, name=name, grid_spec=grid_spec, out_shape=jax.ShapeDtypeStruct((2, rows, Ns), F32),
                  compiler_params=_row_params())(me_arr, p, rb, rb, rb)


def _adamw_math(wv, gv, mv, vv):
    bc1 = 1.0 - ADAM_B1 ** ADAM_STEP
    bc2 = 1.0 - ADAM_B2 ** ADAM_STEP
    mn = ADAM_B1 * mv + (1.0 - ADAM_B1) * gv
    vn = ADAM_B2 * vv + (1.0 - ADAM_B2) * jnp.square(gv)
    m_hat = mn / bc1
    v_hat = vn / bc2
    return -ADAM_LR * (m_hat / (jnp.sqrt(v_hat) + ADAM_EPS) + ADAM_WD * wv), mn, vn


def _adamw_halves(name, w, g2, m, v, c_arr):
    _, R, C = w.shape
    tr = _rowtile(R, C)

    def body(c_ref, w_ref, g_ref, m_ref, v_ref, go_ref, d_ref, nm_ref, nv_ref):
        gv = g_ref[...]
        go_ref[...] = gv
        d_ref[...], nm_ref[...], nv_ref[...] = _adamw_math(w_ref[...], gv, m_ref[...], v_ref[...])

    spec = pl.BlockSpec((None, tr, C), lambda h, r, c_ref: (h, r, 0))
    gspec = pl.BlockSpec((None, tr, C), lambda h, r, c_ref: ((h + c_ref[0]) % 2, r, 0))
    sds = jax.ShapeDtypeStruct(w.shape, F32)
    grid_spec = pltpu.PrefetchScalarGridSpec(num_scalar_prefetch=1, grid=(2, R // tr), in_specs=[spec, gspec, spec, spec],
                                             out_specs=(spec,) * 4)
    return _pcall(body, name=name, grid_spec=grid_spec, out_shape=(sds,) * 4, compiler_params=_grid2_params())(c_arr, w, g2, m, v)


def _adamw(name, w, g, m, v):
    R, C = w.shape
    tr = _rowtile(R, C)

    def body(w_ref, g_ref, m_ref, v_ref, d_ref, nm_ref, nv_ref):
        d_ref[...], nm_ref[...], nv_ref[...] = _adamw_math(w_ref[...], g_ref[...], m_ref[...], v_ref[...])

    spec = pl.BlockSpec((tr, C), lambda i: (i, 0))
    sds = jax.ShapeDtypeStruct((R, C), F32)
    return _pcall(body, name=name, grid=(R // tr,), in_specs=[spec] * 4, out_specs=(spec,) * 3,
                  out_shape=(sds, sds, sds), compiler_params=_row_params())(w, g, m, v)


HBM_SPEC = pl.BlockSpec(memory_space=pltpu.HBM)


def _place():
    x, y, c = lax.axis_index("x"), lax.axis_index("y"), lax.axis_index("c")
    chips = [(1 - x, y), (x, 1 - y), (1 - x, 1 - y)]
    return x, y, c, chips


def _allgather_body(ins, outs, send, recv, handshake):
    n = len(ins)
    x, y, c, chips = _place()
    me = 2 * x + y
    sib = (x, y, 1 - c)
    if handshake:
        barrier = pltpu.get_barrier_semaphore()
        for peer in [(chip[0], chip[1], c) for chip in chips] + [sib]:
            pl.semaphore_signal(barrier, inc=1, device_id=peer, device_id_type=MESH)
        pl.semaphore_wait(barrier, 4)

    def rcopy(src, dst, k, to):
        return pltpu.make_async_remote_copy(src_ref=src, dst_ref=dst, send_sem=send.at[k], recv_sem=recv.at[k],
                                            device_id=to, device_id_type=MESH)

    def rows(t, cc):
        hr = ins[t].shape[0] // 2
        return pl.ds(cc * hr, hr)

    sends = []
    for t in range(n):
        for j, chip in enumerate(chips):
            cp = rcopy(ins[t].at[rows(t, c)], outs[t].at[me, rows(t, c)], 7 * t + j, (chip[0], chip[1], c))
            cp.start()
            sends.append(cp)
    for t in range(n):
        cp = rcopy(ins[t], outs[t].at[me], 7 * t + 6, sib)
        cp.start()
        sends.append(cp)
    for t in range(n):
        for j, chip in enumerate(chips):
            slab = outs[t].at[2 * chip[0] + chip[1], rows(t, c)]
            rcopy(slab, slab, 7 * t + j, (chip[0], chip[1], c)).wait_recv()
            fw = rcopy(slab, slab, 7 * t + 3 + j, sib)
            fw.start()
            sends.append(fw)
    for t in range(n):
        rcopy(ins[t], outs[t].at[me], 7 * t + 6, sib).wait_recv()
        for j, chip in enumerate(chips):
            slab = outs[t].at[2 * chip[0] + chip[1], rows(t, 1 - c)]
            rcopy(slab, slab, 7 * t + 3 + j, sib).wait_recv()
    for cp in sends:
        cp.wait_send()


def _allgather_layer(name, shards):
    n = len(shards)

    def body(*refs):
        _allgather_body(refs[:n], refs[n:2 * n], refs[2 * n], refs[2 * n + 1], False)

    return _pcall(body, name=name, in_specs=[HBM_SPEC] * n, out_specs=tuple([HBM_SPEC] * n),
                  out_shape=tuple(jax.ShapeDtypeStruct((N_SHARD,) + s.shape, s.dtype) for s in shards),
                  scratch_shapes=[pltpu.SemaphoreType.DMA((7 * n,)), pltpu.SemaphoreType.DMA((7 * n,))])(*shards)


def _allgather_layer_async(name, shards, collective_id):
    n = len(shards)
    in_refs = [jax.new_ref(s, memory_space=pltpu.MemorySpace.HBM) for s in shards]
    out_refs = [jax.empty_ref(jax.ShapeDtypeStruct((N_SHARD,) + s.shape, s.dtype), memory_space=pltpu.MemorySpace.HBM)
                for s in shards]

    @pl.kernel(mesh=plsc.ScalarSubcoreMesh(axis_name="seq", num_cores=1), name=name,
               scratch_types=(pltpu.SemaphoreType.DMA((7 * n,)), pltpu.SemaphoreType.DMA((7 * n,))),
               compiler_params=pltpu.CompilerParams(collective_id=collective_id))
    def launch(send, recv):
        _allgather_body(in_refs, out_refs, send, recv, True)

    launch()
    return [r[...] for r in out_refs]


def _exchange_sibling_halves(gws, hd):
    n = len(gws)

    def body(*refs):
        ins, outs = refs[:n], refs[n:2 * n]
        send, recv = refs[2 * n:]
        x, y, c, _ = _place()
        cps = []
        for t in range(n):
            cp = pltpu.make_async_remote_copy(src_ref=ins[t].at[:, pl.ds((1 - c) * hd, hd)], dst_ref=outs[t],
                                              send_sem=send.at[t], recv_sem=recv.at[t],
                                              device_id=(x, y, 1 - c), device_id_type=MESH)
            cp.start()
            cps.append(cp)
        for cp in cps:
            cp.wait()

    return _pcall(body, name="rs_sibling_halves", in_specs=[HBM_SPEC] * n, out_specs=tuple([HBM_SPEC] * n),
                  out_shape=tuple(jax.ShapeDtypeStruct((g.shape[0], hd) + g.shape[2:], g.dtype) for g in gws),
                  scratch_shapes=[pltpu.SemaphoreType.DMA((n,)), pltpu.SemaphoreType.DMA((n,))])(*gws)


def _exchange_chips(ps):
    n = len(ps)

    def body(*refs):
        ins, outs = refs[:n], refs[n:2 * n]
        send, recv = refs[2 * n:]
        x, y, c, chips = _place()
        cps = []
        for t in range(n):
            for j, chip in enumerate(chips):
                cp = pltpu.make_async_remote_copy(src_ref=ins[t].at[2 * chip[0] + chip[1]], dst_ref=outs[t].at[j],
                                                  send_sem=send.at[3 * t + j], recv_sem=recv.at[3 * t + j],
                                                  device_id=(chip[0], chip[1], c), device_id_type=MESH)
                cp.start()
                cps.append(cp)
        for cp in cps:
            cp.wait()

    return _pcall(body, name="rs_chip_exchange", in_specs=[HBM_SPEC] * n, out_specs=tuple([HBM_SPEC] * n),
                  out_shape=tuple(jax.ShapeDtypeStruct((3,) + p.shape[1:], p.dtype) for p in ps),
                  scratch_shapes=[pltpu.SemaphoreType.DMA((3 * n,)), pltpu.SemaphoreType.DMA((3 * n,))])(*ps)


def _share_reduced(gs):
    n = len(gs)

    def body(*refs):
        ins, outs = refs[:n], refs[n:2 * n]
        send, recv = refs[2 * n:]
        x, y, c, _ = _place()
        cps = []
        for t in range(n):
            cp = pltpu.make_async_remote_copy(src_ref=ins[t].at[0], dst_ref=outs[t].at[1], send_sem=send.at[t], recv_sem=recv.at[t],
                                              device_id=(x, y, 1 - c), device_id_type=MESH)
            cp.start()
            cps.append(cp)
        for cp in cps:
            cp.wait()

    return _pcall(body, name="rs_share_reduced", in_specs=[HBM_SPEC] * n, out_specs=tuple([HBM_SPEC] * n),
                  out_shape=tuple(jax.ShapeDtypeStruct(g.shape, g.dtype) for g in gs),
                  input_output_aliases={t: t for t in range(n)},
                  scratch_shapes=[pltpu.SemaphoreType.DMA((n,)), pltpu.SemaphoreType.DMA((n,))])(*gs)


def _allreduce_small(vec):
    R = vec.shape[0]

    def body(v_ref, o_ref, buf, send, recv):
        x, y, c, _ = _place()
        me = 4 * x + 2 * y + c
        buf[me] = v_ref[...]
        cps = []
        for r in range(1, 8):
            fx, fy, fc = (r >> 2) & 1, (r >> 1) & 1, r & 1
            to = (1 - x if fx else x, 1 - y if fy else y, 1 - c if fc else c)
            cp = pltpu.make_async_remote_copy(src_ref=v_ref, dst_ref=buf.at[me], send_sem=send.at[r - 1], recv_sem=recv.at[r - 1],
                                              device_id=to, device_id_type=MESH)
            cp.start()
            cps.append(cp)
        for r in range(1, 8):
            fx, fy, fc = (r >> 2) & 1, (r >> 1) & 1, r & 1
            frm = (1 - x if fx else x, 1 - y if fy else y, 1 - c if fc else c)
            src = 4 * frm[0] + 2 * frm[1] + frm[2]
            pltpu.make_async_remote_copy(src_ref=v_ref, dst_ref=buf.at[src], send_sem=send.at[r - 1], recv_sem=recv.at[r - 1],
                                         device_id=frm, device_id_type=MESH).wait_recv()
        for cp in cps:
            cp.wait_send()
        acc = buf[0]
        for i in range(1, 8):
            acc = acc + buf[i]
        o_ref[...] = acc

    vm = pl.BlockSpec(memory_space=pltpu.VMEM)
    return _pcall(body, name="allreduce_small", in_specs=[vm], out_specs=vm, out_shape=jax.ShapeDtypeStruct((R, LANES), F32),
                  scratch_shapes=[pltpu.VMEM((8, R, LANES), F32), pltpu.SemaphoreType.DMA((7,)), pltpu.SemaphoreType.DMA((7,))])(vec)


def _unshard_cols(wg):
    ns, depth, K, Ns = wg.shape
    return jnp.moveaxis(wg, 0, 2).reshape(depth, K, ns * Ns)


def _shard_cols(w):
    K, N = w.shape
    return jnp.moveaxis(w.reshape(K, N_SHARD, N // N_SHARD), 1, 0)


def _uq_padded(cfg, wuq_g):
    depth = wuq_g.shape[1]
    w = _unshard_cols(wuq_g).reshape(depth, cfg.QL, cfg.AH, 192)
    w = jnp.pad(w, ((0, 0), (0, cfg.QLP - cfg.QL), (0, 0), (0, 64)))
    return w.reshape(1, depth, cfg.QLP, cfg.AH * 256)


def _uq_grad_unpadded(cfg, dw):
    w = dw[:cfg.QL].reshape(cfg.QL, cfg.AH, 256)[:, :, :192].reshape(cfg.QL, cfg.UQ)
    return _shard_cols(w)


def _ukv_padded(cfg, wukv_g):
    w = _unshard_cols(wukv_g)
    w = jnp.pad(w, ((0, 0), (cfg.KOFF, cfg.KVW - cfg.KVL - cfg.KOFF), (0, 0)))
    return w[None]


def _ukv_grad_unpadded(cfg, dw):
    return _shard_cols(dw[cfg.KOFF:cfg.KOFF + cfg.KVL])


def _pad_lanes(v, lo, total):
    return jnp.pad(v, (lo, total - lo - v.shape[0]))[None]


def _layer_fwd(cfg, l, x, W, small, tabs):
    ln1, gq, gkv, gqn, gkn, gout, ln2 = small
    sc_a, sc_h = 1.0 / math.sqrt(192), 1.0 / math.sqrt(128)
    n = f"l{l}_"
    h = _rms_fwd(n + "ln1", x, ln1)
    proj = _mm_nn(n + "proj", h, (W["w_in"], "col", 0))
    cqn, ckvn, kpe = _mla_prep_fwd(n + "mla_prep", cfg, proj, gq, gkv, tabs["akr"])
    qa = _mm_nn(n + "uq", cqn, (W["uq_p"], "col", 0))
    kva = _mm_nn(n + "ukv", ckvn, (W["ukv_p"], "col", 0))
    q_a, k_a, v_a = _mla_build_fwd(n + "mla_build", cfg, qa, kva, kpe, tabs["aq"])
    o_a, lse_a = _flash_fwd(n + "attn_a", q_a, k_a, v_a, cfg.AH, 1, 256, 128, sc_a)
    q_b = _headprep_fwd(n + "bq", proj, cfg.o_bq, cfg.BH, gqn, tabs["b"])
    k_b = _headprep_fwd(n + "bk", proj, cfg.o_bk, cfg.BKV, gkn, tabs["b"])
    v_b = _headprep_fwd(n + "bv", proj, cfg.o_bv, cfg.BKV, None, None)
    o_b, lse_b = _flash_fwd(n + "attn_b", q_b, k_b, v_b, cfg.BH, cfg.G, 128, 128, sc_h)
    q_c = _headprep_fwd(n + "cq", proj, cfg.o_cq, cfg.CH, None, tabs["c"])
    k_c = _headprep_fwd(n + "ck", proj, cfg.o_ck, cfg.CH, None, tabs["c"])
    v_c = _headprep_fwd(n + "cv", proj, cfg.o_cv, cfg.CH, None, None)
    o_c, lse_c = _flash_fwd(n + "attn_c", q_c, k_c, v_c, cfg.CH, 1, 128, 128, sc_h, tabs["bias_c"], cfg.W)
    mixed = _outnorm_fwd(n + "outnorm", cfg, o_a, o_b, o_c, gout)
    x1 = _mm_nn(n + "out", mixed, (W["w_out"], "row", 0), epi=_epi_residual, extra=x)
    h2 = _rms_fwd(n + "ln2", x1, ln2)
    a, u = _mm_nn(n + "ff1", h2, (W["w_ff1"], "col", 0), epi=_epi_relu2, out_dtypes=(MM_DT, MM_DT))
    x2 = _mm_nn(n + "ff2", u, (W["w_ff2"], "row", 0), epi=_epi_residual, extra=x1)
    saved = dict(x=x, h=h, proj=proj, cqn=cqn, ckvn=ckvn, q_a=q_a, k_a=k_a, v_a=v_a, o_a=o_a, lse_a=lse_a,
                 q_b=q_b, k_b=k_b, v_b=v_b, o_b=o_b, lse_b=lse_b, q_c=q_c, k_c=k_c, v_c=v_c, o_c=o_c, lse_c=lse_c,
                 mixed=mixed, x1=x1, h2=h2, a=a, u=u)
    return x2, saved


def _layer_bwd(cfg, l, dx2, sv, W, small, tabs, GW):
    ln1, gq, gkv, gqn, gkn, gout, ln2 = small
    sc_a, sc_h = 1.0 / math.sqrt(192), 1.0 / math.sqrt(128)
    n = f"l{l}_b_"
    S = cfg.S
    mats = {m[0]: m for m in cfg.mats}

    def dw(name, a, g, key):
        _, Rs, Cs, kind = mats[key]
        GW[key] = _mm_tn(n + name, a, g, kind, l, Rs, Cs, GW[key], out_dtype=MM_DT)

    da = _mm_nt(n + "ff2_dx", dx2, (W["w_ff2"], "row", 0), epi=_epi_drelu2, out_dtype=MM_DT, extra=sv["a"])
    dw("ff2_dw", sv["u"], dx2, "w_ff2")
    dh2 = _mm_nt(n + "ff1_dx", da, (W["w_ff1"], "col", 0))
    dw("ff1_dw", sv["h2"], da, "w_ff1")
    dx1, dln2 = _rms_bwd(n + "ln2", sv["x1"], ln2, dh2, dx2)
    dmix = _mm_nt(n + "out_dx", dx1, (W["w_out"], "row", 0))
    dw("out_dw", sv["mixed"], dx1, "w_out")
    do_a, do_b, do_c, dgout = _outnorm_bwd(n + "outnorm", cfg, dmix, sv["o_a"], sv["o_b"], sv["o_c"], gout)
    dproj = jnp.zeros((S, cfg.IN), MM_DT)
    args_c = (sv["q_c"], sv["k_c"], sv["v_c"], do_c, sv["o_c"], sv["lse_c"], cfg.CH, 1, 128, 128, sc_h, tabs["bias_c"], cfg.W)
    dq_c = _flash_dq(n + "attn_c_dq", *args_c)
    dk_c, dv_c = _flash_dkv(n + "attn_c_dkv", *args_c)
    dproj, _ = _headprep_bwd(n + "cq", dq_c, sv["proj"], cfg.o_cq, cfg.CH, None, tabs["c"], dproj)
    dproj, _ = _headprep_bwd(n + "ck", dk_c, sv["proj"], cfg.o_ck, cfg.CH, None, tabs["c"], dproj)
    dproj, _ = _headprep_bwd(n + "cv", dv_c, sv["proj"], cfg.o_cv, cfg.CH, None, None, dproj)
    args_b = (sv["q_b"], sv["k_b"], sv["v_b"], do_b, sv["o_b"], sv["lse_b"], cfg.BH, cfg.G, 128, 128, sc_h)
    dq_b = _flash_dq(n + "attn_b_dq", *args_b)
    dk_b, dv_b = _flash_dkv(n + "attn_b_dkv", *args_b)
    dproj, dgqn = _headprep_bwd(n + "bq", dq_b, sv["proj"], cfg.o_bq, cfg.BH, gqn, tabs["b"], dproj)
    dproj, dgkn = _headprep_bwd(n + "bk", dk_b, sv["proj"], cfg.o_bk, cfg.BKV, gkn, tabs["b"], dproj)
    dproj, _ = _headprep_bwd(n + "bv", dv_b, sv["proj"], cfg.o_bv, cfg.BKV, None, None, dproj)
    args_a = (sv["q_a"], sv["k_a"], sv["v_a"], do_a, sv["o_a"], sv["lse_a"], cfg.AH, 1, 256, 128, sc_a)
    dq_a = _flash_dq(n + "attn_a_dq", *args_a)
    dk_a, dv_a = _flash_dkv(n + "attn_a_dkv", *args_a)
    dqa, dkva, dkpe = _mla_build_bwd(n + "mla_build", cfg, dq_a, dk_a, dv_a, tabs["aq"])
    dcq = _mm_nt(n + "uq_dx", dqa, (W["uq_p"], "col", 0))
    dwuq = _mm_tn(n + "uq_dw", sv["cqn"], dqa, "col", 0, cfg.QLP, cfg.AH * 256, (1, 1, cfg.QLP, cfg.AH * 256))
    dckv = _mm_nt(n + "ukv_dx", dkva, (W["ukv_p"], "col", 0))
    dwukv = _mm_tn(n + "ukv_dw", sv["ckvn"], dkva, "col", 0, cfg.KVW, cfg.AH * 256, (1, 1, cfg.KVW, cfg.AH * 256))
    dproj, dgq, dgkv = _mla_prep_bwd(n + "mla_prep", cfg, dcq, dckv, dkpe, sv["proj"], gq, gkv, tabs["akr"], dproj)
    dh = _mm_nt(n + "proj_dx", dproj, (W["w_in"], "col", 0))
    dw("proj_dw", sv["h"], dproj, "w_in")
    dx, dln1 = _rms_bwd(n + "ln1", sv["x"], ln1, dh, dx1)
    gains = dict(ln1_g=dln1[0], g_q_a=dgq[0, :cfg.QL], g_kv_a=dgkv[0, cfg.KOFF:cfg.KOFF + cfg.KVL], g_qn_b=dgqn[0],
                 g_kn_b=dgkn[0], g_out=dgout[0], ln2_g=dln2[0])
    return dx, gains, _uq_grad_unpadded(cfg, dwuq[0, 0]), _ukv_grad_unpadded(cfg, dwukv[0, 0])


SMALL_NAMES = ("ln1_g", "g_q_a", "g_kv_a", "g_qn_b", "g_kn_b", "g_out", "ln2_g")
MAT_NAMES = ("w_in", "w_uq", "w_ukv", "w_out", "w_ff1", "w_ff2")


def _pack_small(cfg, per_layer, final, scalar=None):
    last = jnp.zeros((1,), F32) if scalar is None else scalar.reshape(1)
    flat = jnp.concatenate([per_layer[k].reshape(-1) for k in SMALL_NAMES] + [final.reshape(-1), last])
    total = flat.shape[0]
    rows = _rup(-(-total // LANES), 8)
    return jnp.pad(flat, (0, rows * LANES - total)).reshape(rows, LANES)


def _unpack_small(cfg, packed, shapes):
    flat = packed.reshape(-1)
    out, off = {}, 0
    for k in SMALL_NAMES + ("ln_f_g",):
        n = math.prod(shapes[k])
        out[k] = flat[off:off + n].reshape(shapes[k])
        off += n
    return out, flat[off]


def _step(cfg, w, m, v, x, tgt):
    DEPTH, hd = cfg.DEPTH, cfg.HD
    c = lax.axis_index("c")
    me_chip = 2 * lax.axis_index("x") + lax.axis_index("y")
    c_arr = jnp.reshape(c, (1,)).astype(jnp.int32)
    me_arr = jnp.reshape(me_chip, (1,)).astype(jnp.int32)
    mats = {mt[0]: mt for mt in cfg.mats}

    shards = []
    for name in MAT_NAMES:
        _, Ks, Ns, _ = mats[name]
        shards.append(_cast_rows("cast_" + name, w[name].reshape(DEPTH * Ks, Ns), MM_DT).reshape(DEPTH, Ks, Ns))
    W_layers = []
    for l in range(DEPTH):
        mine = [s[l] for s in shards]
        got = (_allgather_layer("allgather_l0", mine) if l == 0
               else _allgather_layer_async(f"allgather_l{l}", mine, collective_id=l))
        g = {name: a[:, None] for name, a in zip(MAT_NAMES, got)}
        W_layers.append(dict(w_in=g["w_in"], w_out=g["w_out"], w_ff1=g["w_ff1"], w_ff2=g["w_ff2"],
                             uq_p=_uq_padded(cfg, g["w_uq"]), ukv_p=_ukv_padded(cfg, g["w_ukv"])))
    tabs = _all_tables(cfg)
    tabs["bias_c"] = _band_bias(cfg)

    def small_of(l):
        return (w["ln1_g"][l][None], _pad_lanes(w["g_q_a"][l], 0, cfg.QLP), _pad_lanes(w["g_kv_a"][l], cfg.KOFF, cfg.KVW),
                w["g_qn_b"][l][None], w["g_kn_b"][l][None], w["g_out"][l][None], w["ln2_g"][l][None])

    saved = []
    xc = x
    for l in range(DEPTH):
        xc, sv = _layer_fwd(cfg, l, xc, W_layers[l], small_of(l), tabs)
        saved.append(sv)
    dx, dlnf, loss_rows = _final_loss("final_loss", xc, w["ln_f_g"][None], tgt)
    loss = loss_rows[0, 0]
    GW = {name: (N_SHARD, DEPTH, mats[name][1], mats[name][2]) for name in ("w_in", "w_out", "w_ff1", "w_ff2")}
    gain_rows = [None] * DEPTH
    duq, dukv = [None] * DEPTH, [None] * DEPTH
    for l in reversed(range(DEPTH)):
        dx, gain_rows[l], duq[l], dukv[l] = _layer_bwd(cfg, l, dx, saved[l], W_layers[l], small_of(l), tabs, GW)
    GW["w_uq"] = jnp.stack(duq, axis=1).astype(MM_DT)
    GW["w_ukv"] = jnp.stack(dukv, axis=1).astype(MM_DT)

    gws = [GW[name] for name in MAT_NAMES]
    ras = _exchange_sibling_halves(gws, hd)
    ps = [_add_sibling("rs_add_sib_" + name, g, r, c_arr, hd) for name, g, r in zip(MAT_NAMES, gws, ras)]
    rbs = _exchange_chips(ps)
    rs = [_add_chips("rs_add_chips_" + name, p, rb, me_arr) for name, p, rb in zip(MAT_NAMES, ps, rbs)]
    full = _share_reduced(rs)
    grad, delta, new_m, new_v = {}, {}, {}, {}
    for name, g2 in zip(MAT_NAMES, full):
        _, Ks, Ns, _ = mats[name]
        halves, shp = (2, hd * Ks, Ns), (DEPTH, Ks, Ns)
        res = _adamw_halves("adamw_" + name, w[name].reshape(halves), g2, m[name].reshape(halves), v[name].reshape(halves), c_arr)
        grad[name], delta[name], new_m[name], new_v[name] = (r.reshape(shp) for r in res)

    per_layer = {k: jnp.stack([gain_rows[l][k] for l in range(DEPTH)]) for k in SMALL_NAMES}
    shapes = {k: w[k].shape for k in SMALL_NAMES + ("ln_f_g",)}
    gsum = _allreduce_small(_pack_small(cfg, per_layer, dlnf[0], loss))
    pk = lambda d: _pack_small(cfg, {k: d[k] for k in SMALL_NAMES}, d["ln_f_g"])
    d_s, m_s, v_s = _adamw("adamw_small", pk(w), gsum, pk(m), pk(v))
    for res, packed in ((grad, gsum), (delta, d_s), (new_m, m_s), (new_v, v_s)):
        res.update(_unpack_small(cfg, packed, shapes)[0])
    loss_total = _unpack_small(cfg, gsum, shapes)[1]
    return loss_total, dx, grad, delta, new_m, new_v


WEIGHT_NAMES = ("ln1_g", "w_in", "g_q_a", "w_uq", "g_kv_a", "w_ukv", "g_qn_b", "g_kn_b", "g_out", "w_out", "ln2_g",
                "w_ff1", "w_ff2", "ln_f_g")


def _run(cfg, args):
    nw = len(WEIGHT_NAMES)
    x, tgt = args[0], args[1 + nw]
    w = dict(zip(WEIGHT_NAMES, args[1:1 + nw]))
    m = dict(zip(WEIGHT_NAMES, args[2 + nw:2 + 2 * nw]))
    v = dict(zip(WEIGHT_NAMES, args[2 + 2 * nw:2 + 3 * nw]))
    loss, dx, grad, delta, new_m, new_v = _step(cfg, w, m, v, x.reshape(cfg.S, cfg.D), tgt.reshape(cfg.S, cfg.D))
    return (loss, dx.reshape(x.shape), *[grad[k] for k in WEIGHT_NAMES], *[delta[k] for k in WEIGHT_NAMES],
            *[new_m[k] for k in WEIGHT_NAMES], *[new_v[k] for k in WEIGHT_NAMES])


def kernel(x, ln1_g, w_in, g_q_a, w_uq, g_kv_a, w_ukv, g_qn_b, g_kn_b, g_out, w_out, ln2_g, w_ff1, w_ff2, ln_f_g, loss_target, m_ln1_g, m_w_in, m_g_q_a, m_w_uq, m_g_kv_a, m_w_ukv, m_g_qn_b, m_g_kn_b, m_g_out, m_w_out, m_ln2_g, m_w_ff1, m_w_ff2, m_ln_f_g, v_ln1_g, v_w_in, v_g_q_a, v_w_uq, v_g_kv_a, v_w_ukv, v_g_qn_b, v_g_kn_b, v_g_out, v_w_out, v_ln2_g, v_w_ff1, v_w_ff2, v_ln_f_g):
    return _run(Cfg(), (x, ln1_g, w_in, g_q_a, w_uq, g_kv_a, w_ukv, g_qn_b, g_kn_b, g_out, w_out, ln2_g, w_ff1, w_ff2, ln_f_g, loss_target, m_ln1_g, m_w_in, m_g_q_a, m_w_uq, m_g_kv_a, m_w_ukv, m_g_qn_b, m_g_kn_b, m_g_out, m_w_out, m_ln2_g, m_w_ff1, m_w_ff2, m_ln_f_g, v_ln1_g, v_w_in, v_g_q_a, v_w_uq, v_g_kv_a, v_w_ukv, v_g_qn_b, v_g_kn_b, v_g_out, v_w_out, v_ln2_g, v_w_ff1, v_w_ff2, v_ln_f_g))
```

```python
import functools
import math

import jax
import jax.numpy as jnp
from jax import lax
from jax.experimental import pallas as pl
from jax.experimental.pallas import tpu as pltpu
from jax.experimental.pallas import tpu_sc as plsc

F32 = jnp.float32
MM_DT = jnp.bfloat16
LANES = 128
SUBLANES_F32 = 8
SUBLANES_BF16 = 16
VMEM_LIMIT = 48 * 1024 * 1024
EPS = 1e-6
NEG = -1e30
ROPE_THETA = 10000.0
MM_TK = 1024
ATT_TQ, ATT_TK_FWD, ATT_TK = 512, 4096, 2048
ADAM_LR, ADAM_B1, ADAM_B2, ADAM_EPS, ADAM_WD, ADAM_STEP = 0.001, 0.9, 0.999, 1e-08, 0.01, 10
MESH_AXES = ("x", "y", "c")
N_SHARD = 4
MESH = pl.DeviceIdType.MESH

NN = (((1,), (0,)), ((), ()))
NT = (((1,), (1,)), ((), ()))
TN = (((0,), (0,)), ((), ()))


def _pcall(body, **kw):
    return pl.pallas_call(body, **kw)


def _rup(n, m):
    return -(-n // m) * m


def _pick(n, pref, mult):
    best = None
    for t in range(mult, min(n, pref) + 1, mult):
        if n % t == 0:
            best = t
    return best if best is not None else n


class Cfg:
    def __init__(self, S=4096, D=2048, DEPTH=4, AH=4, QL=448, KVL=512, BH=6, BKV=2, CH=6,
                 BRANCHES=((128, 1), (512, 4), (2048, 16)), DFF=8192, GRID_W=64, TB=512, TBK=1024):
        self.S, self.D, self.DEPTH, self.AH, self.QL, self.KVL = S, D, DEPTH, AH, QL, KVL
        self.BH, self.BKV, self.CH, self.DFF, self.GRID_W, self.TB = BH, BKV, CH, DFF, GRID_W, TB
        self.G = BH // BKV
        self.AW, self.BW, self.CW = AH * 128, BH * 128, CH * 128
        self.MIX = self.AW + self.BW + self.CW
        self.QLP = _rup(QL, LANES)
        self.KV0 = (QL // LANES) * LANES
        self.PW = QL + KVL + 64
        assert self.PW % LANES == 0
        self.KVW = self.PW - self.KV0
        self.KOFF = QL - self.KV0
        self.o_bq = self.PW
        self.o_bk = self.o_bq + self.BW
        self.o_bv = self.o_bk + BKV * 128
        self.o_cq = self.o_bv + BKV * 128
        self.o_ck = self.o_cq + self.CW
        self.o_cv = self.o_ck + self.CW
        self.IN = self.o_cv + self.CW
        self.UQ, self.UKV = AH * 192, AH * 256
        self.branches = tuple(((w // (2 * d)) * d, d) for w, d in BRANCHES)
        for _, d in self.branches:
            assert d & (d - 1) == 0
        self.TBK = TBK
        self.W = -(-max(r for r, _ in self.branches) // TBK)
        assert S % TBK == 0 and TBK % TB == 0 and DEPTH % 2 == 0
        self.HD = DEPTH // 2
        self.mats = (("w_in", D, self.IN // 4, "col"), ("w_uq", QL, self.UQ // 4, "col"),
                     ("w_ukv", KVL, self.UKV // 4, "col"), ("w_out", self.MIX // 4, D, "row"),
                     ("w_ff1", D, DFF // 4, "col"), ("w_ff2", DFF // 4, D, "row"))


def _mm_call(name, mode, operands, in_specs, out_shape, out_specs, grid, acc_shape, epi, n_extra, aliases=None):
    nk = grid[2]

    def body(*refs):
        a_ref, b_ref = refs[0], refs[1]
        ex = refs[2:2 + n_extra]
        outs = refs[2 + n_extra:-1]
        acc = refs[-1]
        k = pl.program_id(2)

        @pl.when(k == 0)
        def _():
            acc[...] = jnp.zeros_like(acc)

        acc[...] += lax.dot_general(a_ref[...].astype(MM_DT), b_ref[...].astype(MM_DT), mode,
                                    preferred_element_type=F32)

        @pl.when(k == nk - 1)
        def _():
            epi(acc[...], ex, outs)

    return _pcall(body, name=name, grid=grid, in_specs=in_specs, out_specs=out_specs, out_shape=out_shape,
                  scratch_shapes=[pltpu.VMEM(acc_shape, F32)], input_output_aliases=aliases or {},
                  compiler_params=pltpu.CompilerParams(dimension_semantics=("parallel", "parallel", "arbitrary"),
                                                       vmem_limit_bytes=VMEM_LIMIT))(*operands)


def _wspec(kind, l, Rs, Cs, br, bc, rfn, cfn):
    assert Rs % br == 0 and Cs % bc == 0
    if kind == "col":
        npc = Cs // bc
        return pl.BlockSpec((None, None, br, bc), lambda i, j, k: (cfn(i, j, k) // npc, l, rfn(i, j, k), cfn(i, j, k) % npc))
    npr = Rs // br
    return pl.BlockSpec((None, None, br, bc), lambda i, j, k: (rfn(i, j, k) // npr, l, rfn(i, j, k) % npr, cfn(i, j, k)))


def _epi_plain(acc, ex, outs):
    outs[0][...] = acc.astype(outs[0].dtype)


def _epi_residual(acc, ex, outs):
    outs[0][...] = ex[0][...] + acc


def _epi_relu2(acc, ex, outs):
    outs[0][...] = acc.astype(outs[0].dtype)
    r = jnp.maximum(acc, 0.0)
    outs[1][...] = (r * r).astype(outs[1].dtype)


def _epi_drelu2(acc, ex, outs):
    a = ex[0][...].astype(F32)
    outs[0][...] = (acc * (2.0 * jnp.maximum(a, 0.0))).astype(outs[0].dtype)


def _wdims(wd):
    Wg, kind, l = wd
    ns, _, Rs, Cs = Wg.shape
    K = Rs * ns if kind == "row" else Rs
    N = Cs * ns if kind == "col" else Cs
    return Wg, kind, l, Rs, Cs, K, N


def _mm_nn(name, a, wd, epi=_epi_plain, out_dtypes=(F32,), extra=None):
    Wg, kind, l, Rs, Cs, K, N = _wdims(wd)
    M = a.shape[0]
    tm, tk, tn = _pick(M, 1024, 16), _pick(Rs, MM_TK, LANES), _pick(Cs, 1152, LANES)
    grid = (M // tm, N // tn, K // tk)
    in_specs = [pl.BlockSpec((tm, tk), lambda i, j, k: (i, k)),
                _wspec(kind, l, Rs, Cs, tk, tn, lambda i, j, k: k, lambda i, j, k: j)]
    ops = [a, Wg]
    if extra is not None:
        in_specs.append(pl.BlockSpec((tm, tn), lambda i, j, k: (i, j)))
        ops.append(extra)
    o_spec = pl.BlockSpec((tm, tn), lambda i, j, k: (i, j))
    outs = tuple(jax.ShapeDtypeStruct((M, N), dt) for dt in out_dtypes)
    res = _mm_call(name, NN, ops, in_specs, outs, tuple(o_spec for _ in outs), grid, (tm, tn), epi,
                   0 if extra is None else 1)
    return res[0] if len(res) == 1 else res


def _mm_nt(name, g, wd, epi=_epi_plain, out_dtype=F32, extra=None):
    Wg, kind, l, Rs, Cs, K, N = _wdims(wd)
    M = g.shape[0]
    tm, tn, tk = _pick(M, 1024, 16), _pick(Rs, 1024, LANES), _pick(Cs, 1152, LANES)
    grid = (M // tm, K // tn, N // tk)
    in_specs = [pl.BlockSpec((tm, tk), lambda i, j, k: (i, k)),
                _wspec(kind, l, Rs, Cs, tn, tk, lambda i, j, k: j, lambda i, j, k: k)]
    ops = [g, Wg]
    if extra is not None:
        in_specs.append(pl.BlockSpec((tm, tn), lambda i, j, k: (i, j)))
        ops.append(extra)
    res = _mm_call(name, NT, ops, in_specs, (jax.ShapeDtypeStruct((M, K), out_dtype),),
                   (pl.BlockSpec((tm, tn), lambda i, j, k: (i, j)),), grid, (tm, tn), epi, 0 if extra is None else 1)
    return res[0]


def _mm_tn(name, a, g, kind, l, Rs, Cs, buf, out_dtype=F32):
    M, K = a.shape
    N = g.shape[1]
    tm, tn, tk = _pick(Rs, 1024, LANES), _pick(Cs, 1152, LANES), _pick(M, MM_TK, LANES)
    grid = (K // tm, N // tn, M // tk)
    in_specs = [pl.BlockSpec((tk, tm), lambda i, j, k: (k, i)),
                pl.BlockSpec((tk, tn), lambda i, j, k: (k, j))]
    o_spec = _wspec(kind, l, Rs, Cs, tm, tn, lambda i, j, k: i, lambda i, j, k: j)
    if isinstance(buf, tuple):
        res = _mm_call(name, TN, [a, g], in_specs, (jax.ShapeDtypeStruct(buf, out_dtype),), (o_spec,), grid, (tm, tn), _epi_plain, 0)
    else:
        res = _mm_call(name, TN, [a, g, buf], in_specs + [pl.BlockSpec(memory_space=pl.ANY)],
                       (jax.ShapeDtypeStruct(buf.shape, buf.dtype),), (o_spec,), grid, (tm, tn), _epi_plain, 1, aliases={2: 0})
    return res[0]


def _row_params():
    return pltpu.CompilerParams(dimension_semantics=("arbitrary",), vmem_limit_bytes=VMEM_LIMIT)


def _rms_fwd(name, x, g):
    S, D = x.shape
    tr = _pick(S, 256, 16)

    def body(x_ref, g_ref, o_ref):
        xv = x_ref[...]
        r = lax.rsqrt(jnp.mean(xv * xv, axis=-1, keepdims=True) + EPS)
        o_ref[...] = (xv * r * g_ref[...]).astype(o_ref.dtype)

    return _pcall(body, name=name, grid=(S // tr,),
                  in_specs=[pl.BlockSpec((tr, D), lambda i: (i, 0)), pl.BlockSpec((1, D), lambda i: (0, 0))],
                  out_specs=pl.BlockSpec((tr, D), lambda i: (i, 0)), out_shape=jax.ShapeDtypeStruct((S, D), MM_DT),
                  compiler_params=_row_params())(x, g)


def _acc_rows(ref, part, first):
    @pl.when(first)
    def _():
        ref[...] = jnp.zeros_like(ref)

    ref[...] += jnp.broadcast_to(part, ref.shape)


def _rms_bwd(name, x, g, dy, res):
    S, D = x.shape
    tr = _pick(S, 256, 16)

    def body(x_ref, g_ref, dy_ref, res_ref, dx_ref, dg_ref):
        xv = x_ref[...]
        r = lax.rsqrt(jnp.mean(xv * xv, axis=-1, keepdims=True) + EPS)
        xh = xv * r
        dyv = dy_ref[...]
        dn = dyv * g_ref[...]
        dx_ref[...] = res_ref[...] + r * (dn - xh * jnp.mean(dn * xh, axis=-1, keepdims=True))
        _acc_rows(dg_ref, jnp.sum(dyv * xh, axis=0, keepdims=True), pl.program_id(0) == 0)

    row = pl.BlockSpec((tr, D), lambda i: (i, 0))
    return _pcall(body, name=name, grid=(S // tr,),
                  in_specs=[row, pl.BlockSpec((1, D), lambda i: (0, 0)), row, row],
                  out_specs=(row, pl.BlockSpec((8, D), lambda i: (0, 0))),
                  out_shape=(jax.ShapeDtypeStruct((S, D), F32), jax.ShapeDtypeStruct((8, D), F32)),
                  compiler_params=_row_params())(x, g, dy, res)


def _final_loss(name, x, g, tgt):
    S, D = x.shape
    tr = _pick(S, 256, 16)

    def body(x_ref, g_ref, t_ref, dx_ref, dg_ref, loss_ref):
        xv = x_ref[...]
        r = lax.rsqrt(jnp.mean(xv * xv, axis=-1, keepdims=True) + EPS)
        xh = xv * r
        gv = g_ref[...]
        e = xh * gv - t_ref[...]
        part = 0.5 * jnp.sum(jnp.mean(e * e, axis=-1, keepdims=True), axis=0, keepdims=True)
        dy = e * (1.0 / D)
        dn = dy * gv
        dx_ref[...] = r * (dn - xh * jnp.mean(dn * xh, axis=-1, keepdims=True))
        first = pl.program_id(0) == 0
        _acc_rows(dg_ref, jnp.sum(dy * xh, axis=0, keepdims=True), first)
        _acc_rows(loss_ref, part, first)

    row = pl.BlockSpec((tr, D), lambda i: (i, 0))
    return _pcall(body, name=name, grid=(S // tr,),
                  in_specs=[row, pl.BlockSpec((1, D), lambda i: (0, 0)), row],
                  out_specs=(row, pl.BlockSpec((8, D), lambda i: (0, 0)), pl.BlockSpec((8, LANES), lambda i: (0, 0))),
                  out_shape=(jax.ShapeDtypeStruct((S, D), F32), jax.ShapeDtypeStruct((8, D), F32),
                             jax.ShapeDtypeStruct((8, LANES), F32)),
                  compiler_params=_row_params())(x, g, tgt)


def _rope_tables(cos, sin, off, w):
    S = cos.shape[0]
    h = w // 2
    z = lambda n: jnp.zeros((S, n), F32)
    C = jnp.concatenate([z(off), cos, cos, z(LANES - off - w)], axis=1)
    SP = jnp.concatenate([z(off + h), sin, z(LANES - off - w)], axis=1)
    SN = jnp.concatenate([z(off), -sin, z(LANES - off - h)], axis=1)
    return C, SP, SN


def _angles(pos, dim):
    inv = jnp.power(ROPE_THETA, -jnp.arange(0, dim, 2, dtype=F32) / dim)
    ang = pos.astype(F32)[:, None] * inv[None, :]
    return jnp.cos(ang), jnp.sin(ang)


def _all_tables(cfg):
    S = cfg.S
    pos = jnp.arange(S, dtype=F32)
    rows = S // cfg.GRID_W
    row = jnp.repeat(jnp.arange(rows, dtype=F32), cfg.GRID_W)
    col = jnp.tile(jnp.arange(cfg.GRID_W, dtype=F32), rows)
    ca, sa = _angles(pos, 64)
    cc, sc = _angles(pos, 128)
    cr, sr = _angles(row, 64)
    cl, sl = _angles(col, 64)
    t_b = tuple(a + b for a, b in zip(_rope_tables(cr, sr, 0, 64), _rope_tables(cl, sl, 64, 64)))
    return {"aq": (_rope_tables(ca, sa, 0, 64), 64), "akr": (_rope_tables(ca, sa, 64, 64), 64),
            "b": (t_b, 64), "c": (_rope_tables(cc, sc, 0, 128), 128)}


def _rope(x, C, SP, SN, w):
    h = w // 2
    if 2 * h == LANES:
        return x * C + pltpu.roll(x, h, 1) * (SP + SN)
    return x * C + pltpu.roll(x, h, 1) * SP + pltpu.roll(x, LANES - h, 1) * SN


def _rope_t(dy, C, SP, SN, w):
    h = w // 2
    if 2 * h == LANES:
        return dy * C + pltpu.roll(dy * (SP + SN), h, 1)
    return dy * C + pltpu.roll(dy * SP, LANES - h, 1) + pltpu.roll(dy * SN, h, 1)


def _grid2_params():
    return pltpu.CompilerParams(dimension_semantics=("arbitrary", "arbitrary"), vmem_limit_bytes=VMEM_LIMIT)


def _headprep_fwd(name, proj, col_off, nb, gain, tabs):
    S = proj.shape[0]
    tr = _pick(S, 1024, 16)
    cb = col_off // LANES
    norm, rope = gain is not None, tabs is not None
    w = tabs[1] if rope else 0

    def body(*refs):
        x_ref = refs[0]
        pos = 1
        xv = x_ref[...]
        if norm:
            r = lax.rsqrt(jnp.mean(xv * xv, axis=-1, keepdims=True) + EPS)
            xv = xv * r * refs[pos][...]
            pos += 1
        if rope:
            xv = _rope(xv, refs[pos][...], refs[pos + 1][...], refs[pos + 2][...], w)
            pos += 3
        refs[pos][...] = xv.astype(refs[pos].dtype)

    ops, in_specs = [proj], [pl.BlockSpec((tr, LANES), lambda i, j: (i, cb + j))]
    if norm:
        ops.append(gain)
        in_specs.append(pl.BlockSpec((1, LANES), lambda i, j: (0, 0)))
    if rope:
        ops += list(tabs[0])
        in_specs += [pl.BlockSpec((tr, LANES), lambda i, j: (i, 0))] * 3
    return _pcall(body, name=name, grid=(S // tr, nb), in_specs=in_specs,
                  out_specs=pl.BlockSpec((tr, LANES), lambda i, j: (i, j)),
                  out_shape=jax.ShapeDtypeStruct((S, nb * LANES), MM_DT), compiler_params=_grid2_params())(*ops)


def _headprep_bwd(name, dy, proj, col_off, nb, gain, tabs, dproj):
    S = proj.shape[0]
    tr = _pick(S, 1024, 16)
    cb = col_off // LANES
    norm, rope = gain is not None, tabs is not None
    w = tabs[1] if rope else 0

    def body(*refs):
        dz = refs[0][...]
        pos = 1
        if norm:
            x_ref, g_ref = refs[pos], refs[pos + 1]
            pos += 2
        if rope:
            dz = _rope_t(dz, refs[pos][...], refs[pos + 1][...], refs[pos + 2][...], w)
            pos += 3
        pos += 1
        o_ref = refs[pos]
        if norm:
            dg_ref = refs[pos + 1]
            xv = x_ref[...]
            r = lax.rsqrt(jnp.mean(xv * xv, axis=-1, keepdims=True) + EPS)
            n = xv * r
            first = (pl.program_id(0) == 0) & (pl.program_id(1) == 0)
            _acc_rows(dg_ref, jnp.sum(dz * n, axis=0, keepdims=True), first)
            dn = dz * g_ref[...]
            dz = r * (dn - n * jnp.mean(dn * n, axis=-1, keepdims=True))
        o_ref[...] = dz.astype(o_ref.dtype)

    ops, in_specs = [dy], [pl.BlockSpec((tr, LANES), lambda i, j: (i, j))]
    if norm:
        ops += [proj, gain]
        in_specs += [pl.BlockSpec((tr, LANES), lambda i, j: (i, cb + j)), pl.BlockSpec((1, LANES), lambda i, j: (0, 0))]
    if rope:
        ops += list(tabs[0])
        in_specs += [pl.BlockSpec((tr, LANES), lambda i, j: (i, 0))] * 3
    alias_idx = len(ops)
    ops.append(dproj)
    in_specs.append(pl.BlockSpec(memory_space=pl.ANY))
    out_specs = [pl.BlockSpec((tr, LANES), lambda i, j: (i, cb + j))]
    out_shape = [jax.ShapeDtypeStruct(dproj.shape, dproj.dtype)]
    if norm:
        out_specs.append(pl.BlockSpec((8, LANES), lambda i, j: (0, 0)))
        out_shape.append(jax.ShapeDtypeStruct((8, LANES), F32))
    res = _pcall(body, name=name, grid=(S // tr, nb), in_specs=in_specs, out_specs=tuple(out_specs),
                 out_shape=tuple(out_shape), input_output_aliases={alias_idx: 0}, compiler_params=_grid2_params())(*ops)
    return (res[0], res[1]) if norm else (res[0], None)


def _masked_rms(xv, lo, n):
    lane = lax.broadcasted_iota(jnp.int32, xv.shape, 1)
    xm = jnp.where((lane >= lo) & (lane < lo + n), xv, 0.0)
    r = lax.rsqrt(jnp.sum(xm * xm, axis=-1, keepdims=True) * (1.0 / n) + EPS)
    return xm * r, r


def _mla_prep_fwd(name, cfg, proj, gq, gkv, tabs):
    S = cfg.S
    tr = _pick(S, 256, 16)
    (C, SP, SN), w = tabs

    def body(p_ref, gq_ref, gkv_ref, c_ref, sp_ref, sn_ref, cq_ref, ckv_ref, kpe_ref):
        nq, _ = _masked_rms(p_ref[:, 0:cfg.QLP], 0, cfg.QL)
        cq_ref[...] = (nq * gq_ref[...]).astype(cq_ref.dtype)
        nk, _ = _masked_rms(p_ref[:, cfg.KV0:cfg.PW], cfg.KOFF, cfg.KVL)
        ckv_ref[...] = (nk * gkv_ref[...]).astype(ckv_ref.dtype)
        kr = _rope(p_ref[:, cfg.PW - LANES:cfg.PW], c_ref[...], sp_ref[...], sn_ref[...], w)
        kpe_ref[...] = pltpu.roll(kr, 64, 1).astype(kpe_ref.dtype)

    tab = pl.BlockSpec((tr, LANES), lambda i: (i, 0))
    return _pcall(body, name=name, grid=(S // tr,),
                  in_specs=[pl.BlockSpec((tr, cfg.PW), lambda i: (i, 0)), pl.BlockSpec((1, cfg.QLP), lambda i: (0, 0)),
                            pl.BlockSpec((1, cfg.KVW), lambda i: (0, 0)), tab, tab, tab],
                  out_specs=(pl.BlockSpec((tr, cfg.QLP), lambda i: (i, 0)), pl.BlockSpec((tr, cfg.KVW), lambda i: (i, 0)), tab),
                  out_shape=(jax.ShapeDtypeStruct((S, cfg.QLP), MM_DT), jax.ShapeDtypeStruct((S, cfg.KVW), MM_DT),
                             jax.ShapeDtypeStruct((S, LANES), MM_DT)),
                  compiler_params=_row_params())(proj, gq, gkv, C, SP, SN)


def _mla_prep_bwd(name, cfg, dcq, dckv, dkpe, proj, gq, gkv, tabs, dproj):
    S = cfg.S
    tr = _pick(S, 256, 16)
    (C, SP, SN), w = tabs

    def body(dcq_ref, dckv_ref, dkpe_ref, p_ref, gq_ref, gkv_ref, c_ref, sp_ref, sn_ref, buf_ref, o_ref, dgq_ref, dgkv_ref):
        first = pl.program_id(0) == 0

        def norm_bwd(xv, lo, n, dz, g_ref, dg_ref):
            nrm, r = _masked_rms(xv, lo, n)
            _acc_rows(dg_ref, jnp.sum(dz * nrm, axis=0, keepdims=True), first)
            dn = dz * g_ref[...]
            return r * (dn - nrm * (jnp.sum(dn * nrm, axis=-1, keepdims=True) * (1.0 / n)))

        dxq = norm_bwd(p_ref[:, 0:cfg.QLP], 0, cfg.QL, dcq_ref[...], gq_ref, dgq_ref)
        dxk = norm_bwd(p_ref[:, cfg.KV0:cfg.PW], cfg.KOFF, cfg.KVL, dckv_ref[...], gkv_ref, dgkv_ref)
        dxr = _rope_t(pltpu.roll(dkpe_ref[...], 64, 1), c_ref[...], sp_ref[...], sn_ref[...], w)
        for cidx in range(cfg.PW // LANES):
            lo = cidx * LANES
            parts = []
            if lo < cfg.QLP:
                parts.append(dxq[:, lo:lo + LANES])
            if lo >= cfg.KV0:
                parts.append(dxk[:, lo - cfg.KV0:lo - cfg.KV0 + LANES])
            if lo == cfg.PW - LANES:
                parts.append(dxr)
            o_ref[:, lo:lo + LANES] = functools.reduce(lambda a, b: a + b, parts).astype(o_ref.dtype)

    tab = pl.BlockSpec((tr, LANES), lambda i: (i, 0))
    res = _pcall(body, name=name, grid=(S // tr,),
                 in_specs=[pl.BlockSpec((tr, cfg.QLP), lambda i: (i, 0)), pl.BlockSpec((tr, cfg.KVW), lambda i: (i, 0)), tab,
                           pl.BlockSpec((tr, cfg.PW), lambda i: (i, 0)), pl.BlockSpec((1, cfg.QLP), lambda i: (0, 0)),
                           pl.BlockSpec((1, cfg.KVW), lambda i: (0, 0)), tab, tab, tab, pl.BlockSpec(memory_space=pl.ANY)],
                 out_specs=(pl.BlockSpec((tr, cfg.PW), lambda i: (i, 0)), pl.BlockSpec((8, cfg.QLP), lambda i: (0, 0)),
                            pl.BlockSpec((8, cfg.KVW), lambda i: (0, 0))),
                 out_shape=(jax.ShapeDtypeStruct(dproj.shape, dproj.dtype), jax.ShapeDtypeStruct((8, cfg.QLP), F32),
                            jax.ShapeDtypeStruct((8, cfg.KVW), F32)),
                 input_output_aliases={9: 0}, compiler_params=_row_params())(dcq, dckv, dkpe, proj, gq, gkv, C, SP, SN, dproj)
    return res


def _mla_build_fwd(name, cfg, qa, kva, kpe, tabs):
    S, AH = cfg.S, cfg.AH
    tr = _pick(S, 256, 16)
    (C, SP, SN), w = tabs

    def body(qa_ref, kva_ref, kpe_ref, c_ref, sp_ref, sn_ref, q_ref, k_ref, v_ref):
        for h in range(AH):
            a, b = 256 * h, 256 * h + LANES
            q_ref[:, a:b] = qa_ref[:, a:b].astype(q_ref.dtype)
            q_ref[:, b:b + LANES] = _rope(qa_ref[:, b:b + LANES], c_ref[...], sp_ref[...], sn_ref[...], w).astype(q_ref.dtype)
            k_ref[:, a:b] = kva_ref[:, a:b].astype(k_ref.dtype)
            k_ref[:, b:b + LANES] = kpe_ref[...]
            v_ref[:, LANES * h:LANES * (h + 1)] = kva_ref[:, b:b + LANES].astype(v_ref.dtype)

    tab = pl.BlockSpec((tr, LANES), lambda i: (i, 0))
    wide = pl.BlockSpec((tr, AH * 256), lambda i: (i, 0))
    return _pcall(body, name=name, grid=(S // tr,), in_specs=[wide, wide, tab, tab, tab, tab],
                  out_specs=(wide, wide, pl.BlockSpec((tr, AH * LANES), lambda i: (i, 0))),
                  out_shape=(jax.ShapeDtypeStruct((S, AH * 256), MM_DT), jax.ShapeDtypeStruct((S, AH * 256), MM_DT),
                             jax.ShapeDtypeStruct((S, AH * LANES), MM_DT)),
                  compiler_params=_row_params())(qa, kva, kpe, C, SP, SN)


def _mla_build_bwd(name, cfg, dq, dk, dv, tabs):
    S, AH = cfg.S, cfg.AH
    tr = _pick(S, 256, 16)
    (C, SP, SN), w = tabs

    def body(dq_ref, dk_ref, dv_ref, c_ref, sp_ref, sn_ref, dqa_ref, dkva_ref, dkpe_ref):
        dkpe = None
        for h in range(AH):
            a, b = 256 * h, 256 * h + LANES
            dqa_ref[:, a:b] = dq_ref[:, a:b].astype(dqa_ref.dtype)
            dqa_ref[:, b:b + LANES] = _rope_t(dq_ref[:, b:b + LANES], c_ref[...], sp_ref[...], sn_ref[...], w).astype(dqa_ref.dtype)
            dkva_ref[:, a:b] = dk_ref[:, a:b].astype(dkva_ref.dtype)
            dkva_ref[:, b:b + LANES] = dv_ref[:, LANES * h:LANES * (h + 1)].astype(dkva_ref.dtype)
            part = dk_ref[:, b:b + LANES]
            dkpe = part if dkpe is None else dkpe + part
        dkpe_ref[...] = dkpe

    tab = pl.BlockSpec((tr, LANES), lambda i: (i, 0))
    wide = pl.BlockSpec((tr, AH * 256), lambda i: (i, 0))
    return _pcall(body, name=name, grid=(S // tr,),
                  in_specs=[wide, wide, pl.BlockSpec((tr, AH * LANES), lambda i: (i, 0)), tab, tab, tab],
                  out_specs=(wide, wide, tab),
                  out_shape=(jax.ShapeDtypeStruct((S, AH * 256), MM_DT), jax.ShapeDtypeStruct((S, AH * 256), MM_DT),
                             jax.ShapeDtypeStruct((S, LANES), F32)),
                  compiler_params=_row_params())(dq, dk, dv, C, SP, SN)


def _outnorm_fwd(name, cfg, oa, ob, oc, g):
    S = cfg.S
    tr = _pick(S, 256, 16)
    widths = (cfg.AW, cfg.BW, cfg.CW)

    def body(a_ref, b_ref, c_ref, g_ref, o_ref):
        off = 0
        for ref, wd in zip((a_ref, b_ref, c_ref), widths):
            v = ref[...]
            r = lax.rsqrt(jnp.mean(v * v, axis=-1, keepdims=True) + EPS)
            o_ref[:, off:off + wd] = (v * r * g_ref[:, off:off + wd]).astype(o_ref.dtype)
            off += wd

    return _pcall(body, name=name, grid=(S // tr,),
                  in_specs=[pl.BlockSpec((tr, wd), lambda i: (i, 0)) for wd in widths] + [pl.BlockSpec((1, cfg.MIX), lambda i: (0, 0))],
                  out_specs=pl.BlockSpec((tr, cfg.MIX), lambda i: (i, 0)),
                  out_shape=jax.ShapeDtypeStruct((S, cfg.MIX), MM_DT), compiler_params=_row_params())(oa, ob, oc, g)


def _outnorm_bwd(name, cfg, dmix, oa, ob, oc, g):
    S = cfg.S
    tr = _pick(S, 256, 16)
    widths = (cfg.AW, cfg.BW, cfg.CW)

    def body(dm_ref, a_ref, b_ref, c_ref, g_ref, da_ref, db_ref, dc_ref, dg_ref):
        off = 0
        parts = []
        for ref, dref, wd in zip((a_ref, b_ref, c_ref), (da_ref, db_ref, dc_ref), widths):
            v = ref[...]
            r = lax.rsqrt(jnp.mean(v * v, axis=-1, keepdims=True) + EPS)
            n = v * r
            dm = dm_ref[:, off:off + wd]
            parts.append(jnp.sum(dm * n, axis=0, keepdims=True))
            dn = dm * g_ref[:, off:off + wd]
            dref[...] = r * (dn - n * jnp.mean(dn * n, axis=-1, keepdims=True))
            off += wd
        _acc_rows(dg_ref, jnp.concatenate(parts, axis=1), pl.program_id(0) == 0)

    segs = [pl.BlockSpec((tr, wd), lambda i: (i, 0)) for wd in widths]
    return _pcall(body, name=name, grid=(S // tr,),
                  in_specs=[pl.BlockSpec((tr, cfg.MIX), lambda i: (i, 0))] + segs + [pl.BlockSpec((1, cfg.MIX), lambda i: (0, 0))],
                  out_specs=tuple(segs) + (pl.BlockSpec((8, cfg.MIX), lambda i: (0, 0)),),
                  out_shape=tuple(jax.ShapeDtypeStruct((S, wd), F32) for wd in widths) + (jax.ShapeDtypeStruct((8, cfg.MIX), F32),),
                  compiler_params=_row_params())(dmix, oa, ob, oc, g)


def _band_bias(cfg):
    tq, tk, W = cfg.TB, cfg.TBK, cfg.W
    ns = 2 * W + 1
    shape = ((tk // tq) * ns, tq, tk)
    slab = lax.broadcasted_iota(jnp.int32, shape, 0)
    row = lax.broadcasted_iota(jnp.int32, shape, 1)
    col = lax.broadcasted_iota(jnp.int32, shape, 2)
    d = (slab // ns) * tq + (W - slab % ns) * tk + row - col
    ad = jnp.abs(d)
    m = jnp.zeros(d.shape, F32)
    for reach, dil in cfg.branches:
        ok = ad <= reach
        if dil > 1:
            ok = ok & ((d & (dil - 1)) == 0)
        m = m + ok.astype(F32)
    return jnp.where(m > 0, jnp.log(jnp.maximum(m, 1.0)), NEG)


def _attn_params():
    return pltpu.CompilerParams(dimension_semantics=("parallel", "parallel", "arbitrary"), vmem_limit_bytes=VMEM_LIMIT)


def _scores(q_ref, k_ref, scale, bias_ref):
    s = lax.dot_general(q_ref[...], k_ref[...], NT, preferred_element_type=F32) * scale
    return s if bias_ref is None else s + bias_ref[...]


def _flash_fwd(name, q, k, v, H, G, dk, dv, scale, bias=None, W=None):
    S = q.shape[0]
    band = bias is not None
    tq = bias.shape[1] if band else _pick(S, ATT_TQ, LANES)
    tk = bias.shape[2] if band else _pick(S, ATT_TK_FWD, LANES)
    n = S // tk
    nsteps = 2 * W + 1 if band else n
    R = tk // tq

    def kblock(qi, st):
        return jnp.clip(qi // R - W + st, 0, n - 1) if band else st

    def body(*refs):
        q_ref, k_ref, v_ref = refs[:3]
        bias_ref = refs[3] if band else None
        o_ref, lse_ref, m_sc, l_sc, acc_sc = refs[-5:]
        qi, st = pl.program_id(1), pl.program_id(2)

        @pl.when(st == 0)
        def _():
            m_sc[...] = jnp.full_like(m_sc, NEG)
            l_sc[...] = jnp.zeros_like(l_sc)
            acc_sc[...] = jnp.zeros_like(acc_sc)

        kj = qi // R - W + st if band else st

        def step():
            s = _scores(q_ref, k_ref, scale, bias_ref)
            m_prev = m_sc[...]
            m_new = jnp.maximum(m_prev, jnp.max(s, axis=-1, keepdims=True))
            alpha = jnp.exp(m_prev - m_new)
            p = jnp.exp(s - m_new)
            l_sc[...] = alpha * l_sc[...] + jnp.sum(p, axis=-1, keepdims=True)
            acc_sc[...] = alpha * acc_sc[...] + lax.dot_general(p.astype(MM_DT), v_ref[...], NN, preferred_element_type=F32)
            m_sc[...] = m_new

        if band:
            pl.when((kj >= 0) & (kj < n))(step)
        else:
            step()

        @pl.when(st == nsteps - 1)
        def _():
            l = l_sc[...]
            o_ref[...] = acc_sc[...] / l
            lse_ref[...] = jnp.broadcast_to(m_sc[...] + jnp.log(l), lse_ref.shape)

    in_specs = [pl.BlockSpec((tq, dk), lambda h, qi, st: (qi, h)),
                pl.BlockSpec((tk, dk), lambda h, qi, st: (kblock(qi, st), h // G)),
                pl.BlockSpec((tk, dv), lambda h, qi, st: (kblock(qi, st), h // G))]
    ops = [q, k, v]
    if band:
        in_specs.append(pl.BlockSpec((None, tq, tk), lambda h, qi, st: ((qi % R) * nsteps + st, 0, 0)))
        ops.append(bias)
    return _pcall(body, name=name, grid=(H, S // tq, nsteps), in_specs=in_specs,
                  out_specs=(pl.BlockSpec((tq, dv), lambda h, qi, st: (qi, h)),
                             pl.BlockSpec((None, tq, LANES), lambda h, qi, st: (h, qi, 0))),
                  out_shape=(jax.ShapeDtypeStruct((S, H * dv), F32), jax.ShapeDtypeStruct((H, S, LANES), F32)),
                  scratch_shapes=[pltpu.VMEM((tq, 1), F32), pltpu.VMEM((tq, 1), F32), pltpu.VMEM((tq, dv), F32)],
                  compiler_params=_attn_params())(*ops)


def _flash_dq(name, q, k, v, do, o, lse, H, G, dk, dv, scale, bias=None, W=None):
    S = q.shape[0]
    band = bias is not None
    tq = bias.shape[1] if band else _pick(S, ATT_TQ, LANES)
    tk = bias.shape[2] if band else _pick(S, ATT_TK, LANES)
    n = S // tk
    nsteps = 2 * W + 1 if band else n
    R = tk // tq

    def kblock(qi, st):
        return jnp.clip(qi // R - W + st, 0, n - 1) if band else st

    def body(*refs):
        q_ref, k_ref, v_ref, do_ref, o_ref, lse_ref = refs[:6]
        bias_ref = refs[6] if band else None
        dq_ref, delta_sc, acc_sc = refs[-3:]
        qi, st = pl.program_id(1), pl.program_id(2)

        @pl.when(st == 0)
        def _():
            delta_sc[...] = jnp.sum(do_ref[...] * o_ref[...], axis=-1, keepdims=True)
            acc_sc[...] = jnp.zeros_like(acc_sc)

        kj = qi // R - W + st if band else st

        def step():
            p = jnp.exp(_scores(q_ref, k_ref, scale, bias_ref) - lse_ref[:, 0:1])
            dp = lax.dot_general(do_ref[...].astype(MM_DT), v_ref[...], NT, preferred_element_type=F32)
            ds = p * (dp - delta_sc[...]) * scale
            acc_sc[...] += lax.dot_general(ds.astype(MM_DT), k_ref[...], NN, preferred_element_type=F32)

        if band:
            pl.when((kj >= 0) & (kj < n))(step)
        else:
            step()

        @pl.when(st == nsteps - 1)
        def _():
            dq_ref[...] = acc_sc[...]

    qspec = lambda wd: pl.BlockSpec((tq, wd), lambda h, qi, st: (qi, h))
    in_specs = [qspec(dk),
                pl.BlockSpec((tk, dk), lambda h, qi, st: (kblock(qi, st), h // G)),
                pl.BlockSpec((tk, dv), lambda h, qi, st: (kblock(qi, st), h // G)),
                qspec(dv), qspec(dv),
                pl.BlockSpec((None, tq, LANES), lambda h, qi, st: (h, qi, 0))]
    ops = [q, k, v, do, o, lse]
    if band:
        in_specs.append(pl.BlockSpec((None, tq, tk), lambda h, qi, st: ((qi % R) * nsteps + st, 0, 0)))
        ops.append(bias)
    return _pcall(body, name=name, grid=(H, S // tq, nsteps), in_specs=in_specs,
                  out_specs=qspec(dk), out_shape=jax.ShapeDtypeStruct((S, H * dk), F32),
                  scratch_shapes=[pltpu.VMEM((tq, 1), F32), pltpu.VMEM((tq, dk), F32)],
                  compiler_params=_attn_params())(*ops)


def _flash_dkv(name, q, k, v, do, o, lse, H, G, dk, dv, scale, bias=None, W=None):
    S = q.shape[0]
    band = bias is not None
    tq = bias.shape[1] if band else _pick(S, ATT_TQ, LANES)
    tk = bias.shape[2] if band else _pick(S, ATT_TK, LANES)
    n = S // tq
    R = tk // tq
    nq = R * (2 * W + 1) if band else n
    nsteps = G * nq
    Hkv = H // G

    def qhead(hk, st):
        return hk * G + st // nq

    def qblock(kj, st):
        return jnp.clip(R * (kj - W) + st % nq, 0, n - 1) if band else st % nq

    def body(*refs):
        q_ref, k_ref, v_ref, do_ref, o_ref, lse_ref = refs[:6]
        bias_ref = refs[6] if band else None
        dk_ref, dv_ref, dk_sc, dv_sc = refs[-4:]
        kj, st = pl.program_id(1), pl.program_id(2)

        @pl.when(st == 0)
        def _():
            dk_sc[...] = jnp.zeros_like(dk_sc)
            dv_sc[...] = jnp.zeros_like(dv_sc)

        qi = R * (kj - W) + st % nq if band else st % nq

        def step():
            p = jnp.exp(_scores(q_ref, k_ref, scale, bias_ref) - lse_ref[:, 0:1])
            dof = do_ref[...]
            dob = dof.astype(MM_DT)
            dv_sc[...] += lax.dot_general(p.astype(MM_DT), dob, TN, preferred_element_type=F32)
            dp = lax.dot_general(dob, v_ref[...], NT, preferred_element_type=F32)
            delta = jnp.sum(dof * o_ref[...], axis=-1, keepdims=True)
            ds = p * (dp - delta) * scale
            dk_sc[...] += lax.dot_general(ds.astype(MM_DT), q_ref[...], TN, preferred_element_type=F32)

        if band:
            pl.when((qi >= 0) & (qi < n))(step)
        else:
            step()

        @pl.when(st == nsteps - 1)
        def _():
            dk_ref[...] = dk_sc[...]
            dv_ref[...] = dv_sc[...]

    qspec = lambda wd: pl.BlockSpec((tq, wd), lambda hk, kj, st: (qblock(kj, st), qhead(hk, st)))
    kspec = lambda wd: pl.BlockSpec((tk, wd), lambda hk, kj, st: (kj, hk))
    in_specs = [qspec(dk), kspec(dk), kspec(dv), qspec(dv), qspec(dv),
                pl.BlockSpec((None, tq, LANES), lambda hk, kj, st: (qhead(hk, st), qblock(kj, st), 0))]
    ops = [q, k, v, do, o, lse]
    if band:
        in_specs.append(pl.BlockSpec((None, tq, tk),
                                     lambda hk, kj, st: ((st % nq % R) * (2 * W + 1) + 2 * W - (st % nq) // R, 0, 0)))
        ops.append(bias)
    return _pcall(body, name=name, grid=(Hkv, S // tk, nsteps), in_specs=in_specs,
                  out_specs=(kspec(dk), kspec(dv)),
                  out_shape=(jax.ShapeDtypeStruct((S, Hkv * dk), F32), jax.ShapeDtypeStruct((S, Hkv * dv), F32)),
                  scratch_shapes=[pltpu.VMEM((tk, dk), F32), pltpu.VMEM((tk, dv), F32)],
                  compiler_params=_attn_params())(*ops)


def _rowtile(rows, cols):
    return _pick(rows, max(16, (512 * 1024) // cols // 16 * 16), 16)


def _cast_rows(name, w, dtype):
    R, C = w.shape
    tr = _rowtile(R, C)

    def body(w_ref, o_ref):
        o_ref[...] = w_ref[...].astype(o_ref.dtype)

    spec = pl.BlockSpec((tr, C), lambda i: (i, 0))
    return _pcall(body, name=name, grid=(R // tr,), in_specs=[spec], out_specs=spec,
                  out_shape=jax.ShapeDtypeStruct((R, C), dtype), compiler_params=_row_params())(w)


def _add_sibling(name, gw, ra, c_arr, hd):
    ns, depth, Ks, Ns = gw.shape
    rows = hd * Ks
    tr = _rowtile(rows, Ns)
    gw_v = gw.reshape(ns, 2, rows, Ns)
    ra_v = ra.reshape(ns, rows, Ns)

    def body(c_ref, g_ref, r_ref, o_ref):
        o_ref[...] = (g_ref[...].astype(F32) + r_ref[...].astype(F32)).astype(o_ref.dtype)

    grid_spec = pltpu.PrefetchScalarGridSpec(
        num_scalar_prefetch=1, grid=(ns, rows // tr),
        in_specs=[pl.BlockSpec((None, None, tr, Ns), lambda s, r, c_ref: (s, c_ref[0], r, 0)),
                  pl.BlockSpec((None, tr, Ns), lambda s, r, c_ref: (s, r, 0))],
        out_specs=pl.BlockSpec((None, tr, Ns), lambda s, r, c_ref: (s, r, 0)))
    return _pcall(body, name=name, grid_spec=grid_spec, out_shape=jax.ShapeDtypeStruct((ns, rows, Ns), MM_DT),
                  compiler_params=_grid2_params())(c_arr, gw_v, ra_v)


def _add_chips(name, p, rb, me_arr):
    ns, rows, Ns = p.shape
    tr = _rowtile(rows, Ns)

    def body(me_ref, p_ref, b0_ref, b1_ref, b2_ref, o_ref):
        o_ref[...] = ((p_ref[...].astype(F32) + b0_ref[...].astype(F32)) + b1_ref[...].astype(F32)) + b2_ref[...].astype(F32)

    grid_spec = pltpu.PrefetchScalarGridSpec(
        num_scalar_prefetch=1, grid=(rows // tr,),
        in_specs=[pl.BlockSpec((None, tr, Ns), lambda r, me_ref: (me_ref[0], r, 0))] +
                 [pl.BlockSpec((None, tr, Ns), functools.partial(lambda r, me_ref, j: (j, r, 0), j=j)) for j in range(3)],
        out_specs=pl.BlockSpec((None, tr, Ns), lambda r, me_ref: (0, r, 0)))
    return _pcall(body, name=name, grid_spec=grid_spec, out_shape=jax.ShapeDtypeStruct((2, rows, Ns), F32),
                  compiler_params=_row_params())(me_arr, p, rb, rb, rb)


def _adamw_math(wv, gv, mv, vv):
    bc1 = 1.0 - ADAM_B1 ** ADAM_STEP
    bc2 = 1.0 - ADAM_B2 ** ADAM_STEP
    mn = ADAM_B1 * mv + (1.0 - ADAM_B1) * gv
    vn = ADAM_B2 * vv + (1.0 - ADAM_B2) * jnp.square(gv)
    m_hat = mn / bc1
    v_hat = vn / bc2
    return -ADAM_LR * (m_hat / (jnp.sqrt(v_hat) + ADAM_EPS) + ADAM_WD * wv), mn, vn


def _adamw_halves(name, w, g2, m, v, c_arr):
    depth, _, R, C = w.shape
    tr = _rowtile(R, C)

    def body(c_ref, w_ref, g_ref, m_ref, v_ref, go_ref, d_ref, nm_ref, nv_ref):
        gv = g_ref[...]
        go_ref[...] = gv
        d_ref[...], nm_ref[...], nv_ref[...] = _adamw_math(w_ref[...], gv, m_ref[...], v_ref[...])

    spec = pl.BlockSpec((None, None, tr, C), lambda l, h, r, c_ref: (l, h, r, 0))
    gspec = pl.BlockSpec((None, None, tr, C), lambda l, h, r, c_ref: (l, (h + c_ref[0]) % 2, r, 0))
    sds = jax.ShapeDtypeStruct(w.shape, F32)
    grid_spec = pltpu.PrefetchScalarGridSpec(num_scalar_prefetch=1, grid=(depth, 2, R // tr), in_specs=[spec, gspec, spec, spec],
                                             out_specs=(spec,) * 4)
    return _pcall(body, name=name, grid_spec=grid_spec, out_shape=(sds,) * 4,
                  compiler_params=pltpu.CompilerParams(dimension_semantics=("arbitrary",) * 3, vmem_limit_bytes=VMEM_LIMIT))(c_arr, w, g2, m, v)


def _adamw(name, w, g, m, v):
    R, C = w.shape
    tr = _rowtile(R, C)

    def body(w_ref, g_ref, m_ref, v_ref, d_ref, nm_ref, nv_ref):
        d_ref[...], nm_ref[...], nv_ref[...] = _adamw_math(w_ref[...], g_ref[...], m_ref[...], v_ref[...])

    spec = pl.BlockSpec((tr, C), lambda i: (i, 0))
    sds = jax.ShapeDtypeStruct((R, C), F32)
    return _pcall(body, name=name, grid=(R // tr,), in_specs=[spec] * 4, out_specs=(spec,) * 3,
                  out_shape=(sds, sds, sds), compiler_params=_row_params())(w, g, m, v)


HBM_SPEC = pl.BlockSpec(memory_space=pltpu.HBM)


def _place():
    x, y, c = lax.axis_index("x"), lax.axis_index("y"), lax.axis_index("c")
    chips = [(1 - x, y), (x, 1 - y), (1 - x, 1 - y)]
    return x, y, c, chips


def _allgather_body(ins, outs, send, recv, handshake):
    n = len(ins)
    x, y, c, chips = _place()
    me = 2 * x + y
    sib = (x, y, 1 - c)
    if handshake:
        barrier = pltpu.get_barrier_semaphore()
        for peer in [(chip[0], chip[1], c) for chip in chips] + [sib]:
            pl.semaphore_signal(barrier, inc=1, device_id=peer, device_id_type=MESH)
        pl.semaphore_wait(barrier, 4)

    def rcopy(src, dst, k, to):
        return pltpu.make_async_remote_copy(src_ref=src, dst_ref=dst, send_sem=send.at[k], recv_sem=recv.at[k],
                                            device_id=to, device_id_type=MESH)

    def rows(t, cc):
        hr = ins[t].shape[0] // 2
        return pl.ds(cc * hr, hr)

    sends = []
    for t in range(n):
        for j, chip in enumerate(chips):
            cp = rcopy(ins[t].at[rows(t, c)], outs[t].at[me, rows(t, c)], 7 * t + j, (chip[0], chip[1], c))
            cp.start()
            sends.append(cp)
    for t in range(n):
        cp = rcopy(ins[t], outs[t].at[me], 7 * t + 6, sib)
        cp.start()
        sends.append(cp)
    for t in range(n):
        for j, chip in enumerate(chips):
            slab = outs[t].at[2 * chip[0] + chip[1], rows(t, c)]
            rcopy(slab, slab, 7 * t + j, (chip[0], chip[1], c)).wait_recv()
            fw = rcopy(slab, slab, 7 * t + 3 + j, sib)
            fw.start()
            sends.append(fw)
    for t in range(n):
        rcopy(ins[t], outs[t].at[me], 7 * t + 6, sib).wait_recv()
        for j, chip in enumerate(chips):
            slab = outs[t].at[2 * chip[0] + chip[1], rows(t, 1 - c)]
            rcopy(slab, slab, 7 * t + 3 + j, sib).wait_recv()
    for cp in sends:
        cp.wait_send()


def _allgather_layer(name, shards):
    n = len(shards)

    def body(*refs):
        _allgather_body(refs[:n], refs[n:2 * n], refs[2 * n], refs[2 * n + 1], False)

    return _pcall(body, name=name, in_specs=[HBM_SPEC] * n, out_specs=tuple([HBM_SPEC] * n),
                  out_shape=tuple(jax.ShapeDtypeStruct((N_SHARD,) + s.shape, s.dtype) for s in shards),
                  scratch_shapes=[pltpu.SemaphoreType.DMA((7 * n,)), pltpu.SemaphoreType.DMA((7 * n,))])(*shards)


def _allgather_layer_async(name, shards, collective_id):
    n = len(shards)
    in_refs = [jax.new_ref(s, memory_space=pltpu.MemorySpace.HBM) for s in shards]
    out_refs = [jax.empty_ref(jax.ShapeDtypeStruct((N_SHARD,) + s.shape, s.dtype), memory_space=pltpu.MemorySpace.HBM)
                for s in shards]

    @pl.kernel(mesh=plsc.ScalarSubcoreMesh(axis_name="seq", num_cores=1), name=name,
               scratch_types=(pltpu.SemaphoreType.DMA((7 * n,)), pltpu.SemaphoreType.DMA((7 * n,))),
               compiler_params=pltpu.CompilerParams(collective_id=collective_id))
    def launch(send, recv):
        _allgather_body(in_refs, out_refs, send, recv, True)

    launch()
    return [r[...] for r in out_refs]


def _exchange_sibling_halves(gws, hd):
    n = len(gws)

    def body(*refs):
        ins, outs = refs[:n], refs[n:2 * n]
        send, recv = refs[2 * n:]
        x, y, c, _ = _place()
        cps = []
        for t in range(n):
            cp = pltpu.make_async_remote_copy(src_ref=ins[t].at[:, pl.ds((1 - c) * hd, hd)], dst_ref=outs[t],
                                              send_sem=send.at[t], recv_sem=recv.at[t],
                                              device_id=(x, y, 1 - c), device_id_type=MESH)
            cp.start()
            cps.append(cp)
        for cp in cps:
            cp.wait()

    return _pcall(body, name="rs_sibling_halves", in_specs=[HBM_SPEC] * n, out_specs=tuple([HBM_SPEC] * n),
                  out_shape=tuple(jax.ShapeDtypeStruct((g.shape[0], hd) + g.shape[2:], g.dtype) for g in gws),
                  scratch_shapes=[pltpu.SemaphoreType.DMA((n,)), pltpu.SemaphoreType.DMA((n,))])(*gws)


def _exchange_chips(ps):
    n = len(ps)

    def body(*refs):
        ins, outs = refs[:n], refs[n:2 * n]
        send, recv = refs[2 * n:]
        x, y, c, chips = _place()
        cps = []
        for t in range(n):
            for j, chip in enumerate(chips):
                cp = pltpu.make_async_remote_copy(src_ref=ins[t].at[2 * chip[0] + chip[1]], dst_ref=outs[t].at[j],
                                                  send_sem=send.at[3 * t + j], recv_sem=recv.at[3 * t + j],
                                                  device_id=(chip[0], chip[1], c), device_id_type=MESH)
                cp.start()
                cps.append(cp)
        for cp in cps:
            cp.wait()

    return _pcall(body, name="rs_chip_exchange", in_specs=[HBM_SPEC] * n, out_specs=tuple([HBM_SPEC] * n),
                  out_shape=tuple(jax.ShapeDtypeStruct((3,) + p.shape[1:], p.dtype) for p in ps),
                  scratch_shapes=[pltpu.SemaphoreType.DMA((3 * n,)), pltpu.SemaphoreType.DMA((3 * n,))])(*ps)


def _flip(x, y, c, r):
    return (1 - x if r & 4 else x, 1 - y if r & 2 else y, 1 - c if r & 1 else c)


def _grad_exchange_async(name, gls, collective_id):
    n = len(gls)
    in_refs = [jax.new_ref(g, memory_space=pltpu.MemorySpace.HBM) for g in gls]
    out_refs = [jax.empty_ref(jax.ShapeDtypeStruct((7, g.shape[1] // 2, g.shape[2]), g.dtype), memory_space=pltpu.MemorySpace.HBM)
                for g in gls]

    @pl.kernel(mesh=plsc.ScalarSubcoreMesh(axis_name="seq", num_cores=1), name=name,
               scratch_types=(pltpu.SemaphoreType.DMA((7 * n,)), pltpu.SemaphoreType.DMA((7 * n,))),
               compiler_params=pltpu.CompilerParams(collective_id=collective_id))
    def launch(send, recv):
        _grad_exchange_body(in_refs, out_refs, send, recv, True)

    launch()
    return [r[...] for r in out_refs]


def _grad_exchange_body(ins, outs, send, recv, handshake):
    x, y, c, _ = _place()
    peers = [_flip(x, y, c, r) for r in range(1, 8)]
    if handshake:
        barrier = pltpu.get_barrier_semaphore()
        for peer in peers:
            pl.semaphore_signal(barrier, inc=1, device_id=peer, device_id_type=MESH)
        pl.semaphore_wait(barrier, 7)
    cps = []
    for t in range(len(ins)):
        hr = ins[t].shape[1] // 2
        for j, (px, py, pc) in enumerate(peers):
            cp = pltpu.make_async_remote_copy(src_ref=ins[t].at[2 * px + py, pl.ds(pc * hr, hr)], dst_ref=outs[t].at[j],
                                              send_sem=send.at[7 * t + j], recv_sem=recv.at[7 * t + j],
                                              device_id=(px, py, pc), device_id_type=MESH)
            cp.start()
            cps.append(cp)
    for cp in cps:
        cp.wait()


def _add_eight(name, own, slots, dev_arr, l, buf):
    ns, Ks, Ns = own.shape
    hr = Ks // 2
    tr = _rowtile(hr, Ns)
    fresh = isinstance(buf, tuple)

    def body(pl_ref, own_ref, *refs):
        acc = own_ref[...].astype(F32)
        for s_ref in refs[:7]:
            acc = acc + s_ref[...].astype(F32)
        refs[-1][...] = acc

    in_specs = [pl.BlockSpec((None, tr, Ns), lambda r, pl_ref: (pl_ref[0], r, 0))]
    in_specs += [pl.BlockSpec((None, tr, Ns), functools.partial(lambda r, pl_ref, j: (j, r, 0), j=j)) for j in range(7)]
    ops = [dev_arr, own.reshape(ns * 2, hr, Ns)] + [slots] * 7
    if not fresh:
        in_specs.append(pl.BlockSpec(memory_space=pl.ANY))
        ops.append(buf)
    grid_spec = pltpu.PrefetchScalarGridSpec(num_scalar_prefetch=1, grid=(hr // tr,), in_specs=in_specs,
                                             out_specs=pl.BlockSpec((None, None, tr, Ns), lambda r, pl_ref: (l, 0, r, 0)))
    return _pcall(body, name=name, grid_spec=grid_spec, out_shape=jax.ShapeDtypeStruct(buf if fresh else buf.shape, F32),
                  input_output_aliases={} if fresh else {9: 0}, compiler_params=_row_params())(*ops)


def _share_reduced(gs):
    n = len(gs)

    def body(*refs):
        ins, outs = refs[:n], refs[n:2 * n]
        send, recv = refs[2 * n:]
        x, y, c, _ = _place()
        cps = []
        for t in range(n):
            cp = pltpu.make_async_remote_copy(src_ref=ins[t].at[:, 0], dst_ref=outs[t].at[:, 1], send_sem=send.at[t], recv_sem=recv.at[t],
                                              device_id=(x, y, 1 - c), device_id_type=MESH)
            cp.start()
            cps.append(cp)
        for cp in cps:
            cp.wait()

    return _pcall(body, name="rs_share_reduced", in_specs=[HBM_SPEC] * n, out_specs=tuple([HBM_SPEC] * n),
                  out_shape=tuple(jax.ShapeDtypeStruct(g.shape, g.dtype) for g in gs),
                  input_output_aliases={t: t for t in range(n)},
                  scratch_shapes=[pltpu.SemaphoreType.DMA((n,)), pltpu.SemaphoreType.DMA((n,))])(*gs)


def _allreduce_small(vec):
    R = vec.shape[0]

    def body(v_ref, o_ref, buf, send, recv):
        x, y, c, _ = _place()
        me = 4 * x + 2 * y + c
        buf[me] = v_ref[...]
        cps = []
        for r in range(1, 8):
            fx, fy, fc = (r >> 2) & 1, (r >> 1) & 1, r & 1
            to = (1 - x if fx else x, 1 - y if fy else y, 1 - c if fc else c)
            cp = pltpu.make_async_remote_copy(src_ref=v_ref, dst_ref=buf.at[me], send_sem=send.at[r - 1], recv_sem=recv.at[r - 1],
                                              device_id=to, device_id_type=MESH)
            cp.start()
            cps.append(cp)
        for r in range(1, 8):
            fx, fy, fc = (r >> 2) & 1, (r >> 1) & 1, r & 1
            frm = (1 - x if fx else x, 1 - y if fy else y, 1 - c if fc else c)
            src = 4 * frm[0] + 2 * frm[1] + frm[2]
            pltpu.make_async_remote_copy(src_ref=v_ref, dst_ref=buf.at[src], send_sem=send.at[r - 1], recv_sem=recv.at[r - 1],
                                         device_id=frm, device_id_type=MESH).wait_recv()
        for cp in cps:
            cp.wait_send()
        acc = buf[0]
        for i in range(1, 8):
            acc = acc + buf[i]
        o_ref[...] = acc

    vm = pl.BlockSpec(memory_space=pltpu.VMEM)
    return _pcall(body, name="allreduce_small", in_specs=[vm], out_specs=vm, out_shape=jax.ShapeDtypeStruct((R, LANES), F32),
                  scratch_shapes=[pltpu.VMEM((8, R, LANES), F32), pltpu.SemaphoreType.DMA((7,)), pltpu.SemaphoreType.DMA((7,))])(vec)


def _unshard_cols(wg):
    ns, depth, K, Ns = wg.shape
    return jnp.moveaxis(wg, 0, 2).reshape(depth, K, ns * Ns)


def _shard_cols(w):
    K, N = w.shape
    return jnp.moveaxis(w.reshape(K, N_SHARD, N // N_SHARD), 1, 0)


def _uq_padded(cfg, wuq_g):
    depth = wuq_g.shape[1]
    w = _unshard_cols(wuq_g).reshape(depth, cfg.QL, cfg.AH, 192)
    w = jnp.pad(w, ((0, 0), (0, cfg.QLP - cfg.QL), (0, 0), (0, 64)))
    return w.reshape(1, depth, cfg.QLP, cfg.AH * 256)


def _uq_grad_unpadded(cfg, dw):
    w = dw[:cfg.QL].reshape(cfg.QL, cfg.AH, 256)[:, :, :192].reshape(cfg.QL, cfg.UQ)
    return _shard_cols(w)


def _ukv_padded(cfg, wukv_g):
    w = _unshard_cols(wukv_g)
    w = jnp.pad(w, ((0, 0), (cfg.KOFF, cfg.KVW - cfg.KVL - cfg.KOFF), (0, 0)))
    return w[None]


def _ukv_grad_unpadded(cfg, dw):
    return _shard_cols(dw[cfg.KOFF:cfg.KOFF + cfg.KVL])


def _pad_lanes(v, lo, total):
    return jnp.pad(v, (lo, total - lo - v.shape[0]))[None]


def _layer_fwd(cfg, l, x, W, small, tabs):
    ln1, gq, gkv, gqn, gkn, gout, ln2 = small
    sc_a, sc_h = 1.0 / math.sqrt(192), 1.0 / math.sqrt(128)
    n = f"l{l}_"
    h = _rms_fwd(n + "ln1", x, ln1)
    proj = _mm_nn(n + "proj", h, (W["w_in"], "col", 0))
    cqn, ckvn, kpe = _mla_prep_fwd(n + "mla_prep", cfg, proj, gq, gkv, tabs["akr"])
    qa = _mm_nn(n + "uq", cqn, (W["uq_p"], "col", 0))
    kva = _mm_nn(n + "ukv", ckvn, (W["ukv_p"], "col", 0))
    q_a, k_a, v_a = _mla_build_fwd(n + "mla_build", cfg, qa, kva, kpe, tabs["aq"])
    o_a, lse_a = _flash_fwd(n + "attn_a", q_a, k_a, v_a, cfg.AH, 1, 256, 128, sc_a)
    q_b = _headprep_fwd(n + "bq", proj, cfg.o_bq, cfg.BH, gqn, tabs["b"])
    k_b = _headprep_fwd(n + "bk", proj, cfg.o_bk, cfg.BKV, gkn, tabs["b"])
    v_b = _headprep_fwd(n + "bv", proj, cfg.o_bv, cfg.BKV, None, None)
    o_b, lse_b = _flash_fwd(n + "attn_b", q_b, k_b, v_b, cfg.BH, cfg.G, 128, 128, sc_h)
    q_c = _headprep_fwd(n + "cq", proj, cfg.o_cq, cfg.CH, None, tabs["c"])
    k_c = _headprep_fwd(n + "ck", proj, cfg.o_ck, cfg.CH, None, tabs["c"])
    v_c = _headprep_fwd(n + "cv", proj, cfg.o_cv, cfg.CH, None, None)
    o_c, lse_c = _flash_fwd(n + "attn_c", q_c, k_c, v_c, cfg.CH, 1, 128, 128, sc_h, tabs["bias_c"], cfg.W)
    mixed = _outnorm_fwd(n + "outnorm", cfg, o_a, o_b, o_c, gout)
    x1 = _mm_nn(n + "out", mixed, (W["w_out"], "row", 0), epi=_epi_residual, extra=x)
    h2 = _rms_fwd(n + "ln2", x1, ln2)
    a, u = _mm_nn(n + "ff1", h2, (W["w_ff1"], "col", 0), epi=_epi_relu2, out_dtypes=(MM_DT, MM_DT))
    x2 = _mm_nn(n + "ff2", u, (W["w_ff2"], "row", 0), epi=_epi_residual, extra=x1)
    saved = dict(x=x, h=h, proj=proj, cqn=cqn, ckvn=ckvn, q_a=q_a, k_a=k_a, v_a=v_a, o_a=o_a, lse_a=lse_a,
                 q_b=q_b, k_b=k_b, v_b=v_b, o_b=o_b, lse_b=lse_b, q_c=q_c, k_c=k_c, v_c=v_c, o_c=o_c, lse_c=lse_c,
                 mixed=mixed, x1=x1, h2=h2, a=a, u=u)
    return x2, saved


def _layer_bwd(cfg, l, dx2, sv, W, small, tabs, GW):
    ln1, gq, gkv, gqn, gkn, gout, ln2 = small
    sc_a, sc_h = 1.0 / math.sqrt(192), 1.0 / math.sqrt(128)
    n = f"l{l}_b_"
    S = cfg.S
    mats = {m[0]: m for m in cfg.mats}

    def dw(name, a, g, key):
        _, Rs, Cs, kind = mats[key]
        GW[key] = _mm_tn(n + name, a, g, kind, 0, Rs, Cs, (N_SHARD, 1, Rs, Cs), out_dtype=MM_DT).reshape(N_SHARD, Rs, Cs)

    da = _mm_nt(n + "ff2_dx", dx2, (W["w_ff2"], "row", 0), epi=_epi_drelu2, out_dtype=MM_DT, extra=sv["a"])
    dw("ff2_dw", sv["u"], dx2, "w_ff2")
    dh2 = _mm_nt(n + "ff1_dx", da, (W["w_ff1"], "col", 0))
    dw("ff1_dw", sv["h2"], da, "w_ff1")
    dx1, dln2 = _rms_bwd(n + "ln2", sv["x1"], ln2, dh2, dx2)
    dmix = _mm_nt(n + "out_dx", dx1, (W["w_out"], "row", 0))
    dw("out_dw", sv["mixed"], dx1, "w_out")
    do_a, do_b, do_c, dgout = _outnorm_bwd(n + "outnorm", cfg, dmix, sv["o_a"], sv["o_b"], sv["o_c"], gout)
    dproj = jnp.zeros((S, cfg.IN), MM_DT)
    args_c = (sv["q_c"], sv["k_c"], sv["v_c"], do_c, sv["o_c"], sv["lse_c"], cfg.CH, 1, 128, 128, sc_h, tabs["bias_c"], cfg.W)
    dq_c = _flash_dq(n + "attn_c_dq", *args_c)
    dk_c, dv_c = _flash_dkv(n + "attn_c_dkv", *args_c)
    dproj, _ = _headprep_bwd(n + "cq", dq_c, sv["proj"], cfg.o_cq, cfg.CH, None, tabs["c"], dproj)
    dproj, _ = _headprep_bwd(n + "ck", dk_c, sv["proj"], cfg.o_ck, cfg.CH, None, tabs["c"], dproj)
    dproj, _ = _headprep_bwd(n + "cv", dv_c, sv["proj"], cfg.o_cv, cfg.CH, None, None, dproj)
    args_b = (sv["q_b"], sv["k_b"], sv["v_b"], do_b, sv["o_b"], sv["lse_b"], cfg.BH, cfg.G, 128, 128, sc_h)
    dq_b = _flash_dq(n + "attn_b_dq", *args_b)
    dk_b, dv_b = _flash_dkv(n + "attn_b_dkv", *args_b)
    dproj, dgqn = _headprep_bwd(n + "bq", dq_b, sv["proj"], cfg.o_bq, cfg.BH, gqn, tabs["b"], dproj)
    dproj, dgkn = _headprep_bwd(n + "bk", dk_b, sv["proj"], cfg.o_bk, cfg.BKV, gkn, tabs["b"], dproj)
    dproj, _ = _headprep_bwd(n + "bv", dv_b, sv["proj"], cfg.o_bv, cfg.BKV, None, None, dproj)
    args_a = (sv["q_a"], sv["k_a"], sv["v_a"], do_a, sv["o_a"], sv["lse_a"], cfg.AH, 1, 256, 128, sc_a)
    dq_a = _flash_dq(n + "attn_a_dq", *args_a)
    dk_a, dv_a = _flash_dkv(n + "attn_a_dkv", *args_a)
    dqa, dkva, dkpe = _mla_build_bwd(n + "mla_build", cfg, dq_a, dk_a, dv_a, tabs["aq"])
    dcq = _mm_nt(n + "uq_dx", dqa, (W["uq_p"], "col", 0))
    dwuq = _mm_tn(n + "uq_dw", sv["cqn"], dqa, "col", 0, cfg.QLP, cfg.AH * 256, (1, 1, cfg.QLP, cfg.AH * 256))
    dckv = _mm_nt(n + "ukv_dx", dkva, (W["ukv_p"], "col", 0))
    dwukv = _mm_tn(n + "ukv_dw", sv["ckvn"], dkva, "col", 0, cfg.KVW, cfg.AH * 256, (1, 1, cfg.KVW, cfg.AH * 256))
    dproj, dgq, dgkv = _mla_prep_bwd(n + "mla_prep", cfg, dcq, dckv, dkpe, sv["proj"], gq, gkv, tabs["akr"], dproj)
    dh = _mm_nt(n + "proj_dx", dproj, (W["w_in"], "col", 0))
    dw("proj_dw", sv["h"], dproj, "w_in")
    dx, dln1 = _rms_bwd(n + "ln1", sv["x"], ln1, dh, dx1)
    gains = dict(ln1_g=dln1[0], g_q_a=dgq[0, :cfg.QL], g_kv_a=dgkv[0, cfg.KOFF:cfg.KOFF + cfg.KVL], g_qn_b=dgqn[0],
                 g_kn_b=dgkn[0], g_out=dgout[0], ln2_g=dln2[0])
    GW["w_uq"] = _uq_grad_unpadded(cfg, dwuq[0, 0]).astype(MM_DT)
    GW["w_ukv"] = _ukv_grad_unpadded(cfg, dwukv[0, 0]).astype(MM_DT)
    return dx, gains


SMALL_NAMES = ("ln1_g", "g_q_a", "g_kv_a", "g_qn_b", "g_kn_b", "g_out", "ln2_g")
MAT_NAMES = ("w_in", "w_uq", "w_ukv", "w_out", "w_ff1", "w_ff2")


def _pack_small(cfg, per_layer, final, scalar=None):
    last = jnp.zeros((1,), F32) if scalar is None else scalar.reshape(1)
    flat = jnp.concatenate([per_layer[k].reshape(-1) for k in SMALL_NAMES] + [final.reshape(-1), last])
    total = flat.shape[0]
    rows = _rup(-(-total // LANES), 8)
    return jnp.pad(flat, (0, rows * LANES - total)).reshape(rows, LANES)


def _unpack_small(cfg, packed, shapes):
    flat = packed.reshape(-1)
    out, off = {}, 0
    for k in SMALL_NAMES + ("ln_f_g",):
        n = math.prod(shapes[k])
        out[k] = flat[off:off + n].reshape(shapes[k])
        off += n
    return out, flat[off]


def _step(cfg, w, m, v, x, tgt):
    DEPTH, hd = cfg.DEPTH, cfg.HD
    c = lax.axis_index("c")
    me_chip = 2 * lax.axis_index("x") + lax.axis_index("y")
    c_arr = jnp.reshape(c, (1,)).astype(jnp.int32)
    dev_arr = jnp.reshape(2 * me_chip + c, (1,)).astype(jnp.int32)
    mats = {mt[0]: mt for mt in cfg.mats}

    shards = []
    for name in MAT_NAMES:
        _, Ks, Ns, _ = mats[name]
        shards.append(_cast_rows("cast_" + name, w[name].reshape(DEPTH * Ks, Ns), MM_DT).reshape(DEPTH, Ks, Ns))
    W_layers, got = [], None
    for l in range(DEPTH):
        mine = [s[l] for s in shards]
        if l == 1:
            mine = list(lax.optimization_barrier((tuple(mine), tuple(got)))[0])
        got = (_allgather_layer("allgather_l0", mine) if l == 0
               else _allgather_layer_async(f"allgather_l{l}", mine, collective_id=l))
        g = {name: a[:, None] for name, a in zip(MAT_NAMES, got)}
        W_layers.append(dict(w_in=g["w_in"], w_out=g["w_out"], w_ff1=g["w_ff1"], w_ff2=g["w_ff2"],
                             uq_p=_uq_padded(cfg, g["w_uq"]), ukv_p=_ukv_padded(cfg, g["w_ukv"])))
    tabs = _all_tables(cfg)
    tabs["bias_c"] = _band_bias(cfg)

    def small_of(l):
        return (w["ln1_g"][l][None], _pad_lanes(w["g_q_a"][l], 0, cfg.QLP), _pad_lanes(w["g_kv_a"][l], cfg.KOFF, cfg.KVW),
                w["g_qn_b"][l][None], w["g_kn_b"][l][None], w["g_out"][l][None], w["ln2_g"][l][None])

    saved = []
    xc = x
    for l in range(DEPTH):
        xc, sv = _layer_fwd(cfg, l, xc, W_layers[l], small_of(l), tabs)
        saved.append(sv)
    dx, dlnf, loss_rows = _final_loss("final_loss", xc, w["ln_f_g"][None], tgt)
    loss = loss_rows[0, 0]
    gain_rows, own, slots = [None] * DEPTH, [None] * DEPTH, [None] * DEPTH
    for l in reversed(range(DEPTH)):
        GW = {}
        dx, gain_rows[l] = _layer_bwd(cfg, l, dx, saved[l], W_layers[l], small_of(l), tabs, GW)
        own[l] = [GW[name] for name in MAT_NAMES]
        slots[l] = _grad_exchange_async(f"grad_exchange_l{l}", own[l], collective_id=DEPTH + l)

    reduced = []
    for t, name in enumerate(MAT_NAMES):
        _, Ks, Ns, _ = mats[name]
        buf = (DEPTH, 2, Ks // 2, Ns)
        for l in reversed(range(DEPTH)):
            buf = _add_eight(f"rs_add_l{l}_" + name, own[l][t], slots[l][t], dev_arr, l, buf)
        reduced.append(buf)
    full = _share_reduced(reduced)
    grad, delta, new_m, new_v = {}, {}, {}, {}
    for name, g2 in zip(MAT_NAMES, full):
        _, Ks, Ns, _ = mats[name]
        halves, shp = (DEPTH, 2, Ks // 2, Ns), (DEPTH, Ks, Ns)
        res = _adamw_halves("adamw_" + name, w[name].reshape(halves), g2, m[name].reshape(halves), v[name].reshape(halves), c_arr)
        grad[name], delta[name], new_m[name], new_v[name] = (r.reshape(shp) for r in res)

    per_layer = {k: jnp.stack([gain_rows[l][k] for l in range(DEPTH)]) for k in SMALL_NAMES}
    shapes = {k: w[k].shape for k in SMALL_NAMES + ("ln_f_g",)}
    gsum = _allreduce_small(_pack_small(cfg, per_layer, dlnf[0], loss))
    pk = lambda d: _pack_small(cfg, {k: d[k] for k in SMALL_NAMES}, d["ln_f_g"])
    d_s, m_s, v_s = _adamw("adamw_small", pk(w), gsum, pk(m), pk(v))
    for res, packed in ((grad, gsum), (delta, d_s), (new_m, m_s), (new_v, v_s)):
        res.update(_unpack_small(cfg, packed, shapes)[0])
    loss_total = _unpack_small(cfg, gsum, shapes)[1]
    return loss_total, dx, grad, delta, new_m, new_v


WEIGHT_NAMES = ("ln1_g", "w_in", "g_q_a", "w_uq", "g_kv_a", "w_ukv", "g_qn_b", "g_kn_b", "g_out", "w_out", "ln2_g",
                "w_ff1", "w_ff2", "ln_f_g")


def _run(cfg, args):
    nw = len(WEIGHT_NAMES)
    x, tgt = args[0], args[1 + nw]
    w = dict(zip(WEIGHT_NAMES, args[1:1 + nw]))
    m = dict(zip(WEIGHT_NAMES, args[2 + nw:2 + 2 * nw]))
    v = dict(zip(WEIGHT_NAMES, args[2 + 2 * nw:2 + 3 * nw]))
    loss, dx, grad, delta, new_m, new_v = _step(cfg, w, m, v, x.reshape(cfg.S, cfg.D), tgt.reshape(cfg.S, cfg.D))
    return (loss, dx.reshape(x.shape), *[grad[k] for k in WEIGHT_NAMES], *[delta[k] for k in WEIGHT_NAMES],
            *[new_m[k] for k in WEIGHT_NAMES], *[new_v[k] for k in WEIGHT_NAMES])


def kernel(x, ln1_g, w_in, g_q_a, w_uq, g_kv_a, w_ukv, g_qn_b, g_kn_b, g_out, w_out, ln2_g, w_ff1, w_ff2, ln_f_g, loss_target, m_ln1_g, m_w_in, m_g_q_a, m_w_uq, m_g_kv_a, m_w_ukv, m_g_qn_b, m_g_kn_b, m_g_out, m_w_out, m_ln2_g, m_w_ff1, m_w_ff2, m_ln_f_g, v_ln1_g, v_w_in, v_g_q_a, v_w_uq, v_g_kv_a, v_w_ukv, v_g_qn_b, v_g_kn_b, v_g_out, v_w_out, v_ln2_g, v_w_ff1, v_w_ff2, v_ln_f_g):
    return _run(Cfg(), (x, ln1_g, w_in, g_q_a, w_uq, g_kv_a, w_ukv, g_qn_b, g_kn_b, g_out, w_out, ln2_g, w_ff1, w_ff2, ln_f_g, loss_target, m_ln1_g, m_w_in, m_g_q_a, m_w_uq, m_g_kv_a, m_w_ukv, m_g_qn_b, m_g_kn_b, m_g_out, m_w_out, m_ln2_g, m_w_ff1, m_w_ff2, m_ln_f_g, v_ln1_g, v_w_in, v_g_q_a, v_w_uq, v_g_kv_a, v_w_ukv, v_g_qn_b, v_g_kn_b, v_g_out, v_w_out, v_ln2_g, v_w_ff1, v_w_ff2, v_ln_f_g))
```

```python
import functools
import math

import jax
import jax.numpy as jnp
from jax import lax
from jax.experimental import pallas as pl
from jax.experimental.pallas import tpu as pltpu
from jax.experimental.pallas import tpu_sc as plsc

F32 = jnp.float32
MM_DT = jnp.bfloat16
LANES = 128
SUBLANES_F32 = 8
SUBLANES_BF16 = 16
VMEM_LIMIT = 48 * 1024 * 1024
EPS = 1e-6
NEG = -1e30
ROPE_THETA = 10000.0
MM_TK = 1024
ATT_TQ, ATT_TK_FWD, ATT_TK = 512, 4096, 2048
ADAM_LR, ADAM_B1, ADAM_B2, ADAM_EPS, ADAM_WD, ADAM_STEP = 0.001, 0.9, 0.999, 1e-08, 0.01, 10
MESH_AXES = ("x", "y", "c")
N_SHARD = 4
MESH = pl.DeviceIdType.MESH

NN = (((1,), (0,)), ((), ()))
NT = (((1,), (1,)), ((), ()))
TN = (((0,), (0,)), ((), ()))


def _pcall(body, **kw):
    return pl.pallas_call(body, **kw)


def _rup(n, m):
    return -(-n // m) * m


def _pick(n, pref, mult):
    best = None
    for t in range(mult, min(n, pref) + 1, mult):
        if n % t == 0:
            best = t
    return best if best is not None else n


class Cfg:
    def __init__(self, S=4096, D=2048, DEPTH=4, AH=4, QL=448, KVL=512, BH=6, BKV=2, CH=6,
                 BRANCHES=((128, 1), (512, 4), (2048, 16)), DFF=8192, GRID_W=64, TB=512, TBK=1024):
        self.S, self.D, self.DEPTH, self.AH, self.QL, self.KVL = S, D, DEPTH, AH, QL, KVL
        self.BH, self.BKV, self.CH, self.DFF, self.GRID_W, self.TB = BH, BKV, CH, DFF, GRID_W, TB
        self.G = BH // BKV
        self.AW, self.BW, self.CW = AH * 128, BH * 128, CH * 128
        self.MIX = self.AW + self.BW + self.CW
        self.QLP = _rup(QL, LANES)
        self.KV0 = (QL // LANES) * LANES
        self.PW = QL + KVL + 64
        assert self.PW % LANES == 0
        self.KVW = self.PW - self.KV0
        self.KOFF = QL - self.KV0
        self.o_bq = self.PW
        self.o_bk = self.o_bq + self.BW
        self.o_bv = self.o_bk + BKV * 128
        self.o_cq = self.o_bv + BKV * 128
        self.o_ck = self.o_cq + self.CW
        self.o_cv = self.o_ck + self.CW
        self.IN = self.o_cv + self.CW
        self.UQ, self.UKV = AH * 192, AH * 256
        self.branches = tuple(((w // (2 * d)) * d, d) for w, d in BRANCHES)
        for _, d in self.branches:
            assert d & (d - 1) == 0
        self.TBK = TBK
        self.W = -(-max(r for r, _ in self.branches) // TBK)
        assert S % TBK == 0 and TBK % TB == 0 and DEPTH % 2 == 0
        self.HD = DEPTH // 2
        self.mats = (("w_in", D, self.IN // 4, "col"), ("w_uq", QL, self.UQ // 4, "col"),
                     ("w_ukv", KVL, self.UKV // 4, "col"), ("w_out", self.MIX // 4, D, "row"),
                     ("w_ff1", D, DFF // 4, "col"), ("w_ff2", DFF // 4, D, "row"))


def _mm_call(name, mode, operands, in_specs, out_shape, out_specs, grid, acc_shape, epi, n_extra, aliases=None):
    nk = grid[2]

    def body(*refs):
        a_ref, b_ref = refs[0], refs[1]
        ex = refs[2:2 + n_extra]
        outs = refs[2 + n_extra:-1]
        acc = refs[-1]
        k = pl.program_id(2)

        @pl.when(k == 0)
        def _():
            acc[...] = jnp.zeros_like(acc)

        acc[...] += lax.dot_general(a_ref[...].astype(MM_DT), b_ref[...].astype(MM_DT), mode,
                                    preferred_element_type=F32)

        @pl.when(k == nk - 1)
        def _():
            epi(acc[...], ex, outs)

    return _pcall(body, name=name, grid=grid, in_specs=in_specs, out_specs=out_specs, out_shape=out_shape,
                  scratch_shapes=[pltpu.VMEM(acc_shape, F32)], input_output_aliases=aliases or {},
                  compiler_params=pltpu.CompilerParams(dimension_semantics=("parallel", "parallel", "arbitrary"),
                                                       vmem_limit_bytes=VMEM_LIMIT))(*operands)


def _wspec(kind, l, Rs, Cs, br, bc, rfn, cfn):
    assert Rs % br == 0 and Cs % bc == 0
    if kind == "col":
        npc = Cs // bc
        return pl.BlockSpec((None, None, br, bc), lambda i, j, k: (cfn(i, j, k) // npc, l, rfn(i, j, k), cfn(i, j, k) % npc))
    npr = Rs // br
    return pl.BlockSpec((None, None, br, bc), lambda i, j, k: (rfn(i, j, k) // npr, l, rfn(i, j, k) % npr, cfn(i, j, k)))


def _epi_plain(acc, ex, outs):
    outs[0][...] = acc.astype(outs[0].dtype)


def _epi_residual(acc, ex, outs):
    outs[0][...] = ex[0][...] + acc


def _epi_relu2(acc, ex, outs):
    outs[0][...] = acc.astype(outs[0].dtype)
    r = jnp.maximum(acc, 0.0)
    outs[1][...] = (r * r).astype(outs[1].dtype)


def _epi_drelu2(acc, ex, outs):
    a = ex[0][...].astype(F32)
    outs[0][...] = (acc * (2.0 * jnp.maximum(a, 0.0))).astype(outs[0].dtype)


def _wdims(wd):
    Wg, kind, l = wd
    ns, _, Rs, Cs = Wg.shape
    K = Rs * ns if kind == "row" else Rs
    N = Cs * ns if kind == "col" else Cs
    return Wg, kind, l, Rs, Cs, K, N


def _mm_nn(name, a, wd, epi=_epi_plain, out_dtypes=(F32,), extra=None):
    Wg, kind, l, Rs, Cs, K, N = _wdims(wd)
    M = a.shape[0]
    tm, tk, tn = _pick(M, 1024, 16), _pick(Rs, MM_TK, LANES), _pick(Cs, 1152, LANES)
    grid = (M // tm, N // tn, K // tk)
    in_specs = [pl.BlockSpec((tm, tk), lambda i, j, k: (i, k)),
                _wspec(kind, l, Rs, Cs, tk, tn, lambda i, j, k: k, lambda i, j, k: j)]
    ops = [a, Wg]
    if extra is not None:
        in_specs.append(pl.BlockSpec((tm, tn), lambda i, j, k: (i, j)))
        ops.append(extra)
    o_spec = pl.BlockSpec((tm, tn), lambda i, j, k: (i, j))
    outs = tuple(jax.ShapeDtypeStruct((M, N), dt) for dt in out_dtypes)
    res = _mm_call(name, NN, ops, in_specs, outs, tuple(o_spec for _ in outs), grid, (tm, tn), epi,
                   0 if extra is None else 1)
    return res[0] if len(res) == 1 else res


def _mm_nt(name, g, wd, epi=_epi_plain, out_dtype=F32, extra=None):
    Wg, kind, l, Rs, Cs, K, N = _wdims(wd)
    M = g.shape[0]
    tm, tn, tk = _pick(M, 1024, 16), _pick(Rs, 1024, LANES), _pick(Cs, 1152, LANES)
    grid = (M // tm, K // tn, N // tk)
    in_specs = [pl.BlockSpec((tm, tk), lambda i, j, k: (i, k)),
                _wspec(kind, l, Rs, Cs, tn, tk, lambda i, j, k: j, lambda i, j, k: k)]
    ops = [g, Wg]
    if extra is not None:
        in_specs.append(pl.BlockSpec((tm, tn), lambda i, j, k: (i, j)))
        ops.append(extra)
    res = _mm_call(name, NT, ops, in_specs, (jax.ShapeDtypeStruct((M, K), out_dtype),),
                   (pl.BlockSpec((tm, tn), lambda i, j, k: (i, j)),), grid, (tm, tn), epi, 0 if extra is None else 1)
    return res[0]


def _mm_tn(name, a, g, kind, l, Rs, Cs, buf, out_dtype=F32):
    M, K = a.shape
    N = g.shape[1]
    tm, tn, tk = _pick(Rs, 1024, LANES), _pick(Cs, 1152, LANES), _pick(M, MM_TK, LANES)
    grid = (K // tm, N // tn, M // tk)
    in_specs = [pl.BlockSpec((tk, tm), lambda i, j, k: (k, i)),
                pl.BlockSpec((tk, tn), lambda i, j, k: (k, j))]
    o_spec = _wspec(kind, l, Rs, Cs, tm, tn, lambda i, j, k: i, lambda i, j, k: j)
    if isinstance(buf, tuple):
        res = _mm_call(name, TN, [a, g], in_specs, (jax.ShapeDtypeStruct(buf, out_dtype),), (o_spec,), grid, (tm, tn), _epi_plain, 0)
    else:
        res = _mm_call(name, TN, [a, g, buf], in_specs + [pl.BlockSpec(memory_space=pl.ANY)],
                       (jax.ShapeDtypeStruct(buf.shape, buf.dtype),), (o_spec,), grid, (tm, tn), _epi_plain, 1, aliases={2: 0})
    return res[0]


def _row_params():
    return pltpu.CompilerParams(dimension_semantics=("arbitrary",), vmem_limit_bytes=VMEM_LIMIT)


def _rms_fwd(name, x, g):
    S, D = x.shape
    tr = _pick(S, 256, 16)

    def body(x_ref, g_ref, o_ref):
        xv = x_ref[...]
        r = lax.rsqrt(jnp.mean(xv * xv, axis=-1, keepdims=True) + EPS)
        o_ref[...] = (xv * r * g_ref[...]).astype(o_ref.dtype)

    return _pcall(body, name=name, grid=(S // tr,),
                  in_specs=[pl.BlockSpec((tr, D), lambda i: (i, 0)), pl.BlockSpec((1, D), lambda i: (0, 0))],
                  out_specs=pl.BlockSpec((tr, D), lambda i: (i, 0)), out_shape=jax.ShapeDtypeStruct((S, D), MM_DT),
                  compiler_params=_row_params())(x, g)


def _acc_rows(ref, part, first):
    @pl.when(first)
    def _():
        ref[...] = jnp.zeros_like(ref)

    ref[...] += jnp.broadcast_to(part, ref.shape)


def _rms_bwd(name, x, g, dy, res):
    S, D = x.shape
    tr = _pick(S, 256, 16)

    def body(x_ref, g_ref, dy_ref, res_ref, dx_ref, dg_ref):
        xv = x_ref[...]
        r = lax.rsqrt(jnp.mean(xv * xv, axis=-1, keepdims=True) + EPS)
        xh = xv * r
        dyv = dy_ref[...]
        dn = dyv * g_ref[...]
        dx_ref[...] = res_ref[...] + r * (dn - xh * jnp.mean(dn * xh, axis=-1, keepdims=True))
        _acc_rows(dg_ref, jnp.sum(dyv * xh, axis=0, keepdims=True), pl.program_id(0) == 0)

    row = pl.BlockSpec((tr, D), lambda i: (i, 0))
    return _pcall(body, name=name, grid=(S // tr,),
                  in_specs=[row, pl.BlockSpec((1, D), lambda i: (0, 0)), row, row],
                  out_specs=(row, pl.BlockSpec((8, D), lambda i: (0, 0))),
                  out_shape=(jax.ShapeDtypeStruct((S, D), F32), jax.ShapeDtypeStruct((8, D), F32)),
                  compiler_params=_row_params())(x, g, dy, res)


def _final_loss(name, x, g, tgt):
    S, D = x.shape
    tr = _pick(S, 256, 16)

    def body(x_ref, g_ref, t_ref, dx_ref, dg_ref, loss_ref):
        xv = x_ref[...]
        r = lax.rsqrt(jnp.mean(xv * xv, axis=-1, keepdims=True) + EPS)
        xh = xv * r
        gv = g_ref[...]
        e = xh * gv - t_ref[...]
        part = 0.5 * jnp.sum(jnp.mean(e * e, axis=-1, keepdims=True), axis=0, keepdims=True)
        dy = e * (1.0 / D)
        dn = dy * gv
        dx_ref[...] = r * (dn - xh * jnp.mean(dn * xh, axis=-1, keepdims=True))
        first = pl.program_id(0) == 0
        _acc_rows(dg_ref, jnp.sum(dy * xh, axis=0, keepdims=True), first)
        _acc_rows(loss_ref, part, first)

    row = pl.BlockSpec((tr, D), lambda i: (i, 0))
    return _pcall(body, name=name, grid=(S // tr,),
                  in_specs=[row, pl.BlockSpec((1, D), lambda i: (0, 0)), row],
                  out_specs=(row, pl.BlockSpec((8, D), lambda i: (0, 0)), pl.BlockSpec((8, LANES), lambda i: (0, 0))),
                  out_shape=(jax.ShapeDtypeStruct((S, D), F32), jax.ShapeDtypeStruct((8, D), F32),
                             jax.ShapeDtypeStruct((8, LANES), F32)),
                  compiler_params=_row_params())(x, g, tgt)


def _rope_tables(cos, sin, off, w):
    S = cos.shape[0]
    h = w // 2
    z = lambda n: jnp.zeros((S, n), F32)
    C = jnp.concatenate([z(off), cos, cos, z(LANES - off - w)], axis=1)
    SP = jnp.concatenate([z(off + h), sin, z(LANES - off - w)], axis=1)
    SN = jnp.concatenate([z(off), -sin, z(LANES - off - h)], axis=1)
    return C, SP, SN


def _angles(pos, dim):
    inv = jnp.power(ROPE_THETA, -jnp.arange(0, dim, 2, dtype=F32) / dim)
    ang = pos.astype(F32)[:, None] * inv[None, :]
    return jnp.cos(ang), jnp.sin(ang)


def _all_tables(cfg):
    S = cfg.S
    pos = jnp.arange(S, dtype=F32)
    rows = S // cfg.GRID_W
    row = jnp.repeat(jnp.arange(rows, dtype=F32), cfg.GRID_W)
    col = jnp.tile(jnp.arange(cfg.GRID_W, dtype=F32), rows)
    ca, sa = _angles(pos, 64)
    cc, sc = _angles(pos, 128)
    cr, sr = _angles(row, 64)
    cl, sl = _angles(col, 64)
    t_b = tuple(a + b for a, b in zip(_rope_tables(cr, sr, 0, 64), _rope_tables(cl, sl, 64, 64)))
    return {"aq": (_rope_tables(ca, sa, 0, 64), 64), "akr": (_rope_tables(ca, sa, 64, 64), 64),
            "b": (t_b, 64), "c": (_rope_tables(cc, sc, 0, 128), 128)}


def _rope(x, C, SP, SN, w):
    h = w // 2
    if 2 * h == LANES:
        return x * C + pltpu.roll(x, h, 1) * (SP + SN)
    return x * C + pltpu.roll(x, h, 1) * SP + pltpu.roll(x, LANES - h, 1) * SN


def _rope_t(dy, C, SP, SN, w):
    h = w // 2
    if 2 * h == LANES:
        return dy * C + pltpu.roll(dy * (SP + SN), h, 1)
    return dy * C + pltpu.roll(dy * SP, LANES - h, 1) + pltpu.roll(dy * SN, h, 1)


def _grid2_params():
    return pltpu.CompilerParams(dimension_semantics=("arbitrary", "arbitrary"), vmem_limit_bytes=VMEM_LIMIT)


def _headprep_fwd(name, proj, col_off, nb, gain, tabs):
    S = proj.shape[0]
    tr = _pick(S, 1024, 16)
    cb = col_off // LANES
    norm, rope = gain is not None, tabs is not None
    w = tabs[1] if rope else 0

    def body(*refs):
        x_ref = refs[0]
        pos = 1
        xv = x_ref[...]
        if norm:
            r = lax.rsqrt(jnp.mean(xv * xv, axis=-1, keepdims=True) + EPS)
            xv = xv * r * refs[pos][...]
            pos += 1
        if rope:
            xv = _rope(xv, refs[pos][...], refs[pos + 1][...], refs[pos + 2][...], w)
            pos += 3
        refs[pos][...] = xv.astype(refs[pos].dtype)

    ops, in_specs = [proj], [pl.BlockSpec((tr, LANES), lambda i, j: (i, cb + j))]
    if norm:
        ops.append(gain)
        in_specs.append(pl.BlockSpec((1, LANES), lambda i, j: (0, 0)))
    if rope:
        ops += list(tabs[0])
        in_specs += [pl.BlockSpec((tr, LANES), lambda i, j: (i, 0))] * 3
    return _pcall(body, name=name, grid=(S // tr, nb), in_specs=in_specs,
                  out_specs=pl.BlockSpec((tr, LANES), lambda i, j: (i, j)),
                  out_shape=jax.ShapeDtypeStruct((S, nb * LANES), MM_DT), compiler_params=_grid2_params())(*ops)


def _headprep_bwd(name, dy, proj, col_off, nb, gain, tabs, dproj):
    S = proj.shape[0]
    tr = _pick(S, 1024, 16)
    cb = col_off // LANES
    norm, rope = gain is not None, tabs is not None
    w = tabs[1] if rope else 0

    def body(*refs):
        dz = refs[0][...]
        pos = 1
        if norm:
            x_ref, g_ref = refs[pos], refs[pos + 1]
            pos += 2
        if rope:
            dz = _rope_t(dz, refs[pos][...], refs[pos + 1][...], refs[pos + 2][...], w)
            pos += 3
        pos += 1
        o_ref = refs[pos]
        if norm:
            dg_ref = refs[pos + 1]
            xv = x_ref[...]
            r = lax.rsqrt(jnp.mean(xv * xv, axis=-1, keepdims=True) + EPS)
            n = xv * r
            first = (pl.program_id(0) == 0) & (pl.program_id(1) == 0)
            _acc_rows(dg_ref, jnp.sum(dz * n, axis=0, keepdims=True), first)
            dn = dz * g_ref[...]
            dz = r * (dn - n * jnp.mean(dn * n, axis=-1, keepdims=True))
        o_ref[...] = dz.astype(o_ref.dtype)

    ops, in_specs = [dy], [pl.BlockSpec((tr, LANES), lambda i, j: (i, j))]
    if norm:
        ops += [proj, gain]
        in_specs += [pl.BlockSpec((tr, LANES), lambda i, j: (i, cb + j)), pl.BlockSpec((1, LANES), lambda i, j: (0, 0))]
    if rope:
        ops += list(tabs[0])
        in_specs += [pl.BlockSpec((tr, LANES), lambda i, j: (i, 0))] * 3
    alias_idx = len(ops)
    ops.append(dproj)
    in_specs.append(pl.BlockSpec(memory_space=pl.ANY))
    out_specs = [pl.BlockSpec((tr, LANES), lambda i, j: (i, cb + j))]
    out_shape = [jax.ShapeDtypeStruct(dproj.shape, dproj.dtype)]
    if norm:
        out_specs.append(pl.BlockSpec((8, LANES), lambda i, j: (0, 0)))
        out_shape.append(jax.ShapeDtypeStruct((8, LANES), F32))
    res = _pcall(body, name=name, grid=(S // tr, nb), in_specs=in_specs, out_specs=tuple(out_specs),
                 out_shape=tuple(out_shape), input_output_aliases={alias_idx: 0}, compiler_params=_grid2_params())(*ops)
    return (res[0], res[1]) if norm else (res[0], None)


def _masked_rms(xv, lo, n):
    lane = lax.broadcasted_iota(jnp.int32, xv.shape, 1)
    xm = jnp.where((lane >= lo) & (lane < lo + n), xv, 0.0)
    r = lax.rsqrt(jnp.sum(xm * xm, axis=-1, keepdims=True) * (1.0 / n) + EPS)
    return xm * r, r


def _mla_prep_fwd(name, cfg, proj, gq, gkv, tabs):
    S = cfg.S
    tr = _pick(S, 256, 16)
    (C, SP, SN), w = tabs

    def body(p_ref, gq_ref, gkv_ref, c_ref, sp_ref, sn_ref, cq_ref, ckv_ref, kpe_ref):
        nq, _ = _masked_rms(p_ref[:, 0:cfg.QLP], 0, cfg.QL)
        cq_ref[...] = (nq * gq_ref[...]).astype(cq_ref.dtype)
        nk, _ = _masked_rms(p_ref[:, cfg.KV0:cfg.PW], cfg.KOFF, cfg.KVL)
        ckv_ref[...] = (nk * gkv_ref[...]).astype(ckv_ref.dtype)
        kr = _rope(p_ref[:, cfg.PW - LANES:cfg.PW], c_ref[...], sp_ref[...], sn_ref[...], w)
        kpe_ref[...] = pltpu.roll(kr, 64, 1).astype(kpe_ref.dtype)

    tab = pl.BlockSpec((tr, LANES), lambda i: (i, 0))
    return _pcall(body, name=name, grid=(S // tr,),
                  in_specs=[pl.BlockSpec((tr, cfg.PW), lambda i: (i, 0)), pl.BlockSpec((1, cfg.QLP), lambda i: (0, 0)),
                            pl.BlockSpec((1, cfg.KVW), lambda i: (0, 0)), tab, tab, tab],
                  out_specs=(pl.BlockSpec((tr, cfg.QLP), lambda i: (i, 0)), pl.BlockSpec((tr, cfg.KVW), lambda i: (i, 0)), tab),
                  out_shape=(jax.ShapeDtypeStruct((S, cfg.QLP), MM_DT), jax.ShapeDtypeStruct((S, cfg.KVW), MM_DT),
                             jax.ShapeDtypeStruct((S, LANES), MM_DT)),
                  compiler_params=_row_params())(proj, gq, gkv, C, SP, SN)


def _mla_prep_bwd(name, cfg, dcq, dckv, dkpe, proj, gq, gkv, tabs, dproj):
    S = cfg.S
    tr = _pick(S, 256, 16)
    (C, SP, SN), w = tabs

    def body(dcq_ref, dckv_ref, dkpe_ref, p_ref, gq_ref, gkv_ref, c_ref, sp_ref, sn_ref, buf_ref, o_ref, dgq_ref, dgkv_ref):
        first = pl.program_id(0) == 0

        def norm_bwd(xv, lo, n, dz, g_ref, dg_ref):
            nrm, r = _masked_rms(xv, lo, n)
            _acc_rows(dg_ref, jnp.sum(dz * nrm, axis=0, keepdims=True), first)
            dn = dz * g_ref[...]
            return r * (dn - nrm * (jnp.sum(dn * nrm, axis=-1, keepdims=True) * (1.0 / n)))

        dxq = norm_bwd(p_ref[:, 0:cfg.QLP], 0, cfg.QL, dcq_ref[...], gq_ref, dgq_ref)
        dxk = norm_bwd(p_ref[:, cfg.KV0:cfg.PW], cfg.KOFF, cfg.KVL, dckv_ref[...], gkv_ref, dgkv_ref)
        dxr = _rope_t(pltpu.roll(dkpe_ref[...], 64, 1), c_ref[...], sp_ref[...], sn_ref[...], w)
        for cidx in range(cfg.PW // LANES):
            lo = cidx * LANES
            parts = []
            if lo < cfg.QLP:
                parts.append(dxq[:, lo:lo + LANES])
            if lo >= cfg.KV0:
                parts.append(dxk[:, lo - cfg.KV0:lo - cfg.KV0 + LANES])
            if lo == cfg.PW - LANES:
                parts.append(dxr)
            o_ref[:, lo:lo + LANES] = functools.reduce(lambda a, b: a + b, parts).astype(o_ref.dtype)

    tab = pl.BlockSpec((tr, LANES), lambda i: (i, 0))
    res = _pcall(body, name=name, grid=(S // tr,),
                 in_specs=[pl.BlockSpec((tr, cfg.QLP), lambda i: (i, 0)), pl.BlockSpec((tr, cfg.KVW), lambda i: (i, 0)), tab,
                           pl.BlockSpec((tr, cfg.PW), lambda i: (i, 0)), pl.BlockSpec((1, cfg.QLP), lambda i: (0, 0)),
                           pl.BlockSpec((1, cfg.KVW), lambda i: (0, 0)), tab, tab, tab, pl.BlockSpec(memory_space=pl.ANY)],
                 out_specs=(pl.BlockSpec((tr, cfg.PW), lambda i: (i, 0)), pl.BlockSpec((8, cfg.QLP), lambda i: (0, 0)),
                            pl.BlockSpec((8, cfg.KVW), lambda i: (0, 0))),
                 out_shape=(jax.ShapeDtypeStruct(dproj.shape, dproj.dtype), jax.ShapeDtypeStruct((8, cfg.QLP), F32),
                            jax.ShapeDtypeStruct((8, cfg.KVW), F32)),
                 input_output_aliases={9: 0}, compiler_params=_row_params())(dcq, dckv, dkpe, proj, gq, gkv, C, SP, SN, dproj)
    return res


def _mla_build_fwd(name, cfg, qa, kva, kpe, tabs):
    S, AH = cfg.S, cfg.AH
    tr = _pick(S, 256, 16)
    (C, SP, SN), w = tabs

    def body(qa_ref, kva_ref, kpe_ref, c_ref, sp_ref, sn_ref, q_ref, k_ref, v_ref):
        for h in range(AH):
            a, b = 256 * h, 256 * h + LANES
            q_ref[:, a:b] = qa_ref[:, a:b].astype(q_ref.dtype)
            q_ref[:, b:b + LANES] = _rope(qa_ref[:, b:b + LANES], c_ref[...], sp_ref[...], sn_ref[...], w).astype(q_ref.dtype)
            k_ref[:, a:b] = kva_ref[:, a:b].astype(k_ref.dtype)
            k_ref[:, b:b + LANES] = kpe_ref[...]
            v_ref[:, LANES * h:LANES * (h + 1)] = kva_ref[:, b:b + LANES].astype(v_ref.dtype)

    tab = pl.BlockSpec((tr, LANES), lambda i: (i, 0))
    wide = pl.BlockSpec((tr, AH * 256), lambda i: (i, 0))
    return _pcall(body, name=name, grid=(S // tr,), in_specs=[wide, wide, tab, tab, tab, tab],
                  out_specs=(wide, wide, pl.BlockSpec((tr, AH * LANES), lambda i: (i, 0))),
                  out_shape=(jax.ShapeDtypeStruct((S, AH * 256), MM_DT), jax.ShapeDtypeStruct((S, AH * 256), MM_DT),
                             jax.ShapeDtypeStruct((S, AH * LANES), MM_DT)),
                  compiler_params=_row_params())(qa, kva, kpe, C, SP, SN)


def _mla_build_bwd(name, cfg, dq, dk, dv, tabs):
    S, AH = cfg.S, cfg.AH
    tr = _pick(S, 256, 16)
    (C, SP, SN), w = tabs

    def body(dq_ref, dk_ref, dv_ref, c_ref, sp_ref, sn_ref, dqa_ref, dkva_ref, dkpe_ref):
        dkpe = None
        for h in range(AH):
            a, b = 256 * h, 256 * h + LANES
            dqa_ref[:, a:b] = dq_ref[:, a:b].astype(dqa_ref.dtype)
            dqa_ref[:, b:b + LANES] = _rope_t(dq_ref[:, b:b + LANES], c_ref[...], sp_ref[...], sn_ref[...], w).astype(dqa_ref.dtype)
            dkva_ref[:, a:b] = dk_ref[:, a:b].astype(dkva_ref.dtype)
            dkva_ref[:, b:b + LANES] = dv_ref[:, LANES * h:LANES * (h + 1)].astype(dkva_ref.dtype)
            part = dk_ref[:, b:b + LANES]
            dkpe = part if dkpe is None else dkpe + part
        dkpe_ref[...] = dkpe

    tab = pl.BlockSpec((tr, LANES), lambda i: (i, 0))
    wide = pl.BlockSpec((tr, AH * 256), lambda i: (i, 0))
    return _pcall(body, name=name, grid=(S // tr,),
                  in_specs=[wide, wide, pl.BlockSpec((tr, AH * LANES), lambda i: (i, 0)), tab, tab, tab],
                  out_specs=(wide, wide, tab),
                  out_shape=(jax.ShapeDtypeStruct((S, AH * 256), MM_DT), jax.ShapeDtypeStruct((S, AH * 256), MM_DT),
                             jax.ShapeDtypeStruct((S, LANES), F32)),
                  compiler_params=_row_params())(dq, dk, dv, C, SP, SN)


def _outnorm_fwd(name, cfg, oa, ob, oc, g):
    S = cfg.S
    tr = _pick(S, 256, 16)
    widths = (cfg.AW, cfg.BW, cfg.CW)

    def body(a_ref, b_ref, c_ref, g_ref, o_ref):
        off = 0
        for ref, wd in zip((a_ref, b_ref, c_ref), widths):
            v = ref[...]
            r = lax.rsqrt(jnp.mean(v * v, axis=-1, keepdims=True) + EPS)
            o_ref[:, off:off + wd] = (v * r * g_ref[:, off:off + wd]).astype(o_ref.dtype)
            off += wd

    return _pcall(body, name=name, grid=(S // tr,),
                  in_specs=[pl.BlockSpec((tr, wd), lambda i: (i, 0)) for wd in widths] + [pl.BlockSpec((1, cfg.MIX), lambda i: (0, 0))],
                  out_specs=pl.BlockSpec((tr, cfg.MIX), lambda i: (i, 0)),
                  out_shape=jax.ShapeDtypeStruct((S, cfg.MIX), MM_DT), compiler_params=_row_params())(oa, ob, oc, g)


def _outnorm_bwd(name, cfg, dmix, oa, ob, oc, g):
    S = cfg.S
    tr = _pick(S, 256, 16)
    widths = (cfg.AW, cfg.BW, cfg.CW)

    def body(dm_ref, a_ref, b_ref, c_ref, g_ref, da_ref, db_ref, dc_ref, dg_ref):
        off = 0
        parts = []
        for ref, dref, wd in zip((a_ref, b_ref, c_ref), (da_ref, db_ref, dc_ref), widths):
            v = ref[...]
            r = lax.rsqrt(jnp.mean(v * v, axis=-1, keepdims=True) + EPS)
            n = v * r
            dm = dm_ref[:, off:off + wd]
            parts.append(jnp.sum(dm * n, axis=0, keepdims=True))
            dn = dm * g_ref[:, off:off + wd]
            dref[...] = r * (dn - n * jnp.mean(dn * n, axis=-1, keepdims=True))
            off += wd
        _acc_rows(dg_ref, jnp.concatenate(parts, axis=1), pl.program_id(0) == 0)

    segs = [pl.BlockSpec((tr, wd), lambda i: (i, 0)) for wd in widths]
    return _pcall(body, name=name, grid=(S // tr,),
                  in_specs=[pl.BlockSpec((tr, cfg.MIX), lambda i: (i, 0))] + segs + [pl.BlockSpec((1, cfg.MIX), lambda i: (0, 0))],
                  out_specs=tuple(segs) + (pl.BlockSpec((8, cfg.MIX), lambda i: (0, 0)),),
                  out_shape=tuple(jax.ShapeDtypeStruct((S, wd), F32) for wd in widths) + (jax.ShapeDtypeStruct((8, cfg.MIX), F32),),
                  compiler_params=_row_params())(dmix, oa, ob, oc, g)


def _band_bias(cfg):
    tq, tk, W = cfg.TB, cfg.TBK, cfg.W
    ns = 2 * W + 1
    shape = ((tk // tq) * ns, tq, tk)
    slab = lax.broadcasted_iota(jnp.int32, shape, 0)
    row = lax.broadcasted_iota(jnp.int32, shape, 1)
    col = lax.broadcasted_iota(jnp.int32, shape, 2)
    d = (slab // ns) * tq + (W - slab % ns) * tk + row - col
    ad = jnp.abs(d)
    m = jnp.zeros(d.shape, F32)
    for reach, dil in cfg.branches:
        ok = ad <= reach
        if dil > 1:
            ok = ok & ((d & (dil - 1)) == 0)
        m = m + ok.astype(F32)
    return jnp.where(m > 0, jnp.log(jnp.maximum(m, 1.0)), NEG)


def _attn_params():
    return pltpu.CompilerParams(dimension_semantics=("parallel", "parallel", "arbitrary"), vmem_limit_bytes=VMEM_LIMIT)


def _scores(q_ref, k_ref, scale, bias_ref):
    s = lax.dot_general(q_ref[...], k_ref[...], NT, preferred_element_type=F32) * scale
    return s if bias_ref is None else s + bias_ref[...]


def _flash_fwd(name, q, k, v, H, G, dk, dv, scale, bias=None, W=None):
    S = q.shape[0]
    band = bias is not None
    tq = bias.shape[1] if band else _pick(S, ATT_TQ, LANES)
    tk = bias.shape[2] if band else _pick(S, ATT_TK_FWD, LANES)
    n = S // tk
    nsteps = 2 * W + 1 if band else n
    R = tk // tq

    def kblock(qi, st):
        return jnp.clip(qi // R - W + st, 0, n - 1) if band else st

    def body(*refs):
        q_ref, k_ref, v_ref = refs[:3]
        bias_ref = refs[3] if band else None
        o_ref, lse_ref, m_sc, l_sc, acc_sc = refs[-5:]
        qi, st = pl.program_id(1), pl.program_id(2)

        @pl.when(st == 0)
        def _():
            m_sc[...] = jnp.full_like(m_sc, NEG)
            l_sc[...] = jnp.zeros_like(l_sc)
            acc_sc[...] = jnp.zeros_like(acc_sc)

        kj = qi // R - W + st if band else st

        def step():
            s = _scores(q_ref, k_ref, scale, bias_ref)
            m_prev = m_sc[...]
            m_new = jnp.maximum(m_prev, jnp.max(s, axis=-1, keepdims=True))
            alpha = jnp.exp(m_prev - m_new)
            p = jnp.exp(s - m_new)
            l_sc[...] = alpha * l_sc[...] + jnp.sum(p, axis=-1, keepdims=True)
            acc_sc[...] = alpha * acc_sc[...] + lax.dot_general(p.astype(MM_DT), v_ref[...], NN, preferred_element_type=F32)
            m_sc[...] = m_new

        if band:
            pl.when((kj >= 0) & (kj < n))(step)
        else:
            step()

        @pl.when(st == nsteps - 1)
        def _():
            l = l_sc[...]
            o_ref[...] = acc_sc[...] / l
            lse_ref[...] = jnp.broadcast_to(m_sc[...] + jnp.log(l), lse_ref.shape)

    in_specs = [pl.BlockSpec((tq, dk), lambda h, qi, st: (qi, h)),
                pl.BlockSpec((tk, dk), lambda h, qi, st: (kblock(qi, st), h // G)),
                pl.BlockSpec((tk, dv), lambda h, qi, st: (kblock(qi, st), h // G))]
    ops = [q, k, v]
    if band:
        in_specs.append(pl.BlockSpec((None, tq, tk), lambda h, qi, st: ((qi % R) * nsteps + st, 0, 0)))
        ops.append(bias)
    return _pcall(body, name=name, grid=(H, S // tq, nsteps), in_specs=in_specs,
                  out_specs=(pl.BlockSpec((tq, dv), lambda h, qi, st: (qi, h)),
                             pl.BlockSpec((None, tq, LANES), lambda h, qi, st: (h, qi, 0))),
                  out_shape=(jax.ShapeDtypeStruct((S, H * dv), F32), jax.ShapeDtypeStruct((H, S, LANES), F32)),
                  scratch_shapes=[pltpu.VMEM((tq, 1), F32), pltpu.VMEM((tq, 1), F32), pltpu.VMEM((tq, dv), F32)],
                  compiler_params=_attn_params())(*ops)


def _flash_dq(name, q, k, v, do, o, lse, H, G, dk, dv, scale, bias=None, W=None):
    S = q.shape[0]
    band = bias is not None
    tq = bias.shape[1] if band else _pick(S, ATT_TQ, LANES)
    tk = bias.shape[2] if band else _pick(S, ATT_TK, LANES)
    n = S // tk
    nsteps = 2 * W + 1 if band else n
    R = tk // tq

    def kblock(qi, st):
        return jnp.clip(qi // R - W + st, 0, n - 1) if band else st

    def body(*refs):
        q_ref, k_ref, v_ref, do_ref, o_ref, lse_ref = refs[:6]
        bias_ref = refs[6] if band else None
        dq_ref, delta_sc, acc_sc = refs[-3:]
        qi, st = pl.program_id(1), pl.program_id(2)

        @pl.when(st == 0)
        def _():
            delta_sc[...] = jnp.sum(do_ref[...] * o_ref[...], axis=-1, keepdims=True)
            acc_sc[...] = jnp.zeros_like(acc_sc)

        kj = qi // R - W + st if band else st

        def step():
            p = jnp.exp(_scores(q_ref, k_ref, scale, bias_ref) - lse_ref[:, 0:1])
            dp = lax.dot_general(do_ref[...].astype(MM_DT), v_ref[...], NT, preferred_element_type=F32)
            ds = p * (dp - delta_sc[...]) * scale
            acc_sc[...] += lax.dot_general(ds.astype(MM_DT), k_ref[...], NN, preferred_element_type=F32)

        if band:
            pl.when((kj >= 0) & (kj < n))(step)
        else:
            step()

        @pl.when(st == nsteps - 1)
        def _():
            dq_ref[...] = acc_sc[...]

    qspec = lambda wd: pl.BlockSpec((tq, wd), lambda h, qi, st: (qi, h))
    in_specs = [qspec(dk),
                pl.BlockSpec((tk, dk), lambda h, qi, st: (kblock(qi, st), h // G)),
                pl.BlockSpec((tk, dv), lambda h, qi, st: (kblock(qi, st), h // G)),
                qspec(dv), qspec(dv),
                pl.BlockSpec((None, tq, LANES), lambda h, qi, st: (h, qi, 0))]
    ops = [q, k, v, do, o, lse]
    if band:
        in_specs.append(pl.BlockSpec((None, tq, tk), lambda h, qi, st: ((qi % R) * nsteps + st, 0, 0)))
        ops.append(bias)
    return _pcall(body, name=name, grid=(H, S // tq, nsteps), in_specs=in_specs,
                  out_specs=qspec(dk), out_shape=jax.ShapeDtypeStruct((S, H * dk), F32),
                  scratch_shapes=[pltpu.VMEM((tq, 1), F32), pltpu.VMEM((tq, dk), F32)],
                  compiler_params=_attn_params())(*ops)


def _flash_dkv(name, q, k, v, do, o, lse, H, G, dk, dv, scale, bias=None, W=None):
    S = q.shape[0]
    band = bias is not None
    tq = bias.shape[1] if band else _pick(S, ATT_TQ, LANES)
    tk = bias.shape[2] if band else _pick(S, ATT_TK, LANES)
    n = S // tq
    R = tk // tq
    nq = R * (2 * W + 1) if band else n
    nsteps = G * nq
    Hkv = H // G

    def qhead(hk, st):
        return hk * G + st // nq

    def qblock(kj, st):
        return jnp.clip(R * (kj - W) + st % nq, 0, n - 1) if band else st % nq

    def body(*refs):
        q_ref, k_ref, v_ref, do_ref, o_ref, lse_ref = refs[:6]
        bias_ref = refs[6] if band else None
        dk_ref, dv_ref, dk_sc, dv_sc = refs[-4:]
        kj, st = pl.program_id(1), pl.program_id(2)

        @pl.when(st == 0)
        def _():
            dk_sc[...] = jnp.zeros_like(dk_sc)
            dv_sc[...] = jnp.zeros_like(dv_sc)

        qi = R * (kj - W) + st % nq if band else st % nq

        def step():
            p = jnp.exp(_scores(q_ref, k_ref, scale, bias_ref) - lse_ref[:, 0:1])
            dof = do_ref[...]
            dob = dof.astype(MM_DT)
            dv_sc[...] += lax.dot_general(p.astype(MM_DT), dob, TN, preferred_element_type=F32)
            dp = lax.dot_general(dob, v_ref[...], NT, preferred_element_type=F32)
            delta = jnp.sum(dof * o_ref[...], axis=-1, keepdims=True)
            ds = p * (dp - delta) * scale
            dk_sc[...] += lax.dot_general(ds.astype(MM_DT), q_ref[...], TN, preferred_element_type=F32)

        if band:
            pl.when((qi >= 0) & (qi < n))(step)
        else:
            step()

        @pl.when(st == nsteps - 1)
        def _():
            dk_ref[...] = dk_sc[...]
            dv_ref[...] = dv_sc[...]

    qspec = lambda wd: pl.BlockSpec((tq, wd), lambda hk, kj, st: (qblock(kj, st), qhead(hk, st)))
    kspec = lambda wd: pl.BlockSpec((tk, wd), lambda hk, kj, st: (kj, hk))
    in_specs = [qspec(dk), kspec(dk), kspec(dv), qspec(dv), qspec(dv),
                pl.BlockSpec((None, tq, LANES), lambda hk, kj, st: (qhead(hk, st), qblock(kj, st), 0))]
    ops = [q, k, v, do, o, lse]
    if band:
        in_specs.append(pl.BlockSpec((None, tq, tk),
                                     lambda hk, kj, st: ((st % nq % R) * (2 * W + 1) + 2 * W - (st % nq) // R, 0, 0)))
        ops.append(bias)
    return _pcall(body, name=name, grid=(Hkv, S // tk, nsteps), in_specs=in_specs,
                  out_specs=(kspec(dk), kspec(dv)),
                  out_shape=(jax.ShapeDtypeStruct((S, Hkv * dk), F32), jax.ShapeDtypeStruct((S, Hkv * dv), F32)),
                  scratch_shapes=[pltpu.VMEM((tk, dk), F32), pltpu.VMEM((tk, dv), F32)],
                  compiler_params=_attn_params())(*ops)


def _rowtile(rows, cols):
    return _pick(rows, max(16, (512 * 1024) // cols // 16 * 16), 16)


def _cast_rows(name, w, dtype):
    R, C = w.shape
    tr = _rowtile(R, C)

    def body(w_ref, o_ref):
        o_ref[...] = w_ref[...].astype(o_ref.dtype)

    spec = pl.BlockSpec((tr, C), lambda i: (i, 0))
    return _pcall(body, name=name, grid=(R // tr,), in_specs=[spec], out_specs=spec,
                  out_shape=jax.ShapeDtypeStruct((R, C), dtype), compiler_params=_row_params())(w)


def _add_sibling(name, gw, ra, c_arr, hd):
    ns, depth, Ks, Ns = gw.shape
    rows = hd * Ks
    tr = _rowtile(rows, Ns)
    gw_v = gw.reshape(ns, 2, rows, Ns)
    ra_v = ra.reshape(ns, rows, Ns)

    def body(c_ref, g_ref, r_ref, o_ref):
        o_ref[...] = (g_ref[...].astype(F32) + r_ref[...].astype(F32)).astype(o_ref.dtype)

    grid_spec = pltpu.PrefetchScalarGridSpec(
        num_scalar_prefetch=1, grid=(ns, rows // tr),
        in_specs=[pl.BlockSpec((None, None, tr, Ns), lambda s, r, c_ref: (s, c_ref[0], r, 0)),
                  pl.BlockSpec((None, tr, Ns), lambda s, r, c_ref: (s, r, 0))],
        out_specs=pl.BlockSpec((None, tr, Ns), lambda s, r, c_ref: (s, r, 0)))
    return _pcall(body, name=name, grid_spec=grid_spec, out_shape=jax.ShapeDtypeStruct((ns, rows, Ns), MM_DT),
                  compiler_params=_grid2_params())(c_arr, gw_v, ra_v)


def _add_chips(name, p, rb, me_arr):
    ns, rows, Ns = p.shape
    tr = _rowtile(rows, Ns)

    def body(me_ref, p_ref, b0_ref, b1_ref, b2_ref, o_ref):
        o_ref[...] = ((p_ref[...].astype(F32) + b0_ref[...].astype(F32)) + b1_ref[...].astype(F32)) + b2_ref[...].astype(F32)

    grid_spec = pltpu.PrefetchScalarGridSpec(
        num_scalar_prefetch=1, grid=(rows // tr,),
        in_specs=[pl.BlockSpec((None, tr, Ns), lambda r, me_ref: (me_ref[0], r, 0))] +
                 [pl.BlockSpec((None, tr, Ns), functools.partial(lambda r, me_ref, j: (j, r, 0), j=j)) for j in range(3)],
        out_specs=pl.BlockSpec((None, tr, Ns), lambda r, me_ref: (0, r, 0)))
    return _pcall(body, name=name, grid_spec=grid_spec, out_shape=jax.ShapeDtypeStruct((2, rows, Ns), F32),
                  compiler_params=_row_params())(me_arr, p, rb, rb, rb)


def _adamw_math(wv, gv, mv, vv):
    bc1 = 1.0 - ADAM_B1 ** ADAM_STEP
    bc2 = 1.0 - ADAM_B2 ** ADAM_STEP
    mn = ADAM_B1 * mv + (1.0 - ADAM_B1) * gv
    vn = ADAM_B2 * vv + (1.0 - ADAM_B2) * jnp.square(gv)
    m_hat = mn / bc1
    v_hat = vn / bc2
    return -ADAM_LR * (m_hat / (jnp.sqrt(v_hat) + ADAM_EPS) + ADAM_WD * wv), mn, vn


def _adamw_halves(name, w, g2, m, v, c_arr):
    depth, _, R, C = w.shape
    tr = _rowtile(R, C)

    def body(c_ref, w_ref, g_ref, m_ref, v_ref, go_ref, d_ref, nm_ref, nv_ref):
        gv = g_ref[...]
        go_ref[...] = gv
        d_ref[...], nm_ref[...], nv_ref[...] = _adamw_math(w_ref[...], gv, m_ref[...], v_ref[...])

    spec = pl.BlockSpec((None, None, tr, C), lambda l, h, r, c_ref: (l, h, r, 0))
    gspec = pl.BlockSpec((None, None, tr, C), lambda l, h, r, c_ref: (l, (h + c_ref[0]) % 2, r, 0))
    sds = jax.ShapeDtypeStruct(w.shape, F32)
    grid_spec = pltpu.PrefetchScalarGridSpec(num_scalar_prefetch=1, grid=(depth, 2, R // tr), in_specs=[spec, gspec, spec, spec],
                                             out_specs=(spec,) * 4)
    return _pcall(body, name=name, grid_spec=grid_spec, out_shape=(sds,) * 4,
                  compiler_params=pltpu.CompilerParams(dimension_semantics=("arbitrary",) * 3, vmem_limit_bytes=VMEM_LIMIT))(c_arr, w, g2, m, v)


def _adamw(name, w, g, m, v):
    R, C = w.shape
    tr = _rowtile(R, C)

    def body(w_ref, g_ref, m_ref, v_ref, d_ref, nm_ref, nv_ref):
        d_ref[...], nm_ref[...], nv_ref[...] = _adamw_math(w_ref[...], g_ref[...], m_ref[...], v_ref[...])

    spec = pl.BlockSpec((tr, C), lambda i: (i, 0))
    sds = jax.ShapeDtypeStruct((R, C), F32)
    return _pcall(body, name=name, grid=(R // tr,), in_specs=[spec] * 4, out_specs=(spec,) * 3,
                  out_shape=(sds, sds, sds), compiler_params=_row_params())(w, g, m, v)


HBM_SPEC = pl.BlockSpec(memory_space=pltpu.HBM)


def _place():
    x, y, c = lax.axis_index("x"), lax.axis_index("y"), lax.axis_index("c")
    chips = [(1 - x, y), (x, 1 - y), (1 - x, 1 - y)]
    return x, y, c, chips


def _allgather_body(ins, outs, send, recv, handshake):
    n = len(ins)
    x, y, c, chips = _place()
    me = 2 * x + y
    sib = (x, y, 1 - c)
    if handshake:
        barrier = pltpu.get_barrier_semaphore()
        for peer in [(chip[0], chip[1], c) for chip in chips] + [sib]:
            pl.semaphore_signal(barrier, inc=1, device_id=peer, device_id_type=MESH)
        pl.semaphore_wait(barrier, 4)

    def rcopy(src, dst, k, to):
        return pltpu.make_async_remote_copy(src_ref=src, dst_ref=dst, send_sem=send.at[k], recv_sem=recv.at[k],
                                            device_id=to, device_id_type=MESH)

    def rows(t, cc):
        hr = ins[t].shape[0] // 2
        return pl.ds(cc * hr, hr)

    sends = []
    for t in range(n):
        for j, chip in enumerate(chips):
            cp = rcopy(ins[t].at[rows(t, c)], outs[t].at[me, rows(t, c)], 7 * t + j, (chip[0], chip[1], c))
            cp.start()
            sends.append(cp)
    for t in range(n):
        cp = rcopy(ins[t], outs[t].at[me], 7 * t + 6, sib)
        cp.start()
        sends.append(cp)
    for t in range(n):
        for j, chip in enumerate(chips):
            slab = outs[t].at[2 * chip[0] + chip[1], rows(t, c)]
            rcopy(slab, slab, 7 * t + j, (chip[0], chip[1], c)).wait_recv()
            fw = rcopy(slab, slab, 7 * t + 3 + j, sib)
            fw.start()
            sends.append(fw)
    for t in range(n):
        rcopy(ins[t], outs[t].at[me], 7 * t + 6, sib).wait_recv()
        for j, chip in enumerate(chips):
            slab = outs[t].at[2 * chip[0] + chip[1], rows(t, 1 - c)]
            rcopy(slab, slab, 7 * t + 3 + j, sib).wait_recv()
    for cp in sends:
        cp.wait_send()


def _allgather_layer(name, shards):
    n = len(shards)

    def body(*refs):
        _allgather_body(refs[:n], refs[n:2 * n], refs[2 * n], refs[2 * n + 1], False)

    return _pcall(body, name=name, in_specs=[HBM_SPEC] * n, out_specs=tuple([HBM_SPEC] * n),
                  out_shape=tuple(jax.ShapeDtypeStruct((N_SHARD,) + s.shape, s.dtype) for s in shards),
                  scratch_shapes=[pltpu.SemaphoreType.DMA((7 * n,)), pltpu.SemaphoreType.DMA((7 * n,))])(*shards)


def _allgather_layer_async(name, shards, collective_id):
    n = len(shards)
    in_refs = [jax.new_ref(s, memory_space=pltpu.MemorySpace.HBM) for s in shards]
    out_refs = [jax.empty_ref(jax.ShapeDtypeStruct((N_SHARD,) + s.shape, s.dtype), memory_space=pltpu.MemorySpace.HBM)
                for s in shards]

    @pl.kernel(mesh=plsc.ScalarSubcoreMesh(axis_name="seq", num_cores=1), name=name,
               scratch_types=(pltpu.SemaphoreType.DMA((7 * n,)), pltpu.SemaphoreType.DMA((7 * n,))),
               compiler_params=pltpu.CompilerParams(collective_id=collective_id))
    def launch(send, recv):
        _allgather_body(in_refs, out_refs, send, recv, True)

    launch()
    return [r[...] for r in out_refs]


def _exchange_sibling_halves(gws, hd):
    n = len(gws)

    def body(*refs):
        ins, outs = refs[:n], refs[n:2 * n]
        send, recv = refs[2 * n:]
        x, y, c, _ = _place()
        cps = []
        for t in range(n):
            cp = pltpu.make_async_remote_copy(src_ref=ins[t].at[:, pl.ds((1 - c) * hd, hd)], dst_ref=outs[t],
                                              send_sem=send.at[t], recv_sem=recv.at[t],
                                              device_id=(x, y, 1 - c), device_id_type=MESH)
            cp.start()
            cps.append(cp)
        for cp in cps:
            cp.wait()

    return _pcall(body, name="rs_sibling_halves", in_specs=[HBM_SPEC] * n, out_specs=tuple([HBM_SPEC] * n),
                  out_shape=tuple(jax.ShapeDtypeStruct((g.shape[0], hd) + g.shape[2:], g.dtype) for g in gws),
                  scratch_shapes=[pltpu.SemaphoreType.DMA((n,)), pltpu.SemaphoreType.DMA((n,))])(*gws)


def _exchange_chips(ps):
    n = len(ps)

    def body(*refs):
        ins, outs = refs[:n], refs[n:2 * n]
        send, recv = refs[2 * n:]
        x, y, c, chips = _place()
        cps = []
        for t in range(n):
            for j, chip in enumerate(chips):
                cp = pltpu.make_async_remote_copy(src_ref=ins[t].at[2 * chip[0] + chip[1]], dst_ref=outs[t].at[j],
                                                  send_sem=send.at[3 * t + j], recv_sem=recv.at[3 * t + j],
                                                  device_id=(chip[0], chip[1], c), device_id_type=MESH)
                cp.start()
                cps.append(cp)
        for cp in cps:
            cp.wait()

    return _pcall(body, name="rs_chip_exchange", in_specs=[HBM_SPEC] * n, out_specs=tuple([HBM_SPEC] * n),
                  out_shape=tuple(jax.ShapeDtypeStruct((3,) + p.shape[1:], p.dtype) for p in ps),
                  scratch_shapes=[pltpu.SemaphoreType.DMA((3 * n,)), pltpu.SemaphoreType.DMA((3 * n,))])(*ps)


def _flip(x, y, c, r):
    return (1 - x if r & 4 else x, 1 - y if r & 2 else y, 1 - c if r & 1 else c)


def _grad_exchange_async(name, gls, collective_id):
    n = len(gls)
    in_refs = [jax.new_ref(g, memory_space=pltpu.MemorySpace.HBM) for g in gls]
    out_refs = [jax.empty_ref(jax.ShapeDtypeStruct((7, g.shape[1] // 2, g.shape[2]), g.dtype), memory_space=pltpu.MemorySpace.HBM)
                for g in gls]

    @pl.kernel(mesh=plsc.ScalarSubcoreMesh(axis_name="seq", num_cores=1), name=name,
               scratch_types=(pltpu.SemaphoreType.DMA((7 * n,)), pltpu.SemaphoreType.DMA((7 * n,))),
               compiler_params=pltpu.CompilerParams(collective_id=collective_id))
    def launch(send, recv):
        _grad_exchange_body(in_refs, out_refs, send, recv, True)

    launch()
    return [r[...] for r in out_refs]


def _grad_exchange_body(ins, outs, send, recv, handshake):
    x, y, c, _ = _place()
    peers = [_flip(x, y, c, r) for r in range(1, 8)]
    if handshake:
        barrier = pltpu.get_barrier_semaphore()
        for peer in peers:
            pl.semaphore_signal(barrier, inc=1, device_id=peer, device_id_type=MESH)
        pl.semaphore_wait(barrier, 7)
    cps = []
    for t in range(len(ins)):
        hr = ins[t].shape[1] // 2
        for j, (px, py, pc) in enumerate(peers):
            cp = pltpu.make_async_remote_copy(src_ref=ins[t].at[2 * px + py, pl.ds(pc * hr, hr)], dst_ref=outs[t].at[j],
                                              send_sem=send.at[7 * t + j], recv_sem=recv.at[7 * t + j],
                                              device_id=(px, py, pc), device_id_type=MESH)
            cp.start()
            cps.append(cp)
    for cp in cps:
        cp.wait()


def _add_eight(name, own, slots, dev_arr, l, buf):
    ns, Ks, Ns = own.shape
    hr = Ks // 2
    tr = _rowtile(hr, Ns)
    fresh = isinstance(buf, tuple)

    def body(pl_ref, own_ref, *refs):
        acc = own_ref[...].astype(F32)
        for s_ref in refs[:7]:
            acc = acc + s_ref[...].astype(F32)
        refs[-1][...] = acc

    in_specs = [pl.BlockSpec((None, tr, Ns), lambda r, pl_ref: (pl_ref[0], r, 0))]
    in_specs += [pl.BlockSpec((None, tr, Ns), functools.partial(lambda r, pl_ref, j: (j, r, 0), j=j)) for j in range(7)]
    ops = [dev_arr, own.reshape(ns * 2, hr, Ns)] + [slots] * 7
    if not fresh:
        in_specs.append(pl.BlockSpec(memory_space=pl.ANY))
        ops.append(buf)
    grid_spec = pltpu.PrefetchScalarGridSpec(num_scalar_prefetch=1, grid=(hr // tr,), in_specs=in_specs,
                                             out_specs=pl.BlockSpec((None, None, tr, Ns), lambda r, pl_ref: (l, 0, r, 0)))
    return _pcall(body, name=name, grid_spec=grid_spec, out_shape=jax.ShapeDtypeStruct(buf if fresh else buf.shape, F32),
                  input_output_aliases={} if fresh else {9: 0}, compiler_params=_row_params())(*ops)


def _share_reduced(gs):
    n = len(gs)

    def body(*refs):
        ins, outs = refs[:n], refs[n:2 * n]
        send, recv = refs[2 * n:]
        x, y, c, _ = _place()
        cps = []
        for t in range(n):
            cp = pltpu.make_async_remote_copy(src_ref=ins[t].at[:, 0], dst_ref=outs[t].at[:, 1], send_sem=send.at[t], recv_sem=recv.at[t],
                                              device_id=(x, y, 1 - c), device_id_type=MESH)
            cp.start()
            cps.append(cp)
        for cp in cps:
            cp.wait()

    return _pcall(body, name="rs_share_reduced", in_specs=[HBM_SPEC] * n, out_specs=tuple([HBM_SPEC] * n),
                  out_shape=tuple(jax.ShapeDtypeStruct(g.shape, g.dtype) for g in gs),
                  input_output_aliases={t: t for t in range(n)},
                  scratch_shapes=[pltpu.SemaphoreType.DMA((n,)), pltpu.SemaphoreType.DMA((n,))])(*gs)


def _allreduce_small(vec):
    R = vec.shape[0]

    def body(v_ref, o_ref, buf, send, recv):
        x, y, c, _ = _place()
        me = 4 * x + 2 * y + c
        buf[me] = v_ref[...]
        cps = []
        for r in range(1, 8):
            fx, fy, fc = (r >> 2) & 1, (r >> 1) & 1, r & 1
            to = (1 - x if fx else x, 1 - y if fy else y, 1 - c if fc else c)
            cp = pltpu.make_async_remote_copy(src_ref=v_ref, dst_ref=buf.at[me], send_sem=send.at[r - 1], recv_sem=recv.at[r - 1],
                                              device_id=to, device_id_type=MESH)
            cp.start()
            cps.append(cp)
        for r in range(1, 8):
            fx, fy, fc = (r >> 2) & 1, (r >> 1) & 1, r & 1
            frm = (1 - x if fx else x, 1 - y if fy else y, 1 - c if fc else c)
            src = 4 * frm[0] + 2 * frm[1] + frm[2]
            pltpu.make_async_remote_copy(src_ref=v_ref, dst_ref=buf.at[src], send_sem=send.at[r - 1], recv_sem=recv.at[r - 1],
                                         device_id=frm, device_id_type=MESH).wait_recv()
        for cp in cps:
            cp.wait_send()
        acc = buf[0]
        for i in range(1, 8):
            acc = acc + buf[i]
        o_ref[...] = acc

    vm = pl.BlockSpec(memory_space=pltpu.VMEM)
    return _pcall(body, name="allreduce_small", in_specs=[vm], out_specs=vm, out_shape=jax.ShapeDtypeStruct((R, LANES), F32),
                  scratch_shapes=[pltpu.VMEM((8, R, LANES), F32), pltpu.SemaphoreType.DMA((7,)), pltpu.SemaphoreType.DMA((7,))])(vec)


def _unshard_cols(wg):
    ns, depth, K, Ns = wg.shape
    return jnp.moveaxis(wg, 0, 2).reshape(depth, K, ns * Ns)


def _shard_cols(w):
    K, N = w.shape
    return jnp.moveaxis(w.reshape(K, N_SHARD, N // N_SHARD), 1, 0)


def _uq_padded(cfg, wuq_g):
    depth = wuq_g.shape[1]
    w = _unshard_cols(wuq_g).reshape(depth, cfg.QL, cfg.AH, 192)
    w = jnp.pad(w, ((0, 0), (0, cfg.QLP - cfg.QL), (0, 0), (0, 64)))
    return w.reshape(1, depth, cfg.QLP, cfg.AH * 256)


def _uq_grad_unpadded(cfg, dw):
    w = dw[:cfg.QL].reshape(cfg.QL, cfg.AH, 256)[:, :, :192].reshape(cfg.QL, cfg.UQ)
    return _shard_cols(w)


def _ukv_padded(cfg, wukv_g):
    w = _unshard_cols(wukv_g)
    w = jnp.pad(w, ((0, 0), (cfg.KOFF, cfg.KVW - cfg.KVL - cfg.KOFF), (0, 0)))
    return w[None]


def _ukv_grad_unpadded(cfg, dw):
    return _shard_cols(dw[cfg.KOFF:cfg.KOFF + cfg.KVL])


def _pad_lanes(v, lo, total):
    return jnp.pad(v, (lo, total - lo - v.shape[0]))[None]


def _layer_fwd(cfg, l, x, W, small, tabs):
    ln1, gq, gkv, gqn, gkn, gout, ln2 = small
    sc_a, sc_h = 1.0 / math.sqrt(192), 1.0 / math.sqrt(128)
    n = f"l{l}_"
    h = _rms_fwd(n + "ln1", x, ln1)
    proj = _mm_nn(n + "proj", h, (W["w_in"], "col", 0))
    cqn, ckvn, kpe = _mla_prep_fwd(n + "mla_prep", cfg, proj, gq, gkv, tabs["akr"])
    qa = _mm_nn(n + "uq", cqn, (W["uq_p"], "col", 0))
    kva = _mm_nn(n + "ukv", ckvn, (W["ukv_p"], "col", 0))
    q_a, k_a, v_a = _mla_build_fwd(n + "mla_build", cfg, qa, kva, kpe, tabs["aq"])
    o_a, lse_a = _flash_fwd(n + "attn_a", q_a, k_a, v_a, cfg.AH, 1, 256, 128, sc_a)
    q_b = _headprep_fwd(n + "bq", proj, cfg.o_bq, cfg.BH, gqn, tabs["b"])
    k_b = _headprep_fwd(n + "bk", proj, cfg.o_bk, cfg.BKV, gkn, tabs["b"])
    v_b = _headprep_fwd(n + "bv", proj, cfg.o_bv, cfg.BKV, None, None)
    o_b, lse_b = _flash_fwd(n + "attn_b", q_b, k_b, v_b, cfg.BH, cfg.G, 128, 128, sc_h)
    q_c = _headprep_fwd(n + "cq", proj, cfg.o_cq, cfg.CH, None, tabs["c"])
    k_c = _headprep_fwd(n + "ck", proj, cfg.o_ck, cfg.CH, None, tabs["c"])
    v_c = _headprep_fwd(n + "cv", proj, cfg.o_cv, cfg.CH, None, None)
    o_c, lse_c = _flash_fwd(n + "attn_c", q_c, k_c, v_c, cfg.CH, 1, 128, 128, sc_h, tabs["bias_c"], cfg.W)
    mixed = _outnorm_fwd(n + "outnorm", cfg, o_a, o_b, o_c, gout)
    x1 = _mm_nn(n + "out", mixed, (W["w_out"], "row", 0), epi=_epi_residual, extra=x)
    h2 = _rms_fwd(n + "ln2", x1, ln2)
    a, u = _mm_nn(n + "ff1", h2, (W["w_ff1"], "col", 0), epi=_epi_relu2, out_dtypes=(MM_DT, MM_DT))
    x2 = _mm_nn(n + "ff2", u, (W["w_ff2"], "row", 0), epi=_epi_residual, extra=x1)
    saved = dict(x=x, h=h, proj=proj, cqn=cqn, ckvn=ckvn, q_a=q_a, k_a=k_a, v_a=v_a, o_a=o_a, lse_a=lse_a,
                 q_b=q_b, k_b=k_b, v_b=v_b, o_b=o_b, lse_b=lse_b, q_c=q_c, k_c=k_c, v_c=v_c, o_c=o_c, lse_c=lse_c,
                 mixed=mixed, x1=x1, h2=h2, a=a, u=u)
    return x2, saved


def _layer_bwd(cfg, l, dx2, sv, W, small, tabs, GW):
    ln1, gq, gkv, gqn, gkn, gout, ln2 = small
    sc_a, sc_h = 1.0 / math.sqrt(192), 1.0 / math.sqrt(128)
    n = f"l{l}_b_"
    S = cfg.S
    mats = {m[0]: m for m in cfg.mats}

    def dw(name, a, g, key):
        _, Rs, Cs, kind = mats[key]
        GW[key] = _mm_tn(n + name, a, g, kind, 0, Rs, Cs, (N_SHARD, 1, Rs, Cs), out_dtype=MM_DT).reshape(N_SHARD, Rs, Cs)

    da = _mm_nt(n + "ff2_dx", dx2, (W["w_ff2"], "row", 0), epi=_epi_drelu2, out_dtype=MM_DT, extra=sv["a"])
    dw("ff2_dw", sv["u"], dx2, "w_ff2")
    dh2 = _mm_nt(n + "ff1_dx", da, (W["w_ff1"], "col", 0))
    dw("ff1_dw", sv["h2"], da, "w_ff1")
    dx1, dln2 = _rms_bwd(n + "ln2", sv["x1"], ln2, dh2, dx2)
    dmix = _mm_nt(n + "out_dx", dx1, (W["w_out"], "row", 0))
    dw("out_dw", sv["mixed"], dx1, "w_out")
    do_a, do_b, do_c, dgout = _outnorm_bwd(n + "outnorm", cfg, dmix, sv["o_a"], sv["o_b"], sv["o_c"], gout)
    dproj = jnp.zeros((S, cfg.IN), MM_DT)
    args_c = (sv["q_c"], sv["k_c"], sv["v_c"], do_c, sv["o_c"], sv["lse_c"], cfg.CH, 1, 128, 128, sc_h, tabs["bias_c"], cfg.W)
    dq_c = _flash_dq(n + "attn_c_dq", *args_c)
    dk_c, dv_c = _flash_dkv(n + "attn_c_dkv", *args_c)
    dproj, _ = _headprep_bwd(n + "cq", dq_c, sv["proj"], cfg.o_cq, cfg.CH, None, tabs["c"], dproj)
    dproj, _ = _headprep_bwd(n + "ck", dk_c, sv["proj"], cfg.o_ck, cfg.CH, None, tabs["c"], dproj)
    dproj, _ = _headprep_bwd(n + "cv", dv_c, sv["proj"], cfg.o_cv, cfg.CH, None, None, dproj)
    args_b = (sv["q_b"], sv["k_b"], sv["v_b"], do_b, sv["o_b"], sv["lse_b"], cfg.BH, cfg.G, 128, 128, sc_h)
    dq_b = _flash_dq(n + "attn_b_dq", *args_b)
    dk_b, dv_b = _flash_dkv(n + "attn_b_dkv", *args_b)
    dproj, dgqn = _headprep_bwd(n + "bq", dq_b, sv["proj"], cfg.o_bq, cfg.BH, gqn, tabs["b"], dproj)
    dproj, dgkn = _headprep_bwd(n + "bk", dk_b, sv["proj"], cfg.o_bk, cfg.BKV, gkn, tabs["b"], dproj)
    dproj, _ = _headprep_bwd(n + "bv", dv_b, sv["proj"], cfg.o_bv, cfg.BKV, None, None, dproj)
    args_a = (sv["q_a"], sv["k_a"], sv["v_a"], do_a, sv["o_a"], sv["lse_a"], cfg.AH, 1, 256, 128, sc_a)
    dq_a = _flash_dq(n + "attn_a_dq", *args_a)
    dk_a, dv_a = _flash_dkv(n + "attn_a_dkv", *args_a)
    dqa, dkva, dkpe = _mla_build_bwd(n + "mla_build", cfg, dq_a, dk_a, dv_a, tabs["aq"])
    dcq = _mm_nt(n + "uq_dx", dqa, (W["uq_p"], "col", 0))
    dwuq = _mm_tn(n + "uq_dw", sv["cqn"], dqa, "col", 0, cfg.QLP, cfg.AH * 256, (1, 1, cfg.QLP, cfg.AH * 256))
    dckv = _mm_nt(n + "ukv_dx", dkva, (W["ukv_p"], "col", 0))
    dwukv = _mm_tn(n + "ukv_dw", sv["ckvn"], dkva, "col", 0, cfg.KVW, cfg.AH * 256, (1, 1, cfg.KVW, cfg.AH * 256))
    dproj, dgq, dgkv = _mla_prep_bwd(n + "mla_prep", cfg, dcq, dckv, dkpe, sv["proj"], gq, gkv, tabs["akr"], dproj)
    dh = _mm_nt(n + "proj_dx", dproj, (W["w_in"], "col", 0))
    dw("proj_dw", sv["h"], dproj, "w_in")
    dx, dln1 = _rms_bwd(n + "ln1", sv["x"], ln1, dh, dx1)
    gains = dict(ln1_g=dln1[0], g_q_a=dgq[0, :cfg.QL], g_kv_a=dgkv[0, cfg.KOFF:cfg.KOFF + cfg.KVL], g_qn_b=dgqn[0],
                 g_kn_b=dgkn[0], g_out=dgout[0], ln2_g=dln2[0])
    GW["w_uq"] = _uq_grad_unpadded(cfg, dwuq[0, 0]).astype(MM_DT)
    GW["w_ukv"] = _ukv_grad_unpadded(cfg, dwukv[0, 0]).astype(MM_DT)
    return dx, gains


SMALL_NAMES = ("ln1_g", "g_q_a", "g_kv_a", "g_qn_b", "g_kn_b", "g_out", "ln2_g")
MAT_NAMES = ("w_in", "w_uq", "w_ukv", "w_out", "w_ff1", "w_ff2")


def _pack_small(cfg, per_layer, final, scalar=None):
    last = jnp.zeros((1,), F32) if scalar is None else scalar.reshape(1)
    flat = jnp.concatenate([per_layer[k].reshape(-1) for k in SMALL_NAMES] + [final.reshape(-1), last])
    total = flat.shape[0]
    rows = _rup(-(-total // LANES), 8)
    return jnp.pad(flat, (0, rows * LANES - total)).reshape(rows, LANES)


def _unpack_small(cfg, packed, shapes):
    flat = packed.reshape(-1)
    out, off = {}, 0
    for k in SMALL_NAMES + ("ln_f_g",):
        n = math.prod(shapes[k])
        out[k] = flat[off:off + n].reshape(shapes[k])
        off += n
    return out, flat[off]


def _step(cfg, w, m, v, x, tgt):
    DEPTH, hd = cfg.DEPTH, cfg.HD
    c = lax.axis_index("c")
    me_chip = 2 * lax.axis_index("x") + lax.axis_index("y")
    c_arr = jnp.reshape(c, (1,)).astype(jnp.int32)
    dev_arr = jnp.reshape(2 * me_chip + c, (1,)).astype(jnp.int32)
    mats = {mt[0]: mt for mt in cfg.mats}

    shards = []
    for name in MAT_NAMES:
        _, Ks, Ns, _ = mats[name]
        shards.append(_cast_rows("cast_" + name, w[name].reshape(DEPTH * Ks, Ns), MM_DT).reshape(DEPTH, Ks, Ns))
    W_layers, got = [], None
    n_first = 3
    for l in range(DEPTH):
        mine = [s[l] for s in shards]
        if l == 0:
            first = _allgather_layer("allgather_l0_first", mine[:n_first])
            rest = list(lax.optimization_barrier((tuple(mine[n_first:]), tuple(first)))[0])
            got = list(first) + list(_allgather_layer_async("allgather_l0_rest", rest, collective_id=3 * DEPTH))
        else:
            if l == 1:
                mine = list(lax.optimization_barrier((tuple(mine), tuple(got)))[0])
            got = _allgather_layer_async(f"allgather_l{l}", mine, collective_id=l)
        g = {name: a[:, None] for name, a in zip(MAT_NAMES, got)}
        W_layers.append(dict(w_in=g["w_in"], w_out=g["w_out"], w_ff1=g["w_ff1"], w_ff2=g["w_ff2"],
                             uq_p=_uq_padded(cfg, g["w_uq"]), ukv_p=_ukv_padded(cfg, g["w_ukv"])))
    tabs = _all_tables(cfg)
    tabs["bias_c"] = _band_bias(cfg)

    def small_of(l):
        return (w["ln1_g"][l][None], _pad_lanes(w["g_q_a"][l], 0, cfg.QLP), _pad_lanes(w["g_kv_a"][l], cfg.KOFF, cfg.KVW),
                w["g_qn_b"][l][None], w["g_kn_b"][l][None], w["g_out"][l][None], w["ln2_g"][l][None])

    saved = []
    xc = x
    for l in range(DEPTH):
        xc, sv = _layer_fwd(cfg, l, xc, W_layers[l], small_of(l), tabs)
        saved.append(sv)
    dx, dlnf, loss_rows = _final_loss("final_loss", xc, w["ln_f_g"][None], tgt)
    loss = loss_rows[0, 0]
    gain_rows, own, slots = [None] * DEPTH, [None] * DEPTH, [None] * DEPTH
    for l in reversed(range(DEPTH)):
        GW = {}
        dx, gain_rows[l] = _layer_bwd(cfg, l, dx, saved[l], W_layers[l], small_of(l), tabs, GW)
        own[l] = [GW[name] for name in MAT_NAMES]
        early = _grad_exchange_async(f"grad_exchange_l{l}_early", own[l][n_first:], collective_id=DEPTH + 2 * l)
        late = _grad_exchange_async(f"grad_exchange_l{l}_late", own[l][:n_first], collective_id=DEPTH + 2 * l + 1)
        slots[l] = late + early

    reduced = []
    for t, name in enumerate(MAT_NAMES):
        _, Ks, Ns, _ = mats[name]
        buf = (DEPTH, 2, Ks // 2, Ns)
        for l in reversed(range(DEPTH)):
            buf = _add_eight(f"rs_add_l{l}_" + name, own[l][t], slots[l][t], dev_arr, l, buf)
        reduced.append(buf)
    full = _share_reduced(reduced)
    grad, delta, new_m, new_v = {}, {}, {}, {}
    for name, g2 in zip(MAT_NAMES, full):
        _, Ks, Ns, _ = mats[name]
        halves, shp = (DEPTH, 2, Ks // 2, Ns), (DEPTH, Ks, Ns)
        res = _adamw_halves("adamw_" + name, w[name].reshape(halves), g2, m[name].reshape(halves), v[name].reshape(halves), c_arr)
        grad[name], delta[name], new_m[name], new_v[name] = (r.reshape(shp) for r in res)

    per_layer = {k: jnp.stack([gain_rows[l][k] for l in range(DEPTH)]) for k in SMALL_NAMES}
    shapes = {k: w[k].shape for k in SMALL_NAMES + ("ln_f_g",)}
    gsum = _allreduce_small(_pack_small(cfg, per_layer, dlnf[0], loss))
    pk = lambda d: _pack_small(cfg, {k: d[k] for k in SMALL_NAMES}, d["ln_f_g"])
    d_s, m_s, v_s = _adamw("adamw_small", pk(w), gsum, pk(m), pk(v))
    for res, packed in ((grad, gsum), (delta, d_s), (new_m, m_s), (new_v, v_s)):
        res.update(_unpack_small(cfg, packed, shapes)[0])
    loss_total = _unpack_small(cfg, gsum, shapes)[1]
    return loss_total, dx, grad, delta, new_m, new_v


WEIGHT_NAMES = ("ln1_g", "w_in", "g_q_a", "w_uq", "g_kv_a", "w_ukv", "g_qn_b", "g_kn_b", "g_out", "w_out", "ln2_g",
                "w_ff1", "w_ff2", "ln_f_g")


def _run(cfg, args):
    nw = len(WEIGHT_NAMES)
    x, tgt = args[0], args[1 + nw]
    w = dict(zip(WEIGHT_NAMES, args[1:1 + nw]))
    m = dict(zip(WEIGHT_NAMES, args[2 + nw:2 + 2 * nw]))
    v = dict(zip(WEIGHT_NAMES, args[2 + 2 * nw:2 + 3 * nw]))
    loss, dx, grad, delta, new_m, new_v = _step(cfg, w, m, v, x.reshape(cfg.S, cfg.D), tgt.reshape(cfg.S, cfg.D))
    return (loss, dx.reshape(x.shape), *[grad[k] for k in WEIGHT_NAMES], *[delta[k] for k in WEIGHT_NAMES],
            *[new_m[k] for k in WEIGHT_NAMES], *[new_v[k] for k in WEIGHT_NAMES])


def kernel(x, ln1_g, w_in, g_q_a, w_uq, g_kv_a, w_ukv, g_qn_b, g_kn_b, g_out, w_out, ln2_g, w_ff1, w_ff2, ln_f_g, loss_target, m_ln1_g, m_w_in, m_g_q_a, m_w_uq, m_g_kv_a, m_w_ukv, m_g_qn_b, m_g_kn_b, m_g_out, m_w_out, m_ln2_g, m_w_ff1, m_w_ff2, m_ln_f_g, v_ln1_g, v_w_in, v_g_q_a, v_w_uq, v_g_kv_a, v_w_ukv, v_g_qn_b, v_g_kn_b, v_g_out, v_w_out, v_ln2_g, v_w_ff1, v_w_ff2, v_ln_f_g):
    return _run(Cfg(), (x, ln1_g, w_in, g_q_a, w_uq, g_kv_a, w_ukv, g_qn_b, g_kn_b, g_out, w_out, ln2_g, w_ff1, w_ff2, ln_f_g, loss_target, m_ln1_g, m_w_in, m_g_q_a, m_w_uq, m_g_kv_a, m_w_ukv, m_g_qn_b, m_g_kn_b, m_g_out, m_w_out, m_ln2_g, m_w_ff1, m_w_ff2, m_ln_f_g, v_ln1_g, v_w_in, v_g_q_a, v_w_uq, v_g_kv_a, v_w_ukv, v_g_qn_b, v_g_kn_b, v_g_out, v_w_out, v_ln2_g, v_w_ff1, v_w_ff2, v_ln_f_g))
```

```python
import functools
import math

import jax
import jax.numpy as jnp
from jax import lax
from jax.experimental import pallas as pl
from jax.experimental.pallas import tpu as pltpu
from jax.experimental.pallas import tpu_sc as plsc

F32 = jnp.float32
MM_DT = jnp.bfloat16
LANES = 128
SUBLANES_F32 = 8
SUBLANES_BF16 = 16
VMEM_LIMIT = 48 * 1024 * 1024
EPS = 1e-6
NEG = -1e30
ROPE_THETA = 10000.0
MM_TK = 1024
ATT_TQ, ATT_TK_FWD, ATT_TK = 512, 4096, 2048
ADAM_LR, ADAM_B1, ADAM_B2, ADAM_EPS, ADAM_WD, ADAM_STEP = 0.001, 0.9, 0.999, 1e-08, 0.01, 10
MESH_AXES = ("x", "y", "c")
N_SHARD = 4
MESH = pl.DeviceIdType.MESH

NN = (((1,), (0,)), ((), ()))
NT = (((1,), (1,)), ((), ()))
TN = (((0,), (0,)), ((), ()))


def _pcall(body, **kw):
    return pl.pallas_call(body, **kw)


def _rup(n, m):
    return -(-n // m) * m


def _pick(n, pref, mult):
    best = None
    for t in range(mult, min(n, pref) + 1, mult):
        if n % t == 0:
            best = t
    return best if best is not None else n


class Cfg:
    def __init__(self, S=4096, D=2048, DEPTH=4, AH=4, QL=448, KVL=512, BH=6, BKV=2, CH=6,
                 BRANCHES=((128, 1), (512, 4), (2048, 16)), DFF=8192, GRID_W=64, TB=512, TBK=1024):
        self.S, self.D, self.DEPTH, self.AH, self.QL, self.KVL = S, D, DEPTH, AH, QL, KVL
        self.BH, self.BKV, self.CH, self.DFF, self.GRID_W, self.TB = BH, BKV, CH, DFF, GRID_W, TB
        self.G = BH // BKV
        self.AW, self.BW, self.CW = AH * 128, BH * 128, CH * 128
        self.MIX = self.AW + self.BW + self.CW
        self.QLP = _rup(QL, LANES)
        self.KV0 = (QL // LANES) * LANES
        self.PW = QL + KVL + 64
        assert self.PW % LANES == 0
        self.KVW = self.PW - self.KV0
        self.KOFF = QL - self.KV0
        self.o_bq = self.PW
        self.o_bk = self.o_bq + self.BW
        self.o_bv = self.o_bk + BKV * 128
        self.o_cq = self.o_bv + BKV * 128
        self.o_ck = self.o_cq + self.CW
        self.o_cv = self.o_ck + self.CW
        self.IN = self.o_cv + self.CW
        self.UQ, self.UKV = AH * 192, AH * 256
        self.branches = tuple(((w // (2 * d)) * d, d) for w, d in BRANCHES)
        for _, d in self.branches:
            assert d & (d - 1) == 0
        self.TBK = TBK
        self.W = -(-max(r for r, _ in self.branches) // TBK)
        assert S % TBK == 0 and TBK % TB == 0 and DEPTH % 2 == 0
        self.HD = DEPTH // 2
        self.mats = (("w_in", D, self.IN // 4, "col"), ("w_uq", QL, self.UQ // 4, "col"),
                     ("w_ukv", KVL, self.UKV // 4, "col"), ("w_out", self.MIX // 4, D, "row"),
                     ("w_ff1", D, DFF // 4, "col"), ("w_ff2", DFF // 4, D, "row"))


def _mm_call(name, mode, operands, in_specs, out_shape, out_specs, grid, acc_shape, epi, n_extra, aliases=None):
    nk = grid[2]

    def body(*refs):
        a_ref, b_ref = refs[0], refs[1]
        ex = refs[2:2 + n_extra]
        outs = refs[2 + n_extra:-1]
        acc = refs[-1]
        k = pl.program_id(2)

        @pl.when(k == 0)
        def _():
            acc[...] = jnp.zeros_like(acc)

        acc[...] += lax.dot_general(a_ref[...].astype(MM_DT), b_ref[...].astype(MM_DT), mode,
                                    preferred_element_type=F32)

        @pl.when(k == nk - 1)
        def _():
            epi(acc[...], ex, outs)

    return _pcall(body, name=name, grid=grid, in_specs=in_specs, out_specs=out_specs, out_shape=out_shape,
                  scratch_shapes=[pltpu.VMEM(acc_shape, F32)], input_output_aliases=aliases or {},
                  compiler_params=pltpu.CompilerParams(dimension_semantics=("parallel", "parallel", "arbitrary"),
                                                       vmem_limit_bytes=VMEM_LIMIT))(*operands)


def _wspec(kind, l, Rs, Cs, br, bc, rfn, cfn):
    assert Rs % br == 0 and Cs % bc == 0
    if kind == "col":
        npc = Cs // bc
        return pl.BlockSpec((None, None, br, bc), lambda i, j, k: (cfn(i, j, k) // npc, l, rfn(i, j, k), cfn(i, j, k) % npc))
    npr = Rs // br
    return pl.BlockSpec((None, None, br, bc), lambda i, j, k: (rfn(i, j, k) // npr, l, rfn(i, j, k) % npr, cfn(i, j, k)))


def _epi_plain(acc, ex, outs):
    outs[0][...] = acc.astype(outs[0].dtype)


def _epi_residual(acc, ex, outs):
    outs[0][...] = ex[0][...] + acc


def _epi_relu2(acc, ex, outs):
    outs[0][...] = acc.astype(outs[0].dtype)
    r = jnp.maximum(acc, 0.0)
    outs[1][...] = (r * r).astype(outs[1].dtype)


def _epi_drelu2(acc, ex, outs):
    a = ex[0][...].astype(F32)
    outs[0][...] = (acc * (2.0 * jnp.maximum(a, 0.0))).astype(outs[0].dtype)


def _wdims(wd):
    Wg, kind, l = wd
    ns, _, Rs, Cs = Wg.shape
    K = Rs * ns if kind == "row" else Rs
    N = Cs * ns if kind == "col" else Cs
    return Wg, kind, l, Rs, Cs, K, N


def _mm_nn(name, a, wd, epi=_epi_plain, out_dtypes=(F32,), extra=None):
    Wg, kind, l, Rs, Cs, K, N = _wdims(wd)
    M = a.shape[0]
    tm, tk, tn = _pick(M, 1024, 16), _pick(Rs, MM_TK, LANES), _pick(Cs, 1152, LANES)
    grid = (M // tm, N // tn, K // tk)
    in_specs = [pl.BlockSpec((tm, tk), lambda i, j, k: (i, k)),
                _wspec(kind, l, Rs, Cs, tk, tn, lambda i, j, k: k, lambda i, j, k: j)]
    ops = [a, Wg]
    if extra is not None:
        in_specs.append(pl.BlockSpec((tm, tn), lambda i, j, k: (i, j)))
        ops.append(extra)
    o_spec = pl.BlockSpec((tm, tn), lambda i, j, k: (i, j))
    outs = tuple(jax.ShapeDtypeStruct((M, N), dt) for dt in out_dtypes)
    res = _mm_call(name, NN, ops, in_specs, outs, tuple(o_spec for _ in outs), grid, (tm, tn), epi,
                   0 if extra is None else 1)
    return res[0] if len(res) == 1 else res


def _mm_nt(name, g, wd, epi=_epi_plain, out_dtype=F32, extra=None):
    Wg, kind, l, Rs, Cs, K, N = _wdims(wd)
    M = g.shape[0]
    tm, tn, tk = _pick(M, 1024, 16), _pick(Rs, 1024, LANES), _pick(Cs, 1152, LANES)
    grid = (M // tm, K // tn, N // tk)
    in_specs = [pl.BlockSpec((tm, tk), lambda i, j, k: (i, k)),
                _wspec(kind, l, Rs, Cs, tn, tk, lambda i, j, k: j, lambda i, j, k: k)]
    ops = [g, Wg]
    if extra is not None:
        in_specs.append(pl.BlockSpec((tm, tn), lambda i, j, k: (i, j)))
        ops.append(extra)
    res = _mm_call(name, NT, ops, in_specs, (jax.ShapeDtypeStruct((M, K), out_dtype),),
                   (pl.BlockSpec((tm, tn), lambda i, j, k: (i, j)),), grid, (tm, tn), epi, 0 if extra is None else 1)
    return res[0]


def _mm_tn(name, a, g, kind, l, Rs, Cs, buf, out_dtype=F32):
    M, K = a.shape
    N = g.shape[1]
    tm, tn, tk = _pick(Rs, 1024, LANES), _pick(Cs, 1152, LANES), _pick(M, MM_TK, LANES)
    grid = (K // tm, N // tn, M // tk)
    in_specs = [pl.BlockSpec((tk, tm), lambda i, j, k: (k, i)),
                pl.BlockSpec((tk, tn), lambda i, j, k: (k, j))]
    o_spec = _wspec(kind, l, Rs, Cs, tm, tn, lambda i, j, k: i, lambda i, j, k: j)
    if isinstance(buf, tuple):
        res = _mm_call(name, TN, [a, g], in_specs, (jax.ShapeDtypeStruct(buf, out_dtype),), (o_spec,), grid, (tm, tn), _epi_plain, 0)
    else:
        res = _mm_call(name, TN, [a, g, buf], in_specs + [pl.BlockSpec(memory_space=pl.ANY)],
                       (jax.ShapeDtypeStruct(buf.shape, buf.dtype),), (o_spec,), grid, (tm, tn), _epi_plain, 1, aliases={2: 0})
    return res[0]


def _row_params():
    return pltpu.CompilerParams(dimension_semantics=("arbitrary",), vmem_limit_bytes=VMEM_LIMIT)


def _rms_fwd(name, x, g):
    S, D = x.shape
    tr = _pick(S, 256, 16)

    def body(x_ref, g_ref, o_ref):
        xv = x_ref[...]
        r = lax.rsqrt(jnp.mean(xv * xv, axis=-1, keepdims=True) + EPS)
        o_ref[...] = (xv * r * g_ref[...]).astype(o_ref.dtype)

    return _pcall(body, name=name, grid=(S // tr,),
                  in_specs=[pl.BlockSpec((tr, D), lambda i: (i, 0)), pl.BlockSpec((1, D), lambda i: (0, 0))],
                  out_specs=pl.BlockSpec((tr, D), lambda i: (i, 0)), out_shape=jax.ShapeDtypeStruct((S, D), MM_DT),
                  compiler_params=_row_params())(x, g)


def _acc_rows(ref, part, first):
    @pl.when(first)
    def _():
        ref[...] = jnp.zeros_like(ref)

    ref[...] += jnp.broadcast_to(part, ref.shape)


def _rms_bwd(name, x, g, dy, res):
    S, D = x.shape
    tr = _pick(S, 256, 16)

    def body(x_ref, g_ref, dy_ref, res_ref, dx_ref, dg_ref):
        xv = x_ref[...]
        r = lax.rsqrt(jnp.mean(xv * xv, axis=-1, keepdims=True) + EPS)
        xh = xv * r
        dyv = dy_ref[...]
        dn = dyv * g_ref[...]
        dx_ref[...] = res_ref[...] + r * (dn - xh * jnp.mean(dn * xh, axis=-1, keepdims=True))
        _acc_rows(dg_ref, jnp.sum(dyv * xh, axis=0, keepdims=True), pl.program_id(0) == 0)

    row = pl.BlockSpec((tr, D), lambda i: (i, 0))
    return _pcall(body, name=name, grid=(S // tr,),
                  in_specs=[row, pl.BlockSpec((1, D), lambda i: (0, 0)), row, row],
                  out_specs=(row, pl.BlockSpec((8, D), lambda i: (0, 0))),
                  out_shape=(jax.ShapeDtypeStruct((S, D), F32), jax.ShapeDtypeStruct((8, D), F32)),
                  compiler_params=_row_params())(x, g, dy, res)


def _final_loss(name, x, g, tgt):
    S, D = x.shape
    tr = _pick(S, 256, 16)

    def body(x_ref, g_ref, t_ref, dx_ref, dg_ref, loss_ref):
        xv = x_ref[...]
        r = lax.rsqrt(jnp.mean(xv * xv, axis=-1, keepdims=True) + EPS)
        xh = xv * r
        gv = g_ref[...]
        e = xh * gv - t_ref[...]
        part = 0.5 * jnp.sum(jnp.mean(e * e, axis=-1, keepdims=True), axis=0, keepdims=True)
        dy = e * (1.0 / D)
        dn = dy * gv
        dx_ref[...] = r * (dn - xh * jnp.mean(dn * xh, axis=-1, keepdims=True))
        first = pl.program_id(0) == 0
        _acc_rows(dg_ref, jnp.sum(dy * xh, axis=0, keepdims=True), first)
        _acc_rows(loss_ref, part, first)

    row = pl.BlockSpec((tr, D), lambda i: (i, 0))
    return _pcall(body, name=name, grid=(S // tr,),
                  in_specs=[row, pl.BlockSpec((1, D), lambda i: (0, 0)), row],
                  out_specs=(row, pl.BlockSpec((8, D), lambda i: (0, 0)), pl.BlockSpec((8, LANES), lambda i: (0, 0))),
                  out_shape=(jax.ShapeDtypeStruct((S, D), F32), jax.ShapeDtypeStruct((8, D), F32),
                             jax.ShapeDtypeStruct((8, LANES), F32)),
                  compiler_params=_row_params())(x, g, tgt)


def _rope_tables(cos, sin, off, w):
    S = cos.shape[0]
    h = w // 2
    z = lambda n: jnp.zeros((S, n), F32)
    C = jnp.concatenate([z(off), cos, cos, z(LANES - off - w)], axis=1)
    SP = jnp.concatenate([z(off + h), sin, z(LANES - off - w)], axis=1)
    SN = jnp.concatenate([z(off), -sin, z(LANES - off - h)], axis=1)
    return C, SP, SN


def _angles(pos, dim):
    inv = jnp.power(ROPE_THETA, -jnp.arange(0, dim, 2, dtype=F32) / dim)
    ang = pos.astype(F32)[:, None] * inv[None, :]
    return jnp.cos(ang), jnp.sin(ang)


def _all_tables(cfg):
    S = cfg.S
    pos = jnp.arange(S, dtype=F32)
    rows = S // cfg.GRID_W
    row = jnp.repeat(jnp.arange(rows, dtype=F32), cfg.GRID_W)
    col = jnp.tile(jnp.arange(cfg.GRID_W, dtype=F32), rows)
    ca, sa = _angles(pos, 64)
    cc, sc = _angles(pos, 128)
    cr, sr = _angles(row, 64)
    cl, sl = _angles(col, 64)
    t_b = tuple(a + b for a, b in zip(_rope_tables(cr, sr, 0, 64), _rope_tables(cl, sl, 64, 64)))
    return {"aq": (_rope_tables(ca, sa, 0, 64), 64), "akr": (_rope_tables(ca, sa, 64, 64), 64),
            "b": (t_b, 64), "c": (_rope_tables(cc, sc, 0, 128), 128)}


def _rope(x, C, SP, SN, w):
    h = w // 2
    if 2 * h == LANES:
        return x * C + pltpu.roll(x, h, 1) * (SP + SN)
    return x * C + pltpu.roll(x, h, 1) * SP + pltpu.roll(x, LANES - h, 1) * SN


def _rope_t(dy, C, SP, SN, w):
    h = w // 2
    if 2 * h == LANES:
        return dy * C + pltpu.roll(dy * (SP + SN), h, 1)
    return dy * C + pltpu.roll(dy * SP, LANES - h, 1) + pltpu.roll(dy * SN, h, 1)


def _grid2_params():
    return pltpu.CompilerParams(dimension_semantics=("arbitrary", "arbitrary"), vmem_limit_bytes=VMEM_LIMIT)


def _headprep_fwd(name, proj, col_off, nb, gain, tabs):
    S = proj.shape[0]
    tr = _pick(S, 1024, 16)
    cb = col_off // LANES
    norm, rope = gain is not None, tabs is not None
    w = tabs[1] if rope else 0

    def body(*refs):
        x_ref = refs[0]
        pos = 1
        xv = x_ref[...]
        if norm:
            r = lax.rsqrt(jnp.mean(xv * xv, axis=-1, keepdims=True) + EPS)
            xv = xv * r * refs[pos][...]
            pos += 1
        if rope:
            xv = _rope(xv, refs[pos][...], refs[pos + 1][...], refs[pos + 2][...], w)
            pos += 3
        refs[pos][...] = xv.astype(refs[pos].dtype)

    ops, in_specs = [proj], [pl.BlockSpec((tr, LANES), lambda i, j: (i, cb + j))]
    if norm:
        ops.append(gain)
        in_specs.append(pl.BlockSpec((1, LANES), lambda i, j: (0, 0)))
    if rope:
        ops += list(tabs[0])
        in_specs += [pl.BlockSpec((tr, LANES), lambda i, j: (i, 0))] * 3
    return _pcall(body, name=name, grid=(S // tr, nb), in_specs=in_specs,
                  out_specs=pl.BlockSpec((tr, LANES), lambda i, j: (i, j)),
                  out_shape=jax.ShapeDtypeStruct((S, nb * LANES), MM_DT), compiler_params=_grid2_params())(*ops)


def _headprep_bwd(name, dy, proj, col_off, nb, gain, tabs, dproj):
    S = proj.shape[0]
    tr = _pick(S, 1024, 16)
    cb = col_off // LANES
    norm, rope = gain is not None, tabs is not None
    w = tabs[1] if rope else 0

    def body(*refs):
        dz = refs[0][...]
        pos = 1
        if norm:
            x_ref, g_ref = refs[pos], refs[pos + 1]
            pos += 2
        if rope:
            dz = _rope_t(dz, refs[pos][...], refs[pos + 1][...], refs[pos + 2][...], w)
            pos += 3
        pos += 1
        o_ref = refs[pos]
        if norm:
            dg_ref = refs[pos + 1]
            xv = x_ref[...]
            r = lax.rsqrt(jnp.mean(xv * xv, axis=-1, keepdims=True) + EPS)
            n = xv * r
            first = (pl.program_id(0) == 0) & (pl.program_id(1) == 0)
            _acc_rows(dg_ref, jnp.sum(dz * n, axis=0, keepdims=True), first)
            dn = dz * g_ref[...]
            dz = r * (dn - n * jnp.mean(dn * n, axis=-1, keepdims=True))
        o_ref[...] = dz.astype(o_ref.dtype)

    ops, in_specs = [dy], [pl.BlockSpec((tr, LANES), lambda i, j: (i, j))]
    if norm:
        ops += [proj, gain]
        in_specs += [pl.BlockSpec((tr, LANES), lambda i, j: (i, cb + j)), pl.BlockSpec((1, LANES), lambda i, j: (0, 0))]
    if rope:
        ops += list(tabs[0])
        in_specs += [pl.BlockSpec((tr, LANES), lambda i, j: (i, 0))] * 3
    alias_idx = len(ops)
    ops.append(dproj)
    in_specs.append(pl.BlockSpec(memory_space=pl.ANY))
    out_specs = [pl.BlockSpec((tr, LANES), lambda i, j: (i, cb + j))]
    out_shape = [jax.ShapeDtypeStruct(dproj.shape, dproj.dtype)]
    if norm:
        out_specs.append(pl.BlockSpec((8, LANES), lambda i, j: (0, 0)))
        out_shape.append(jax.ShapeDtypeStruct((8, LANES), F32))
    res = _pcall(body, name=name, grid=(S // tr, nb), in_specs=in_specs, out_specs=tuple(out_specs),
                 out_shape=tuple(out_shape), input_output_aliases={alias_idx: 0}, compiler_params=_grid2_params())(*ops)
    return (res[0], res[1]) if norm else (res[0], None)


def _masked_rms(xv, lo, n):
    lane = lax.broadcasted_iota(jnp.int32, xv.shape, 1)
    xm = jnp.where((lane >= lo) & (lane < lo + n), xv, 0.0)
    r = lax.rsqrt(jnp.sum(xm * xm, axis=-1, keepdims=True) * (1.0 / n) + EPS)
    return xm * r, r


def _mla_prep_fwd(name, cfg, proj, gq, gkv, tabs):
    S = cfg.S
    tr = _pick(S, 256, 16)
    (C, SP, SN), w = tabs

    def body(p_ref, gq_ref, gkv_ref, c_ref, sp_ref, sn_ref, cq_ref, ckv_ref, kpe_ref):
        nq, _ = _masked_rms(p_ref[:, 0:cfg.QLP], 0, cfg.QL)
        cq_ref[...] = (nq * gq_ref[...]).astype(cq_ref.dtype)
        nk, _ = _masked_rms(p_ref[:, cfg.KV0:cfg.PW], cfg.KOFF, cfg.KVL)
        ckv_ref[...] = (nk * gkv_ref[...]).astype(ckv_ref.dtype)
        kr = _rope(p_ref[:, cfg.PW - LANES:cfg.PW], c_ref[...], sp_ref[...], sn_ref[...], w)
        kpe_ref[...] = pltpu.roll(kr, 64, 1).astype(kpe_ref.dtype)

    tab = pl.BlockSpec((tr, LANES), lambda i: (i, 0))
    return _pcall(body, name=name, grid=(S // tr,),
                  in_specs=[pl.BlockSpec((tr, cfg.PW), lambda i: (i, 0)), pl.BlockSpec((1, cfg.QLP), lambda i: (0, 0)),
                            pl.BlockSpec((1, cfg.KVW), lambda i: (0, 0)), tab, tab, tab],
                  out_specs=(pl.BlockSpec((tr, cfg.QLP), lambda i: (i, 0)), pl.BlockSpec((tr, cfg.KVW), lambda i: (i, 0)), tab),
                  out_shape=(jax.ShapeDtypeStruct((S, cfg.QLP), MM_DT), jax.ShapeDtypeStruct((S, cfg.KVW), MM_DT),
                             jax.ShapeDtypeStruct((S, LANES), MM_DT)),
                  compiler_params=_row_params())(proj, gq, gkv, C, SP, SN)


def _mla_prep_bwd(name, cfg, dcq, dckv, dkpe, proj, gq, gkv, tabs, dproj):
    S = cfg.S
    tr = _pick(S, 256, 16)
    (C, SP, SN), w = tabs

    def body(dcq_ref, dckv_ref, dkpe_ref, p_ref, gq_ref, gkv_ref, c_ref, sp_ref, sn_ref, buf_ref, o_ref, dgq_ref, dgkv_ref):
        first = pl.program_id(0) == 0

        def norm_bwd(xv, lo, n, dz, g_ref, dg_ref):
            nrm, r = _masked_rms(xv, lo, n)
            _acc_rows(dg_ref, jnp.sum(dz * nrm, axis=0, keepdims=True), first)
            dn = dz * g_ref[...]
            return r * (dn - nrm * (jnp.sum(dn * nrm, axis=-1, keepdims=True) * (1.0 / n)))

        dxq = norm_bwd(p_ref[:, 0:cfg.QLP], 0, cfg.QL, dcq_ref[...], gq_ref, dgq_ref)
        dxk = norm_bwd(p_ref[:, cfg.KV0:cfg.PW], cfg.KOFF, cfg.KVL, dckv_ref[...], gkv_ref, dgkv_ref)
        dxr = _rope_t(pltpu.roll(dkpe_ref[...], 64, 1), c_ref[...], sp_ref[...], sn_ref[...], w)
        for cidx in range(cfg.PW // LANES):
            lo = cidx * LANES
            parts = []
            if lo < cfg.QLP:
                parts.append(dxq[:, lo:lo + LANES])
            if lo >= cfg.KV0:
                parts.append(dxk[:, lo - cfg.KV0:lo - cfg.KV0 + LANES])
            if lo == cfg.PW - LANES:
                parts.append(dxr)
            o_ref[:, lo:lo + LANES] = functools.reduce(lambda a, b: a + b, parts).astype(o_ref.dtype)

    tab = pl.BlockSpec((tr, LANES), lambda i: (i, 0))
    res = _pcall(body, name=name, grid=(S // tr,),
                 in_specs=[pl.BlockSpec((tr, cfg.QLP), lambda i: (i, 0)), pl.BlockSpec((tr, cfg.KVW), lambda i: (i, 0)), tab,
                           pl.BlockSpec((tr, cfg.PW), lambda i: (i, 0)), pl.BlockSpec((1, cfg.QLP), lambda i: (0, 0)),
                           pl.BlockSpec((1, cfg.KVW), lambda i: (0, 0)), tab, tab, tab, pl.BlockSpec(memory_space=pl.ANY)],
                 out_specs=(pl.BlockSpec((tr, cfg.PW), lambda i: (i, 0)), pl.BlockSpec((8, cfg.QLP), lambda i: (0, 0)),
                            pl.BlockSpec((8, cfg.KVW), lambda i: (0, 0))),
                 out_shape=(jax.ShapeDtypeStruct(dproj.shape, dproj.dtype), jax.ShapeDtypeStruct((8, cfg.QLP), F32),
                            jax.ShapeDtypeStruct((8, cfg.KVW), F32)),
                 input_output_aliases={9: 0}, compiler_params=_row_params())(dcq, dckv, dkpe, proj, gq, gkv, C, SP, SN, dproj)
    return res


def _mla_build_fwd(name, cfg, qa, kva, kpe, tabs):
    S, AH = cfg.S, cfg.AH
    tr = _pick(S, 256, 16)
    (C, SP, SN), w = tabs

    def body(qa_ref, kva_ref, kpe_ref, c_ref, sp_ref, sn_ref, q_ref, k_ref, v_ref):
        for h in range(AH):
            a, b = 256 * h, 256 * h + LANES
            q_ref[:, a:b] = qa_ref[:, a:b].astype(q_ref.dtype)
            q_ref[:, b:b + LANES] = _rope(qa_ref[:, b:b + LANES], c_ref[...], sp_ref[...], sn_ref[...], w).astype(q_ref.dtype)
            k_ref[:, a:b] = kva_ref[:, a:b].astype(k_ref.dtype)
            k_ref[:, b:b + LANES] = kpe_ref[...]
            v_ref[:, LANES * h:LANES * (h + 1)] = kva_ref[:, b:b + LANES].astype(v_ref.dtype)

    tab = pl.BlockSpec((tr, LANES), lambda i: (i, 0))
    wide = pl.BlockSpec((tr, AH * 256), lambda i: (i, 0))
    return _pcall(body, name=name, grid=(S // tr,), in_specs=[wide, wide, tab, tab, tab, tab],
                  out_specs=(wide, wide, pl.BlockSpec((tr, AH * LANES), lambda i: (i, 0))),
                  out_shape=(jax.ShapeDtypeStruct((S, AH * 256), MM_DT), jax.ShapeDtypeStruct((S, AH * 256), MM_DT),
                             jax.ShapeDtypeStruct((S, AH * LANES), MM_DT)),
                  compiler_params=_row_params())(qa, kva, kpe, C, SP, SN)


def _mla_build_bwd(name, cfg, dq, dk, dv, tabs):
    S, AH = cfg.S, cfg.AH
    tr = _pick(S, 256, 16)
    (C, SP, SN), w = tabs

    def body(dq_ref, dk_ref, dv_ref, c_ref, sp_ref, sn_ref, dqa_ref, dkva_ref, dkpe_ref):
        dkpe = None
        for h in range(AH):
            a, b = 256 * h, 256 * h + LANES
            dqa_ref[:, a:b] = dq_ref[:, a:b].astype(dqa_ref.dtype)
            dqa_ref[:, b:b + LANES] = _rope_t(dq_ref[:, b:b + LANES], c_ref[...], sp_ref[...], sn_ref[...], w).astype(dqa_ref.dtype)
            dkva_ref[:, a:b] = dk_ref[:, a:b].astype(dkva_ref.dtype)
            dkva_ref[:, b:b + LANES] = dv_ref[:, LANES * h:LANES * (h + 1)].astype(dkva_ref.dtype)
            part = dk_ref[:, b:b + LANES]
            dkpe = part if dkpe is None else dkpe + part
        dkpe_ref[...] = dkpe

    tab = pl.BlockSpec((tr, LANES), lambda i: (i, 0))
    wide = pl.BlockSpec((tr, AH * 256), lambda i: (i, 0))
    return _pcall(body, name=name, grid=(S // tr,),
                  in_specs=[wide, wide, pl.BlockSpec((tr, AH * LANES), lambda i: (i, 0)), tab, tab, tab],
                  out_specs=(wide, wide, tab),
                  out_shape=(jax.ShapeDtypeStruct((S, AH * 256), MM_DT), jax.ShapeDtypeStruct((S, AH * 256), MM_DT),
                             jax.ShapeDtypeStruct((S, LANES), F32)),
                  compiler_params=_row_params())(dq, dk, dv, C, SP, SN)


def _outnorm_fwd(name, cfg, oa, ob, oc, g):
    S = cfg.S
    tr = _pick(S, 256, 16)
    widths = (cfg.AW, cfg.BW, cfg.CW)

    def body(a_ref, b_ref, c_ref, g_ref, o_ref):
        off = 0
        for ref, wd in zip((a_ref, b_ref, c_ref), widths):
            v = ref[...]
            r = lax.rsqrt(jnp.mean(v * v, axis=-1, keepdims=True) + EPS)
            o_ref[:, off:off + wd] = (v * r * g_ref[:, off:off + wd]).astype(o_ref.dtype)
            off += wd

    return _pcall(body, name=name, grid=(S // tr,),
                  in_specs=[pl.BlockSpec((tr, wd), lambda i: (i, 0)) for wd in widths] + [pl.BlockSpec((1, cfg.MIX), lambda i: (0, 0))],
                  out_specs=pl.BlockSpec((tr, cfg.MIX), lambda i: (i, 0)),
                  out_shape=jax.ShapeDtypeStruct((S, cfg.MIX), MM_DT), compiler_params=_row_params())(oa, ob, oc, g)


def _outnorm_bwd(name, cfg, dmix, oa, ob, oc, g):
    S = cfg.S
    tr = _pick(S, 256, 16)
    widths = (cfg.AW, cfg.BW, cfg.CW)

    def body(dm_ref, a_ref, b_ref, c_ref, g_ref, da_ref, db_ref, dc_ref, dg_ref):
        off = 0
        parts = []
        for ref, dref, wd in zip((a_ref, b_ref, c_ref), (da_ref, db_ref, dc_ref), widths):
            v = ref[...]
            r = lax.rsqrt(jnp.mean(v * v, axis=-1, keepdims=True) + EPS)
            n = v * r
            dm = dm_ref[:, off:off + wd]
            parts.append(jnp.sum(dm * n, axis=0, keepdims=True))
            dn = dm * g_ref[:, off:off + wd]
            dref[...] = r * (dn - n * jnp.mean(dn * n, axis=-1, keepdims=True))
            off += wd
        _acc_rows(dg_ref, jnp.concatenate(parts, axis=1), pl.program_id(0) == 0)

    segs = [pl.BlockSpec((tr, wd), lambda i: (i, 0)) for wd in widths]
    return _pcall(body, name=name, grid=(S // tr,),
                  in_specs=[pl.BlockSpec((tr, cfg.MIX), lambda i: (i, 0))] + segs + [pl.BlockSpec((1, cfg.MIX), lambda i: (0, 0))],
                  out_specs=tuple(segs) + (pl.BlockSpec((8, cfg.MIX), lambda i: (0, 0)),),
                  out_shape=tuple(jax.ShapeDtypeStruct((S, wd), F32) for wd in widths) + (jax.ShapeDtypeStruct((8, cfg.MIX), F32),),
                  compiler_params=_row_params())(dmix, oa, ob, oc, g)


def _band_bias(cfg):
    tq, tk, W = cfg.TB, cfg.TBK, cfg.W
    ns = 2 * W + 1
    shape = ((tk // tq) * ns, tq, tk)
    slab = lax.broadcasted_iota(jnp.int32, shape, 0)
    row = lax.broadcasted_iota(jnp.int32, shape, 1)
    col = lax.broadcasted_iota(jnp.int32, shape, 2)
    d = (slab // ns) * tq + (W - slab % ns) * tk + row - col
    ad = jnp.abs(d)
    m = jnp.zeros(d.shape, F32)
    for reach, dil in cfg.branches:
        ok = ad <= reach
        if dil > 1:
            ok = ok & ((d & (dil - 1)) == 0)
        m = m + ok.astype(F32)
    return jnp.where(m > 0, jnp.log(jnp.maximum(m, 1.0)), NEG)


def _attn_params():
    return pltpu.CompilerParams(dimension_semantics=("parallel", "parallel", "arbitrary"), vmem_limit_bytes=VMEM_LIMIT)


def _scores(q_ref, k_ref, scale, bias_ref):
    s = lax.dot_general(q_ref[...], k_ref[...], NT, preferred_element_type=F32) * scale
    return s if bias_ref is None else s + bias_ref[...]


def _flash_fwd(name, q, k, v, H, G, dk, dv, scale, bias=None, W=None):
    S = q.shape[0]
    band = bias is not None
    tq = bias.shape[1] if band else _pick(S, ATT_TQ, LANES)
    tk = bias.shape[2] if band else _pick(S, ATT_TK_FWD, LANES)
    n = S // tk
    nsteps = 2 * W + 1 if band else n
    R = tk // tq

    def kblock(qi, st):
        return jnp.clip(qi // R - W + st, 0, n - 1) if band else st

    def body(*refs):
        q_ref, k_ref, v_ref = refs[:3]
        bias_ref = refs[3] if band else None
        o_ref, lse_ref, m_sc, l_sc, acc_sc = refs[-5:]
        qi, st = pl.program_id(1), pl.program_id(2)

        @pl.when(st == 0)
        def _():
            m_sc[...] = jnp.full_like(m_sc, NEG)
            l_sc[...] = jnp.zeros_like(l_sc)
            acc_sc[...] = jnp.zeros_like(acc_sc)

        kj = qi // R - W + st if band else st

        def step():
            s = _scores(q_ref, k_ref, scale, bias_ref)
            m_prev = m_sc[...]
            m_new = jnp.maximum(m_prev, jnp.max(s, axis=-1, keepdims=True))
            alpha = jnp.exp(m_prev - m_new)
            p = jnp.exp(s - m_new)
            l_sc[...] = alpha * l_sc[...] + jnp.sum(p, axis=-1, keepdims=True)
            acc_sc[...] = alpha * acc_sc[...] + lax.dot_general(p.astype(MM_DT), v_ref[...], NN, preferred_element_type=F32)
            m_sc[...] = m_new

        if band:
            pl.when((kj >= 0) & (kj < n))(step)
        else:
            step()

        @pl.when(st == nsteps - 1)
        def _():
            l = l_sc[...]
            o_ref[...] = acc_sc[...] / l
            lse_ref[...] = jnp.broadcast_to(m_sc[...] + jnp.log(l), lse_ref.shape)

    in_specs = [pl.BlockSpec((tq, dk), lambda h, qi, st: (qi, h)),
                pl.BlockSpec((tk, dk), lambda h, qi, st: (kblock(qi, st), h // G)),
                pl.BlockSpec((tk, dv), lambda h, qi, st: (kblock(qi, st), h // G))]
    ops = [q, k, v]
    if band:
        in_specs.append(pl.BlockSpec((None, tq, tk), lambda h, qi, st: ((qi % R) * nsteps + st, 0, 0)))
        ops.append(bias)
    return _pcall(body, name=name, grid=(H, S // tq, nsteps), in_specs=in_specs,
                  out_specs=(pl.BlockSpec((tq, dv), lambda h, qi, st: (qi, h)),
                             pl.BlockSpec((None, tq, LANES), lambda h, qi, st: (h, qi, 0))),
                  out_shape=(jax.ShapeDtypeStruct((S, H * dv), F32), jax.ShapeDtypeStruct((H, S, LANES), F32)),
                  scratch_shapes=[pltpu.VMEM((tq, 1), F32), pltpu.VMEM((tq, 1), F32), pltpu.VMEM((tq, dv), F32)],
                  compiler_params=_attn_params())(*ops)


def _flash_dq(name, q, k, v, do, o, lse, H, G, dk, dv, scale, bias=None, W=None):
    S = q.shape[0]
    band = bias is not None
    tq = bias.shape[1] if band else _pick(S, ATT_TQ, LANES)
    tk = bias.shape[2] if band else _pick(S, ATT_TK, LANES)
    n = S // tk
    nsteps = 2 * W + 1 if band else n
    R = tk // tq

    def kblock(qi, st):
        return jnp.clip(qi // R - W + st, 0, n - 1) if band else st

    def body(*refs):
        q_ref, k_ref, v_ref, do_ref, o_ref, lse_ref = refs[:6]
        bias_ref = refs[6] if band else None
        dq_ref, delta_sc, acc_sc = refs[-3:]
        qi, st = pl.program_id(1), pl.program_id(2)

        @pl.when(st == 0)
        def _():
            delta_sc[...] = jnp.sum(do_ref[...] * o_ref[...], axis=-1, keepdims=True)
            acc_sc[...] = jnp.zeros_like(acc_sc)

        kj = qi // R - W + st if band else st

        def step():
            p = jnp.exp(_scores(q_ref, k_ref, scale, bias_ref) - lse_ref[:, 0:1])
            dp = lax.dot_general(do_ref[...].astype(MM_DT), v_ref[...], NT, preferred_element_type=F32)
            ds = p * (dp - delta_sc[...]) * scale
            acc_sc[...] += lax.dot_general(ds.astype(MM_DT), k_ref[...], NN, preferred_element_type=F32)

        if band:
            pl.when((kj >= 0) & (kj < n))(step)
        else:
            step()

        @pl.when(st == nsteps - 1)
        def _():
            dq_ref[...] = acc_sc[...]

    qspec = lambda wd: pl.BlockSpec((tq, wd), lambda h, qi, st: (qi, h))
    in_specs = [qspec(dk),
                pl.BlockSpec((tk, dk), lambda h, qi, st: (kblock(qi, st), h // G)),
                pl.BlockSpec((tk, dv), lambda h, qi, st: (kblock(qi, st), h // G)),
                qspec(dv), qspec(dv),
                pl.BlockSpec((None, tq, LANES), lambda h, qi, st: (h, qi, 0))]
    ops = [q, k, v, do, o, lse]
    if band:
        in_specs.append(pl.BlockSpec((None, tq, tk), lambda h, qi, st: ((qi % R) * nsteps + st, 0, 0)))
        ops.append(bias)
    return _pcall(body, name=name, grid=(H, S // tq, nsteps), in_specs=in_specs,
                  out_specs=qspec(dk), out_shape=jax.ShapeDtypeStruct((S, H * dk), F32),
                  scratch_shapes=[pltpu.VMEM((tq, 1), F32), pltpu.VMEM((tq, dk), F32)],
                  compiler_params=_attn_params())(*ops)


def _flash_dkv(name, q, k, v, do, o, lse, H, G, dk, dv, scale, bias=None, W=None):
    S = q.shape[0]
    band = bias is not None
    tq = bias.shape[1] if band else _pick(S, ATT_TQ, LANES)
    tk = bias.shape[2] if band else _pick(S, ATT_TK, LANES)
    n = S // tq
    R = tk // tq
    nq = R * (2 * W + 1) if band else n
    nsteps = G * nq
    Hkv = H // G

    def qhead(hk, st):
        return hk * G + st // nq

    def qblock(kj, st):
        return jnp.clip(R * (kj - W) + st % nq, 0, n - 1) if band else st % nq

    def body(*refs):
        q_ref, k_ref, v_ref, do_ref, o_ref, lse_ref = refs[:6]
        bias_ref = refs[6] if band else None
        dk_ref, dv_ref, dk_sc, dv_sc = refs[-4:]
        kj, st = pl.program_id(1), pl.program_id(2)

        @pl.when(st == 0)
        def _():
            dk_sc[...] = jnp.zeros_like(dk_sc)
            dv_sc[...] = jnp.zeros_like(dv_sc)

        qi = R * (kj - W) + st % nq if band else st % nq

        def step():
            p = jnp.exp(_scores(q_ref, k_ref, scale, bias_ref) - lse_ref[:, 0:1])
            dof = do_ref[...]
            dob = dof.astype(MM_DT)
            dv_sc[...] += lax.dot_general(p.astype(MM_DT), dob, TN, preferred_element_type=F32)
            dp = lax.dot_general(dob, v_ref[...], NT, preferred_element_type=F32)
            delta = jnp.sum(dof * o_ref[...], axis=-1, keepdims=True)
            ds = p * (dp - delta) * scale
            dk_sc[...] += lax.dot_general(ds.astype(MM_DT), q_ref[...], TN, preferred_element_type=F32)

        if band:
            pl.when((qi >= 0) & (qi < n))(step)
        else:
            step()

        @pl.when(st == nsteps - 1)
        def _():
            dk_ref[...] = dk_sc[...]
            dv_ref[...] = dv_sc[...]

    qspec = lambda wd: pl.BlockSpec((tq, wd), lambda hk, kj, st: (qblock(kj, st), qhead(hk, st)))
    kspec = lambda wd: pl.BlockSpec((tk, wd), lambda hk, kj, st: (kj, hk))
    in_specs = [qspec(dk), kspec(dk), kspec(dv), qspec(dv), qspec(dv),
                pl.BlockSpec((None, tq, LANES), lambda hk, kj, st: (qhead(hk, st), qblock(kj, st), 0))]
    ops = [q, k, v, do, o, lse]
    if band:
        in_specs.append(pl.BlockSpec((None, tq, tk),
                                     lambda hk, kj, st: ((st % nq % R) * (2 * W + 1) + 2 * W - (st % nq) // R, 0, 0)))
        ops.append(bias)
    return _pcall(body, name=name, grid=(Hkv, S // tk, nsteps), in_specs=in_specs,
                  out_specs=(kspec(dk), kspec(dv)),
                  out_shape=(jax.ShapeDtypeStruct((S, Hkv * dk), F32), jax.ShapeDtypeStruct((S, Hkv * dv), F32)),
                  scratch_shapes=[pltpu.VMEM((tk, dk), F32), pltpu.VMEM((tk, dv), F32)],
                  compiler_params=_attn_params())(*ops)


def _rowtile(rows, cols):
    return _pick(rows, max(16, (512 * 1024) // cols // 16 * 16), 16)


def _cast_rows(name, w, dtype):
    R, C = w.shape
    tr = _rowtile(R, C)

    def body(w_ref, o_ref):
        o_ref[...] = w_ref[...].astype(o_ref.dtype)

    spec = pl.BlockSpec((tr, C), lambda i: (i, 0))
    return _pcall(body, name=name, grid=(R // tr,), in_specs=[spec], out_specs=spec,
                  out_shape=jax.ShapeDtypeStruct((R, C), dtype), compiler_params=_row_params())(w)


def _add_sibling(name, gw, ra, c_arr, hd):
    ns, depth, Ks, Ns = gw.shape
    rows = hd * Ks
    tr = _rowtile(rows, Ns)
    gw_v = gw.reshape(ns, 2, rows, Ns)
    ra_v = ra.reshape(ns, rows, Ns)

    def body(c_ref, g_ref, r_ref, o_ref):
        o_ref[...] = (g_ref[...].astype(F32) + r_ref[...].astype(F32)).astype(o_ref.dtype)

    grid_spec = pltpu.PrefetchScalarGridSpec(
        num_scalar_prefetch=1, grid=(ns, rows // tr),
        in_specs=[pl.BlockSpec((None, None, tr, Ns), lambda s, r, c_ref: (s, c_ref[0], r, 0)),
                  pl.BlockSpec((None, tr, Ns), lambda s, r, c_ref: (s, r, 0))],
        out_specs=pl.BlockSpec((None, tr, Ns), lambda s, r, c_ref: (s, r, 0)))
    return _pcall(body, name=name, grid_spec=grid_spec, out_shape=jax.ShapeDtypeStruct((ns, rows, Ns), MM_DT),
                  compiler_params=_grid2_params())(c_arr, gw_v, ra_v)


def _add_chips(name, p, rb, me_arr):
    ns, rows, Ns = p.shape
    tr = _rowtile(rows, Ns)

    def body(me_ref, p_ref, b0_ref, b1_ref, b2_ref, o_ref):
        o_ref[...] = ((p_ref[...].astype(F32) + b0_ref[...].astype(F32)) + b1_ref[...].astype(F32)) + b2_ref[...].astype(F32)

    grid_spec = pltpu.PrefetchScalarGridSpec(
        num_scalar_prefetch=1, grid=(rows // tr,),
        in_specs=[pl.BlockSpec((None, tr, Ns), lambda r, me_ref: (me_ref[0], r, 0))] +
                 [pl.BlockSpec((None, tr, Ns), functools.partial(lambda r, me_ref, j: (j, r, 0), j=j)) for j in range(3)],
        out_specs=pl.BlockSpec((None, tr, Ns), lambda r, me_ref: (0, r, 0)))
    return _pcall(body, name=name, grid_spec=grid_spec, out_shape=jax.ShapeDtypeStruct((2, rows, Ns), F32),
                  compiler_params=_row_params())(me_arr, p, rb, rb, rb)


def _adamw_math(wv, gv, mv, vv):
    bc1 = 1.0 - ADAM_B1 ** ADAM_STEP
    bc2 = 1.0 - ADAM_B2 ** ADAM_STEP
    mn = ADAM_B1 * mv + (1.0 - ADAM_B1) * gv
    vn = ADAM_B2 * vv + (1.0 - ADAM_B2) * jnp.square(gv)
    m_hat = mn / bc1
    v_hat = vn / bc2
    return -ADAM_LR * (m_hat / (jnp.sqrt(v_hat) + ADAM_EPS) + ADAM_WD * wv), mn, vn


def _adamw_halves(name, w, g2, m, v, c_arr):
    depth, _, R, C = w.shape
    tr = _rowtile(R, C)

    def body(c_ref, w_ref, g_ref, m_ref, v_ref, go_ref, d_ref, nm_ref, nv_ref):
        gv = g_ref[...]
        go_ref[...] = gv
        d_ref[...], nm_ref[...], nv_ref[...] = _adamw_math(w_ref[...], gv, m_ref[...], v_ref[...])

    spec = pl.BlockSpec((None, None, tr, C), lambda l, h, r, c_ref: (l, h, r, 0))
    gspec = pl.BlockSpec((None, None, tr, C), lambda l, h, r, c_ref: (l, (h + c_ref[0]) % 2, r, 0))
    sds = jax.ShapeDtypeStruct(w.shape, F32)
    grid_spec = pltpu.PrefetchScalarGridSpec(num_scalar_prefetch=1, grid=(depth, 2, R // tr), in_specs=[spec, gspec, spec, spec],
                                             out_specs=(spec,) * 4)
    return _pcall(body, name=name, grid_spec=grid_spec, out_shape=(sds,) * 4,
                  compiler_params=pltpu.CompilerParams(dimension_semantics=("arbitrary",) * 3, vmem_limit_bytes=VMEM_LIMIT))(c_arr, w, g2, m, v)


def _adamw(name, w, g, m, v):
    R, C = w.shape
    tr = _rowtile(R, C)

    def body(w_ref, g_ref, m_ref, v_ref, d_ref, nm_ref, nv_ref):
        d_ref[...], nm_ref[...], nv_ref[...] = _adamw_math(w_ref[...], g_ref[...], m_ref[...], v_ref[...])

    spec = pl.BlockSpec((tr, C), lambda i: (i, 0))
    sds = jax.ShapeDtypeStruct((R, C), F32)
    return _pcall(body, name=name, grid=(R // tr,), in_specs=[spec] * 4, out_specs=(spec,) * 3,
                  out_shape=(sds, sds, sds), compiler_params=_row_params())(w, g, m, v)


HBM_SPEC = pl.BlockSpec(memory_space=pltpu.HBM)


def _place():
    x, y, c = lax.axis_index("x"), lax.axis_index("y"), lax.axis_index("c")
    chips = [(1 - x, y), (x, 1 - y), (1 - x, 1 - y)]
    return x, y, c, chips


def _allgather_body(ins, outs, send, recv, handshake):
    n = len(ins)
    x, y, c, chips = _place()
    me = 2 * x + y
    sib = (x, y, 1 - c)
    if handshake:
        barrier = pltpu.get_barrier_semaphore()
        for peer in [(chip[0], chip[1], c) for chip in chips] + [sib]:
            pl.semaphore_signal(barrier, inc=1, device_id=peer, device_id_type=MESH)
        pl.semaphore_wait(barrier, 4)

    def rcopy(src, dst, k, to):
        return pltpu.make_async_remote_copy(src_ref=src, dst_ref=dst, send_sem=send.at[k], recv_sem=recv.at[k],
                                            device_id=to, device_id_type=MESH)

    def rows(t, cc):
        hr = ins[t].shape[0] // 2
        return pl.ds(cc * hr, hr)

    sends = []
    for t in range(n):
        for j, chip in enumerate(chips):
            cp = rcopy(ins[t].at[rows(t, c)], outs[t].at[me, rows(t, c)], 7 * t + j, (chip[0], chip[1], c))
            cp.start()
            sends.append(cp)
    for t in range(n):
        cp = rcopy(ins[t], outs[t].at[me], 7 * t + 6, sib)
        cp.start()
        sends.append(cp)
    for t in range(n):
        for j, chip in enumerate(chips):
            slab = outs[t].at[2 * chip[0] + chip[1], rows(t, c)]
            rcopy(slab, slab, 7 * t + j, (chip[0], chip[1], c)).wait_recv()
            fw = rcopy(slab, slab, 7 * t + 3 + j, sib)
            fw.start()
            sends.append(fw)
    for t in range(n):
        rcopy(ins[t], outs[t].at[me], 7 * t + 6, sib).wait_recv()
        for j, chip in enumerate(chips):
            slab = outs[t].at[2 * chip[0] + chip[1], rows(t, 1 - c)]
            rcopy(slab, slab, 7 * t + 3 + j, sib).wait_recv()
    for cp in sends:
        cp.wait_send()


def _allgather_layer(name, shards):
    n = len(shards)

    def body(*refs):
        _allgather_body(refs[:n], refs[n:2 * n], refs[2 * n], refs[2 * n + 1], False)

    return _pcall(body, name=name, in_specs=[HBM_SPEC] * n, out_specs=tuple([HBM_SPEC] * n),
                  out_shape=tuple(jax.ShapeDtypeStruct((N_SHARD,) + s.shape, s.dtype) for s in shards),
                  scratch_shapes=[pltpu.SemaphoreType.DMA((7 * n,)), pltpu.SemaphoreType.DMA((7 * n,))])(*shards)


def _allgather_layer_async(name, shards, collective_id):
    n = len(shards)
    in_refs = [jax.new_ref(s, memory_space=pltpu.MemorySpace.HBM) for s in shards]
    out_refs = [jax.empty_ref(jax.ShapeDtypeStruct((N_SHARD,) + s.shape, s.dtype), memory_space=pltpu.MemorySpace.HBM)
                for s in shards]

    @pl.kernel(mesh=plsc.ScalarSubcoreMesh(axis_name="seq", num_cores=1), name=name,
               scratch_types=(pltpu.SemaphoreType.DMA((7 * n,)), pltpu.SemaphoreType.DMA((7 * n,))),
               compiler_params=pltpu.CompilerParams(collective_id=collective_id))
    def launch(send, recv):
        _allgather_body(in_refs, out_refs, send, recv, True)

    launch()
    return [r[...] for r in out_refs]


def _exchange_sibling_halves(gws, hd):
    n = len(gws)

    def body(*refs):
        ins, outs = refs[:n], refs[n:2 * n]
        send, recv = refs[2 * n:]
        x, y, c, _ = _place()
        cps = []
        for t in range(n):
            cp = pltpu.make_async_remote_copy(src_ref=ins[t].at[:, pl.ds((1 - c) * hd, hd)], dst_ref=outs[t],
                                              send_sem=send.at[t], recv_sem=recv.at[t],
                                              device_id=(x, y, 1 - c), device_id_type=MESH)
            cp.start()
            cps.append(cp)
        for cp in cps:
            cp.wait()

    return _pcall(body, name="rs_sibling_halves", in_specs=[HBM_SPEC] * n, out_specs=tuple([HBM_SPEC] * n),
                  out_shape=tuple(jax.ShapeDtypeStruct((g.shape[0], hd) + g.shape[2:], g.dtype) for g in gws),
                  scratch_shapes=[pltpu.SemaphoreType.DMA((n,)), pltpu.SemaphoreType.DMA((n,))])(*gws)


def _exchange_chips(ps):
    n = len(ps)

    def body(*refs):
        ins, outs = refs[:n], refs[n:2 * n]
        send, recv = refs[2 * n:]
        x, y, c, chips = _place()
        cps = []
        for t in range(n):
            for j, chip in enumerate(chips):
                cp = pltpu.make_async_remote_copy(src_ref=ins[t].at[2 * chip[0] + chip[1]], dst_ref=outs[t].at[j],
                                                  send_sem=send.at[3 * t + j], recv_sem=recv.at[3 * t + j],
                                                  device_id=(chip[0], chip[1], c), device_id_type=MESH)
                cp.start()
                cps.append(cp)
        for cp in cps:
            cp.wait()

    return _pcall(body, name="rs_chip_exchange", in_specs=[HBM_SPEC] * n, out_specs=tuple([HBM_SPEC] * n),
                  out_shape=tuple(jax.ShapeDtypeStruct((3,) + p.shape[1:], p.dtype) for p in ps),
                  scratch_shapes=[pltpu.SemaphoreType.DMA((3 * n,)), pltpu.SemaphoreType.DMA((3 * n,))])(*ps)


def _flip(x, y, c, r):
    return (1 - x if r & 4 else x, 1 - y if r & 2 else y, 1 - c if r & 1 else c)


def _grad_exchange_async(name, gls, collective_id):
    n = len(gls)
    in_refs = [jax.new_ref(g, memory_space=pltpu.MemorySpace.HBM) for g in gls]
    out_refs = [jax.empty_ref(jax.ShapeDtypeStruct((7, g.shape[1] // 2, g.shape[2]), g.dtype), memory_space=pltpu.MemorySpace.HBM)
                for g in gls]

    @pl.kernel(mesh=plsc.ScalarSubcoreMesh(axis_name="seq", num_cores=1), name=name,
               scratch_types=(pltpu.SemaphoreType.DMA((7 * n,)), pltpu.SemaphoreType.DMA((7 * n,))),
               compiler_params=pltpu.CompilerParams(collective_id=collective_id))
    def launch(send, recv):
        _grad_exchange_body(in_refs, out_refs, send, recv, True)

    launch()
    return [r[...] for r in out_refs]


def _grad_exchange_body(ins, outs, send, recv, handshake):
    x, y, c, _ = _place()
    peers = [_flip(x, y, c, r) for r in range(1, 8)]
    if handshake:
        barrier = pltpu.get_barrier_semaphore()
        for peer in peers:
            pl.semaphore_signal(barrier, inc=1, device_id=peer, device_id_type=MESH)
        pl.semaphore_wait(barrier, 7)
    cps = []
    for t in range(len(ins)):
        hr = ins[t].shape[1] // 2
        for j, (px, py, pc) in enumerate(peers):
            cp = pltpu.make_async_remote_copy(src_ref=ins[t].at[2 * px + py, pl.ds(pc * hr, hr)], dst_ref=outs[t].at[j],
                                              send_sem=send.at[7 * t + j], recv_sem=recv.at[7 * t + j],
                                              device_id=(px, py, pc), device_id_type=MESH)
            cp.start()
            cps.append(cp)
    for cp in cps:
        cp.wait()


def _add_eight(name, own, slots, dev_arr, l, buf):
    ns, Ks, Ns = own.shape
    hr = Ks // 2
    tr = _rowtile(hr, Ns)
    fresh = isinstance(buf, tuple)

    def body(pl_ref, own_ref, *refs):
        acc = own_ref[...].astype(F32)
        for s_ref in refs[:7]:
            acc = acc + s_ref[...].astype(F32)
        refs[-1][...] = acc

    in_specs = [pl.BlockSpec((None, tr, Ns), lambda r, pl_ref: (pl_ref[0], r, 0))]
    in_specs += [pl.BlockSpec((None, tr, Ns), functools.partial(lambda r, pl_ref, j: (j, r, 0), j=j)) for j in range(7)]
    ops = [dev_arr, own.reshape(ns * 2, hr, Ns)] + [slots] * 7
    if not fresh:
        in_specs.append(pl.BlockSpec(memory_space=pl.ANY))
        ops.append(buf)
    grid_spec = pltpu.PrefetchScalarGridSpec(num_scalar_prefetch=1, grid=(hr // tr,), in_specs=in_specs,
                                             out_specs=pl.BlockSpec((None, None, tr, Ns), lambda r, pl_ref: (l, 0, r, 0)))
    return _pcall(body, name=name, grid_spec=grid_spec, out_shape=jax.ShapeDtypeStruct(buf if fresh else buf.shape, F32),
                  input_output_aliases={} if fresh else {9: 0}, compiler_params=_row_params())(*ops)


def _share_reduced(gs):
    n = len(gs)

    def body(*refs):
        ins, outs = refs[:n], refs[n:2 * n]
        send, recv = refs[2 * n:]
        x, y, c, _ = _place()
        cps = []
        for t in range(n):
            cp = pltpu.make_async_remote_copy(src_ref=ins[t].at[:, 0], dst_ref=outs[t].at[:, 1], send_sem=send.at[t], recv_sem=recv.at[t],
                                              device_id=(x, y, 1 - c), device_id_type=MESH)
            cp.start()
            cps.append(cp)
        for cp in cps:
            cp.wait()

    return _pcall(body, name="rs_share_reduced", in_specs=[HBM_SPEC] * n, out_specs=tuple([HBM_SPEC] * n),
                  out_shape=tuple(jax.ShapeDtypeStruct(g.shape, g.dtype) for g in gs),
                  input_output_aliases={t: t for t in range(n)},
                  scratch_shapes=[pltpu.SemaphoreType.DMA((n,)), pltpu.SemaphoreType.DMA((n,))])(*gs)


def _allreduce_small(vec):
    R = vec.shape[0]

    def body(v_ref, o_ref, buf, send, recv):
        x, y, c, _ = _place()
        me = 4 * x + 2 * y + c
        buf[me] = v_ref[...]
        cps = []
        for r in range(1, 8):
            fx, fy, fc = (r >> 2) & 1, (r >> 1) & 1, r & 1
            to = (1 - x if fx else x, 1 - y if fy else y, 1 - c if fc else c)
            cp = pltpu.make_async_remote_copy(src_ref=v_ref, dst_ref=buf.at[me], send_sem=send.at[r - 1], recv_sem=recv.at[r - 1],
                                              device_id=to, device_id_type=MESH)
            cp.start()
            cps.append(cp)
        for r in range(1, 8):
            fx, fy, fc = (r >> 2) & 1, (r >> 1) & 1, r & 1
            frm = (1 - x if fx else x, 1 - y if fy else y, 1 - c if fc else c)
            src = 4 * frm[0] + 2 * frm[1] + frm[2]
            pltpu.make_async_remote_copy(src_ref=v_ref, dst_ref=buf.at[src], send_sem=send.at[r - 1], recv_sem=recv.at[r - 1],
                                         device_id=frm, device_id_type=MESH).wait_recv()
        for cp in cps:
            cp.wait_send()
        acc = buf[0]
        for i in range(1, 8):
            acc = acc + buf[i]
        o_ref[...] = acc

    vm = pl.BlockSpec(memory_space=pltpu.VMEM)
    return _pcall(body, name="allreduce_small", in_specs=[vm], out_specs=vm, out_shape=jax.ShapeDtypeStruct((R, LANES), F32),
                  scratch_shapes=[pltpu.VMEM((8, R, LANES), F32), pltpu.SemaphoreType.DMA((7,)), pltpu.SemaphoreType.DMA((7,))])(vec)


def _unshard_cols(wg):
    ns, depth, K, Ns = wg.shape
    return jnp.moveaxis(wg, 0, 2).reshape(depth, K, ns * Ns)


def _shard_cols(w):
    K, N = w.shape
    return jnp.moveaxis(w.reshape(K, N_SHARD, N // N_SHARD), 1, 0)


def _uq_padded(cfg, wuq_g):
    depth = wuq_g.shape[1]
    w = _unshard_cols(wuq_g).reshape(depth, cfg.QL, cfg.AH, 192)
    w = jnp.pad(w, ((0, 0), (0, cfg.QLP - cfg.QL), (0, 0), (0, 64)))
    return w.reshape(1, depth, cfg.QLP, cfg.AH * 256)


def _uq_grad_unpadded(cfg, dw):
    w = dw[:cfg.QL].reshape(cfg.QL, cfg.AH, 256)[:, :, :192].reshape(cfg.QL, cfg.UQ)
    return _shard_cols(w)


def _ukv_padded(cfg, wukv_g):
    w = _unshard_cols(wukv_g)
    w = jnp.pad(w, ((0, 0), (cfg.KOFF, cfg.KVW - cfg.KVL - cfg.KOFF), (0, 0)))
    return w[None]


def _ukv_grad_unpadded(cfg, dw):
    return _shard_cols(dw[cfg.KOFF:cfg.KOFF + cfg.KVL])


def _pad_lanes(v, lo, total):
    return jnp.pad(v, (lo, total - lo - v.shape[0]))[None]


def _layer_fwd(cfg, l, x, W, small, tabs):
    ln1, gq, gkv, gqn, gkn, gout, ln2 = small
    sc_a, sc_h = 1.0 / math.sqrt(192), 1.0 / math.sqrt(128)
    n = f"l{l}_"
    h = _rms_fwd(n + "ln1", x, ln1)
    proj = _mm_nn(n + "proj", h, (W["w_in"], "col", 0))
    cqn, ckvn, kpe = _mla_prep_fwd(n + "mla_prep", cfg, proj, gq, gkv, tabs["akr"])
    qa = _mm_nn(n + "uq", cqn, (W["uq_p"], "col", 0))
    kva = _mm_nn(n + "ukv", ckvn, (W["ukv_p"], "col", 0))
    q_a, k_a, v_a = _mla_build_fwd(n + "mla_build", cfg, qa, kva, kpe, tabs["aq"])
    o_a, lse_a = _flash_fwd(n + "attn_a", q_a, k_a, v_a, cfg.AH, 1, 256, 128, sc_a)
    q_b = _headprep_fwd(n + "bq", proj, cfg.o_bq, cfg.BH, gqn, tabs["b"])
    k_b = _headprep_fwd(n + "bk", proj, cfg.o_bk, cfg.BKV, gkn, tabs["b"])
    v_b = _headprep_fwd(n + "bv", proj, cfg.o_bv, cfg.BKV, None, None)
    o_b, lse_b = _flash_fwd(n + "attn_b", q_b, k_b, v_b, cfg.BH, cfg.G, 128, 128, sc_h)
    q_c = _headprep_fwd(n + "cq", proj, cfg.o_cq, cfg.CH, None, tabs["c"])
    k_c = _headprep_fwd(n + "ck", proj, cfg.o_ck, cfg.CH, None, tabs["c"])
    v_c = _headprep_fwd(n + "cv", proj, cfg.o_cv, cfg.CH, None, None)
    o_c, lse_c = _flash_fwd(n + "attn_c", q_c, k_c, v_c, cfg.CH, 1, 128, 128, sc_h, tabs["bias_c"], cfg.W)
    mixed = _outnorm_fwd(n + "outnorm", cfg, o_a, o_b, o_c, gout)
    x1 = _mm_nn(n + "out", mixed, (W["w_out"], "row", 0), epi=_epi_residual, extra=x)
    h2 = _rms_fwd(n + "ln2", x1, ln2)
    a, u = _mm_nn(n + "ff1", h2, (W["w_ff1"], "col", 0), epi=_epi_relu2, out_dtypes=(MM_DT, MM_DT))
    x2 = _mm_nn(n + "ff2", u, (W["w_ff2"], "row", 0), epi=_epi_residual, extra=x1)
    saved = dict(x=x, h=h, proj=proj, cqn=cqn, ckvn=ckvn, q_a=q_a, k_a=k_a, v_a=v_a, o_a=o_a, lse_a=lse_a,
                 q_b=q_b, k_b=k_b, v_b=v_b, o_b=o_b, lse_b=lse_b, q_c=q_c, k_c=k_c, v_c=v_c, o_c=o_c, lse_c=lse_c,
                 mixed=mixed, x1=x1, h2=h2, a=a, u=u)
    return x2, saved


def _layer_bwd(cfg, l, dx2, sv, W, small, tabs, GW):
    ln1, gq, gkv, gqn, gkn, gout, ln2 = small
    sc_a, sc_h = 1.0 / math.sqrt(192), 1.0 / math.sqrt(128)
    n = f"l{l}_b_"
    S = cfg.S
    mats = {m[0]: m for m in cfg.mats}

    def dw(name, a, g, key):
        _, Rs, Cs, kind = mats[key]
        GW[key] = _mm_tn(n + name, a, g, kind, 0, Rs, Cs, (N_SHARD, 1, Rs, Cs), out_dtype=MM_DT).reshape(N_SHARD, Rs, Cs)

    da = _mm_nt(n + "ff2_dx", dx2, (W["w_ff2"], "row", 0), epi=_epi_drelu2, out_dtype=MM_DT, extra=sv["a"])
    dw("ff2_dw", sv["u"], dx2, "w_ff2")
    dh2 = _mm_nt(n + "ff1_dx", da, (W["w_ff1"], "col", 0))
    dw("ff1_dw", sv["h2"], da, "w_ff1")
    dx1, dln2 = _rms_bwd(n + "ln2", sv["x1"], ln2, dh2, dx2)
    dmix = _mm_nt(n + "out_dx", dx1, (W["w_out"], "row", 0))
    dw("out_dw", sv["mixed"], dx1, "w_out")
    do_a, do_b, do_c, dgout = _outnorm_bwd(n + "outnorm", cfg, dmix, sv["o_a"], sv["o_b"], sv["o_c"], gout)
    dproj = jnp.zeros((S, cfg.IN), MM_DT)
    args_c = (sv["q_c"], sv["k_c"], sv["v_c"], do_c, sv["o_c"], sv["lse_c"], cfg.CH, 1, 128, 128, sc_h, tabs["bias_c"], cfg.W)
    dq_c = _flash_dq(n + "attn_c_dq", *args_c)
    dk_c, dv_c = _flash_dkv(n + "attn_c_dkv", *args_c)
    dproj, _ = _headprep_bwd(n + "cq", dq_c, sv["proj"], cfg.o_cq, cfg.CH, None, tabs["c"], dproj)
    dproj, _ = _headprep_bwd(n + "ck", dk_c, sv["proj"], cfg.o_ck, cfg.CH, None, tabs["c"], dproj)
    dproj, _ = _headprep_bwd(n + "cv", dv_c, sv["proj"], cfg.o_cv, cfg.CH, None, None, dproj)
    args_b = (sv["q_b"], sv["k_b"], sv["v_b"], do_b, sv["o_b"], sv["lse_b"], cfg.BH, cfg.G, 128, 128, sc_h)
    dq_b = _flash_dq(n + "attn_b_dq", *args_b)
    dk_b, dv_b = _flash_dkv(n + "attn_b_dkv", *args_b)
    dproj, dgqn = _headprep_bwd(n + "bq", dq_b, sv["proj"], cfg.o_bq, cfg.BH, gqn, tabs["b"], dproj)
    dproj, dgkn = _headprep_bwd(n + "bk", dk_b, sv["proj"], cfg.o_bk, cfg.BKV, gkn, tabs["b"], dproj)
    dproj, _ = _headprep_bwd(n + "bv", dv_b, sv["proj"], cfg.o_bv, cfg.BKV, None, None, dproj)
    args_a = (sv["q_a"], sv["k_a"], sv["v_a"], do_a, sv["o_a"], sv["lse_a"], cfg.AH, 1, 256, 128, sc_a)
    dq_a = _flash_dq(n + "attn_a_dq", *args_a)
    dk_a, dv_a = _flash_dkv(n + "attn_a_dkv", *args_a)
    dqa, dkva, dkpe = _mla_build_bwd(n + "mla_build", cfg, dq_a, dk_a, dv_a, tabs["aq"])
    dcq = _mm_nt(n + "uq_dx", dqa, (W["uq_p"], "col", 0))
    dwuq = _mm_tn(n + "uq_dw", sv["cqn"], dqa, "col", 0, cfg.QLP, cfg.AH * 256, (1, 1, cfg.QLP, cfg.AH * 256))
    dckv = _mm_nt(n + "ukv_dx", dkva, (W["ukv_p"], "col", 0))
    dwukv = _mm_tn(n + "ukv_dw", sv["ckvn"], dkva, "col", 0, cfg.KVW, cfg.AH * 256, (1, 1, cfg.KVW, cfg.AH * 256))
    dproj, dgq, dgkv = _mla_prep_bwd(n + "mla_prep", cfg, dcq, dckv, dkpe, sv["proj"], gq, gkv, tabs["akr"], dproj)
    dh = _mm_nt(n + "proj_dx", dproj, (W["w_in"], "col", 0))
    dw("proj_dw", sv["h"], dproj, "w_in")
    dx, dln1 = _rms_bwd(n + "ln1", sv["x"], ln1, dh, dx1)
    gains = dict(ln1_g=dln1[0], g_q_a=dgq[0, :cfg.QL], g_kv_a=dgkv[0, cfg.KOFF:cfg.KOFF + cfg.KVL], g_qn_b=dgqn[0],
                 g_kn_b=dgkn[0], g_out=dgout[0], ln2_g=dln2[0])
    GW["w_uq"] = _uq_grad_unpadded(cfg, dwuq[0, 0]).astype(MM_DT)
    GW["w_ukv"] = _ukv_grad_unpadded(cfg, dwukv[0, 0]).astype(MM_DT)
    return dx, gains


SMALL_NAMES = ("ln1_g", "g_q_a", "g_kv_a", "g_qn_b", "g_kn_b", "g_out", "ln2_g")
MAT_NAMES = ("w_in", "w_uq", "w_ukv", "w_out", "w_ff1", "w_ff2")


def _pack_small(cfg, per_layer, final, scalar=None):
    last = jnp.zeros((1,), F32) if scalar is None else scalar.reshape(1)
    flat = jnp.concatenate([per_layer[k].reshape(-1) for k in SMALL_NAMES] + [final.reshape(-1), last])
    total = flat.shape[0]
    rows = _rup(-(-total // LANES), 8)
    return jnp.pad(flat, (0, rows * LANES - total)).reshape(rows, LANES)


def _unpack_small(cfg, packed, shapes):
    flat = packed.reshape(-1)
    out, off = {}, 0
    for k in SMALL_NAMES + ("ln_f_g",):
        n = math.prod(shapes[k])
        out[k] = flat[off:off + n].reshape(shapes[k])
        off += n
    return out, flat[off]


def _step(cfg, w, m, v, x, tgt):
    DEPTH, hd = cfg.DEPTH, cfg.HD
    c = lax.axis_index("c")
    me_chip = 2 * lax.axis_index("x") + lax.axis_index("y")
    c_arr = jnp.reshape(c, (1,)).astype(jnp.int32)
    dev_arr = jnp.reshape(2 * me_chip + c, (1,)).astype(jnp.int32)
    mats = {mt[0]: mt for mt in cfg.mats}

    shards = []
    for name in MAT_NAMES:
        _, Ks, Ns, _ = mats[name]
        shards.append(_cast_rows("cast_" + name, w[name].reshape(DEPTH * Ks, Ns), MM_DT).reshape(DEPTH, Ks, Ns))
    n_first = 3

    def gather(l, after):
        mine = [s[l] for s in shards]
        if l == 0:
            first = _allgather_layer("allgather_l0_first", mine[:n_first])
            rest = list(lax.optimization_barrier((tuple(mine[n_first:]), tuple(first)))[0])
            got = list(first) + list(_allgather_layer_async("allgather_l0_rest", rest, collective_id=3 * DEPTH))
        else:
            mine = list(lax.optimization_barrier((tuple(mine), after))[0])
            got = _allgather_layer_async(f"allgather_l{l}", mine, collective_id=l)
        g = {name: a[:, None] for name, a in zip(MAT_NAMES, got)}
        return dict(w_in=g["w_in"], w_out=g["w_out"], w_ff1=g["w_ff1"], w_ff2=g["w_ff2"],
                    uq_p=_uq_padded(cfg, g["w_uq"]), ukv_p=_ukv_padded(cfg, g["w_ukv"]))

    tabs = _all_tables(cfg)
    tabs["bias_c"] = _band_bias(cfg)

    def small_of(l):
        return (w["ln1_g"][l][None], _pad_lanes(w["g_q_a"][l], 0, cfg.QLP), _pad_lanes(w["g_kv_a"][l], cfg.KOFF, cfg.KVW),
                w["g_qn_b"][l][None], w["g_kn_b"][l][None], w["g_out"][l][None], w["ln2_g"][l][None])

    saved, W_layers = [], []
    xc = x
    for l in range(DEPTH):
        W_layers.append(gather(l, saved[l - 1]["mixed"] if l else None))
        xc, sv = _layer_fwd(cfg, l, xc, W_layers[l], small_of(l), tabs)
        saved.append(sv)
    dx, dlnf, loss_rows = _final_loss("final_loss", xc, w["ln_f_g"][None], tgt)
    loss = loss_rows[0, 0]
    gain_rows, own, slots = [None] * DEPTH, [None] * DEPTH, [None] * DEPTH
    for l in reversed(range(DEPTH)):
        GW = {}
        dx, gain_rows[l] = _layer_bwd(cfg, l, dx, saved[l], W_layers[l], small_of(l), tabs, GW)
        own[l] = [GW[name] for name in MAT_NAMES]
        early = _grad_exchange_async(f"grad_exchange_l{l}_early", own[l][n_first:], collective_id=DEPTH + 2 * l)
        late = _grad_exchange_async(f"grad_exchange_l{l}_late", own[l][:n_first], collective_id=DEPTH + 2 * l + 1)
        slots[l] = late + early

    reduced = []
    for t, name in enumerate(MAT_NAMES):
        _, Ks, Ns, _ = mats[name]
        buf = (DEPTH, 2, Ks // 2, Ns)
        for l in reversed(range(DEPTH)):
            buf = _add_eight(f"rs_add_l{l}_" + name, own[l][t], slots[l][t], dev_arr, l, buf)
        reduced.append(buf)
    full = _share_reduced(reduced)
    grad, delta, new_m, new_v = {}, {}, {}, {}
    for name, g2 in zip(MAT_NAMES, full):
        _, Ks, Ns, _ = mats[name]
        halves, shp = (DEPTH, 2, Ks // 2, Ns), (DEPTH, Ks, Ns)
        res = _adamw_halves("adamw_" + name, w[name].reshape(halves), g2, m[name].reshape(halves), v[name].reshape(halves), c_arr)
        grad[name], delta[name], new_m[name], new_v[name] = (r.reshape(shp) for r in res)

    per_layer = {k: jnp.stack([gain_rows[l][k] for l in range(DEPTH)]) for k in SMALL_NAMES}
    shapes = {k: w[k].shape for k in SMALL_NAMES + ("ln_f_g",)}
    gsum = _allreduce_small(_pack_small(cfg, per_layer, dlnf[0], loss))
    pk = lambda d: _pack_small(cfg, {k: d[k] for k in SMALL_NAMES}, d["ln_f_g"])
    d_s, m_s, v_s = _adamw("adamw_small", pk(w), gsum, pk(m), pk(v))
    for res, packed in ((grad, gsum), (delta, d_s), (new_m, m_s), (new_v, v_s)):
        res.update(_unpack_small(cfg, packed, shapes)[0])
    loss_total = _unpack_small(cfg, gsum, shapes)[1]
    return loss_total, dx, grad, delta, new_m, new_v


WEIGHT_NAMES = ("ln1_g", "w_in", "g_q_a", "w_uq", "g_kv_a", "w_ukv", "g_qn_b", "g_kn_b", "g_out", "w_out", "ln2_g",
                "w_ff1", "w_ff2", "ln_f_g")


def _run(cfg, args):
    nw = len(WEIGHT_NAMES)
    x, tgt = args[0], args[1 + nw]
    w = dict(zip(WEIGHT_NAMES, args[1:1 + nw]))
    m = dict(zip(WEIGHT_NAMES, args[2 + nw:2 + 2 * nw]))
    v = dict(zip(WEIGHT_NAMES, args[2 + 2 * nw:2 + 3 * nw]))
    loss, dx, grad, delta, new_m, new_v = _step(cfg, w, m, v, x.reshape(cfg.S, cfg.D), tgt.reshape(cfg.S, cfg.D))
    return (loss, dx.reshape(x.shape), *[grad[k] for k in WEIGHT_NAMES], *[delta[k] for k in WEIGHT_NAMES],
            *[new_m[k] for k in WEIGHT_NAMES], *[new_v[k] for k in WEIGHT_NAMES])


def kernel(x, ln1_g, w_in, g_q_a, w_uq, g_kv_a, w_ukv, g_qn_b, g_kn_b, g_out, w_out, ln2_g, w_ff1, w_ff2, ln_f_g, loss_target, m_ln1_g, m_w_in, m_g_q_a, m_w_uq, m_g_kv_a, m_w_ukv, m_g_qn_b, m_g_kn_b, m_g_out, m_w_out, m_ln2_g, m_w_ff1, m_w_ff2, m_ln_f_g, v_ln1_g, v_w_in, v_g_q_a, v_w_uq, v_g_kv_a, v_w_ukv, v_g_qn_b, v_g_kn_b, v_g_out, v_w_out, v_ln2_g, v_w_ff1, v_w_ff2, v_ln_f_g):
    return _run(Cfg(), (x, ln1_g, w_in, g_q_a, w_uq, g_kv_a, w_ukv, g_qn_b, g_kn_b, g_out, w_out, ln2_g, w_ff1, w_ff2, ln_f_g, loss_target, m_ln1_g, m_w_in, m_g_q_a, m_w_uq, m_g_kv_a, m_w_ukv, m_g_qn_b, m_g_kn_b, m_g_out, m_w_out, m_ln2_g, m_w_ff1, m_w_ff2, m_ln_f_g, v_ln1_g, v_w_in, v_g_q_a, v_w_uq, v_g_kv_a, v_w_ukv, v_g_qn_b, v_g_kn_b, v_g_out, v_w_out, v_ln2_g, v_w_ff1, v_w_ff2, v_ln_f_g))
```

```python
import functools
import math

import jax
import jax.numpy as jnp
from jax import lax
from jax.experimental import pallas as pl
from jax.experimental.pallas import tpu as pltpu
from jax.experimental.pallas import tpu_sc as plsc

F32 = jnp.float32
MM_DT = jnp.bfloat16
LANES = 128
SUBLANES_F32 = 8
SUBLANES_BF16 = 16
VMEM_LIMIT = 48 * 1024 * 1024
EPS = 1e-6
NEG = -1e30
ROPE_THETA = 10000.0
MM_TK = 1024
ATT_TQ, ATT_TK_FWD, ATT_TK, ATT_TK_DKV = 512, 4096, 2048, 4096
ADAM_LR, ADAM_B1, ADAM_B2, ADAM_EPS, ADAM_WD, ADAM_STEP = 0.001, 0.9, 0.999, 1e-08, 0.01, 10
N_SHARD = 4
MESH = pl.DeviceIdType.MESH

NN = (((1,), (0,)), ((), ()))
NT = (((1,), (1,)), ((), ()))
TN = (((0,), (0,)), ((), ()))


def _pcall(body, **kw):
    return pl.pallas_call(body, **kw)


def _rup(n, m):
    return -(-n // m) * m


def _pick(n, pref, mult):
    best = None
    for t in range(mult, min(n, pref) + 1, mult):
        if n % t == 0:
            best = t
    return best if best is not None else n


class Cfg:
    def __init__(self, S=4096, D=2048, DEPTH=4, AH=4, QL=448, KVL=512, BH=6, BKV=2, CH=6,
                 BRANCHES=((128, 1), (512, 4), (2048, 16)), DFF=8192, GRID_W=64, TB=512, TBK=1024):
        self.S, self.D, self.DEPTH, self.AH, self.QL, self.KVL = S, D, DEPTH, AH, QL, KVL
        self.BH, self.BKV, self.CH, self.DFF, self.GRID_W, self.TB = BH, BKV, CH, DFF, GRID_W, TB
        self.G = BH // BKV
        self.AW, self.BW, self.CW = AH * 128, BH * 128, CH * 128
        self.MIX = self.AW + self.BW + self.CW
        self.QLP = _rup(QL, LANES)
        self.KV0 = (QL // LANES) * LANES
        self.PW = QL + KVL + 64
        assert self.PW % LANES == 0
        self.KVW = self.PW - self.KV0
        self.KOFF = QL - self.KV0
        self.o_bq = self.PW
        self.o_bk = self.o_bq + self.BW
        self.o_bv = self.o_bk + BKV * 128
        self.o_cq = self.o_bv + BKV * 128
        self.o_ck = self.o_cq + self.CW
        self.o_cv = self.o_ck + self.CW
        self.IN = self.o_cv + self.CW
        self.UQ, self.UKV = AH * 192, AH * 256
        self.branches = tuple(((w // (2 * d)) * d, d) for w, d in BRANCHES)
        for _, d in self.branches:
            assert d & (d - 1) == 0
        self.TBK = TBK
        self.W = -(-max(r for r, _ in self.branches) // TBK)
        assert S % TBK == 0 and TBK % TB == 0 and DEPTH % 2 == 0
        self.HD = DEPTH // 2
        self.mats = (("w_in", D, self.IN // 4, "col"), ("w_uq", QL, self.UQ // 4, "col"),
                     ("w_ukv", KVL, self.UKV // 4, "col"), ("w_out", self.MIX // 4, D, "row"),
                     ("w_ff1", D, DFF // 4, "col"), ("w_ff2", DFF // 4, D, "row"))


def _mm_call(name, mode, operands, in_specs, out_shape, out_specs, grid, acc_shape, epi, n_extra, aliases=None):
    nk = grid[2]

    def body(*refs):
        a_ref, b_ref = refs[0], refs[1]
        ex = refs[2:2 + n_extra]
        outs = refs[2 + n_extra:-1]
        acc = refs[-1]
        k = pl.program_id(2)

        @pl.when(k == 0)
        def _():
            acc[...] = jnp.zeros_like(acc)

        acc[...] += lax.dot_general(a_ref[...].astype(MM_DT), b_ref[...].astype(MM_DT), mode,
                                    preferred_element_type=F32)

        @pl.when(k == nk - 1)
        def _():
            epi(acc[...], ex, outs)

    return _pcall(body, name=name, grid=grid, in_specs=in_specs, out_specs=out_specs, out_shape=out_shape,
                  scratch_shapes=[pltpu.VMEM(acc_shape, F32)], input_output_aliases=aliases or {},
                  compiler_params=pltpu.CompilerParams(dimension_semantics=("parallel", "parallel", "arbitrary"),
                                                       vmem_limit_bytes=VMEM_LIMIT))(*operands)


def _wspec(kind, l, Rs, Cs, br, bc, rfn, cfn):
    assert Rs % br == 0 and Cs % bc == 0
    if kind == "col":
        npc = Cs // bc
        return pl.BlockSpec((None, None, br, bc), lambda i, j, k: (cfn(i, j, k) // npc, l, rfn(i, j, k), cfn(i, j, k) % npc))
    npr = Rs // br
    return pl.BlockSpec((None, None, br, bc), lambda i, j, k: (rfn(i, j, k) // npr, l, rfn(i, j, k) % npr, cfn(i, j, k)))


def _epi_plain(acc, ex, outs):
    outs[0][...] = acc.astype(outs[0].dtype)


def _epi_residual(acc, ex, outs):
    outs[0][...] = ex[0][...] + acc


def _epi_relu2(acc, ex, outs):
    outs[0][...] = acc.astype(outs[0].dtype)
    r = jnp.maximum(acc, 0.0)
    outs[1][...] = (r * r).astype(outs[1].dtype)


def _epi_drelu2(acc, ex, outs):
    a = ex[0][...].astype(F32)
    outs[0][...] = (acc * (2.0 * jnp.maximum(a, 0.0))).astype(outs[0].dtype)


def _wdims(wd):
    Wg, kind, l = wd
    ns, _, Rs, Cs = Wg.shape
    K = Rs * ns if kind == "row" else Rs
    N = Cs * ns if kind == "col" else Cs
    return Wg, kind, l, Rs, Cs, K, N


def _mm_nn(name, a, wd, epi=_epi_plain, out_dtypes=(F32,), extra=None):
    Wg, kind, l, Rs, Cs, K, N = _wdims(wd)
    M = a.shape[0]
    tm, tk, tn = _pick(M, 1024, 16), _pick(Rs, MM_TK, LANES), _pick(Cs, 1152, LANES)
    grid = (M // tm, N // tn, K // tk)
    in_specs = [pl.BlockSpec((tm, tk), lambda i, j, k: (i, k)),
                _wspec(kind, l, Rs, Cs, tk, tn, lambda i, j, k: k, lambda i, j, k: j)]
    ops = [a, Wg]
    if extra is not None:
        in_specs.append(pl.BlockSpec((tm, tn), lambda i, j, k: (i, j)))
        ops.append(extra)
    o_spec = pl.BlockSpec((tm, tn), lambda i, j, k: (i, j))
    outs = tuple(jax.ShapeDtypeStruct((M, N), dt) for dt in out_dtypes)
    res = _mm_call(name, NN, ops, in_specs, outs, tuple(o_spec for _ in outs), grid, (tm, tn), epi,
                   0 if extra is None else 1)
    return res[0] if len(res) == 1 else res


def _mm_nt(name, g, wd, epi=_epi_plain, out_dtype=F32, extra=None):
    Wg, kind, l, Rs, Cs, K, N = _wdims(wd)
    M = g.shape[0]
    tm, tn, tk = _pick(M, 1024, 16), _pick(Rs, 1024, LANES), _pick(Cs, 1152, LANES)
    grid = (M // tm, K // tn, N // tk)
    in_specs = [pl.BlockSpec((tm, tk), lambda i, j, k: (i, k)),
                _wspec(kind, l, Rs, Cs, tn, tk, lambda i, j, k: j, lambda i, j, k: k)]
    ops = [g, Wg]
    if extra is not None:
        in_specs.append(pl.BlockSpec((tm, tn), lambda i, j, k: (i, j)))
        ops.append(extra)
    res = _mm_call(name, NT, ops, in_specs, (jax.ShapeDtypeStruct((M, K), out_dtype),),
                   (pl.BlockSpec((tm, tn), lambda i, j, k: (i, j)),), grid, (tm, tn), epi, 0 if extra is None else 1)
    return res[0]


def _mm_tn(name, a, g, kind, l, Rs, Cs, buf, out_dtype=F32):
    M, K = a.shape
    N = g.shape[1]
    tm, tn, tk = _pick(Rs, 1024, LANES), _pick(Cs, 1152, LANES), _pick(M, MM_TK, LANES)
    grid = (K // tm, N // tn, M // tk)
    in_specs = [pl.BlockSpec((tk, tm), lambda i, j, k: (k, i)),
                pl.BlockSpec((tk, tn), lambda i, j, k: (k, j))]
    o_spec = _wspec(kind, l, Rs, Cs, tm, tn, lambda i, j, k: i, lambda i, j, k: j)
    if isinstance(buf, tuple):
        res = _mm_call(name, TN, [a, g], in_specs, (jax.ShapeDtypeStruct(buf, out_dtype),), (o_spec,), grid, (tm, tn), _epi_plain, 0)
    else:
        res = _mm_call(name, TN, [a, g, buf], in_specs + [pl.BlockSpec(memory_space=pl.ANY)],
                       (jax.ShapeDtypeStruct(buf.shape, buf.dtype),), (o_spec,), grid, (tm, tn), _epi_plain, 1, aliases={2: 0})
    return res[0]


def _row_params():
    return pltpu.CompilerParams(dimension_semantics=("arbitrary",), vmem_limit_bytes=VMEM_LIMIT)


def _rms_fwd(name, x, g):
    S, D = x.shape
    tr = _pick(S, 256, 16)

    def body(x_ref, g_ref, o_ref):
        xv = x_ref[...]
        r = lax.rsqrt(jnp.mean(xv * xv, axis=-1, keepdims=True) + EPS)
        o_ref[...] = (xv * r * g_ref[...]).astype(o_ref.dtype)

    return _pcall(body, name=name, grid=(S // tr,),
                  in_specs=[pl.BlockSpec((tr, D), lambda i: (i, 0)), pl.BlockSpec((1, D), lambda i: (0, 0))],
                  out_specs=pl.BlockSpec((tr, D), lambda i: (i, 0)), out_shape=jax.ShapeDtypeStruct((S, D), MM_DT),
                  compiler_params=_row_params())(x, g)


def _acc_rows(ref, part, first):
    @pl.when(first)
    def _():
        ref[...] = jnp.zeros_like(ref)

    ref[...] += jnp.broadcast_to(part, ref.shape)


def _rms_bwd(name, x, g, dy, res):
    S, D = x.shape
    tr = _pick(S, 256, 16)

    def body(x_ref, g_ref, dy_ref, res_ref, dx_ref, dg_ref):
        xv = x_ref[...]
        r = lax.rsqrt(jnp.mean(xv * xv, axis=-1, keepdims=True) + EPS)
        xh = xv * r
        dyv = dy_ref[...]
        dn = dyv * g_ref[...]
        dx_ref[...] = res_ref[...] + r * (dn - xh * jnp.mean(dn * xh, axis=-1, keepdims=True))
        _acc_rows(dg_ref, jnp.sum(dyv * xh, axis=0, keepdims=True), pl.program_id(0) == 0)

    row = pl.BlockSpec((tr, D), lambda i: (i, 0))
    return _pcall(body, name=name, grid=(S // tr,),
                  in_specs=[row, pl.BlockSpec((1, D), lambda i: (0, 0)), row, row],
                  out_specs=(row, pl.BlockSpec((8, D), lambda i: (0, 0))),
                  out_shape=(jax.ShapeDtypeStruct((S, D), F32), jax.ShapeDtypeStruct((8, D), F32)),
                  compiler_params=_row_params())(x, g, dy, res)


def _final_loss(name, x, g, tgt):
    S, D = x.shape
    tr = _pick(S, 256, 16)

    def body(x_ref, g_ref, t_ref, dx_ref, dg_ref, loss_ref):
        xv = x_ref[...]
        r = lax.rsqrt(jnp.mean(xv * xv, axis=-1, keepdims=True) + EPS)
        xh = xv * r
        gv = g_ref[...]
        e = xh * gv - t_ref[...]
        part = 0.5 * jnp.sum(jnp.mean(e * e, axis=-1, keepdims=True), axis=0, keepdims=True)
        dy = e * (1.0 / D)
        dn = dy * gv
        dx_ref[...] = r * (dn - xh * jnp.mean(dn * xh, axis=-1, keepdims=True))
        first = pl.program_id(0) == 0
        _acc_rows(dg_ref, jnp.sum(dy * xh, axis=0, keepdims=True), first)
        _acc_rows(loss_ref, part, first)

    row = pl.BlockSpec((tr, D), lambda i: (i, 0))
    return _pcall(body, name=name, grid=(S // tr,),
                  in_specs=[row, pl.BlockSpec((1, D), lambda i: (0, 0)), row],
                  out_specs=(row, pl.BlockSpec((8, D), lambda i: (0, 0)), pl.BlockSpec((8, LANES), lambda i: (0, 0))),
                  out_shape=(jax.ShapeDtypeStruct((S, D), F32), jax.ShapeDtypeStruct((8, D), F32),
                             jax.ShapeDtypeStruct((8, LANES), F32)),
                  compiler_params=_row_params())(x, g, tgt)


def _rope_tables(cos, sin, off, w):
    S = cos.shape[0]
    h = w // 2
    z = lambda n: jnp.zeros((S, n), F32)
    C = jnp.concatenate([z(off), cos, cos, z(LANES - off - w)], axis=1)
    SP = jnp.concatenate([z(off + h), sin, z(LANES - off - w)], axis=1)
    SN = jnp.concatenate([z(off), -sin, z(LANES - off - h)], axis=1)
    return C, SP, SN


def _angles(pos, dim):
    inv = jnp.power(ROPE_THETA, -jnp.arange(0, dim, 2, dtype=F32) / dim)
    ang = pos.astype(F32)[:, None] * inv[None, :]
    return jnp.cos(ang), jnp.sin(ang)


def _all_tables(cfg):
    S = cfg.S
    pos = jnp.arange(S, dtype=F32)
    rows = S // cfg.GRID_W
    row = jnp.repeat(jnp.arange(rows, dtype=F32), cfg.GRID_W)
    col = jnp.tile(jnp.arange(cfg.GRID_W, dtype=F32), rows)
    ca, sa = _angles(pos, 64)
    cc, sc = _angles(pos, 128)
    cr, sr = _angles(row, 64)
    cl, sl = _angles(col, 64)
    t_b = tuple(a + b for a, b in zip(_rope_tables(cr, sr, 0, 64), _rope_tables(cl, sl, 64, 64)))
    return {"aq": (_rope_tables(ca, sa, 0, 64), 64), "akr": (_rope_tables(ca, sa, 64, 64), 64),
            "b": (t_b, 64), "c": (_rope_tables(cc, sc, 0, 128), 128)}


def _rope(x, C, SP, SN, w):
    h = w // 2
    if 2 * h == LANES:
        return x * C + pltpu.roll(x, h, 1) * (SP + SN)
    return x * C + pltpu.roll(x, h, 1) * SP + pltpu.roll(x, LANES - h, 1) * SN


def _rope_t(dy, C, SP, SN, w):
    h = w // 2
    if 2 * h == LANES:
        return dy * C + pltpu.roll(dy * (SP + SN), h, 1)
    return dy * C + pltpu.roll(dy * SP, LANES - h, 1) + pltpu.roll(dy * SN, h, 1)


def _grid2_params():
    return pltpu.CompilerParams(dimension_semantics=("arbitrary", "arbitrary"), vmem_limit_bytes=VMEM_LIMIT)


def _headprep_fwd(name, proj, col_off, nb, gain, tabs):
    S = proj.shape[0]
    tr = _pick(S, 1024, 16)
    cb = col_off // LANES
    norm, rope = gain is not None, tabs is not None
    w = tabs[1] if rope else 0

    def body(*refs):
        x_ref = refs[0]
        pos = 1
        xv = x_ref[...]
        if norm:
            r = lax.rsqrt(jnp.mean(xv * xv, axis=-1, keepdims=True) + EPS)
            xv = xv * r * refs[pos][...]
            pos += 1
        if rope:
            xv = _rope(xv, refs[pos][...], refs[pos + 1][...], refs[pos + 2][...], w)
            pos += 3
        refs[pos][...] = xv.astype(refs[pos].dtype)

    ops, in_specs = [proj], [pl.BlockSpec((tr, LANES), lambda i, j: (i, cb + j))]
    if norm:
        ops.append(gain)
        in_specs.append(pl.BlockSpec((1, LANES), lambda i, j: (0, 0)))
    if rope:
        ops += list(tabs[0])
        in_specs += [pl.BlockSpec((tr, LANES), lambda i, j: (i, 0))] * 3
    return _pcall(body, name=name, grid=(S // tr, nb), in_specs=in_specs,
                  out_specs=pl.BlockSpec((tr, LANES), lambda i, j: (i, j)),
                  out_shape=jax.ShapeDtypeStruct((S, nb * LANES), MM_DT), compiler_params=_grid2_params())(*ops)


def _headprep_bwd(name, dy, proj, col_off, nb, gain, tabs, dproj):
    S = proj.shape[0]
    tr = _pick(S, 1024, 16)
    cb = col_off // LANES
    norm, rope = gain is not None, tabs is not None
    w = tabs[1] if rope else 0

    def body(*refs):
        dz = refs[0][...]
        pos = 1
        if norm:
            x_ref, g_ref = refs[pos], refs[pos + 1]
            pos += 2
        if rope:
            dz = _rope_t(dz, refs[pos][...], refs[pos + 1][...], refs[pos + 2][...], w)
            pos += 3
        pos += 1
        o_ref = refs[pos]
        if norm:
            dg_ref = refs[pos + 1]
            xv = x_ref[...]
            r = lax.rsqrt(jnp.mean(xv * xv, axis=-1, keepdims=True) + EPS)
            n = xv * r
            first = (pl.program_id(0) == 0) & (pl.program_id(1) == 0)
            _acc_rows(dg_ref, jnp.sum(dz * n, axis=0, keepdims=True), first)
            dn = dz * g_ref[...]
            dz = r * (dn - n * jnp.mean(dn * n, axis=-1, keepdims=True))
        o_ref[...] = dz.astype(o_ref.dtype)

    ops, in_specs = [dy], [pl.BlockSpec((tr, LANES), lambda i, j: (i, j))]
    if norm:
        ops += [proj, gain]
        in_specs += [pl.BlockSpec((tr, LANES), lambda i, j: (i, cb + j)), pl.BlockSpec((1, LANES), lambda i, j: (0, 0))]
    if rope:
        ops += list(tabs[0])
        in_specs += [pl.BlockSpec((tr, LANES), lambda i, j: (i, 0))] * 3
    alias_idx = len(ops)
    ops.append(dproj)
    in_specs.append(pl.BlockSpec(memory_space=pl.ANY))
    out_specs = [pl.BlockSpec((tr, LANES), lambda i, j: (i, cb + j))]
    out_shape = [jax.ShapeDtypeStruct(dproj.shape, dproj.dtype)]
    if norm:
        out_specs.append(pl.BlockSpec((8, LANES), lambda i, j: (0, 0)))
        out_shape.append(jax.ShapeDtypeStruct((8, LANES), F32))
    res = _pcall(body, name=name, grid=(S // tr, nb), in_specs=in_specs, out_specs=tuple(out_specs),
                 out_shape=tuple(out_shape), input_output_aliases={alias_idx: 0}, compiler_params=_grid2_params())(*ops)
    return (res[0], res[1]) if norm else (res[0], None)


def _masked_rms(xv, lo, n):
    lane = lax.broadcasted_iota(jnp.int32, xv.shape, 1)
    xm = jnp.where((lane >= lo) & (lane < lo + n), xv, 0.0)
    r = lax.rsqrt(jnp.sum(xm * xm, axis=-1, keepdims=True) * (1.0 / n) + EPS)
    return xm * r, r


def _mla_prep_fwd(name, cfg, proj, gq, gkv, tabs):
    S = cfg.S
    tr = _pick(S, 256, 16)
    (C, SP, SN), w = tabs

    def body(p_ref, gq_ref, gkv_ref, c_ref, sp_ref, sn_ref, cq_ref, ckv_ref, kpe_ref):
        nq, _ = _masked_rms(p_ref[:, 0:cfg.QLP], 0, cfg.QL)
        cq_ref[...] = (nq * gq_ref[...]).astype(cq_ref.dtype)
        nk, _ = _masked_rms(p_ref[:, cfg.KV0:cfg.PW], cfg.KOFF, cfg.KVL)
        ckv_ref[...] = (nk * gkv_ref[...]).astype(ckv_ref.dtype)
        kr = _rope(p_ref[:, cfg.PW - LANES:cfg.PW], c_ref[...], sp_ref[...], sn_ref[...], w)
        kpe_ref[...] = pltpu.roll(kr, 64, 1).astype(kpe_ref.dtype)

    tab = pl.BlockSpec((tr, LANES), lambda i: (i, 0))
    return _pcall(body, name=name, grid=(S // tr,),
                  in_specs=[pl.BlockSpec((tr, cfg.PW), lambda i: (i, 0)), pl.BlockSpec((1, cfg.QLP), lambda i: (0, 0)),
                            pl.BlockSpec((1, cfg.KVW), lambda i: (0, 0)), tab, tab, tab],
                  out_specs=(pl.BlockSpec((tr, cfg.QLP), lambda i: (i, 0)), pl.BlockSpec((tr, cfg.KVW), lambda i: (i, 0)), tab),
                  out_shape=(jax.ShapeDtypeStruct((S, cfg.QLP), MM_DT), jax.ShapeDtypeStruct((S, cfg.KVW), MM_DT),
                             jax.ShapeDtypeStruct((S, LANES), MM_DT)),
                  compiler_params=_row_params())(proj, gq, gkv, C, SP, SN)


def _mla_prep_bwd(name, cfg, dcq, dckv, dkpe, proj, gq, gkv, tabs, dproj):
    S = cfg.S
    tr = _pick(S, 256, 16)
    (C, SP, SN), w = tabs

    def body(dcq_ref, dckv_ref, dkpe_ref, p_ref, gq_ref, gkv_ref, c_ref, sp_ref, sn_ref, buf_ref, o_ref, dgq_ref, dgkv_ref):
        first = pl.program_id(0) == 0

        def norm_bwd(xv, lo, n, dz, g_ref, dg_ref):
            nrm, r = _masked_rms(xv, lo, n)
            _acc_rows(dg_ref, jnp.sum(dz * nrm, axis=0, keepdims=True), first)
            dn = dz * g_ref[...]
            return r * (dn - nrm * (jnp.sum(dn * nrm, axis=-1, keepdims=True) * (1.0 / n)))

        dxq = norm_bwd(p_ref[:, 0:cfg.QLP], 0, cfg.QL, dcq_ref[...], gq_ref, dgq_ref)
        dxk = norm_bwd(p_ref[:, cfg.KV0:cfg.PW], cfg.KOFF, cfg.KVL, dckv_ref[...], gkv_ref, dgkv_ref)
        dxr = _rope_t(pltpu.roll(dkpe_ref[...], 64, 1), c_ref[...], sp_ref[...], sn_ref[...], w)
        for cidx in range(cfg.PW // LANES):
            lo = cidx * LANES
            parts = []
            if lo < cfg.QLP:
                parts.append(dxq[:, lo:lo + LANES])
            if lo >= cfg.KV0:
                parts.append(dxk[:, lo - cfg.KV0:lo - cfg.KV0 + LANES])
            if lo == cfg.PW - LANES:
                parts.append(dxr)
            o_ref[:, lo:lo + LANES] = functools.reduce(lambda a, b: a + b, parts).astype(o_ref.dtype)

    tab = pl.BlockSpec((tr, LANES), lambda i: (i, 0))
    res = _pcall(body, name=name, grid=(S // tr,),
                 in_specs=[pl.BlockSpec((tr, cfg.QLP), lambda i: (i, 0)), pl.BlockSpec((tr, cfg.KVW), lambda i: (i, 0)), tab,
                           pl.BlockSpec((tr, cfg.PW), lambda i: (i, 0)), pl.BlockSpec((1, cfg.QLP), lambda i: (0, 0)),
                           pl.BlockSpec((1, cfg.KVW), lambda i: (0, 0)), tab, tab, tab, pl.BlockSpec(memory_space=pl.ANY)],
                 out_specs=(pl.BlockSpec((tr, cfg.PW), lambda i: (i, 0)), pl.BlockSpec((8, cfg.QLP), lambda i: (0, 0)),
                            pl.BlockSpec((8, cfg.KVW), lambda i: (0, 0))),
                 out_shape=(jax.ShapeDtypeStruct(dproj.shape, dproj.dtype), jax.ShapeDtypeStruct((8, cfg.QLP), F32),
                            jax.ShapeDtypeStruct((8, cfg.KVW), F32)),
                 input_output_aliases={9: 0}, compiler_params=_row_params())(dcq, dckv, dkpe, proj, gq, gkv, C, SP, SN, dproj)
    return res


def _mla_build_fwd(name, cfg, qa, kva, kpe, tabs):
    S, AH = cfg.S, cfg.AH
    tr = _pick(S, 256, 16)
    (C, SP, SN), w = tabs

    def body(qa_ref, kva_ref, kpe_ref, c_ref, sp_ref, sn_ref, q_ref, k_ref, v_ref):
        for h in range(AH):
            a, b = 256 * h, 256 * h + LANES
            q_ref[:, a:b] = qa_ref[:, a:b].astype(q_ref.dtype)
            q_ref[:, b:b + LANES] = _rope(qa_ref[:, b:b + LANES], c_ref[...], sp_ref[...], sn_ref[...], w).astype(q_ref.dtype)
            k_ref[:, a:b] = kva_ref[:, a:b].astype(k_ref.dtype)
            k_ref[:, b:b + LANES] = kpe_ref[...]
            v_ref[:, LANES * h:LANES * (h + 1)] = kva_ref[:, b:b + LANES].astype(v_ref.dtype)

    tab = pl.BlockSpec((tr, LANES), lambda i: (i, 0))
    wide = pl.BlockSpec((tr, AH * 256), lambda i: (i, 0))
    return _pcall(body, name=name, grid=(S // tr,), in_specs=[wide, wide, tab, tab, tab, tab],
                  out_specs=(wide, wide, pl.BlockSpec((tr, AH * LANES), lambda i: (i, 0))),
                  out_shape=(jax.ShapeDtypeStruct((S, AH * 256), MM_DT), jax.ShapeDtypeStruct((S, AH * 256), MM_DT),
                             jax.ShapeDtypeStruct((S, AH * LANES), MM_DT)),
                  compiler_params=_row_params())(qa, kva, kpe, C, SP, SN)


def _mla_build_bwd(name, cfg, dq, dk, dv, tabs):
    S, AH = cfg.S, cfg.AH
    tr = _pick(S, 256, 16)
    (C, SP, SN), w = tabs

    def body(dq_ref, dk_ref, dv_ref, c_ref, sp_ref, sn_ref, dqa_ref, dkva_ref, dkpe_ref):
        dkpe = None
        for h in range(AH):
            a, b = 256 * h, 256 * h + LANES
            dqa_ref[:, a:b] = dq_ref[:, a:b].astype(dqa_ref.dtype)
            dqa_ref[:, b:b + LANES] = _rope_t(dq_ref[:, b:b + LANES], c_ref[...], sp_ref[...], sn_ref[...], w).astype(dqa_ref.dtype)
            dkva_ref[:, a:b] = dk_ref[:, a:b].astype(dkva_ref.dtype)
            dkva_ref[:, b:b + LANES] = dv_ref[:, LANES * h:LANES * (h + 1)].astype(dkva_ref.dtype)
            part = dk_ref[:, b:b + LANES]
            dkpe = part if dkpe is None else dkpe + part
        dkpe_ref[...] = dkpe

    tab = pl.BlockSpec((tr, LANES), lambda i: (i, 0))
    wide = pl.BlockSpec((tr, AH * 256), lambda i: (i, 0))
    return _pcall(body, name=name, grid=(S // tr,),
                  in_specs=[wide, wide, pl.BlockSpec((tr, AH * LANES), lambda i: (i, 0)), tab, tab, tab],
                  out_specs=(wide, wide, tab),
                  out_shape=(jax.ShapeDtypeStruct((S, AH * 256), MM_DT), jax.ShapeDtypeStruct((S, AH * 256), MM_DT),
                             jax.ShapeDtypeStruct((S, LANES), F32)),
                  compiler_params=_row_params())(dq, dk, dv, C, SP, SN)


def _outnorm_fwd(name, cfg, oa, ob, oc, g):
    S = cfg.S
    tr = _pick(S, 256, 16)
    widths = (cfg.AW, cfg.BW, cfg.CW)

    def body(a_ref, b_ref, c_ref, g_ref, o_ref):
        off = 0
        for ref, wd in zip((a_ref, b_ref, c_ref), widths):
            v = ref[...]
            r = lax.rsqrt(jnp.mean(v * v, axis=-1, keepdims=True) + EPS)
            o_ref[:, off:off + wd] = (v * r * g_ref[:, off:off + wd]).astype(o_ref.dtype)
            off += wd

    return _pcall(body, name=name, grid=(S // tr,),
                  in_specs=[pl.BlockSpec((tr, wd), lambda i: (i, 0)) for wd in widths] + [pl.BlockSpec((1, cfg.MIX), lambda i: (0, 0))],
                  out_specs=pl.BlockSpec((tr, cfg.MIX), lambda i: (i, 0)),
                  out_shape=jax.ShapeDtypeStruct((S, cfg.MIX), MM_DT), compiler_params=_row_params())(oa, ob, oc, g)


def _outnorm_bwd(name, cfg, dmix, oa, ob, oc, g):
    S = cfg.S
    tr = _pick(S, 256, 16)
    widths = (cfg.AW, cfg.BW, cfg.CW)

    def body(dm_ref, a_ref, b_ref, c_ref, g_ref, da_ref, db_ref, dc_ref, dg_ref):
        off = 0
        parts = []
        for ref, dref, wd in zip((a_ref, b_ref, c_ref), (da_ref, db_ref, dc_ref), widths):
            v = ref[...]
            r = lax.rsqrt(jnp.mean(v * v, axis=-1, keepdims=True) + EPS)
            n = v * r
            dm = dm_ref[:, off:off + wd]
            parts.append(jnp.sum(dm * n, axis=0, keepdims=True))
            dn = dm * g_ref[:, off:off + wd]
            dref[...] = r * (dn - n * jnp.mean(dn * n, axis=-1, keepdims=True))
            off += wd
        _acc_rows(dg_ref, jnp.concatenate(parts, axis=1), pl.program_id(0) == 0)

    segs = [pl.BlockSpec((tr, wd), lambda i: (i, 0)) for wd in widths]
    return _pcall(body, name=name, grid=(S // tr,),
                  in_specs=[pl.BlockSpec((tr, cfg.MIX), lambda i: (i, 0))] + segs + [pl.BlockSpec((1, cfg.MIX), lambda i: (0, 0))],
                  out_specs=tuple(segs) + (pl.BlockSpec((8, cfg.MIX), lambda i: (0, 0)),),
                  out_shape=tuple(jax.ShapeDtypeStruct((S, wd), F32) for wd in widths) + (jax.ShapeDtypeStruct((8, cfg.MIX), F32),),
                  compiler_params=_row_params())(dmix, oa, ob, oc, g)


def _band_bias(cfg):
    tq, tk, W = cfg.TB, cfg.TBK, cfg.W
    ns = 2 * W + 1
    shape = ((tk // tq) * ns, tq, tk)
    slab = lax.broadcasted_iota(jnp.int32, shape, 0)
    row = lax.broadcasted_iota(jnp.int32, shape, 1)
    col = lax.broadcasted_iota(jnp.int32, shape, 2)
    d = (slab // ns) * tq + (W - slab % ns) * tk + row - col
    ad = jnp.abs(d)
    m = jnp.zeros(d.shape, F32)
    for reach, dil in cfg.branches:
        ok = ad <= reach
        if dil > 1:
            ok = ok & ((d & (dil - 1)) == 0)
        m = m + ok.astype(F32)
    return jnp.where(m > 0, jnp.log(jnp.maximum(m, 1.0)), NEG)


def _attn_params():
    return pltpu.CompilerParams(dimension_semantics=("parallel", "parallel", "arbitrary"), vmem_limit_bytes=VMEM_LIMIT)


def _scores(q_ref, k_ref, scale, bias_ref):
    s = lax.dot_general(q_ref[...], k_ref[...], NT, preferred_element_type=F32) * scale
    return s if bias_ref is None else s + bias_ref[...]


def _flash_fwd(name, q, k, v, H, G, dk, dv, scale, bias=None, W=None):
    S = q.shape[0]
    band = bias is not None
    tq = bias.shape[1] if band else _pick(S, ATT_TQ, LANES)
    tk = bias.shape[2] if band else _pick(S, ATT_TK_FWD, LANES)
    n = S // tk
    nsteps = 2 * W + 1 if band else n
    R = tk // tq

    def kblock(qi, st):
        return jnp.clip(qi // R - W + st, 0, n - 1) if band else st

    def body(*refs):
        q_ref, k_ref, v_ref = refs[:3]
        bias_ref = refs[3] if band else None
        o_ref, lse_ref, m_sc, l_sc, acc_sc = refs[-5:]
        qi, st = pl.program_id(1), pl.program_id(2)

        @pl.when(st == 0)
        def _():
            m_sc[...] = jnp.full_like(m_sc, NEG)
            l_sc[...] = jnp.zeros_like(l_sc)
            acc_sc[...] = jnp.zeros_like(acc_sc)

        kj = qi // R - W + st if band else st

        def step():
            s = _scores(q_ref, k_ref, scale, bias_ref)
            m_prev = m_sc[...]
            m_new = jnp.maximum(m_prev, jnp.max(s, axis=-1, keepdims=True))
            alpha = jnp.exp(m_prev - m_new)
            p = jnp.exp(s - m_new)
            l_sc[...] = alpha * l_sc[...] + jnp.sum(p, axis=-1, keepdims=True)
            acc_sc[...] = alpha * acc_sc[...] + lax.dot_general(p.astype(MM_DT), v_ref[...], NN, preferred_element_type=F32)
            m_sc[...] = m_new

        if band:
            pl.when((kj >= 0) & (kj < n))(step)
        else:
            step()

        @pl.when(st == nsteps - 1)
        def _():
            l = l_sc[...]
            o_ref[...] = acc_sc[...] / l
            lse_ref[...] = jnp.broadcast_to(m_sc[...] + jnp.log(l), lse_ref.shape)

    in_specs = [pl.BlockSpec((tq, dk), lambda h, qi, st: (qi, h)),
                pl.BlockSpec((tk, dk), lambda h, qi, st: (kblock(qi, st), h // G)),
                pl.BlockSpec((tk, dv), lambda h, qi, st: (kblock(qi, st), h // G))]
    ops = [q, k, v]
    if band:
        in_specs.append(pl.BlockSpec((None, tq, tk), lambda h, qi, st: ((qi % R) * nsteps + st, 0, 0)))
        ops.append(bias)
    return _pcall(body, name=name, grid=(H, S // tq, nsteps), in_specs=in_specs,
                  out_specs=(pl.BlockSpec((tq, dv), lambda h, qi, st: (qi, h)),
                             pl.BlockSpec((None, tq, LANES), lambda h, qi, st: (h, qi, 0))),
                  out_shape=(jax.ShapeDtypeStruct((S, H * dv), F32), jax.ShapeDtypeStruct((H, S, LANES), F32)),
                  scratch_shapes=[pltpu.VMEM((tq, 1), F32), pltpu.VMEM((tq, 1), F32), pltpu.VMEM((tq, dv), F32)],
                  compiler_params=_attn_params())(*ops)


def _flash_dq(name, q, k, v, do, o, lse, H, G, dk, dv, scale, bias=None, W=None):
    S = q.shape[0]
    band = bias is not None
    tq = bias.shape[1] if band else _pick(S, ATT_TQ, LANES)
    tk = bias.shape[2] if band else _pick(S, ATT_TK, LANES)
    n = S // tk
    nsteps = 2 * W + 1 if band else n
    R = tk // tq

    def kblock(qi, st):
        return jnp.clip(qi // R - W + st, 0, n - 1) if band else st

    def body(*refs):
        q_ref, k_ref, v_ref, do_ref, o_ref, lse_ref = refs[:6]
        bias_ref = refs[6] if band else None
        dq_ref, delta_sc, acc_sc = refs[-3:]
        qi, st = pl.program_id(1), pl.program_id(2)

        @pl.when(st == 0)
        def _():
            delta_sc[...] = jnp.sum(do_ref[...] * o_ref[...], axis=-1, keepdims=True)
            acc_sc[...] = jnp.zeros_like(acc_sc)

        kj = qi // R - W + st if band else st

        def step():
            p = jnp.exp(_scores(q_ref, k_ref, scale, bias_ref) - lse_ref[:, 0:1])
            dp = lax.dot_general(do_ref[...].astype(MM_DT), v_ref[...], NT, preferred_element_type=F32)
            ds = p * (dp - delta_sc[...]) * scale
            acc_sc[...] += lax.dot_general(ds.astype(MM_DT), k_ref[...], NN, preferred_element_type=F32)

        if band:
            pl.when((kj >= 0) & (kj < n))(step)
        else:
            step()

        @pl.when(st == nsteps - 1)
        def _():
            dq_ref[...] = acc_sc[...]

    qspec = lambda wd: pl.BlockSpec((tq, wd), lambda h, qi, st: (qi, h))
    in_specs = [qspec(dk),
                pl.BlockSpec((tk, dk), lambda h, qi, st: (kblock(qi, st), h // G)),
                pl.BlockSpec((tk, dv), lambda h, qi, st: (kblock(qi, st), h // G)),
                qspec(dv), qspec(dv),
                pl.BlockSpec((None, tq, LANES), lambda h, qi, st: (h, qi, 0))]
    ops = [q, k, v, do, o, lse]
    if band:
        in_specs.append(pl.BlockSpec((None, tq, tk), lambda h, qi, st: ((qi % R) * nsteps + st, 0, 0)))
        ops.append(bias)
    return _pcall(body, name=name, grid=(H, S // tq, nsteps), in_specs=in_specs,
                  out_specs=qspec(dk), out_shape=jax.ShapeDtypeStruct((S, H * dk), F32),
                  scratch_shapes=[pltpu.VMEM((tq, 1), F32), pltpu.VMEM((tq, dk), F32)],
                  compiler_params=_attn_params())(*ops)


def _flash_dkv(name, q, k, v, do, o, lse, H, G, dk, dv, scale, bias=None, W=None):
    S = q.shape[0]
    band = bias is not None
    tq = bias.shape[1] if band else _pick(S, ATT_TQ, LANES)
    tk = bias.shape[2] if band else _pick(S, ATT_TK_DKV, LANES)
    n = S // tq
    R = tk // tq
    nq = R * (2 * W + 1) if band else n
    nsteps = G * nq
    Hkv = H // G

    def qhead(hk, st):
        return hk * G + st // nq

    def qblock(kj, st):
        return jnp.clip(R * (kj - W) + st % nq, 0, n - 1) if band else st % nq

    def body(*refs):
        q_ref, k_ref, v_ref, do_ref, o_ref, lse_ref = refs[:6]
        bias_ref = refs[6] if band else None
        dk_ref, dv_ref, dk_sc, dv_sc = refs[-4:]
        kj, st = pl.program_id(1), pl.program_id(2)

        @pl.when(st == 0)
        def _():
            dk_sc[...] = jnp.zeros_like(dk_sc)
            dv_sc[...] = jnp.zeros_like(dv_sc)

        qi = R * (kj - W) + st % nq if band else st % nq

        def step():
            p = jnp.exp(_scores(q_ref, k_ref, scale, bias_ref) - lse_ref[:, 0:1])
            dof = do_ref[...]
            dob = dof.astype(MM_DT)
            dv_sc[...] += lax.dot_general(p.astype(MM_DT), dob, TN, preferred_element_type=F32)
            dp = lax.dot_general(dob, v_ref[...], NT, preferred_element_type=F32)
            delta = jnp.sum(dof * o_ref[...], axis=-1, keepdims=True)
            ds = p * (dp - delta) * scale
            dk_sc[...] += lax.dot_general(ds.astype(MM_DT), q_ref[...], TN, preferred_element_type=F32)

        if band:
            pl.when((qi >= 0) & (qi < n))(step)
        else:
            step()

        @pl.when(st == nsteps - 1)
        def _():
            dk_ref[...] = dk_sc[...]
            dv_ref[...] = dv_sc[...]

    qspec = lambda wd: pl.BlockSpec((tq, wd), lambda hk, kj, st: (qblock(kj, st), qhead(hk, st)))
    kspec = lambda wd: pl.BlockSpec((tk, wd), lambda hk, kj, st: (kj, hk))
    in_specs = [qspec(dk), kspec(dk), kspec(dv), qspec(dv), qspec(dv),
                pl.BlockSpec((None, tq, LANES), lambda hk, kj, st: (qhead(hk, st), qblock(kj, st), 0))]
    ops = [q, k, v, do, o, lse]
    if band:
        in_specs.append(pl.BlockSpec((None, tq, tk),
                                     lambda hk, kj, st: ((st % nq % R) * (2 * W + 1) + 2 * W - (st % nq) // R, 0, 0)))
        ops.append(bias)
    return _pcall(body, name=name, grid=(Hkv, S // tk, nsteps), in_specs=in_specs,
                  out_specs=(kspec(dk), kspec(dv)),
                  out_shape=(jax.ShapeDtypeStruct((S, Hkv * dk), F32), jax.ShapeDtypeStruct((S, Hkv * dv), F32)),
                  scratch_shapes=[pltpu.VMEM((tk, dk), F32), pltpu.VMEM((tk, dv), F32)],
                  compiler_params=_attn_params())(*ops)


def _rowtile(rows, cols):
    return _pick(rows, max(16, (512 * 1024) // cols // 16 * 16), 16)


def _cast_rows(name, w, dtype):
    R, C = w.shape
    tr = _rowtile(R, C)

    def body(w_ref, o_ref):
        o_ref[...] = w_ref[...].astype(o_ref.dtype)

    spec = pl.BlockSpec((tr, C), lambda i: (i, 0))
    return _pcall(body, name=name, grid=(R // tr,), in_specs=[spec], out_specs=spec,
                  out_shape=jax.ShapeDtypeStruct((R, C), dtype), compiler_params=_row_params())(w)


def _adamw_math(wv, gv, mv, vv):
    bc1 = 1.0 - ADAM_B1 ** ADAM_STEP
    bc2 = 1.0 - ADAM_B2 ** ADAM_STEP
    mn = ADAM_B1 * mv + (1.0 - ADAM_B1) * gv
    vn = ADAM_B2 * vv + (1.0 - ADAM_B2) * jnp.square(gv)
    m_hat = mn / bc1
    v_hat = vn / bc2
    return -ADAM_LR * (m_hat / (jnp.sqrt(v_hat) + ADAM_EPS) + ADAM_WD * wv), mn, vn


def _adamw_halves(name, w, g2, m, v, c_arr):
    depth, _, R, C = w.shape
    tr = _rowtile(R, C)

    def body(c_ref, w_ref, g_ref, m_ref, v_ref, go_ref, d_ref, nm_ref, nv_ref):
        gv = g_ref[...]
        go_ref[...] = gv
        d_ref[...], nm_ref[...], nv_ref[...] = _adamw_math(w_ref[...], gv, m_ref[...], v_ref[...])

    spec = pl.BlockSpec((None, None, tr, C), lambda l, h, r, c_ref: (l, h, r, 0))
    gspec = pl.BlockSpec((None, None, tr, C), lambda l, h, r, c_ref: (l, (h + c_ref[0]) % 2, r, 0))
    sds = jax.ShapeDtypeStruct(w.shape, F32)
    grid_spec = pltpu.PrefetchScalarGridSpec(num_scalar_prefetch=1, grid=(depth, 2, R // tr), in_specs=[spec, gspec, spec, spec],
                                             out_specs=(spec,) * 4)
    return _pcall(body, name=name, grid_spec=grid_spec, out_shape=(sds,) * 4,
                  compiler_params=pltpu.CompilerParams(dimension_semantics=("arbitrary",) * 3, vmem_limit_bytes=VMEM_LIMIT))(c_arr, w, g2, m, v)


def _adamw(name, w, g, m, v):
    R, C = w.shape
    tr = _rowtile(R, C)

    def body(w_ref, g_ref, m_ref, v_ref, d_ref, nm_ref, nv_ref):
        d_ref[...], nm_ref[...], nv_ref[...] = _adamw_math(w_ref[...], g_ref[...], m_ref[...], v_ref[...])

    spec = pl.BlockSpec((tr, C), lambda i: (i, 0))
    sds = jax.ShapeDtypeStruct((R, C), F32)
    return _pcall(body, name=name, grid=(R // tr,), in_specs=[spec] * 4, out_specs=(spec,) * 3,
                  out_shape=(sds, sds, sds), compiler_params=_row_params())(w, g, m, v)


HBM_SPEC = pl.BlockSpec(memory_space=pltpu.HBM)


def _place():
    x, y, c = lax.axis_index("x"), lax.axis_index("y"), lax.axis_index("c")
    chips = [(1 - x, y), (x, 1 - y), (1 - x, 1 - y)]
    return x, y, c, chips


def _allgather_body(ins, outs, send, recv, handshake):
    n = len(ins)
    x, y, c, chips = _place()
    me = 2 * x + y
    sib = (x, y, 1 - c)
    if handshake:
        barrier = pltpu.get_barrier_semaphore()
        for peer in [(chip[0], chip[1], c) for chip in chips] + [sib]:
            pl.semaphore_signal(barrier, inc=1, device_id=peer, device_id_type=MESH)
        pl.semaphore_wait(barrier, 4)

    def rcopy(src, dst, k, to):
        return pltpu.make_async_remote_copy(src_ref=src, dst_ref=dst, send_sem=send.at[k], recv_sem=recv.at[k],
                                            device_id=to, device_id_type=MESH)

    def rows(t, cc):
        hr = ins[t].shape[0] // 2
        return pl.ds(cc * hr, hr)

    sends = []
    for t in range(n):
        for j, chip in enumerate(chips):
            cp = rcopy(ins[t].at[rows(t, c)], outs[t].at[me, rows(t, c)], 7 * t + j, (chip[0], chip[1], c))
            cp.start()
            sends.append(cp)
    for t in range(n):
        cp = rcopy(ins[t], outs[t].at[me], 7 * t + 6, sib)
        cp.start()
        sends.append(cp)
    for t in range(n):
        for j, chip in enumerate(chips):
            slab = outs[t].at[2 * chip[0] + chip[1], rows(t, c)]
            rcopy(slab, slab, 7 * t + j, (chip[0], chip[1], c)).wait_recv()
            fw = rcopy(slab, slab, 7 * t + 3 + j, sib)
            fw.start()
            sends.append(fw)
    for t in range(n):
        rcopy(ins[t], outs[t].at[me], 7 * t + 6, sib).wait_recv()
        for j, chip in enumerate(chips):
            slab = outs[t].at[2 * chip[0] + chip[1], rows(t, 1 - c)]
            rcopy(slab, slab, 7 * t + 3 + j, sib).wait_recv()
    for cp in sends:
        cp.wait_send()


def _allgather_layer(name, shards):
    n = len(shards)

    def body(*refs):
        _allgather_body(refs[:n], refs[n:2 * n], refs[2 * n], refs[2 * n + 1], False)

    return _pcall(body, name=name, in_specs=[HBM_SPEC] * n, out_specs=tuple([HBM_SPEC] * n),
                  out_shape=tuple(jax.ShapeDtypeStruct((N_SHARD,) + s.shape, s.dtype) for s in shards),
                  scratch_shapes=[pltpu.SemaphoreType.DMA((7 * n,)), pltpu.SemaphoreType.DMA((7 * n,))])(*shards)


def _allgather_layer_async(name, shards, collective_id):
    n = len(shards)
    in_refs = [jax.new_ref(s, memory_space=pltpu.MemorySpace.HBM) for s in shards]
    out_refs = [jax.empty_ref(jax.ShapeDtypeStruct((N_SHARD,) + s.shape, s.dtype), memory_space=pltpu.MemorySpace.HBM)
                for s in shards]

    @pl.kernel(mesh=plsc.ScalarSubcoreMesh(axis_name="seq", num_cores=1), name=name,
               scratch_types=(pltpu.SemaphoreType.DMA((7 * n,)), pltpu.SemaphoreType.DMA((7 * n,))),
               compiler_params=pltpu.CompilerParams(collective_id=collective_id))
    def launch(send, recv):
        _allgather_body(in_refs, out_refs, send, recv, True)

    launch()
    return [r[...] for r in out_refs]


def _flip(x, y, c, r):
    return (1 - x if r & 4 else x, 1 - y if r & 2 else y, 1 - c if r & 1 else c)


def _grad_exchange_async(name, gls, collective_id):
    n = len(gls)
    in_refs = [jax.new_ref(g, memory_space=pltpu.MemorySpace.HBM) for g in gls]
    out_refs = [jax.empty_ref(jax.ShapeDtypeStruct((7, g.shape[1] // 2, g.shape[2]), g.dtype), memory_space=pltpu.MemorySpace.HBM)
                for g in gls]

    @pl.kernel(mesh=plsc.ScalarSubcoreMesh(axis_name="seq", num_cores=1), name=name,
               scratch_types=(pltpu.SemaphoreType.DMA((7 * n,)), pltpu.SemaphoreType.DMA((7 * n,))),
               compiler_params=pltpu.CompilerParams(collective_id=collective_id))
    def launch(send, recv):
        _grad_exchange_body(in_refs, out_refs, send, recv, True)

    launch()
    return [r[...] for r in out_refs]


def _grad_exchange_body(ins, outs, send, recv, handshake):
    x, y, c, _ = _place()
    peers = [_flip(x, y, c, r) for r in range(1, 8)]
    if handshake:
        barrier = pltpu.get_barrier_semaphore()
        for peer in peers:
            pl.semaphore_signal(barrier, inc=1, device_id=peer, device_id_type=MESH)
        pl.semaphore_wait(barrier, 7)
    cps = []
    for t in range(len(ins)):
        hr = ins[t].shape[1] // 2
        for j, (px, py, pc) in enumerate(peers):
            cp = pltpu.make_async_remote_copy(src_ref=ins[t].at[2 * px + py, pl.ds(pc * hr, hr)], dst_ref=outs[t].at[j],
                                              send_sem=send.at[7 * t + j], recv_sem=recv.at[7 * t + j],
                                              device_id=(px, py, pc), device_id_type=MESH)
            cp.start()
            cps.append(cp)
    for cp in cps:
        cp.wait()


def _add_eight(name, own, slots, dev_arr, l, buf):
    ns, Ks, Ns = own.shape
    hr = Ks // 2
    tr = _rowtile(hr, Ns)
    fresh = isinstance(buf, tuple)

    def body(pl_ref, own_ref, *refs):
        acc = own_ref[...].astype(F32)
        for s_ref in refs[:7]:
            acc = acc + s_ref[...].astype(F32)
        refs[-1][...] = acc

    in_specs = [pl.BlockSpec((None, tr, Ns), lambda r, pl_ref: (pl_ref[0], r, 0))]
    in_specs += [pl.BlockSpec((None, tr, Ns), functools.partial(lambda r, pl_ref, j: (j, r, 0), j=j)) for j in range(7)]
    ops = [dev_arr, own.reshape(ns * 2, hr, Ns)] + [slots] * 7
    if not fresh:
        in_specs.append(pl.BlockSpec(memory_space=pl.ANY))
        ops.append(buf)
    grid_spec = pltpu.PrefetchScalarGridSpec(num_scalar_prefetch=1, grid=(hr // tr,), in_specs=in_specs,
                                             out_specs=pl.BlockSpec((None, None, tr, Ns), lambda r, pl_ref: (l, 0, r, 0)))
    return _pcall(body, name=name, grid_spec=grid_spec, out_shape=jax.ShapeDtypeStruct(buf if fresh else buf.shape, F32),
                  input_output_aliases={} if fresh else {9: 0}, compiler_params=_row_params())(*ops)


def _share_reduced(gs):
    n = len(gs)

    def body(*refs):
        ins, outs = refs[:n], refs[n:2 * n]
        send, recv = refs[2 * n:]
        x, y, c, _ = _place()
        cps = []
        for t in range(n):
            cp = pltpu.make_async_remote_copy(src_ref=ins[t].at[:, 0], dst_ref=outs[t].at[:, 1], send_sem=send.at[t], recv_sem=recv.at[t],
                                              device_id=(x, y, 1 - c), device_id_type=MESH)
            cp.start()
            cps.append(cp)
        for cp in cps:
            cp.wait()

    return _pcall(body, name="rs_share_reduced", in_specs=[HBM_SPEC] * n, out_specs=tuple([HBM_SPEC] * n),
                  out_shape=tuple(jax.ShapeDtypeStruct(g.shape, g.dtype) for g in gs),
                  input_output_aliases={t: t for t in range(n)},
                  scratch_shapes=[pltpu.SemaphoreType.DMA((n,)), pltpu.SemaphoreType.DMA((n,))])(*gs)


def _allreduce_small(vec):
    R = vec.shape[0]

    def body(v_ref, o_ref, buf, send, recv):
        x, y, c, _ = _place()
        me = 4 * x + 2 * y + c
        buf[me] = v_ref[...]
        cps = []
        for r in range(1, 8):
            fx, fy, fc = (r >> 2) & 1, (r >> 1) & 1, r & 1
            to = (1 - x if fx else x, 1 - y if fy else y, 1 - c if fc else c)
            cp = pltpu.make_async_remote_copy(src_ref=v_ref, dst_ref=buf.at[me], send_sem=send.at[r - 1], recv_sem=recv.at[r - 1],
                                              device_id=to, device_id_type=MESH)
            cp.start()
            cps.append(cp)
        for r in range(1, 8):
            fx, fy, fc = (r >> 2) & 1, (r >> 1) & 1, r & 1
            frm = (1 - x if fx else x, 1 - y if fy else y, 1 - c if fc else c)
            src = 4 * frm[0] + 2 * frm[1] + frm[2]
            pltpu.make_async_remote_copy(src_ref=v_ref, dst_ref=buf.at[src], send_sem=send.at[r - 1], recv_sem=recv.at[r - 1],
                                         device_id=frm, device_id_type=MESH).wait_recv()
        for cp in cps:
            cp.wait_send()
        acc = buf[0]
        for i in range(1, 8):
            acc = acc + buf[i]
        o_ref[...] = acc

    vm = pl.BlockSpec(memory_space=pltpu.VMEM)
    return _pcall(body, name="allreduce_small", in_specs=[vm], out_specs=vm, out_shape=jax.ShapeDtypeStruct((R, LANES), F32),
                  scratch_shapes=[pltpu.VMEM((8, R, LANES), F32), pltpu.SemaphoreType.DMA((7,)), pltpu.SemaphoreType.DMA((7,))])(vec)


def _unshard_cols(wg):
    ns, depth, K, Ns = wg.shape
    return jnp.moveaxis(wg, 0, 2).reshape(depth, K, ns * Ns)


def _shard_cols(w):
    K, N = w.shape
    return jnp.moveaxis(w.reshape(K, N_SHARD, N // N_SHARD), 1, 0)


def _uq_padded(cfg, wuq_g):
    depth = wuq_g.shape[1]
    w = _unshard_cols(wuq_g).reshape(depth, cfg.QL, cfg.AH, 192)
    w = jnp.pad(w, ((0, 0), (0, cfg.QLP - cfg.QL), (0, 0), (0, 64)))
    return w.reshape(1, depth, cfg.QLP, cfg.AH * 256)


def _uq_grad_unpadded(cfg, dw):
    w = dw[:cfg.QL].reshape(cfg.QL, cfg.AH, 256)[:, :, :192].reshape(cfg.QL, cfg.UQ)
    return _shard_cols(w)


def _ukv_padded(cfg, wukv_g):
    w = _unshard_cols(wukv_g)
    w = jnp.pad(w, ((0, 0), (cfg.KOFF, cfg.KVW - cfg.KVL - cfg.KOFF), (0, 0)))
    return w[None]


def _ukv_grad_unpadded(cfg, dw):
    return _shard_cols(dw[cfg.KOFF:cfg.KOFF + cfg.KVL])


def _pad_lanes(v, lo, total):
    return jnp.pad(v, (lo, total - lo - v.shape[0]))[None]


def _layer_fwd(cfg, l, x, W, small, tabs):
    ln1, gq, gkv, gqn, gkn, gout, ln2 = small
    sc_a, sc_h = 1.0 / math.sqrt(192), 1.0 / math.sqrt(128)
    n = f"l{l}_"
    h = _rms_fwd(n + "ln1", x, ln1)
    proj = _mm_nn(n + "proj", h, (W["w_in"], "col", 0))
    cqn, ckvn, kpe = _mla_prep_fwd(n + "mla_prep", cfg, proj, gq, gkv, tabs["akr"])
    qa = _mm_nn(n + "uq", cqn, (W["uq_p"], "col", 0))
    kva = _mm_nn(n + "ukv", ckvn, (W["ukv_p"], "col", 0))
    q_a, k_a, v_a = _mla_build_fwd(n + "mla_build", cfg, qa, kva, kpe, tabs["aq"])
    o_a, lse_a = _flash_fwd(n + "attn_a", q_a, k_a, v_a, cfg.AH, 1, 256, 128, sc_a)
    q_b = _headprep_fwd(n + "bq", proj, cfg.o_bq, cfg.BH, gqn, tabs["b"])
    k_b = _headprep_fwd(n + "bk", proj, cfg.o_bk, cfg.BKV, gkn, tabs["b"])
    v_b = _headprep_fwd(n + "bv", proj, cfg.o_bv, cfg.BKV, None, None)
    o_b, lse_b = _flash_fwd(n + "attn_b", q_b, k_b, v_b, cfg.BH, cfg.G, 128, 128, sc_h)
    q_c = _headprep_fwd(n + "cq", proj, cfg.o_cq, cfg.CH, None, tabs["c"])
    k_c = _headprep_fwd(n + "ck", proj, cfg.o_ck, cfg.CH, None, tabs["c"])
    v_c = _headprep_fwd(n + "cv", proj, cfg.o_cv, cfg.CH, None, None)
    o_c, lse_c = _flash_fwd(n + "attn_c", q_c, k_c, v_c, cfg.CH, 1, 128, 128, sc_h, tabs["bias_c"], cfg.W)
    mixed = _outnorm_fwd(n + "outnorm", cfg, o_a, o_b, o_c, gout)
    x1 = _mm_nn(n + "out", mixed, (W["w_out"], "row", 0), epi=_epi_residual, extra=x)
    h2 = _rms_fwd(n + "ln2", x1, ln2)
    a, u = _mm_nn(n + "ff1", h2, (W["w_ff1"], "col", 0), epi=_epi_relu2, out_dtypes=(MM_DT, MM_DT))
    x2 = _mm_nn(n + "ff2", u, (W["w_ff2"], "row", 0), epi=_epi_residual, extra=x1)
    saved = dict(x=x, h=h, proj=proj, cqn=cqn, ckvn=ckvn, q_a=q_a, k_a=k_a, v_a=v_a, o_a=o_a, lse_a=lse_a,
                 q_b=q_b, k_b=k_b, v_b=v_b, o_b=o_b, lse_b=lse_b, q_c=q_c, k_c=k_c, v_c=v_c, o_c=o_c, lse_c=lse_c,
                 mixed=mixed, x1=x1, h2=h2, a=a, u=u)
    return x2, saved


def _layer_bwd(cfg, l, dx2, sv, W, small, tabs, GW):
    ln1, gq, gkv, gqn, gkn, gout, ln2 = small
    sc_a, sc_h = 1.0 / math.sqrt(192), 1.0 / math.sqrt(128)
    n = f"l{l}_b_"
    S = cfg.S
    mats = {m[0]: m for m in cfg.mats}

    def dw(name, a, g, key):
        _, Rs, Cs, kind = mats[key]
        GW[key] = _mm_tn(n + name, a, g, kind, 0, Rs, Cs, (N_SHARD, 1, Rs, Cs), out_dtype=MM_DT).reshape(N_SHARD, Rs, Cs)

    da = _mm_nt(n + "ff2_dx", dx2, (W["w_ff2"], "row", 0), epi=_epi_drelu2, out_dtype=MM_DT, extra=sv["a"])
    dw("ff2_dw", sv["u"], dx2, "w_ff2")
    dh2 = _mm_nt(n + "ff1_dx", da, (W["w_ff1"], "col", 0))
    dw("ff1_dw", sv["h2"], da, "w_ff1")
    dx1, dln2 = _rms_bwd(n + "ln2", sv["x1"], ln2, dh2, dx2)
    dmix = _mm_nt(n + "out_dx", dx1, (W["w_out"], "row", 0))
    dw("out_dw", sv["mixed"], dx1, "w_out")
    do_a, do_b, do_c, dgout = _outnorm_bwd(n + "outnorm", cfg, dmix, sv["o_a"], sv["o_b"], sv["o_c"], gout)
    dproj = jnp.zeros((S, cfg.IN), MM_DT)
    args_c = (sv["q_c"], sv["k_c"], sv["v_c"], do_c, sv["o_c"], sv["lse_c"], cfg.CH, 1, 128, 128, sc_h, tabs["bias_c"], cfg.W)
    dq_c = _flash_dq(n + "attn_c_dq", *args_c)
    dk_c, dv_c = _flash_dkv(n + "attn_c_dkv", *args_c)
    dproj, _ = _headprep_bwd(n + "cq", dq_c, sv["proj"], cfg.o_cq, cfg.CH, None, tabs["c"], dproj)
    dproj, _ = _headprep_bwd(n + "ck", dk_c, sv["proj"], cfg.o_ck, cfg.CH, None, tabs["c"], dproj)
    dproj, _ = _headprep_bwd(n + "cv", dv_c, sv["proj"], cfg.o_cv, cfg.CH, None, None, dproj)
    args_b = (sv["q_b"], sv["k_b"], sv["v_b"], do_b, sv["o_b"], sv["lse_b"], cfg.BH, cfg.G, 128, 128, sc_h)
    dq_b = _flash_dq(n + "attn_b_dq", *args_b)
    dk_b, dv_b = _flash_dkv(n + "attn_b_dkv", *args_b)
    dproj, dgqn = _headprep_bwd(n + "bq", dq_b, sv["proj"], cfg.o_bq, cfg.BH, gqn, tabs["b"], dproj)
    dproj, dgkn = _headprep_bwd(n + "bk", dk_b, sv["proj"], cfg.o_bk, cfg.BKV, gkn, tabs["b"], dproj)
    dproj, _ = _headprep_bwd(n + "bv", dv_b, sv["proj"], cfg.o_bv, cfg.BKV, None, None, dproj)
    args_a = (sv["q_a"], sv["k_a"], sv["v_a"], do_a, sv["o_a"], sv["lse_a"], cfg.AH, 1, 256, 128, sc_a)
    dq_a = _flash_dq(n + "attn_a_dq", *args_a)
    dk_a, dv_a = _flash_dkv(n + "attn_a_dkv", *args_a)
    dqa, dkva, dkpe = _mla_build_bwd(n + "mla_build", cfg, dq_a, dk_a, dv_a, tabs["aq"])
    dcq = _mm_nt(n + "uq_dx", dqa, (W["uq_p"], "col", 0))
    dwuq = _mm_tn(n + "uq_dw", sv["cqn"], dqa, "col", 0, cfg.QLP, cfg.AH * 256, (1, 1, cfg.QLP, cfg.AH * 256))
    dckv = _mm_nt(n + "ukv_dx", dkva, (W["ukv_p"], "col", 0))
    dwukv = _mm_tn(n + "ukv_dw", sv["ckvn"], dkva, "col", 0, cfg.KVW, cfg.AH * 256, (1, 1, cfg.KVW, cfg.AH * 256))
    dproj, dgq, dgkv = _mla_prep_bwd(n + "mla_prep", cfg, dcq, dckv, dkpe, sv["proj"], gq, gkv, tabs["akr"], dproj)
    dh = _mm_nt(n + "proj_dx", dproj, (W["w_in"], "col", 0))
    dw("proj_dw", sv["h"], dproj, "w_in")
    dx, dln1 = _rms_bwd(n + "ln1", sv["x"], ln1, dh, dx1)
    gains = dict(ln1_g=dln1[0], g_q_a=dgq[0, :cfg.QL], g_kv_a=dgkv[0, cfg.KOFF:cfg.KOFF + cfg.KVL], g_qn_b=dgqn[0],
                 g_kn_b=dgkn[0], g_out=dgout[0], ln2_g=dln2[0])
    GW["w_uq"] = _uq_grad_unpadded(cfg, dwuq[0, 0]).astype(MM_DT)
    GW["w_ukv"] = _ukv_grad_unpadded(cfg, dwukv[0, 0]).astype(MM_DT)
    return dx, gains


SMALL_NAMES = ("ln1_g", "g_q_a", "g_kv_a", "g_qn_b", "g_kn_b", "g_out", "ln2_g")
MAT_NAMES = ("w_in", "w_uq", "w_ukv", "w_out", "w_ff1", "w_ff2")


def _pack_small(cfg, per_layer, final, scalar=None):
    last = jnp.zeros((1,), F32) if scalar is None else scalar.reshape(1)
    flat = jnp.concatenate([per_layer[k].reshape(-1) for k in SMALL_NAMES] + [final.reshape(-1), last])
    total = flat.shape[0]
    rows = _rup(-(-total // LANES), 8)
    return jnp.pad(flat, (0, rows * LANES - total)).reshape(rows, LANES)


def _unpack_small(cfg, packed, shapes):
    flat = packed.reshape(-1)
    out, off = {}, 0
    for k in SMALL_NAMES + ("ln_f_g",):
        n = math.prod(shapes[k])
        out[k] = flat[off:off + n].reshape(shapes[k])
        off += n
    return out, flat[off]


def _step(cfg, w, m, v, x, tgt):
    DEPTH, hd = cfg.DEPTH, cfg.HD
    c = lax.axis_index("c")
    me_chip = 2 * lax.axis_index("x") + lax.axis_index("y")
    c_arr = jnp.reshape(c, (1,)).astype(jnp.int32)
    dev_arr = jnp.reshape(2 * me_chip + c, (1,)).astype(jnp.int32)
    mats = {mt[0]: mt for mt in cfg.mats}

    shards = []
    for name in MAT_NAMES:
        _, Ks, Ns, _ = mats[name]
        shards.append(_cast_rows("cast_" + name, w[name].reshape(DEPTH * Ks, Ns), MM_DT).reshape(DEPTH, Ks, Ns))
    n_first = 3

    def gather(l, after):
        mine = [s[l] for s in shards]
        if l == 0:
            first = _allgather_layer("allgather_l0_first", mine[:n_first])
            rest = list(lax.optimization_barrier((tuple(mine[n_first:]), tuple(first)))[0])
            got = list(first) + list(_allgather_layer_async("allgather_l0_rest", rest, collective_id=3 * DEPTH))
        else:
            mine = list(lax.optimization_barrier((tuple(mine), after))[0])
            got = _allgather_layer_async(f"allgather_l{l}", mine, collective_id=l)
        g = {name: a[:, None] for name, a in zip(MAT_NAMES, got)}
        return dict(w_in=g["w_in"], w_out=g["w_out"], w_ff1=g["w_ff1"], w_ff2=g["w_ff2"],
                    uq_p=_uq_padded(cfg, g["w_uq"]), ukv_p=_ukv_padded(cfg, g["w_ukv"]))

    tabs = _all_tables(cfg)
    tabs["bias_c"] = _band_bias(cfg)

    def small_of(l):
        return (w["ln1_g"][l][None], _pad_lanes(w["g_q_a"][l], 0, cfg.QLP), _pad_lanes(w["g_kv_a"][l], cfg.KOFF, cfg.KVW),
                w["g_qn_b"][l][None], w["g_kn_b"][l][None], w["g_out"][l][None], w["ln2_g"][l][None])

    saved, W_layers = [], []
    xc = x
    for l in range(DEPTH):
        W_layers.append(gather(l, saved[l - 1]["mixed"] if l else None))
        xc, sv = _layer_fwd(cfg, l, xc, W_layers[l], small_of(l), tabs)
        saved.append(sv)
    dx, dlnf, loss_rows = _final_loss("final_loss", xc, w["ln_f_g"][None], tgt)
    loss = loss_rows[0, 0]
    gain_rows, own, slots = [None] * DEPTH, [None] * DEPTH, [None] * DEPTH
    for l in reversed(range(DEPTH)):
        GW = {}
        dx, gain_rows[l] = _layer_bwd(cfg, l, dx, saved[l], W_layers[l], small_of(l), tabs, GW)
        own[l] = [GW[name] for name in MAT_NAMES]
        early = _grad_exchange_async(f"grad_exchange_l{l}_early", own[l][n_first:], collective_id=DEPTH + 2 * l)
        late = _grad_exchange_async(f"grad_exchange_l{l}_late", own[l][:n_first], collective_id=DEPTH + 2 * l + 1)
        slots[l] = late + early

    reduced = []
    for t, name in enumerate(MAT_NAMES):
        _, Ks, Ns, _ = mats[name]
        buf = (DEPTH, 2, Ks // 2, Ns)
        for l in reversed(range(DEPTH)):
            buf = _add_eight(f"rs_add_l{l}_" + name, own[l][t], slots[l][t], dev_arr, l, buf)
        reduced.append(buf)
    full = _share_reduced(reduced)
    grad, delta, new_m, new_v = {}, {}, {}, {}
    for name, g2 in zip(MAT_NAMES, full):
        _, Ks, Ns, _ = mats[name]
        halves, shp = (DEPTH, 2, Ks // 2, Ns), (DEPTH, Ks, Ns)
        res = _adamw_halves("adamw_" + name, w[name].reshape(halves), g2, m[name].reshape(halves), v[name].reshape(halves), c_arr)
        grad[name], delta[name], new_m[name], new_v[name] = (r.reshape(shp) for r in res)

    per_layer = {k: jnp.stack([gain_rows[l][k] for l in range(DEPTH)]) for k in SMALL_NAMES}
    shapes = {k: w[k].shape for k in SMALL_NAMES + ("ln_f_g",)}
    gsum = _allreduce_small(_pack_small(cfg, per_layer, dlnf[0], loss))
    pk = lambda d: _pack_small(cfg, {k: d[k] for k in SMALL_NAMES}, d["ln_f_g"])
    d_s, m_s, v_s = _adamw("adamw_small", pk(w), gsum, pk(m), pk(v))
    for res, packed in ((grad, gsum), (delta, d_s), (new_m, m_s), (new_v, v_s)):
        res.update(_unpack_small(cfg, packed, shapes)[0])
    loss_total = _unpack_small(cfg, gsum, shapes)[1]
    return loss_total, dx, grad, delta, new_m, new_v


WEIGHT_NAMES = ("ln1_g", "w_in", "g_q_a", "w_uq", "g_kv_a", "w_ukv", "g_qn_b", "g_kn_b", "g_out", "w_out", "ln2_g",
                "w_ff1", "w_ff2", "ln_f_g")


def _run(cfg, args):
    nw = len(WEIGHT_NAMES)
    x, tgt = args[0], args[1 + nw]
    w = dict(zip(WEIGHT_NAMES, args[1:1 + nw]))
    m = dict(zip(WEIGHT_NAMES, args[2 + nw:2 + 2 * nw]))
    v = dict(zip(WEIGHT_NAMES, args[2 + 2 * nw:2 + 3 * nw]))
    loss, dx, grad, delta, new_m, new_v = _step(cfg, w, m, v, x.reshape(cfg.S, cfg.D), tgt.reshape(cfg.S, cfg.D))
    return (loss, dx.reshape(x.shape), *[grad[k] for k in WEIGHT_NAMES], *[delta[k] for k in WEIGHT_NAMES],
            *[new_m[k] for k in WEIGHT_NAMES], *[new_v[k] for k in WEIGHT_NAMES])


def kernel(x, ln1_g, w_in, g_q_a, w_uq, g_kv_a, w_ukv, g_qn_b, g_kn_b, g_out, w_out, ln2_g, w_ff1, w_ff2, ln_f_g, loss_target, m_ln1_g, m_w_in, m_g_q_a, m_w_uq, m_g_kv_a, m_w_ukv, m_g_qn_b, m_g_kn_b, m_g_out, m_w_out, m_ln2_g, m_w_ff1, m_w_ff2, m_ln_f_g, v_ln1_g, v_w_in, v_g_q_a, v_w_uq, v_g_kv_a, v_w_ukv, v_g_qn_b, v_g_kn_b, v_g_out, v_w_out, v_ln2_g, v_w_ff1, v_w_ff2, v_ln_f_g):
    return _run(Cfg(), (x, ln1_g, w_in, g_q_a, w_uq, g_kv_a, w_ukv, g_qn_b, g_kn_b, g_out, w_out, ln2_g, w_ff1, w_ff2, ln_f_g, loss_target, m_ln1_g, m_w_in, m_g_q_a, m_w_uq, m_g_kv_a, m_w_ukv, m_g_qn_b, m_g_kn_b, m_g_out, m_w_out, m_ln2_g, m_w_ff1, m_w_ff2, m_ln_f_g, v_ln1_g, v_w_in, v_g_q_a, v_w_uq, v_g_kv_a, v_w_ukv, v_g_qn_b, v_g_kn_b, v_g_out, v_w_out, v_ln2_g, v_w_ff1, v_w_ff2, v_ln_f_g))
```

```python
import functools
import math

import jax
import jax.numpy as jnp
from jax import lax
from jax.experimental import pallas as pl
from jax.experimental.pallas import tpu as pltpu
from jax.experimental.pallas import tpu_sc as plsc

F32 = jnp.float32
MM_DT = jnp.bfloat16
LANES = 128
SUBLANES_F32 = 8
SUBLANES_BF16 = 16
VMEM_LIMIT = 48 * 1024 * 1024
EPS = 1e-6
NEG = -1e30
ROPE_THETA = 10000.0
MM_TK = 1024
ATT_TQ, ATT_TK_FWD, ATT_TK, ATT_TK_DKV = 512, 4096, 2048, 4096
ADAM_LR, ADAM_B1, ADAM_B2, ADAM_EPS, ADAM_WD, ADAM_STEP = 0.001, 0.9, 0.999, 1e-08, 0.01, 10
N_SHARD = 4
MESH = pl.DeviceIdType.MESH

NN = (((1,), (0,)), ((), ()))
NT = (((1,), (1,)), ((), ()))
TN = (((0,), (0,)), ((), ()))


def _pcall(body, **kw):
    return pl.pallas_call(body, **kw)


def _rup(n, m):
    return -(-n // m) * m


def _pick(n, pref, mult):
    best = None
    for t in range(mult, min(n, pref) + 1, mult):
        if n % t == 0:
            best = t
    return best if best is not None else n


class Cfg:
    def __init__(self, S=4096, D=2048, DEPTH=4, AH=4, QL=448, KVL=512, BH=6, BKV=2, CH=6,
                 BRANCHES=((128, 1), (512, 4), (2048, 16)), DFF=8192, GRID_W=64, TB=512, TBK=1024):
        self.S, self.D, self.DEPTH, self.AH, self.QL, self.KVL = S, D, DEPTH, AH, QL, KVL
        self.BH, self.BKV, self.CH, self.DFF, self.GRID_W, self.TB = BH, BKV, CH, DFF, GRID_W, TB
        self.G = BH // BKV
        self.AW, self.BW, self.CW = AH * 128, BH * 128, CH * 128
        self.MIX = self.AW + self.BW + self.CW
        self.QLP = _rup(QL, LANES)
        self.KV0 = (QL // LANES) * LANES
        self.PW = QL + KVL + 64
        assert self.PW % LANES == 0
        self.KVW = self.PW - self.KV0
        self.KOFF = QL - self.KV0
        self.o_bq = self.PW
        self.o_bk = self.o_bq + self.BW
        self.o_bv = self.o_bk + BKV * 128
        self.o_cq = self.o_bv + BKV * 128
        self.o_ck = self.o_cq + self.CW
        self.o_cv = self.o_ck + self.CW
        self.IN = self.o_cv + self.CW
        self.UQ, self.UKV = AH * 192, AH * 256
        self.branches = tuple(((w // (2 * d)) * d, d) for w, d in BRANCHES)
        for _, d in self.branches:
            assert d & (d - 1) == 0
        self.TBK = TBK
        self.W = -(-max(r for r, _ in self.branches) // TBK)
        assert S % TBK == 0 and TBK % TB == 0 and DEPTH % 2 == 0
        self.HD = DEPTH // 2
        self.mats = (("w_in", D, self.IN // 4, "col"), ("w_uq", QL, self.UQ // 4, "col"),
                     ("w_ukv", KVL, self.UKV // 4, "col"), ("w_out", self.MIX // 4, D, "row"),
                     ("w_ff1", D, DFF // 4, "col"), ("w_ff2", DFF // 4, D, "row"))


def _mm_call(name, mode, operands, in_specs, out_shape, out_specs, grid, acc_shape, epi, n_extra, aliases=None):
    nk = grid[2]

    def body(*refs):
        a_ref, b_ref = refs[0], refs[1]
        ex = refs[2:2 + n_extra]
        outs = refs[2 + n_extra:-1]
        acc = refs[-1]
        k = pl.program_id(2)

        @pl.when(k == 0)
        def _():
            acc[...] = jnp.zeros_like(acc)

        acc[...] += lax.dot_general(a_ref[...].astype(MM_DT), b_ref[...].astype(MM_DT), mode,
                                    preferred_element_type=F32)

        @pl.when(k == nk - 1)
        def _():
            epi(acc[...], ex, outs)

    return _pcall(body, name=name, grid=grid, in_specs=in_specs, out_specs=out_specs, out_shape=out_shape,
                  scratch_shapes=[pltpu.VMEM(acc_shape, F32)], input_output_aliases=aliases or {},
                  compiler_params=pltpu.CompilerParams(dimension_semantics=("parallel", "parallel", "arbitrary"),
                                                       vmem_limit_bytes=VMEM_LIMIT))(*operands)


def _wspec(kind, l, Rs, Cs, br, bc, rfn, cfn):
    assert Rs % br == 0 and Cs % bc == 0
    if kind == "col":
        npc = Cs // bc
        return pl.BlockSpec((None, None, br, bc), lambda i, j, k: (cfn(i, j, k) // npc, l, rfn(i, j, k), cfn(i, j, k) % npc))
    npr = Rs // br
    return pl.BlockSpec((None, None, br, bc), lambda i, j, k: (rfn(i, j, k) // npr, l, rfn(i, j, k) % npr, cfn(i, j, k)))


def _epi_plain(acc, ex, outs):
    outs[0][...] = acc.astype(outs[0].dtype)


def _epi_residual(acc, ex, outs):
    outs[0][...] = ex[0][...] + acc


def _epi_relu2(acc, ex, outs):
    outs[0][...] = acc.astype(outs[0].dtype)
    r = jnp.maximum(acc, 0.0)
    outs[1][...] = (r * r).astype(outs[1].dtype)


def _epi_drelu2(acc, ex, outs):
    a = ex[0][...].astype(F32)
    outs[0][...] = (acc * (2.0 * jnp.maximum(a, 0.0))).astype(outs[0].dtype)


def _wdims(wd):
    Wg, kind, l = wd
    ns, _, Rs, Cs = Wg.shape
    K = Rs * ns if kind == "row" else Rs
    N = Cs * ns if kind == "col" else Cs
    return Wg, kind, l, Rs, Cs, K, N


def _mm_nn(name, a, wd, epi=_epi_plain, out_dtypes=(F32,), extra=None):
    Wg, kind, l, Rs, Cs, K, N = _wdims(wd)
    M = a.shape[0]
    tm, tk, tn = _pick(M, 1024, 16), _pick(Rs, MM_TK, LANES), _pick(Cs, 1152, LANES)
    grid = (M // tm, N // tn, K // tk)
    in_specs = [pl.BlockSpec((tm, tk), lambda i, j, k: (i, k)),
                _wspec(kind, l, Rs, Cs, tk, tn, lambda i, j, k: k, lambda i, j, k: j)]
    ops = [a, Wg]
    if extra is not None:
        in_specs.append(pl.BlockSpec((tm, tn), lambda i, j, k: (i, j)))
        ops.append(extra)
    o_spec = pl.BlockSpec((tm, tn), lambda i, j, k: (i, j))
    outs = tuple(jax.ShapeDtypeStruct((M, N), dt) for dt in out_dtypes)
    res = _mm_call(name, NN, ops, in_specs, outs, tuple(o_spec for _ in outs), grid, (tm, tn), epi,
                   0 if extra is None else 1)
    return res[0] if len(res) == 1 else res


def _mm_nt(name, g, wd, epi=_epi_plain, out_dtype=F32, extra=None):
    Wg, kind, l, Rs, Cs, K, N = _wdims(wd)
    M = g.shape[0]
    tm, tn, tk = _pick(M, 1024, 16), _pick(Rs, 1024, LANES), _pick(Cs, 1152, LANES)
    grid = (M // tm, K // tn, N // tk)
    in_specs = [pl.BlockSpec((tm, tk), lambda i, j, k: (i, k)),
                _wspec(kind, l, Rs, Cs, tn, tk, lambda i, j, k: j, lambda i, j, k: k)]
    ops = [g, Wg]
    if extra is not None:
        in_specs.append(pl.BlockSpec((tm, tn), lambda i, j, k: (i, j)))
        ops.append(extra)
    res = _mm_call(name, NT, ops, in_specs, (jax.ShapeDtypeStruct((M, K), out_dtype),),
                   (pl.BlockSpec((tm, tn), lambda i, j, k: (i, j)),), grid, (tm, tn), epi, 0 if extra is None else 1)
    return res[0]


def _mm_tn(name, a, g, kind, l, Rs, Cs, buf, out_dtype=F32):
    M, K = a.shape
    N = g.shape[1]
    tm, tn, tk = _pick(Rs, 1024, LANES), _pick(Cs, 1152, LANES), _pick(M, MM_TK, LANES)
    grid = (K // tm, N // tn, M // tk)
    in_specs = [pl.BlockSpec((tk, tm), lambda i, j, k: (k, i)),
                pl.BlockSpec((tk, tn), lambda i, j, k: (k, j))]
    o_spec = _wspec(kind, l, Rs, Cs, tm, tn, lambda i, j, k: i, lambda i, j, k: j)
    if isinstance(buf, tuple):
        res = _mm_call(name, TN, [a, g], in_specs, (jax.ShapeDtypeStruct(buf, out_dtype),), (o_spec,), grid, (tm, tn), _epi_plain, 0)
    else:
        res = _mm_call(name, TN, [a, g, buf], in_specs + [pl.BlockSpec(memory_space=pl.ANY)],
                       (jax.ShapeDtypeStruct(buf.shape, buf.dtype),), (o_spec,), grid, (tm, tn), _epi_plain, 1, aliases={2: 0})
    return res[0]


def _row_params():
    return pltpu.CompilerParams(dimension_semantics=("arbitrary",), vmem_limit_bytes=VMEM_LIMIT)


def _rms_fwd(name, x, g):
    S, D = x.shape
    tr = _pick(S, 256, 16)

    def body(x_ref, g_ref, o_ref):
        xv = x_ref[...]
        r = lax.rsqrt(jnp.mean(xv * xv, axis=-1, keepdims=True) + EPS)
        o_ref[...] = (xv * r * g_ref[...]).astype(o_ref.dtype)

    return _pcall(body, name=name, grid=(S // tr,),
                  in_specs=[pl.BlockSpec((tr, D), lambda i: (i, 0)), pl.BlockSpec((1, D), lambda i: (0, 0))],
                  out_specs=pl.BlockSpec((tr, D), lambda i: (i, 0)), out_shape=jax.ShapeDtypeStruct((S, D), MM_DT),
                  compiler_params=_row_params())(x, g)


def _acc_rows(ref, part, first):
    @pl.when(first)
    def _():
        ref[...] = jnp.zeros_like(ref)

    ref[...] += jnp.broadcast_to(part, ref.shape)


def _rms_bwd(name, x, g, dy, res):
    S, D = x.shape
    tr = _pick(S, 256, 16)

    def body(x_ref, g_ref, dy_ref, res_ref, dx_ref, dxb_ref, dg_ref):
        xv = x_ref[...]
        r = lax.rsqrt(jnp.mean(xv * xv, axis=-1, keepdims=True) + EPS)
        xh = xv * r
        dyv = dy_ref[...]
        dn = dyv * g_ref[...]
        dx = res_ref[...] + r * (dn - xh * jnp.mean(dn * xh, axis=-1, keepdims=True))
        dx_ref[...] = dx
        dxb_ref[...] = dx.astype(dxb_ref.dtype)
        _acc_rows(dg_ref, jnp.sum(dyv * xh, axis=0, keepdims=True), pl.program_id(0) == 0)

    row = pl.BlockSpec((tr, D), lambda i: (i, 0))
    return _pcall(body, name=name, grid=(S // tr,),
                  in_specs=[row, pl.BlockSpec((1, D), lambda i: (0, 0)), row, row],
                  out_specs=(row, row, pl.BlockSpec((8, D), lambda i: (0, 0))),
                  out_shape=(jax.ShapeDtypeStruct((S, D), F32), jax.ShapeDtypeStruct((S, D), MM_DT),
                             jax.ShapeDtypeStruct((8, D), F32)),
                  compiler_params=_row_params())(x, g, dy, res)


def _final_loss(name, x, g, tgt):
    S, D = x.shape
    tr = _pick(S, 256, 16)

    def body(x_ref, g_ref, t_ref, dx_ref, dg_ref, loss_ref):
        xv = x_ref[...]
        r = lax.rsqrt(jnp.mean(xv * xv, axis=-1, keepdims=True) + EPS)
        xh = xv * r
        gv = g_ref[...]
        e = xh * gv - t_ref[...]
        part = 0.5 * jnp.sum(jnp.mean(e * e, axis=-1, keepdims=True), axis=0, keepdims=True)
        dy = e * (1.0 / D)
        dn = dy * gv
        dx_ref[...] = r * (dn - xh * jnp.mean(dn * xh, axis=-1, keepdims=True))
        first = pl.program_id(0) == 0
        _acc_rows(dg_ref, jnp.sum(dy * xh, axis=0, keepdims=True), first)
        _acc_rows(loss_ref, part, first)

    row = pl.BlockSpec((tr, D), lambda i: (i, 0))
    return _pcall(body, name=name, grid=(S // tr,),
                  in_specs=[row, pl.BlockSpec((1, D), lambda i: (0, 0)), row],
                  out_specs=(row, pl.BlockSpec((8, D), lambda i: (0, 0)), pl.BlockSpec((8, LANES), lambda i: (0, 0))),
                  out_shape=(jax.ShapeDtypeStruct((S, D), F32), jax.ShapeDtypeStruct((8, D), F32),
                             jax.ShapeDtypeStruct((8, LANES), F32)),
                  compiler_params=_row_params())(x, g, tgt)


def _rope_tables(cos, sin, off, w):
    S = cos.shape[0]
    h = w // 2
    z = lambda n: jnp.zeros((S, n), F32)
    C = jnp.concatenate([z(off), cos, cos, z(LANES - off - w)], axis=1)
    SP = jnp.concatenate([z(off + h), sin, z(LANES - off - w)], axis=1)
    SN = jnp.concatenate([z(off), -sin, z(LANES - off - h)], axis=1)
    return C, SP, SN


def _angles(pos, dim):
    inv = jnp.power(ROPE_THETA, -jnp.arange(0, dim, 2, dtype=F32) / dim)
    ang = pos.astype(F32)[:, None] * inv[None, :]
    return jnp.cos(ang), jnp.sin(ang)


def _all_tables(cfg):
    S = cfg.S
    pos = jnp.arange(S, dtype=F32)
    rows = S // cfg.GRID_W
    row = jnp.repeat(jnp.arange(rows, dtype=F32), cfg.GRID_W)
    col = jnp.tile(jnp.arange(cfg.GRID_W, dtype=F32), rows)
    ca, sa = _angles(pos, 64)
    cc, sc = _angles(pos, 128)
    cr, sr = _angles(row, 64)
    cl, sl = _angles(col, 64)
    t_b = tuple(a + b for a, b in zip(_rope_tables(cr, sr, 0, 64), _rope_tables(cl, sl, 64, 64)))
    return {"aq": (_rope_tables(ca, sa, 0, 64), 64), "akr": (_rope_tables(ca, sa, 64, 64), 64),
            "b": (t_b, 64), "c": (_rope_tables(cc, sc, 0, 128), 128)}


def _rope(x, C, SP, SN, w):
    h = w // 2
    if 2 * h == LANES:
        return x * C + pltpu.roll(x, h, 1) * (SP + SN)
    return x * C + pltpu.roll(x, h, 1) * SP + pltpu.roll(x, LANES - h, 1) * SN


def _rope_t(dy, C, SP, SN, w):
    h = w // 2
    if 2 * h == LANES:
        return dy * C + pltpu.roll(dy * (SP + SN), h, 1)
    return dy * C + pltpu.roll(dy * SP, LANES - h, 1) + pltpu.roll(dy * SN, h, 1)


def _grid2_params():
    return pltpu.CompilerParams(dimension_semantics=("arbitrary", "arbitrary"), vmem_limit_bytes=VMEM_LIMIT)


def _headprep_fwd(name, proj, col_off, nb, gain, tabs):
    S = proj.shape[0]
    tr = _pick(S, 1024, 16)
    cb = col_off // LANES
    norm, rope = gain is not None, tabs is not None
    w = tabs[1] if rope else 0

    def body(*refs):
        x_ref = refs[0]
        pos = 1
        xv = x_ref[...]
        if norm:
            r = lax.rsqrt(jnp.mean(xv * xv, axis=-1, keepdims=True) + EPS)
            xv = xv * r * refs[pos][...]
            pos += 1
        if rope:
            xv = _rope(xv, refs[pos][...], refs[pos + 1][...], refs[pos + 2][...], w)
            pos += 3
        refs[pos][...] = xv.astype(refs[pos].dtype)

    ops, in_specs = [proj], [pl.BlockSpec((tr, LANES), lambda i, j: (i, cb + j))]
    if norm:
        ops.append(gain)
        in_specs.append(pl.BlockSpec((1, LANES), lambda i, j: (0, 0)))
    if rope:
        ops += list(tabs[0])
        in_specs += [pl.BlockSpec((tr, LANES), lambda i, j: (i, 0))] * 3
    return _pcall(body, name=name, grid=(S // tr, nb), in_specs=in_specs,
                  out_specs=pl.BlockSpec((tr, LANES), lambda i, j: (i, j)),
                  out_shape=jax.ShapeDtypeStruct((S, nb * LANES), MM_DT), compiler_params=_grid2_params())(*ops)


def _headprep_bwd(name, dy, proj, col_off, nb, gain, tabs, dproj):
    S = proj.shape[0]
    tr = _pick(S, 1024, 16)
    cb = col_off // LANES
    norm, rope = gain is not None, tabs is not None
    w = tabs[1] if rope else 0

    def body(*refs):
        dz = refs[0][...]
        pos = 1
        if norm:
            x_ref, g_ref = refs[pos], refs[pos + 1]
            pos += 2
        if rope:
            dz = _rope_t(dz, refs[pos][...], refs[pos + 1][...], refs[pos + 2][...], w)
            pos += 3
        pos += 1
        o_ref = refs[pos]
        if norm:
            dg_ref = refs[pos + 1]
            xv = x_ref[...]
            r = lax.rsqrt(jnp.mean(xv * xv, axis=-1, keepdims=True) + EPS)
            n = xv * r
            first = (pl.program_id(0) == 0) & (pl.program_id(1) == 0)
            _acc_rows(dg_ref, jnp.sum(dz * n, axis=0, keepdims=True), first)
            dn = dz * g_ref[...]
            dz = r * (dn - n * jnp.mean(dn * n, axis=-1, keepdims=True))
        o_ref[...] = dz.astype(o_ref.dtype)

    ops, in_specs = [dy], [pl.BlockSpec((tr, LANES), lambda i, j: (i, j))]
    if norm:
        ops += [proj, gain]
        in_specs += [pl.BlockSpec((tr, LANES), lambda i, j: (i, cb + j)), pl.BlockSpec((1, LANES), lambda i, j: (0, 0))]
    if rope:
        ops += list(tabs[0])
        in_specs += [pl.BlockSpec((tr, LANES), lambda i, j: (i, 0))] * 3
    alias_idx = len(ops)
    ops.append(dproj)
    in_specs.append(pl.BlockSpec(memory_space=pl.ANY))
    out_specs = [pl.BlockSpec((tr, LANES), lambda i, j: (i, cb + j))]
    out_shape = [jax.ShapeDtypeStruct(dproj.shape, dproj.dtype)]
    if norm:
        out_specs.append(pl.BlockSpec((8, LANES), lambda i, j: (0, 0)))
        out_shape.append(jax.ShapeDtypeStruct((8, LANES), F32))
    res = _pcall(body, name=name, grid=(S // tr, nb), in_specs=in_specs, out_specs=tuple(out_specs),
                 out_shape=tuple(out_shape), input_output_aliases={alias_idx: 0}, compiler_params=_grid2_params())(*ops)
    return (res[0], res[1]) if norm else (res[0], None)


def _masked_rms(xv, lo, n):
    lane = lax.broadcasted_iota(jnp.int32, xv.shape, 1)
    xm = jnp.where((lane >= lo) & (lane < lo + n), xv, 0.0)
    r = lax.rsqrt(jnp.sum(xm * xm, axis=-1, keepdims=True) * (1.0 / n) + EPS)
    return xm * r, r


def _mla_prep_fwd(name, cfg, proj, gq, gkv, tabs):
    S = cfg.S
    tr = _pick(S, 256, 16)
    (C, SP, SN), w = tabs

    def body(p_ref, gq_ref, gkv_ref, c_ref, sp_ref, sn_ref, cq_ref, ckv_ref, kpe_ref):
        nq, _ = _masked_rms(p_ref[:, 0:cfg.QLP], 0, cfg.QL)
        cq_ref[...] = (nq * gq_ref[...]).astype(cq_ref.dtype)
        nk, _ = _masked_rms(p_ref[:, cfg.KV0:cfg.PW], cfg.KOFF, cfg.KVL)
        ckv_ref[...] = (nk * gkv_ref[...]).astype(ckv_ref.dtype)
        kr = _rope(p_ref[:, cfg.PW - LANES:cfg.PW], c_ref[...], sp_ref[...], sn_ref[...], w)
        kpe_ref[...] = pltpu.roll(kr, 64, 1).astype(kpe_ref.dtype)

    tab = pl.BlockSpec((tr, LANES), lambda i: (i, 0))
    return _pcall(body, name=name, grid=(S // tr,),
                  in_specs=[pl.BlockSpec((tr, cfg.PW), lambda i: (i, 0)), pl.BlockSpec((1, cfg.QLP), lambda i: (0, 0)),
                            pl.BlockSpec((1, cfg.KVW), lambda i: (0, 0)), tab, tab, tab],
                  out_specs=(pl.BlockSpec((tr, cfg.QLP), lambda i: (i, 0)), pl.BlockSpec((tr, cfg.KVW), lambda i: (i, 0)), tab),
                  out_shape=(jax.ShapeDtypeStruct((S, cfg.QLP), MM_DT), jax.ShapeDtypeStruct((S, cfg.KVW), MM_DT),
                             jax.ShapeDtypeStruct((S, LANES), MM_DT)),
                  compiler_params=_row_params())(proj, gq, gkv, C, SP, SN)


def _mla_prep_bwd(name, cfg, dcq, dckv, dkpe, proj, gq, gkv, tabs, dproj):
    S = cfg.S
    tr = _pick(S, 256, 16)
    (C, SP, SN), w = tabs

    def body(dcq_ref, dckv_ref, dkpe_ref, p_ref, gq_ref, gkv_ref, c_ref, sp_ref, sn_ref, buf_ref, o_ref, dgq_ref, dgkv_ref):
        first = pl.program_id(0) == 0

        def norm_bwd(xv, lo, n, dz, g_ref, dg_ref):
            nrm, r = _masked_rms(xv, lo, n)
            _acc_rows(dg_ref, jnp.sum(dz * nrm, axis=0, keepdims=True), first)
            dn = dz * g_ref[...]
            return r * (dn - nrm * (jnp.sum(dn * nrm, axis=-1, keepdims=True) * (1.0 / n)))

        dxq = norm_bwd(p_ref[:, 0:cfg.QLP], 0, cfg.QL, dcq_ref[...], gq_ref, dgq_ref)
        dxk = norm_bwd(p_ref[:, cfg.KV0:cfg.PW], cfg.KOFF, cfg.KVL, dckv_ref[...], gkv_ref, dgkv_ref)
        dxr = _rope_t(pltpu.roll(dkpe_ref[...], 64, 1), c_ref[...], sp_ref[...], sn_ref[...], w)
        for cidx in range(cfg.PW // LANES):
            lo = cidx * LANES
            parts = []
            if lo < cfg.QLP:
                parts.append(dxq[:, lo:lo + LANES])
            if lo >= cfg.KV0:
                parts.append(dxk[:, lo - cfg.KV0:lo - cfg.KV0 + LANES])
            if lo == cfg.PW - LANES:
                parts.append(dxr)
            o_ref[:, lo:lo + LANES] = functools.reduce(lambda a, b: a + b, parts).astype(o_ref.dtype)

    tab = pl.BlockSpec((tr, LANES), lambda i: (i, 0))
    res = _pcall(body, name=name, grid=(S // tr,),
                 in_specs=[pl.BlockSpec((tr, cfg.QLP), lambda i: (i, 0)), pl.BlockSpec((tr, cfg.KVW), lambda i: (i, 0)), tab,
                           pl.BlockSpec((tr, cfg.PW), lambda i: (i, 0)), pl.BlockSpec((1, cfg.QLP), lambda i: (0, 0)),
                           pl.BlockSpec((1, cfg.KVW), lambda i: (0, 0)), tab, tab, tab, pl.BlockSpec(memory_space=pl.ANY)],
                 out_specs=(pl.BlockSpec((tr, cfg.PW), lambda i: (i, 0)), pl.BlockSpec((8, cfg.QLP), lambda i: (0, 0)),
                            pl.BlockSpec((8, cfg.KVW), lambda i: (0, 0))),
                 out_shape=(jax.ShapeDtypeStruct(dproj.shape, dproj.dtype), jax.ShapeDtypeStruct((8, cfg.QLP), F32),
                            jax.ShapeDtypeStruct((8, cfg.KVW), F32)),
                 input_output_aliases={9: 0}, compiler_params=_row_params())(dcq, dckv, dkpe, proj, gq, gkv, C, SP, SN, dproj)
    return res


def _mla_build_fwd(name, cfg, qa, kva, kpe, tabs):
    S, AH = cfg.S, cfg.AH
    tr = _pick(S, 256, 16)
    (C, SP, SN), w = tabs

    def body(qa_ref, kva_ref, kpe_ref, c_ref, sp_ref, sn_ref, q_ref, k_ref, v_ref):
        for h in range(AH):
            a, b = 256 * h, 256 * h + LANES
            q_ref[:, a:b] = qa_ref[:, a:b].astype(q_ref.dtype)
            q_ref[:, b:b + LANES] = _rope(qa_ref[:, b:b + LANES], c_ref[...], sp_ref[...], sn_ref[...], w).astype(q_ref.dtype)
            k_ref[:, a:b] = kva_ref[:, a:b].astype(k_ref.dtype)
            k_ref[:, b:b + LANES] = kpe_ref[...]
            v_ref[:, LANES * h:LANES * (h + 1)] = kva_ref[:, b:b + LANES].astype(v_ref.dtype)

    tab = pl.BlockSpec((tr, LANES), lambda i: (i, 0))
    wide = pl.BlockSpec((tr, AH * 256), lambda i: (i, 0))
    return _pcall(body, name=name, grid=(S // tr,), in_specs=[wide, wide, tab, tab, tab, tab],
                  out_specs=(wide, wide, pl.BlockSpec((tr, AH * LANES), lambda i: (i, 0))),
                  out_shape=(jax.ShapeDtypeStruct((S, AH * 256), MM_DT), jax.ShapeDtypeStruct((S, AH * 256), MM_DT),
                             jax.ShapeDtypeStruct((S, AH * LANES), MM_DT)),
                  compiler_params=_row_params())(qa, kva, kpe, C, SP, SN)


def _mla_build_bwd(name, cfg, dq, dk, dv, tabs):
    S, AH = cfg.S, cfg.AH
    tr = _pick(S, 256, 16)
    (C, SP, SN), w = tabs

    def body(dq_ref, dk_ref, dv_ref, c_ref, sp_ref, sn_ref, dqa_ref, dkva_ref, dkpe_ref):
        dkpe = None
        for h in range(AH):
            a, b = 256 * h, 256 * h + LANES
            dqa_ref[:, a:b] = dq_ref[:, a:b].astype(dqa_ref.dtype)
            dqa_ref[:, b:b + LANES] = _rope_t(dq_ref[:, b:b + LANES], c_ref[...], sp_ref[...], sn_ref[...], w).astype(dqa_ref.dtype)
            dkva_ref[:, a:b] = dk_ref[:, a:b].astype(dkva_ref.dtype)
            dkva_ref[:, b:b + LANES] = dv_ref[:, LANES * h:LANES * (h + 1)].astype(dkva_ref.dtype)
            part = dk_ref[:, b:b + LANES]
            dkpe = part if dkpe is None else dkpe + part
        dkpe_ref[...] = dkpe

    tab = pl.BlockSpec((tr, LANES), lambda i: (i, 0))
    wide = pl.BlockSpec((tr, AH * 256), lambda i: (i, 0))
    return _pcall(body, name=name, grid=(S // tr,),
                  in_specs=[wide, wide, pl.BlockSpec((tr, AH * LANES), lambda i: (i, 0)), tab, tab, tab],
                  out_specs=(wide, wide, tab),
                  out_shape=(jax.ShapeDtypeStruct((S, AH * 256), MM_DT), jax.ShapeDtypeStruct((S, AH * 256), MM_DT),
                             jax.ShapeDtypeStruct((S, LANES), F32)),
                  compiler_params=_row_params())(dq, dk, dv, C, SP, SN)


def _outnorm_fwd(name, cfg, oa, ob, oc, g):
    S = cfg.S
    tr = _pick(S, 256, 16)
    widths = (cfg.AW, cfg.BW, cfg.CW)

    def body(a_ref, b_ref, c_ref, g_ref, o_ref):
        off = 0
        for ref, wd in zip((a_ref, b_ref, c_ref), widths):
            v = ref[...]
            r = lax.rsqrt(jnp.mean(v * v, axis=-1, keepdims=True) + EPS)
            o_ref[:, off:off + wd] = (v * r * g_ref[:, off:off + wd]).astype(o_ref.dtype)
            off += wd

    return _pcall(body, name=name, grid=(S // tr,),
                  in_specs=[pl.BlockSpec((tr, wd), lambda i: (i, 0)) for wd in widths] + [pl.BlockSpec((1, cfg.MIX), lambda i: (0, 0))],
                  out_specs=pl.BlockSpec((tr, cfg.MIX), lambda i: (i, 0)),
                  out_shape=jax.ShapeDtypeStruct((S, cfg.MIX), MM_DT), compiler_params=_row_params())(oa, ob, oc, g)


def _outnorm_bwd(name, cfg, dmix, oa, ob, oc, g):
    S = cfg.S
    tr = _pick(S, 256, 16)
    widths = (cfg.AW, cfg.BW, cfg.CW)

    def body(dm_ref, a_ref, b_ref, c_ref, g_ref, da_ref, db_ref, dc_ref, dg_ref):
        off = 0
        parts = []
        for ref, dref, wd in zip((a_ref, b_ref, c_ref), (da_ref, db_ref, dc_ref), widths):
            v = ref[...]
            r = lax.rsqrt(jnp.mean(v * v, axis=-1, keepdims=True) + EPS)
            n = v * r
            dm = dm_ref[:, off:off + wd]
            parts.append(jnp.sum(dm * n, axis=0, keepdims=True))
            dn = dm * g_ref[:, off:off + wd]
            dref[...] = r * (dn - n * jnp.mean(dn * n, axis=-1, keepdims=True))
            off += wd
        _acc_rows(dg_ref, jnp.concatenate(parts, axis=1), pl.program_id(0) == 0)

    segs = [pl.BlockSpec((tr, wd), lambda i: (i, 0)) for wd in widths]
    return _pcall(body, name=name, grid=(S // tr,),
                  in_specs=[pl.BlockSpec((tr, cfg.MIX), lambda i: (i, 0))] + segs + [pl.BlockSpec((1, cfg.MIX), lambda i: (0, 0))],
                  out_specs=tuple(segs) + (pl.BlockSpec((8, cfg.MIX), lambda i: (0, 0)),),
                  out_shape=tuple(jax.ShapeDtypeStruct((S, wd), F32) for wd in widths) + (jax.ShapeDtypeStruct((8, cfg.MIX), F32),),
                  compiler_params=_row_params())(dmix, oa, ob, oc, g)


def _band_bias(cfg):
    tq, tk, W = cfg.TB, cfg.TBK, cfg.W
    ns = 2 * W + 1
    shape = ((tk // tq) * ns, tq, tk)
    slab = lax.broadcasted_iota(jnp.int32, shape, 0)
    row = lax.broadcasted_iota(jnp.int32, shape, 1)
    col = lax.broadcasted_iota(jnp.int32, shape, 2)
    d = (slab // ns) * tq + (W - slab % ns) * tk + row - col
    ad = jnp.abs(d)
    m = jnp.zeros(d.shape, F32)
    for reach, dil in cfg.branches:
        ok = ad <= reach
        if dil > 1:
            ok = ok & ((d & (dil - 1)) == 0)
        m = m + ok.astype(F32)
    return jnp.where(m > 0, jnp.log(jnp.maximum(m, 1.0)), NEG)


def _attn_params():
    return pltpu.CompilerParams(dimension_semantics=("parallel", "parallel", "arbitrary"), vmem_limit_bytes=VMEM_LIMIT)


def _scores(q_ref, k_ref, scale, bias_ref):
    s = lax.dot_general(q_ref[...], k_ref[...], NT, preferred_element_type=F32) * scale
    return s if bias_ref is None else s + bias_ref[...]


def _flash_fwd(name, q, k, v, H, G, dk, dv, scale, bias=None, W=None):
    S = q.shape[0]
    band = bias is not None
    tq = bias.shape[1] if band else _pick(S, ATT_TQ, LANES)
    tk = bias.shape[2] if band else _pick(S, ATT_TK_FWD, LANES)
    n = S // tk
    nsteps = 2 * W + 1 if band else n
    R = tk // tq

    def kblock(qi, st):
        return jnp.clip(qi // R - W + st, 0, n - 1) if band else st

    def body(*refs):
        q_ref, k_ref, v_ref = refs[:3]
        bias_ref = refs[3] if band else None
        o_ref, lse_ref, m_sc, l_sc, acc_sc = refs[-5:]
        qi, st = pl.program_id(1), pl.program_id(2)

        @pl.when(st == 0)
        def _():
            m_sc[...] = jnp.full_like(m_sc, NEG)
            l_sc[...] = jnp.zeros_like(l_sc)
            acc_sc[...] = jnp.zeros_like(acc_sc)

        kj = qi // R - W + st if band else st

        def step():
            s = _scores(q_ref, k_ref, scale, bias_ref)
            m_prev = m_sc[...]
            m_new = jnp.maximum(m_prev, jnp.max(s, axis=-1, keepdims=True))
            alpha = jnp.exp(m_prev - m_new)
            p = jnp.exp(s - m_new)
            l_sc[...] = alpha * l_sc[...] + jnp.sum(p, axis=-1, keepdims=True)
            acc_sc[...] = alpha * acc_sc[...] + lax.dot_general(p.astype(MM_DT), v_ref[...], NN, preferred_element_type=F32)
            m_sc[...] = m_new

        if band:
            pl.when((kj >= 0) & (kj < n))(step)
        else:
            step()

        @pl.when(st == nsteps - 1)
        def _():
            l = l_sc[...]
            o_ref[...] = acc_sc[...] / l
            lse_ref[...] = jnp.broadcast_to(m_sc[...] + jnp.log(l), lse_ref.shape)

    in_specs = [pl.BlockSpec((tq, dk), lambda h, qi, st: (qi, h)),
                pl.BlockSpec((tk, dk), lambda h, qi, st: (kblock(qi, st), h // G)),
                pl.BlockSpec((tk, dv), lambda h, qi, st: (kblock(qi, st), h // G))]
    ops = [q, k, v]
    if band:
        in_specs.append(pl.BlockSpec((None, tq, tk), lambda h, qi, st: ((qi % R) * nsteps + st, 0, 0)))
        ops.append(bias)
    return _pcall(body, name=name, grid=(H, S // tq, nsteps), in_specs=in_specs,
                  out_specs=(pl.BlockSpec((tq, dv), lambda h, qi, st: (qi, h)),
                             pl.BlockSpec((None, tq, LANES), lambda h, qi, st: (h, qi, 0))),
                  out_shape=(jax.ShapeDtypeStruct((S, H * dv), F32), jax.ShapeDtypeStruct((H, S, LANES), F32)),
                  scratch_shapes=[pltpu.VMEM((tq, 1), F32), pltpu.VMEM((tq, 1), F32), pltpu.VMEM((tq, dv), F32)],
                  compiler_params=_attn_params())(*ops)


def _flash_dq(name, q, k, v, do, o, lse, H, G, dk, dv, scale, bias=None, W=None):
    S = q.shape[0]
    band = bias is not None
    tq = bias.shape[1] if band else _pick(S, ATT_TQ, LANES)
    tk = bias.shape[2] if band else _pick(S, ATT_TK, LANES)
    n = S // tk
    nsteps = 2 * W + 1 if band else n
    R = tk // tq

    def kblock(qi, st):
        return jnp.clip(qi // R - W + st, 0, n - 1) if band else st

    def body(*refs):
        q_ref, k_ref, v_ref, do_ref, o_ref, lse_ref = refs[:6]
        bias_ref = refs[6] if band else None
        dq_ref, delta_sc, acc_sc = refs[-3:]
        qi, st = pl.program_id(1), pl.program_id(2)

        @pl.when(st == 0)
        def _():
            delta_sc[...] = jnp.sum(do_ref[...] * o_ref[...], axis=-1, keepdims=True)
            acc_sc[...] = jnp.zeros_like(acc_sc)

        kj = qi // R - W + st if band else st

        def step():
            p = jnp.exp(_scores(q_ref, k_ref, scale, bias_ref) - lse_ref[:, 0:1])
            dp = lax.dot_general(do_ref[...].astype(MM_DT), v_ref[...], NT, preferred_element_type=F32)
            ds = p * (dp - delta_sc[...]) * scale
            acc_sc[...] += lax.dot_general(ds.astype(MM_DT), k_ref[...], NN, preferred_element_type=F32)

        if band:
            pl.when((kj >= 0) & (kj < n))(step)
        else:
            step()

        @pl.when(st == nsteps - 1)
        def _():
            dq_ref[...] = acc_sc[...]

    qspec = lambda wd: pl.BlockSpec((tq, wd), lambda h, qi, st: (qi, h))
    in_specs = [qspec(dk),
                pl.BlockSpec((tk, dk), lambda h, qi, st: (kblock(qi, st), h // G)),
                pl.BlockSpec((tk, dv), lambda h, qi, st: (kblock(qi, st), h // G)),
                qspec(dv), qspec(dv),
                pl.BlockSpec((None, tq, LANES), lambda h, qi, st: (h, qi, 0))]
    ops = [q, k, v, do, o, lse]
    if band:
        in_specs.append(pl.BlockSpec((None, tq, tk), lambda h, qi, st: ((qi % R) * nsteps + st, 0, 0)))
        ops.append(bias)
    return _pcall(body, name=name, grid=(H, S // tq, nsteps), in_specs=in_specs,
                  out_specs=qspec(dk), out_shape=jax.ShapeDtypeStruct((S, H * dk), F32),
                  scratch_shapes=[pltpu.VMEM((tq, 1), F32), pltpu.VMEM((tq, dk), F32)],
                  compiler_params=_attn_params())(*ops)


def _flash_dkv(name, q, k, v, do, o, lse, H, G, dk, dv, scale, bias=None, W=None):
    S = q.shape[0]
    band = bias is not None
    tq = bias.shape[1] if band else _pick(S, ATT_TQ, LANES)
    tk = bias.shape[2] if band else _pick(S, ATT_TK_DKV, LANES)
    n = S // tq
    R = tk // tq
    nq = R * (2 * W + 1) if band else n
    nsteps = G * nq
    Hkv = H // G

    def qhead(hk, st):
        return hk * G + st // nq

    def qblock(kj, st):
        return jnp.clip(R * (kj - W) + st % nq, 0, n - 1) if band else st % nq

    def body(*refs):
        q_ref, k_ref, v_ref, do_ref, o_ref, lse_ref = refs[:6]
        bias_ref = refs[6] if band else None
        dk_ref, dv_ref, dk_sc, dv_sc = refs[-4:]
        kj, st = pl.program_id(1), pl.program_id(2)

        @pl.when(st == 0)
        def _():
            dk_sc[...] = jnp.zeros_like(dk_sc)
            dv_sc[...] = jnp.zeros_like(dv_sc)

        qi = R * (kj - W) + st % nq if band else st % nq

        def step():
            p = jnp.exp(_scores(q_ref, k_ref, scale, bias_ref) - lse_ref[:, 0:1])
            dof = do_ref[...]
            dob = dof.astype(MM_DT)
            dv_sc[...] += lax.dot_general(p.astype(MM_DT), dob, TN, preferred_element_type=F32)
            dp = lax.dot_general(dob, v_ref[...], NT, preferred_element_type=F32)
            delta = jnp.sum(dof * o_ref[...], axis=-1, keepdims=True)
            ds = p * (dp - delta) * scale
            dk_sc[...] += lax.dot_general(ds.astype(MM_DT), q_ref[...], TN, preferred_element_type=F32)

        if band:
            pl.when((qi >= 0) & (qi < n))(step)
        else:
            step()

        @pl.when(st == nsteps - 1)
        def _():
            dk_ref[...] = dk_sc[...]
            dv_ref[...] = dv_sc[...]

    qspec = lambda wd: pl.BlockSpec((tq, wd), lambda hk, kj, st: (qblock(kj, st), qhead(hk, st)))
    kspec = lambda wd: pl.BlockSpec((tk, wd), lambda hk, kj, st: (kj, hk))
    in_specs = [qspec(dk), kspec(dk), kspec(dv), qspec(dv), qspec(dv),
                pl.BlockSpec((None, tq, LANES), lambda hk, kj, st: (qhead(hk, st), qblock(kj, st), 0))]
    ops = [q, k, v, do, o, lse]
    if band:
        in_specs.append(pl.BlockSpec((None, tq, tk),
                                     lambda hk, kj, st: ((st % nq % R) * (2 * W + 1) + 2 * W - (st % nq) // R, 0, 0)))
        ops.append(bias)
    return _pcall(body, name=name, grid=(Hkv, S // tk, nsteps), in_specs=in_specs,
                  out_specs=(kspec(dk), kspec(dv)),
                  out_shape=(jax.ShapeDtypeStruct((S, Hkv * dk), F32), jax.ShapeDtypeStruct((S, Hkv * dv), F32)),
                  scratch_shapes=[pltpu.VMEM((tk, dk), F32), pltpu.VMEM((tk, dv), F32)],
                  compiler_params=_attn_params())(*ops)


def _rowtile(rows, cols):
    return _pick(rows, max(16, (512 * 1024) // cols // 16 * 16), 16)


def _cast_rows(name, w, dtype):
    R, C = w.shape
    tr = _rowtile(R, C)

    def body(w_ref, o_ref):
        o_ref[...] = w_ref[...].astype(o_ref.dtype)

    spec = pl.BlockSpec((tr, C), lambda i: (i, 0))
    return _pcall(body, name=name, grid=(R // tr,), in_specs=[spec], out_specs=spec,
                  out_shape=jax.ShapeDtypeStruct((R, C), dtype), compiler_params=_row_params())(w)


def _adamw_math(wv, gv, mv, vv):
    bc1 = 1.0 - ADAM_B1 ** ADAM_STEP
    bc2 = 1.0 - ADAM_B2 ** ADAM_STEP
    mn = ADAM_B1 * mv + (1.0 - ADAM_B1) * gv
    vn = ADAM_B2 * vv + (1.0 - ADAM_B2) * jnp.square(gv)
    m_hat = mn / bc1
    v_hat = vn / bc2
    return -ADAM_LR * (m_hat / (jnp.sqrt(v_hat) + ADAM_EPS) + ADAM_WD * wv), mn, vn


def _adamw_halves(name, w, g2, m, v, c_arr):
    depth, _, R, C = w.shape
    tr = _rowtile(R, C)

    def body(c_ref, w_ref, g_ref, m_ref, v_ref, go_ref, d_ref, nm_ref, nv_ref):
        gv = g_ref[...]
        go_ref[...] = gv
        d_ref[...], nm_ref[...], nv_ref[...] = _adamw_math(w_ref[...], gv, m_ref[...], v_ref[...])

    spec = pl.BlockSpec((None, None, tr, C), lambda l, h, r, c_ref: (l, h, r, 0))
    gspec = pl.BlockSpec((None, None, tr, C), lambda l, h, r, c_ref: (l, (h + c_ref[0]) % 2, r, 0))
    sds = jax.ShapeDtypeStruct(w.shape, F32)
    grid_spec = pltpu.PrefetchScalarGridSpec(num_scalar_prefetch=1, grid=(depth, 2, R // tr), in_specs=[spec, gspec, spec, spec],
                                             out_specs=(spec,) * 4)
    return _pcall(body, name=name, grid_spec=grid_spec, out_shape=(sds,) * 4,
                  compiler_params=pltpu.CompilerParams(dimension_semantics=("arbitrary",) * 3, vmem_limit_bytes=VMEM_LIMIT))(c_arr, w, g2, m, v)


def _adamw(name, w, g, m, v):
    R, C = w.shape
    tr = _rowtile(R, C)

    def body(w_ref, g_ref, m_ref, v_ref, d_ref, nm_ref, nv_ref):
        d_ref[...], nm_ref[...], nv_ref[...] = _adamw_math(w_ref[...], g_ref[...], m_ref[...], v_ref[...])

    spec = pl.BlockSpec((tr, C), lambda i: (i, 0))
    sds = jax.ShapeDtypeStruct((R, C), F32)
    return _pcall(body, name=name, grid=(R // tr,), in_specs=[spec] * 4, out_specs=(spec,) * 3,
                  out_shape=(sds, sds, sds), compiler_params=_row_params())(w, g, m, v)


HBM_SPEC = pl.BlockSpec(memory_space=pltpu.HBM)


def _place():
    x, y, c = lax.axis_index("x"), lax.axis_index("y"), lax.axis_index("c")
    chips = [(1 - x, y), (x, 1 - y), (1 - x, 1 - y)]
    return x, y, c, chips


def _allgather_body(ins, outs, send, recv, handshake):
    n = len(ins)
    x, y, c, chips = _place()
    me = 2 * x + y
    sib = (x, y, 1 - c)
    if handshake:
        barrier = pltpu.get_barrier_semaphore()
        for peer in [(chip[0], chip[1], c) for chip in chips] + [sib]:
            pl.semaphore_signal(barrier, inc=1, device_id=peer, device_id_type=MESH)
        pl.semaphore_wait(barrier, 4)

    def rcopy(src, dst, k, to):
        return pltpu.make_async_remote_copy(src_ref=src, dst_ref=dst, send_sem=send.at[k], recv_sem=recv.at[k],
                                            device_id=to, device_id_type=MESH)

    def rows(t, cc):
        hr = ins[t].shape[0] // 2
        return pl.ds(cc * hr, hr)

    sends = []
    for t in range(n):
        for j, chip in enumerate(chips):
            cp = rcopy(ins[t].at[rows(t, c)], outs[t].at[me, rows(t, c)], 7 * t + j, (chip[0], chip[1], c))
            cp.start()
            sends.append(cp)
    for t in range(n):
        cp = rcopy(ins[t], outs[t].at[me], 7 * t + 6, sib)
        cp.start()
        sends.append(cp)
    for t in range(n):
        for j, chip in enumerate(chips):
            slab = outs[t].at[2 * chip[0] + chip[1], rows(t, c)]
            rcopy(slab, slab, 7 * t + j, (chip[0], chip[1], c)).wait_recv()
            fw = rcopy(slab, slab, 7 * t + 3 + j, sib)
            fw.start()
            sends.append(fw)
    for t in range(n):
        rcopy(ins[t], outs[t].at[me], 7 * t + 6, sib).wait_recv()
        for j, chip in enumerate(chips):
            slab = outs[t].at[2 * chip[0] + chip[1], rows(t, 1 - c)]
            rcopy(slab, slab, 7 * t + 3 + j, sib).wait_recv()
    for cp in sends:
        cp.wait_send()


def _allgather_layer(name, shards):
    n = len(shards)

    def body(*refs):
        _allgather_body(refs[:n], refs[n:2 * n], refs[2 * n], refs[2 * n + 1], False)

    return _pcall(body, name=name, in_specs=[HBM_SPEC] * n, out_specs=tuple([HBM_SPEC] * n),
                  out_shape=tuple(jax.ShapeDtypeStruct((N_SHARD,) + s.shape, s.dtype) for s in shards),
                  scratch_shapes=[pltpu.SemaphoreType.DMA((7 * n,)), pltpu.SemaphoreType.DMA((7 * n,))])(*shards)


def _allgather_layer_async(name, shards, collective_id):
    n = len(shards)
    in_refs = [jax.new_ref(s, memory_space=pltpu.MemorySpace.HBM) for s in shards]
    out_refs = [jax.empty_ref(jax.ShapeDtypeStruct((N_SHARD,) + s.shape, s.dtype), memory_space=pltpu.MemorySpace.HBM)
                for s in shards]

    @pl.kernel(mesh=plsc.ScalarSubcoreMesh(axis_name="seq", num_cores=1), name=name,
               scratch_types=(pltpu.SemaphoreType.DMA((7 * n,)), pltpu.SemaphoreType.DMA((7 * n,))),
               compiler_params=pltpu.CompilerParams(collective_id=collective_id))
    def launch(send, recv):
        _allgather_body(in_refs, out_refs, send, recv, True)

    launch()
    return [r[...] for r in out_refs]


def _flip(x, y, c, r):
    return (1 - x if r & 4 else x, 1 - y if r & 2 else y, 1 - c if r & 1 else c)


def _grad_exchange_async(name, gls, collective_id):
    n = len(gls)
    in_refs = [jax.new_ref(g, memory_space=pltpu.MemorySpace.HBM) for g in gls]
    out_refs = [jax.empty_ref(jax.ShapeDtypeStruct((7, g.shape[1] // 2, g.shape[2]), g.dtype), memory_space=pltpu.MemorySpace.HBM)
                for g in gls]

    @pl.kernel(mesh=plsc.ScalarSubcoreMesh(axis_name="seq", num_cores=1), name=name,
               scratch_types=(pltpu.SemaphoreType.DMA((7 * n,)), pltpu.SemaphoreType.DMA((7 * n,))),
               compiler_params=pltpu.CompilerParams(collective_id=collective_id))
    def launch(send, recv):
        _grad_exchange_body(in_refs, out_refs, send, recv, True)

    launch()
    return [r[...] for r in out_refs]


def _grad_exchange_body(ins, outs, send, recv, handshake):
    x, y, c, _ = _place()
    peers = [_flip(x, y, c, r) for r in range(1, 8)]
    if handshake:
        barrier = pltpu.get_barrier_semaphore()
        for peer in peers:
            pl.semaphore_signal(barrier, inc=1, device_id=peer, device_id_type=MESH)
        pl.semaphore_wait(barrier, 7)
    cps = []
    for t in range(len(ins)):
        hr = ins[t].shape[1] // 2
        for j, (px, py, pc) in enumerate(peers):
            cp = pltpu.make_async_remote_copy(src_ref=ins[t].at[2 * px + py, pl.ds(pc * hr, hr)], dst_ref=outs[t].at[j],
                                              send_sem=send.at[7 * t + j], recv_sem=recv.at[7 * t + j],
                                              device_id=(px, py, pc), device_id_type=MESH)
            cp.start()
            cps.append(cp)
    for cp in cps:
        cp.wait()


def _add_eight(name, own, slots, dev_arr, l, buf):
    ns, Ks, Ns = own.shape
    hr = Ks // 2
    tr = _rowtile(hr, Ns)
    fresh = isinstance(buf, tuple)

    def body(pl_ref, own_ref, *refs):
        acc = own_ref[...].astype(F32)
        for s_ref in refs[:7]:
            acc = acc + s_ref[...].astype(F32)
        refs[-1][...] = acc

    in_specs = [pl.BlockSpec((None, tr, Ns), lambda r, pl_ref: (pl_ref[0], r, 0))]
    in_specs += [pl.BlockSpec((None, tr, Ns), functools.partial(lambda r, pl_ref, j: (j, r, 0), j=j)) for j in range(7)]
    ops = [dev_arr, own.reshape(ns * 2, hr, Ns)] + [slots] * 7
    if not fresh:
        in_specs.append(pl.BlockSpec(memory_space=pl.ANY))
        ops.append(buf)
    grid_spec = pltpu.PrefetchScalarGridSpec(num_scalar_prefetch=1, grid=(hr // tr,), in_specs=in_specs,
                                             out_specs=pl.BlockSpec((None, None, tr, Ns), lambda r, pl_ref: (l, 0, r, 0)))
    return _pcall(body, name=name, grid_spec=grid_spec, out_shape=jax.ShapeDtypeStruct(buf if fresh else buf.shape, F32),
                  input_output_aliases={} if fresh else {9: 0}, compiler_params=_row_params())(*ops)


def _share_reduced(gs):
    n = len(gs)

    def body(*refs):
        ins, outs = refs[:n], refs[n:2 * n]
        send, recv = refs[2 * n:]
        x, y, c, _ = _place()
        cps = []
        for t in range(n):
            cp = pltpu.make_async_remote_copy(src_ref=ins[t].at[:, 0], dst_ref=outs[t].at[:, 1], send_sem=send.at[t], recv_sem=recv.at[t],
                                              device_id=(x, y, 1 - c), device_id_type=MESH)
            cp.start()
            cps.append(cp)
        for cp in cps:
            cp.wait()

    return _pcall(body, name="rs_share_reduced", in_specs=[HBM_SPEC] * n, out_specs=tuple([HBM_SPEC] * n),
                  out_shape=tuple(jax.ShapeDtypeStruct(g.shape, g.dtype) for g in gs),
                  input_output_aliases={t: t for t in range(n)},
                  scratch_shapes=[pltpu.SemaphoreType.DMA((n,)), pltpu.SemaphoreType.DMA((n,))])(*gs)


def _allreduce_small(vec):
    R = vec.shape[0]

    def body(v_ref, o_ref, buf, send, recv):
        x, y, c, _ = _place()
        me = 4 * x + 2 * y + c
        buf[me] = v_ref[...]
        cps = []
        for r in range(1, 8):
            fx, fy, fc = (r >> 2) & 1, (r >> 1) & 1, r & 1
            to = (1 - x if fx else x, 1 - y if fy else y, 1 - c if fc else c)
            cp = pltpu.make_async_remote_copy(src_ref=v_ref, dst_ref=buf.at[me], send_sem=send.at[r - 1], recv_sem=recv.at[r - 1],
                                              device_id=to, device_id_type=MESH)
            cp.start()
            cps.append(cp)
        for r in range(1, 8):
            fx, fy, fc = (r >> 2) & 1, (r >> 1) & 1, r & 1
            frm = (1 - x if fx else x, 1 - y if fy else y, 1 - c if fc else c)
            src = 4 * frm[0] + 2 * frm[1] + frm[2]
            pltpu.make_async_remote_copy(src_ref=v_ref, dst_ref=buf.at[src], send_sem=send.at[r - 1], recv_sem=recv.at[r - 1],
                                         device_id=frm, device_id_type=MESH).wait_recv()
        for cp in cps:
            cp.wait_send()
        acc = buf[0]
        for i in range(1, 8):
            acc = acc + buf[i]
        o_ref[...] = acc

    vm = pl.BlockSpec(memory_space=pltpu.VMEM)
    return _pcall(body, name="allreduce_small", in_specs=[vm], out_specs=vm, out_shape=jax.ShapeDtypeStruct((R, LANES), F32),
                  scratch_shapes=[pltpu.VMEM((8, R, LANES), F32), pltpu.SemaphoreType.DMA((7,)), pltpu.SemaphoreType.DMA((7,))])(vec)


def _unshard_cols(wg):
    ns, depth, K, Ns = wg.shape
    return jnp.moveaxis(wg, 0, 2).reshape(depth, K, ns * Ns)


def _shard_cols(w):
    K, N = w.shape
    return jnp.moveaxis(w.reshape(K, N_SHARD, N // N_SHARD), 1, 0)


def _uq_padded(cfg, wuq_g):
    depth = wuq_g.shape[1]
    w = _unshard_cols(wuq_g).reshape(depth, cfg.QL, cfg.AH, 192)
    w = jnp.pad(w, ((0, 0), (0, cfg.QLP - cfg.QL), (0, 0), (0, 64)))
    return w.reshape(1, depth, cfg.QLP, cfg.AH * 256)


def _uq_grad_unpadded(cfg, dw):
    w = dw[:cfg.QL].reshape(cfg.QL, cfg.AH, 256)[:, :, :192].reshape(cfg.QL, cfg.UQ)
    return _shard_cols(w)


def _ukv_padded(cfg, wukv_g):
    w = _unshard_cols(wukv_g)
    w = jnp.pad(w, ((0, 0), (cfg.KOFF, cfg.KVW - cfg.KVL - cfg.KOFF), (0, 0)))
    return w[None]


def _ukv_grad_unpadded(cfg, dw):
    return _shard_cols(dw[cfg.KOFF:cfg.KOFF + cfg.KVL])


def _pad_lanes(v, lo, total):
    return jnp.pad(v, (lo, total - lo - v.shape[0]))[None]


def _layer_fwd(cfg, l, x, W, small, tabs):
    ln1, gq, gkv, gqn, gkn, gout, ln2 = small
    sc_a, sc_h = 1.0 / math.sqrt(192), 1.0 / math.sqrt(128)
    n = f"l{l}_"
    h = _rms_fwd(n + "ln1", x, ln1)
    proj = _mm_nn(n + "proj", h, (W["w_in"], "col", 0))
    cqn, ckvn, kpe = _mla_prep_fwd(n + "mla_prep", cfg, proj, gq, gkv, tabs["akr"])
    qa = _mm_nn(n + "uq", cqn, (W["uq_p"], "col", 0))
    kva = _mm_nn(n + "ukv", ckvn, (W["ukv_p"], "col", 0))
    q_a, k_a, v_a = _mla_build_fwd(n + "mla_build", cfg, qa, kva, kpe, tabs["aq"])
    o_a, lse_a = _flash_fwd(n + "attn_a", q_a, k_a, v_a, cfg.AH, 1, 256, 128, sc_a)
    q_b = _headprep_fwd(n + "bq", proj, cfg.o_bq, cfg.BH, gqn, tabs["b"])
    k_b = _headprep_fwd(n + "bk", proj, cfg.o_bk, cfg.BKV, gkn, tabs["b"])
    v_b = _headprep_fwd(n + "bv", proj, cfg.o_bv, cfg.BKV, None, None)
    o_b, lse_b = _flash_fwd(n + "attn_b", q_b, k_b, v_b, cfg.BH, cfg.G, 128, 128, sc_h)
    q_c = _headprep_fwd(n + "cq", proj, cfg.o_cq, cfg.CH, None, tabs["c"])
    k_c = _headprep_fwd(n + "ck", proj, cfg.o_ck, cfg.CH, None, tabs["c"])
    v_c = _headprep_fwd(n + "cv", proj, cfg.o_cv, cfg.CH, None, None)
    o_c, lse_c = _flash_fwd(n + "attn_c", q_c, k_c, v_c, cfg.CH, 1, 128, 128, sc_h, tabs["bias_c"], cfg.W)
    mixed = _outnorm_fwd(n + "outnorm", cfg, o_a, o_b, o_c, gout)
    x1 = _mm_nn(n + "out", mixed, (W["w_out"], "row", 0), epi=_epi_residual, extra=x)
    h2 = _rms_fwd(n + "ln2", x1, ln2)
    a, u = _mm_nn(n + "ff1", h2, (W["w_ff1"], "col", 0), epi=_epi_relu2, out_dtypes=(MM_DT, MM_DT))
    x2 = _mm_nn(n + "ff2", u, (W["w_ff2"], "row", 0), epi=_epi_residual, extra=x1)
    saved = dict(x=x, h=h, proj=proj, cqn=cqn, ckvn=ckvn, q_a=q_a, k_a=k_a, v_a=v_a, o_a=o_a, lse_a=lse_a,
                 q_b=q_b, k_b=k_b, v_b=v_b, o_b=o_b, lse_b=lse_b, q_c=q_c, k_c=k_c, v_c=v_c, o_c=o_c, lse_c=lse_c,
                 mixed=mixed, x1=x1, h2=h2, a=a, u=u)
    return x2, saved


def _layer_bwd(cfg, l, dx2, dx2_mm, sv, W, small, tabs, GW):
    ln1, gq, gkv, gqn, gkn, gout, ln2 = small
    sc_a, sc_h = 1.0 / math.sqrt(192), 1.0 / math.sqrt(128)
    n = f"l{l}_b_"
    S = cfg.S
    mats = {m[0]: m for m in cfg.mats}

    def dw(name, a, g, key):
        _, Rs, Cs, kind = mats[key]
        GW[key] = _mm_tn(n + name, a, g, kind, 0, Rs, Cs, (N_SHARD, 1, Rs, Cs), out_dtype=MM_DT).reshape(N_SHARD, Rs, Cs)

    da = _mm_nt(n + "ff2_dx", dx2_mm, (W["w_ff2"], "row", 0), epi=_epi_drelu2, out_dtype=MM_DT, extra=sv["a"])
    dw("ff2_dw", sv["u"], dx2_mm, "w_ff2")
    dh2 = _mm_nt(n + "ff1_dx", da, (W["w_ff1"], "col", 0))
    dw("ff1_dw", sv["h2"], da, "w_ff1")
    dx1, dx1_mm, dln2 = _rms_bwd(n + "ln2", sv["x1"], ln2, dh2, dx2)
    dmix = _mm_nt(n + "out_dx", dx1_mm, (W["w_out"], "row", 0))
    dw("out_dw", sv["mixed"], dx1_mm, "w_out")
    do_a, do_b, do_c, dgout = _outnorm_bwd(n + "outnorm", cfg, dmix, sv["o_a"], sv["o_b"], sv["o_c"], gout)
    dproj = jnp.zeros((S, cfg.IN), MM_DT)
    args_c = (sv["q_c"], sv["k_c"], sv["v_c"], do_c, sv["o_c"], sv["lse_c"], cfg.CH, 1, 128, 128, sc_h, tabs["bias_c"], cfg.W)
    dq_c = _flash_dq(n + "attn_c_dq", *args_c)
    dk_c, dv_c = _flash_dkv(n + "attn_c_dkv", *args_c)
    dproj, _ = _headprep_bwd(n + "cq", dq_c, sv["proj"], cfg.o_cq, cfg.CH, None, tabs["c"], dproj)
    dproj, _ = _headprep_bwd(n + "ck", dk_c, sv["proj"], cfg.o_ck, cfg.CH, None, tabs["c"], dproj)
    dproj, _ = _headprep_bwd(n + "cv", dv_c, sv["proj"], cfg.o_cv, cfg.CH, None, None, dproj)
    args_b = (sv["q_b"], sv["k_b"], sv["v_b"], do_b, sv["o_b"], sv["lse_b"], cfg.BH, cfg.G, 128, 128, sc_h)
    dq_b = _flash_dq(n + "attn_b_dq", *args_b)
    dk_b, dv_b = _flash_dkv(n + "attn_b_dkv", *args_b)
    dproj, dgqn = _headprep_bwd(n + "bq", dq_b, sv["proj"], cfg.o_bq, cfg.BH, gqn, tabs["b"], dproj)
    dproj, dgkn = _headprep_bwd(n + "bk", dk_b, sv["proj"], cfg.o_bk, cfg.BKV, gkn, tabs["b"], dproj)
    dproj, _ = _headprep_bwd(n + "bv", dv_b, sv["proj"], cfg.o_bv, cfg.BKV, None, None, dproj)
    args_a = (sv["q_a"], sv["k_a"], sv["v_a"], do_a, sv["o_a"], sv["lse_a"], cfg.AH, 1, 256, 128, sc_a)
    dq_a = _flash_dq(n + "attn_a_dq", *args_a)
    dk_a, dv_a = _flash_dkv(n + "attn_a_dkv", *args_a)
    dqa, dkva, dkpe = _mla_build_bwd(n + "mla_build", cfg, dq_a, dk_a, dv_a, tabs["aq"])
    dcq = _mm_nt(n + "uq_dx", dqa, (W["uq_p"], "col", 0))
    dwuq = _mm_tn(n + "uq_dw", sv["cqn"], dqa, "col", 0, cfg.QLP, cfg.AH * 256, (1, 1, cfg.QLP, cfg.AH * 256))
    dckv = _mm_nt(n + "ukv_dx", dkva, (W["ukv_p"], "col", 0))
    dwukv = _mm_tn(n + "ukv_dw", sv["ckvn"], dkva, "col", 0, cfg.KVW, cfg.AH * 256, (1, 1, cfg.KVW, cfg.AH * 256))
    dproj, dgq, dgkv = _mla_prep_bwd(n + "mla_prep", cfg, dcq, dckv, dkpe, sv["proj"], gq, gkv, tabs["akr"], dproj)
    dh = _mm_nt(n + "proj_dx", dproj, (W["w_in"], "col", 0))
    dw("proj_dw", sv["h"], dproj, "w_in")
    dx, dx_mm, dln1 = _rms_bwd(n + "ln1", sv["x"], ln1, dh, dx1)
    gains = dict(ln1_g=dln1[0], g_q_a=dgq[0, :cfg.QL], g_kv_a=dgkv[0, cfg.KOFF:cfg.KOFF + cfg.KVL], g_qn_b=dgqn[0],
                 g_kn_b=dgkn[0], g_out=dgout[0], ln2_g=dln2[0])
    GW["w_uq"] = _uq_grad_unpadded(cfg, dwuq[0, 0]).astype(MM_DT)
    GW["w_ukv"] = _ukv_grad_unpadded(cfg, dwukv[0, 0]).astype(MM_DT)
    return dx, dx_mm, gains


SMALL_NAMES = ("ln1_g", "g_q_a", "g_kv_a", "g_qn_b", "g_kn_b", "g_out", "ln2_g")
MAT_NAMES = ("w_in", "w_uq", "w_ukv", "w_out", "w_ff1", "w_ff2")


def _pack_small(cfg, per_layer, final, scalar=None):
    last = jnp.zeros((1,), F32) if scalar is None else scalar.reshape(1)
    flat = jnp.concatenate([per_layer[k].reshape(-1) for k in SMALL_NAMES] + [final.reshape(-1), last])
    total = flat.shape[0]
    rows = _rup(-(-total // LANES), 8)
    return jnp.pad(flat, (0, rows * LANES - total)).reshape(rows, LANES)


def _unpack_small(cfg, packed, shapes):
    flat = packed.reshape(-1)
    out, off = {}, 0
    for k in SMALL_NAMES + ("ln_f_g",):
        n = math.prod(shapes[k])
        out[k] = flat[off:off + n].reshape(shapes[k])
        off += n
    return out, flat[off]


def _step(cfg, w, m, v, x, tgt):
    DEPTH, hd = cfg.DEPTH, cfg.HD
    c = lax.axis_index("c")
    me_chip = 2 * lax.axis_index("x") + lax.axis_index("y")
    c_arr = jnp.reshape(c, (1,)).astype(jnp.int32)
    dev_arr = jnp.reshape(2 * me_chip + c, (1,)).astype(jnp.int32)
    mats = {mt[0]: mt for mt in cfg.mats}

    shards = []
    for name in MAT_NAMES:
        _, Ks, Ns, _ = mats[name]
        shards.append(_cast_rows("cast_" + name, w[name].reshape(DEPTH * Ks, Ns), MM_DT).reshape(DEPTH, Ks, Ns))
    n_first = 3

    def gather(l, after):
        mine = [s[l] for s in shards]
        if l == 0:
            first = _allgather_layer("allgather_l0_first", mine[:n_first])
            rest = list(lax.optimization_barrier((tuple(mine[n_first:]), tuple(first)))[0])
            got = list(first) + list(_allgather_layer_async("allgather_l0_rest", rest, collective_id=3 * DEPTH))
        else:
            mine = list(lax.optimization_barrier((tuple(mine), after))[0])
            got = _allgather_layer_async(f"allgather_l{l}", mine, collective_id=l)
        g = {name: a[:, None] for name, a in zip(MAT_NAMES, got)}
        return dict(w_in=g["w_in"], w_out=g["w_out"], w_ff1=g["w_ff1"], w_ff2=g["w_ff2"],
                    uq_p=_uq_padded(cfg, g["w_uq"]), ukv_p=_ukv_padded(cfg, g["w_ukv"]))

    tabs = _all_tables(cfg)
    tabs["bias_c"] = _band_bias(cfg)

    def small_of(l):
        return (w["ln1_g"][l][None], _pad_lanes(w["g_q_a"][l], 0, cfg.QLP), _pad_lanes(w["g_kv_a"][l], cfg.KOFF, cfg.KVW),
                w["g_qn_b"][l][None], w["g_kn_b"][l][None], w["g_out"][l][None], w["ln2_g"][l][None])

    saved, W_layers = [], []
    xc = x
    for l in range(DEPTH):
        W_layers.append(gather(l, saved[l - 1]["mixed"] if l else None))
        xc, sv = _layer_fwd(cfg, l, xc, W_layers[l], small_of(l), tabs)
        saved.append(sv)
    dx, dlnf, loss_rows = _final_loss("final_loss", xc, w["ln_f_g"][None], tgt)
    dx_mm = _cast_rows("cast_dx", dx, MM_DT)
    loss = loss_rows[0, 0]
    gain_rows, own, slots = [None] * DEPTH, [None] * DEPTH, [None] * DEPTH
    for l in reversed(range(DEPTH)):
        GW = {}
        dx, dx_mm, gain_rows[l] = _layer_bwd(cfg, l, dx, dx_mm, saved[l], W_layers[l], small_of(l), tabs, GW)
        own[l] = [GW[name] for name in MAT_NAMES]
        early = _grad_exchange_async(f"grad_exchange_l{l}_early", own[l][n_first:], collective_id=DEPTH + 2 * l)
        late = _grad_exchange_async(f"grad_exchange_l{l}_late", own[l][:n_first], collective_id=DEPTH + 2 * l + 1)
        slots[l] = late + early

    reduced = []
    for t, name in enumerate(MAT_NAMES):
        _, Ks, Ns, _ = mats[name]
        buf = (DEPTH, 2, Ks // 2, Ns)
        for l in reversed(range(DEPTH)):
            buf = _add_eight(f"rs_add_l{l}_" + name, own[l][t], slots[l][t], dev_arr, l, buf)
        reduced.append(buf)
    full = _share_reduced(reduced)
    grad, delta, new_m, new_v = {}, {}, {}, {}
    for name, g2 in zip(MAT_NAMES, full):
        _, Ks, Ns, _ = mats[name]
        halves, shp = (DEPTH, 2, Ks // 2, Ns), (DEPTH, Ks, Ns)
        res = _adamw_halves("adamw_" + name, w[name].reshape(halves), g2, m[name].reshape(halves), v[name].reshape(halves), c_arr)
        grad[name], delta[name], new_m[name], new_v[name] = (r.reshape(shp) for r in res)

    per_layer = {k: jnp.stack([gain_rows[l][k] for l in range(DEPTH)]) for k in SMALL_NAMES}
    shapes = {k: w[k].shape for k in SMALL_NAMES + ("ln_f_g",)}
    gsum = _allreduce_small(_pack_small(cfg, per_layer, dlnf[0], loss))
    pk = lambda d: _pack_small(cfg, {k: d[k] for k in SMALL_NAMES}, d["ln_f_g"])
    d_s, m_s, v_s = _adamw("adamw_small", pk(w), gsum, pk(m), pk(v))
    for res, packed in ((grad, gsum), (delta, d_s), (new_m, m_s), (new_v, v_s)):
        res.update(_unpack_small(cfg, packed, shapes)[0])
    loss_total = _unpack_small(cfg, gsum, shapes)[1]
    return loss_total, dx, grad, delta, new_m, new_v


WEIGHT_NAMES = ("ln1_g", "w_in", "g_q_a", "w_uq", "g_kv_a", "w_ukv", "g_qn_b", "g_kn_b", "g_out", "w_out", "ln2_g",
                "w_ff1", "w_ff2", "ln_f_g")


def _run(cfg, args):
    nw = len(WEIGHT_NAMES)
    x, tgt = args[0], args[1 + nw]
    w = dict(zip(WEIGHT_NAMES, args[1:1 + nw]))
    m = dict(zip(WEIGHT_NAMES, args[2 + nw:2 + 2 * nw]))
    v = dict(zip(WEIGHT_NAMES, args[2 + 2 * nw:2 + 3 * nw]))
    loss, dx, grad, delta, new_m, new_v = _step(cfg, w, m, v, x.reshape(cfg.S, cfg.D), tgt.reshape(cfg.S, cfg.D))
    return (loss, dx.reshape(x.shape), *[grad[k] for k in WEIGHT_NAMES], *[delta[k] for k in WEIGHT_NAMES],
            *[new_m[k] for k in WEIGHT_NAMES], *[new_v[k] for k in WEIGHT_NAMES])


def kernel(x, ln1_g, w_in, g_q_a, w_uq, g_kv_a, w_ukv, g_qn_b, g_kn_b, g_out, w_out, ln2_g, w_ff1, w_ff2, ln_f_g, loss_target, m_ln1_g, m_w_in, m_g_q_a, m_w_uq, m_g_kv_a, m_w_ukv, m_g_qn_b, m_g_kn_b, m_g_out, m_w_out, m_ln2_g, m_w_ff1, m_w_ff2, m_ln_f_g, v_ln1_g, v_w_in, v_g_q_a, v_w_uq, v_g_kv_a, v_w_ukv, v_g_qn_b, v_g_kn_b, v_g_out, v_w_out, v_ln2_g, v_w_ff1, v_w_ff2, v_ln_f_g):
    return _run(Cfg(), (x, ln1_g, w_in, g_q_a, w_uq, g_kv_a, w_ukv, g_qn_b, g_kn_b, g_out, w_out, ln2_g, w_ff1, w_ff2, ln_f_g, loss_target, m_ln1_g, m_w_in, m_g_q_a, m_w_uq, m_g_kv_a, m_w_ukv, m_g_qn_b, m_g_kn_b, m_g_out, m_w_out, m_ln2_g, m_w_ff1, m_w_ff2, m_ln_f_g, v_ln1_g, v_w_in, v_g_q_a, v_w_uq, v_g_kv_a, v_w_ukv, v_g_qn_b, v_g_kn_b, v_g_out, v_w_out, v_ln2_g, v_w_ff1, v_w_ff2, v_ln_f_g))
```

```python
import functools
import math

import jax
import jax.numpy as jnp
from jax import lax
from jax.experimental import pallas as pl
from jax.experimental.pallas import tpu as pltpu
from jax.experimental.pallas import tpu_sc as plsc

F32 = jnp.float32
MM_DT = jnp.bfloat16
LANES = 128
SUBLANES_F32 = 8
SUBLANES_BF16 = 16
VMEM_LIMIT = 48 * 1024 * 1024
EPS = 1e-6
NEG = -1e30
ROPE_THETA = 10000.0
MM_TK = 1024
ATT_TQ, ATT_TK_FWD, ATT_TK, ATT_TK_DKV = 512, 4096, 2048, 4096
ADAM_LR, ADAM_B1, ADAM_B2, ADAM_EPS, ADAM_WD, ADAM_STEP = 0.001, 0.9, 0.999, 1e-08, 0.01, 10
N_SHARD = 4
MESH = pl.DeviceIdType.MESH

NN = (((1,), (0,)), ((), ()))
NT = (((1,), (1,)), ((), ()))
TN = (((0,), (0,)), ((), ()))


def _pcall(body, **kw):
    return pl.pallas_call(body, **kw)


def _rup(n, m):
    return -(-n // m) * m


def _pick(n, pref, mult):
    best = None
    for t in range(mult, min(n, pref) + 1, mult):
        if n % t == 0:
            best = t
    return best if best is not None else n


class Cfg:
    def __init__(self, S=4096, D=2048, DEPTH=4, AH=4, QL=448, KVL=512, BH=6, BKV=2, CH=6,
                 BRANCHES=((128, 1), (512, 4), (2048, 16)), DFF=8192, GRID_W=64, TB=512, TBK=1024):
        self.S, self.D, self.DEPTH, self.AH, self.QL, self.KVL = S, D, DEPTH, AH, QL, KVL
        self.BH, self.BKV, self.CH, self.DFF, self.GRID_W, self.TB = BH, BKV, CH, DFF, GRID_W, TB
        self.G = BH // BKV
        self.AW, self.BW, self.CW = AH * 128, BH * 128, CH * 128
        self.MIX = self.AW + self.BW + self.CW
        self.QLP = _rup(QL, LANES)
        self.KV0 = (QL // LANES) * LANES
        self.PW = QL + KVL + 64
        assert self.PW % LANES == 0
        self.KVW = self.PW - self.KV0
        self.KOFF = QL - self.KV0
        self.o_bq = self.PW
        self.o_bk = self.o_bq + self.BW
        self.o_bv = self.o_bk + BKV * 128
        self.o_cq = self.o_bv + BKV * 128
        self.o_ck = self.o_cq + self.CW
        self.o_cv = self.o_ck + self.CW
        self.IN = self.o_cv + self.CW
        self.UQ, self.UKV = AH * 192, AH * 256
        self.branches = tuple(((w // (2 * d)) * d, d) for w, d in BRANCHES)
        for _, d in self.branches:
            assert d & (d - 1) == 0
        self.TBK = TBK
        self.W = -(-max(r for r, _ in self.branches) // TBK)
        assert S % TBK == 0 and TBK % TB == 0 and DEPTH % 2 == 0
        self.HD = DEPTH // 2
        self.mats = (("w_in", D, self.IN // 4, "col"), ("w_uq", QL, self.UQ // 4, "col"),
                     ("w_ukv", KVL, self.UKV // 4, "col"), ("w_out", self.MIX // 4, D, "row"),
                     ("w_ff1", D, DFF // 4, "col"), ("w_ff2", DFF // 4, D, "row"))


def _mm_call(name, mode, operands, in_specs, out_shape, out_specs, grid, acc_shape, epi, n_extra, aliases=None):
    nk = grid[2]

    def body(*refs):
        a_ref, b_ref = refs[0], refs[1]
        ex = refs[2:2 + n_extra]
        outs = refs[2 + n_extra:-1]
        acc = refs[-1]
        k = pl.program_id(2)

        @pl.when(k == 0)
        def _():
            acc[...] = jnp.zeros_like(acc)

        acc[...] += lax.dot_general(a_ref[...].astype(MM_DT), b_ref[...].astype(MM_DT), mode,
                                    preferred_element_type=F32)

        @pl.when(k == nk - 1)
        def _():
            epi(acc[...], ex, outs)

    return _pcall(body, name=name, grid=grid, in_specs=in_specs, out_specs=out_specs, out_shape=out_shape,
                  scratch_shapes=[pltpu.VMEM(acc_shape, F32)], input_output_aliases=aliases or {},
                  compiler_params=pltpu.CompilerParams(dimension_semantics=("parallel", "parallel", "arbitrary"),
                                                       vmem_limit_bytes=VMEM_LIMIT))(*operands)


def _wspec(kind, l, Rs, Cs, br, bc, rfn, cfn):
    assert Rs % br == 0 and Cs % bc == 0
    if kind == "col":
        npc = Cs // bc
        return pl.BlockSpec((None, None, br, bc), lambda i, j, k: (cfn(i, j, k) // npc, l, rfn(i, j, k), cfn(i, j, k) % npc))
    npr = Rs // br
    return pl.BlockSpec((None, None, br, bc), lambda i, j, k: (rfn(i, j, k) // npr, l, rfn(i, j, k) % npr, cfn(i, j, k)))


def _epi_plain(acc, ex, outs):
    outs[0][...] = acc.astype(outs[0].dtype)


def _epi_residual(acc, ex, outs):
    outs[0][...] = ex[0][...] + acc


def _epi_relu2(acc, ex, outs):
    outs[0][...] = acc.astype(outs[0].dtype)
    r = jnp.maximum(acc, 0.0)
    outs[1][...] = (r * r).astype(outs[1].dtype)


def _epi_drelu2(acc, ex, outs):
    a = ex[0][...].astype(F32)
    outs[0][...] = (acc * (2.0 * jnp.maximum(a, 0.0))).astype(outs[0].dtype)


def _wdims(wd):
    Wg, kind, l = wd
    ns, _, Rs, Cs = Wg.shape
    K = Rs * ns if kind == "row" else Rs
    N = Cs * ns if kind == "col" else Cs
    return Wg, kind, l, Rs, Cs, K, N


def _mm_nn(name, a, wd, epi=_epi_plain, out_dtypes=(F32,), extra=None):
    Wg, kind, l, Rs, Cs, K, N = _wdims(wd)
    M = a.shape[0]
    tm, tk, tn = _pick(M, 1024, 16), _pick(Rs, MM_TK, LANES), _pick(Cs, 1152, LANES)
    grid = (M // tm, N // tn, K // tk)
    in_specs = [pl.BlockSpec((tm, tk), lambda i, j, k: (i, k)),
                _wspec(kind, l, Rs, Cs, tk, tn, lambda i, j, k: k, lambda i, j, k: j)]
    ops = [a, Wg]
    if extra is not None:
        in_specs.append(pl.BlockSpec((tm, tn), lambda i, j, k: (i, j)))
        ops.append(extra)
    o_spec = pl.BlockSpec((tm, tn), lambda i, j, k: (i, j))
    outs = tuple(jax.ShapeDtypeStruct((M, N), dt) for dt in out_dtypes)
    res = _mm_call(name, NN, ops, in_specs, outs, tuple(o_spec for _ in outs), grid, (tm, tn), epi,
                   0 if extra is None else 1)
    return res[0] if len(res) == 1 else res


def _mm_nt(name, g, wd, epi=_epi_plain, out_dtype=F32, extra=None):
    Wg, kind, l, Rs, Cs, K, N = _wdims(wd)
    M = g.shape[0]
    tm, tn, tk = _pick(M, 1024, 16), _pick(Rs, 1024, LANES), _pick(Cs, 1152, LANES)
    grid = (M // tm, K // tn, N // tk)
    in_specs = [pl.BlockSpec((tm, tk), lambda i, j, k: (i, k)),
                _wspec(kind, l, Rs, Cs, tn, tk, lambda i, j, k: j, lambda i, j, k: k)]
    ops = [g, Wg]
    if extra is not None:
        in_specs.append(pl.BlockSpec((tm, tn), lambda i, j, k: (i, j)))
        ops.append(extra)
    res = _mm_call(name, NT, ops, in_specs, (jax.ShapeDtypeStruct((M, K), out_dtype),),
                   (pl.BlockSpec((tm, tn), lambda i, j, k: (i, j)),), grid, (tm, tn), epi, 0 if extra is None else 1)
    return res[0]


def _mm_tn(name, a, g, kind, l, Rs, Cs, buf, out_dtype=F32):
    M, K = a.shape
    N = g.shape[1]
    tm, tn, tk = _pick(Rs, 1024, LANES), _pick(Cs, 1152, LANES), _pick(M, MM_TK, LANES)
    grid = (K // tm, N // tn, M // tk)
    in_specs = [pl.BlockSpec((tk, tm), lambda i, j, k: (k, i)),
                pl.BlockSpec((tk, tn), lambda i, j, k: (k, j))]
    o_spec = _wspec(kind, l, Rs, Cs, tm, tn, lambda i, j, k: i, lambda i, j, k: j)
    if isinstance(buf, tuple):
        res = _mm_call(name, TN, [a, g], in_specs, (jax.ShapeDtypeStruct(buf, out_dtype),), (o_spec,), grid, (tm, tn), _epi_plain, 0)
    else:
        res = _mm_call(name, TN, [a, g, buf], in_specs + [pl.BlockSpec(memory_space=pl.ANY)],
                       (jax.ShapeDtypeStruct(buf.shape, buf.dtype),), (o_spec,), grid, (tm, tn), _epi_plain, 1, aliases={2: 0})
    return res[0]


def _row_params():
    return pltpu.CompilerParams(dimension_semantics=("arbitrary",), vmem_limit_bytes=VMEM_LIMIT)


def _rms_fwd(name, x, g):
    S, D = x.shape
    tr = _pick(S, 256, 16)

    def body(x_ref, g_ref, o_ref):
        xv = x_ref[...]
        r = lax.rsqrt(jnp.mean(xv * xv, axis=-1, keepdims=True) + EPS)
        o_ref[...] = (xv * r * g_ref[...]).astype(o_ref.dtype)

    return _pcall(body, name=name, grid=(S // tr,),
                  in_specs=[pl.BlockSpec((tr, D), lambda i: (i, 0)), pl.BlockSpec((1, D), lambda i: (0, 0))],
                  out_specs=pl.BlockSpec((tr, D), lambda i: (i, 0)), out_shape=jax.ShapeDtypeStruct((S, D), MM_DT),
                  compiler_params=_row_params())(x, g)


def _acc_rows(ref, part, first):
    @pl.when(first)
    def _():
        ref[...] = jnp.zeros_like(ref)

    ref[...] += jnp.broadcast_to(part, ref.shape)


def _rms_bwd(name, x, g, dy, res):
    S, D = x.shape
    tr = _pick(S, 256, 16)

    def body(x_ref, g_ref, dy_ref, res_ref, dx_ref, dxb_ref, dg_ref):
        xv = x_ref[...]
        r = lax.rsqrt(jnp.mean(xv * xv, axis=-1, keepdims=True) + EPS)
        xh = xv * r
        dyv = dy_ref[...]
        dn = dyv * g_ref[...]
        dx = res_ref[...] + r * (dn - xh * jnp.mean(dn * xh, axis=-1, keepdims=True))
        dx_ref[...] = dx
        dxb_ref[...] = dx.astype(dxb_ref.dtype)
        _acc_rows(dg_ref, jnp.sum(dyv * xh, axis=0, keepdims=True), pl.program_id(0) == 0)

    row = pl.BlockSpec((tr, D), lambda i: (i, 0))
    return _pcall(body, name=name, grid=(S // tr,),
                  in_specs=[row, pl.BlockSpec((1, D), lambda i: (0, 0)), row, row],
                  out_specs=(row, row, pl.BlockSpec((8, D), lambda i: (0, 0))),
                  out_shape=(jax.ShapeDtypeStruct((S, D), F32), jax.ShapeDtypeStruct((S, D), MM_DT),
                             jax.ShapeDtypeStruct((8, D), F32)),
                  compiler_params=_row_params())(x, g, dy, res)


def _final_loss(name, x, g, tgt):
    S, D = x.shape
    tr = _pick(S, 256, 16)

    def body(x_ref, g_ref, t_ref, dx_ref, dg_ref, loss_ref):
        xv = x_ref[...]
        r = lax.rsqrt(jnp.mean(xv * xv, axis=-1, keepdims=True) + EPS)
        xh = xv * r
        gv = g_ref[...]
        e = xh * gv - t_ref[...]
        part = 0.5 * jnp.sum(jnp.mean(e * e, axis=-1, keepdims=True), axis=0, keepdims=True)
        dy = e * (1.0 / D)
        dn = dy * gv
        dx_ref[...] = r * (dn - xh * jnp.mean(dn * xh, axis=-1, keepdims=True))
        first = pl.program_id(0) == 0
        _acc_rows(dg_ref, jnp.sum(dy * xh, axis=0, keepdims=True), first)
        _acc_rows(loss_ref, part, first)

    row = pl.BlockSpec((tr, D), lambda i: (i, 0))
    return _pcall(body, name=name, grid=(S // tr,),
                  in_specs=[row, pl.BlockSpec((1, D), lambda i: (0, 0)), row],
                  out_specs=(row, pl.BlockSpec((8, D), lambda i: (0, 0)), pl.BlockSpec((8, LANES), lambda i: (0, 0))),
                  out_shape=(jax.ShapeDtypeStruct((S, D), F32), jax.ShapeDtypeStruct((8, D), F32),
                             jax.ShapeDtypeStruct((8, LANES), F32)),
                  compiler_params=_row_params())(x, g, tgt)


def _rope_tables(cos, sin, off, w):
    S = cos.shape[0]
    h = w // 2
    z = lambda n: jnp.zeros((S, n), F32)
    C = jnp.concatenate([z(off), cos, cos, z(LANES - off - w)], axis=1)
    SP = jnp.concatenate([z(off + h), sin, z(LANES - off - w)], axis=1)
    SN = jnp.concatenate([z(off), -sin, z(LANES - off - h)], axis=1)
    return C, SP, SN


def _angles(pos, dim):
    inv = jnp.power(ROPE_THETA, -jnp.arange(0, dim, 2, dtype=F32) / dim)
    ang = pos.astype(F32)[:, None] * inv[None, :]
    return jnp.cos(ang), jnp.sin(ang)


def _all_tables(cfg):
    S = cfg.S
    pos = jnp.arange(S, dtype=F32)
    rows = S // cfg.GRID_W
    row = jnp.repeat(jnp.arange(rows, dtype=F32), cfg.GRID_W)
    col = jnp.tile(jnp.arange(cfg.GRID_W, dtype=F32), rows)
    ca, sa = _angles(pos, 64)
    cc, sc = _angles(pos, 128)
    cr, sr = _angles(row, 64)
    cl, sl = _angles(col, 64)
    t_b = tuple(a + b for a, b in zip(_rope_tables(cr, sr, 0, 64), _rope_tables(cl, sl, 64, 64)))
    return {"aq": (_rope_tables(ca, sa, 0, 64), 64), "akr": (_rope_tables(ca, sa, 64, 64), 64),
            "b": (t_b, 64), "c": (_rope_tables(cc, sc, 0, 128), 128)}


def _rope(x, C, SP, SN, w):
    h = w // 2
    if 2 * h == LANES:
        return x * C + pltpu.roll(x, h, 1) * (SP + SN)
    return x * C + pltpu.roll(x, h, 1) * SP + pltpu.roll(x, LANES - h, 1) * SN


def _rope_t(dy, C, SP, SN, w):
    h = w // 2
    if 2 * h == LANES:
        return dy * C + pltpu.roll(dy * (SP + SN), h, 1)
    return dy * C + pltpu.roll(dy * SP, LANES - h, 1) + pltpu.roll(dy * SN, h, 1)


def _grid2_params():
    return pltpu.CompilerParams(dimension_semantics=("arbitrary", "arbitrary"), vmem_limit_bytes=VMEM_LIMIT)


def _headprep_fwd(name, proj, col_off, nb, gain, tabs):
    S = proj.shape[0]
    tr = _pick(S, 1024, 16)
    cb = col_off // LANES
    norm, rope = gain is not None, tabs is not None
    w = tabs[1] if rope else 0

    def body(*refs):
        x_ref = refs[0]
        pos = 1
        xv = x_ref[...]
        if norm:
            r = lax.rsqrt(jnp.mean(xv * xv, axis=-1, keepdims=True) + EPS)
            xv = xv * r * refs[pos][...]
            pos += 1
        if rope:
            xv = _rope(xv, refs[pos][...], refs[pos + 1][...], refs[pos + 2][...], w)
            pos += 3
        refs[pos][...] = xv.astype(refs[pos].dtype)

    ops, in_specs = [proj], [pl.BlockSpec((tr, LANES), lambda i, j: (i, cb + j))]
    if norm:
        ops.append(gain)
        in_specs.append(pl.BlockSpec((1, LANES), lambda i, j: (0, 0)))
    if rope:
        ops += list(tabs[0])
        in_specs += [pl.BlockSpec((tr, LANES), lambda i, j: (i, 0))] * 3
    return _pcall(body, name=name, grid=(S // tr, nb), in_specs=in_specs,
                  out_specs=pl.BlockSpec((tr, LANES), lambda i, j: (i, j)),
                  out_shape=jax.ShapeDtypeStruct((S, nb * LANES), MM_DT), compiler_params=_grid2_params())(*ops)


def _headprep_bwd(name, dy, proj, col_off, nb, gain, tabs, dproj):
    S = proj.shape[0]
    tr = _pick(S, 1024, 16)
    cb = col_off // LANES
    norm, rope = gain is not None, tabs is not None
    w = tabs[1] if rope else 0

    def body(*refs):
        dz = refs[0][...]
        pos = 1
        if norm:
            x_ref, g_ref = refs[pos], refs[pos + 1]
            pos += 2
        if rope:
            dz = _rope_t(dz, refs[pos][...], refs[pos + 1][...], refs[pos + 2][...], w)
            pos += 3
        pos += 1
        o_ref = refs[pos]
        if norm:
            dg_ref = refs[pos + 1]
            xv = x_ref[...]
            r = lax.rsqrt(jnp.mean(xv * xv, axis=-1, keepdims=True) + EPS)
            n = xv * r
            first = (pl.program_id(0) == 0) & (pl.program_id(1) == 0)
            _acc_rows(dg_ref, jnp.sum(dz * n, axis=0, keepdims=True), first)
            dn = dz * g_ref[...]
            dz = r * (dn - n * jnp.mean(dn * n, axis=-1, keepdims=True))
        o_ref[...] = dz.astype(o_ref.dtype)

    ops, in_specs = [dy], [pl.BlockSpec((tr, LANES), lambda i, j: (i, j))]
    if norm:
        ops += [proj, gain]
        in_specs += [pl.BlockSpec((tr, LANES), lambda i, j: (i, cb + j)), pl.BlockSpec((1, LANES), lambda i, j: (0, 0))]
    if rope:
        ops += list(tabs[0])
        in_specs += [pl.BlockSpec((tr, LANES), lambda i, j: (i, 0))] * 3
    alias_idx = len(ops)
    ops.append(dproj)
    in_specs.append(pl.BlockSpec(memory_space=pl.ANY))
    out_specs = [pl.BlockSpec((tr, LANES), lambda i, j: (i, cb + j))]
    out_shape = [jax.ShapeDtypeStruct(dproj.shape, dproj.dtype)]
    if norm:
        out_specs.append(pl.BlockSpec((8, LANES), lambda i, j: (0, 0)))
        out_shape.append(jax.ShapeDtypeStruct((8, LANES), F32))
    res = _pcall(body, name=name, grid=(S // tr, nb), in_specs=in_specs, out_specs=tuple(out_specs),
                 out_shape=tuple(out_shape), input_output_aliases={alias_idx: 0}, compiler_params=_grid2_params())(*ops)
    return (res[0], res[1]) if norm else (res[0], None)


def _masked_rms(xv, lo, n):
    lane = lax.broadcasted_iota(jnp.int32, xv.shape, 1)
    xm = jnp.where((lane >= lo) & (lane < lo + n), xv, 0.0)
    r = lax.rsqrt(jnp.sum(xm * xm, axis=-1, keepdims=True) * (1.0 / n) + EPS)
    return xm * r, r


def _mla_prep_fwd(name, cfg, proj, gq, gkv, tabs):
    S = cfg.S
    tr = _pick(S, 256, 16)
    (C, SP, SN), w = tabs

    def body(p_ref, gq_ref, gkv_ref, c_ref, sp_ref, sn_ref, cq_ref, ckv_ref, kpe_ref):
        nq, _ = _masked_rms(p_ref[:, 0:cfg.QLP], 0, cfg.QL)
        cq_ref[...] = (nq * gq_ref[...]).astype(cq_ref.dtype)
        nk, _ = _masked_rms(p_ref[:, cfg.KV0:cfg.PW], cfg.KOFF, cfg.KVL)
        ckv_ref[...] = (nk * gkv_ref[...]).astype(ckv_ref.dtype)
        kr = _rope(p_ref[:, cfg.PW - LANES:cfg.PW], c_ref[...], sp_ref[...], sn_ref[...], w)
        kpe_ref[...] = pltpu.roll(kr, 64, 1).astype(kpe_ref.dtype)

    tab = pl.BlockSpec((tr, LANES), lambda i: (i, 0))
    return _pcall(body, name=name, grid=(S // tr,),
                  in_specs=[pl.BlockSpec((tr, cfg.PW), lambda i: (i, 0)), pl.BlockSpec((1, cfg.QLP), lambda i: (0, 0)),
                            pl.BlockSpec((1, cfg.KVW), lambda i: (0, 0)), tab, tab, tab],
                  out_specs=(pl.BlockSpec((tr, cfg.QLP), lambda i: (i, 0)), pl.BlockSpec((tr, cfg.KVW), lambda i: (i, 0)), tab),
                  out_shape=(jax.ShapeDtypeStruct((S, cfg.QLP), MM_DT), jax.ShapeDtypeStruct((S, cfg.KVW), MM_DT),
                             jax.ShapeDtypeStruct((S, LANES), MM_DT)),
                  compiler_params=_row_params())(proj, gq, gkv, C, SP, SN)


def _mla_prep_bwd(name, cfg, dcq, dckv, dkpe, proj, gq, gkv, tabs, dproj):
    S = cfg.S
    tr = _pick(S, 256, 16)
    (C, SP, SN), w = tabs

    def body(dcq_ref, dckv_ref, dkpe_ref, p_ref, gq_ref, gkv_ref, c_ref, sp_ref, sn_ref, buf_ref, o_ref, dgq_ref, dgkv_ref):
        first = pl.program_id(0) == 0

        def norm_bwd(xv, lo, n, dz, g_ref, dg_ref):
            nrm, r = _masked_rms(xv, lo, n)
            _acc_rows(dg_ref, jnp.sum(dz * nrm, axis=0, keepdims=True), first)
            dn = dz * g_ref[...]
            return r * (dn - nrm * (jnp.sum(dn * nrm, axis=-1, keepdims=True) * (1.0 / n)))

        dxq = norm_bwd(p_ref[:, 0:cfg.QLP], 0, cfg.QL, dcq_ref[...], gq_ref, dgq_ref)
        dxk = norm_bwd(p_ref[:, cfg.KV0:cfg.PW], cfg.KOFF, cfg.KVL, dckv_ref[...], gkv_ref, dgkv_ref)
        dxr = _rope_t(pltpu.roll(dkpe_ref[...], 64, 1), c_ref[...], sp_ref[...], sn_ref[...], w)
        for cidx in range(cfg.PW // LANES):
            lo = cidx * LANES
            parts = []
            if lo < cfg.QLP:
                parts.append(dxq[:, lo:lo + LANES])
            if lo >= cfg.KV0:
                parts.append(dxk[:, lo - cfg.KV0:lo - cfg.KV0 + LANES])
            if lo == cfg.PW - LANES:
                parts.append(dxr)
            o_ref[:, lo:lo + LANES] = functools.reduce(lambda a, b: a + b, parts).astype(o_ref.dtype)

    tab = pl.BlockSpec((tr, LANES), lambda i: (i, 0))
    res = _pcall(body, name=name, grid=(S // tr,),
                 in_specs=[pl.BlockSpec((tr, cfg.QLP), lambda i: (i, 0)), pl.BlockSpec((tr, cfg.KVW), lambda i: (i, 0)), tab,
                           pl.BlockSpec((tr, cfg.PW), lambda i: (i, 0)), pl.BlockSpec((1, cfg.QLP), lambda i: (0, 0)),
                           pl.BlockSpec((1, cfg.KVW), lambda i: (0, 0)), tab, tab, tab, pl.BlockSpec(memory_space=pl.ANY)],
                 out_specs=(pl.BlockSpec((tr, cfg.PW), lambda i: (i, 0)), pl.BlockSpec((8, cfg.QLP), lambda i: (0, 0)),
                            pl.BlockSpec((8, cfg.KVW), lambda i: (0, 0))),
                 out_shape=(jax.ShapeDtypeStruct(dproj.shape, dproj.dtype), jax.ShapeDtypeStruct((8, cfg.QLP), F32),
                            jax.ShapeDtypeStruct((8, cfg.KVW), F32)),
                 input_output_aliases={9: 0}, compiler_params=_row_params())(dcq, dckv, dkpe, proj, gq, gkv, C, SP, SN, dproj)
    return res


def _mla_build_fwd(name, cfg, qa, kva, kpe, tabs):
    S, AH = cfg.S, cfg.AH
    tr = _pick(S, 256, 16)
    (C, SP, SN), w = tabs

    def body(qa_ref, kva_ref, kpe_ref, c_ref, sp_ref, sn_ref, q_ref, k_ref, v_ref):
        for h in range(AH):
            a, b = 256 * h, 256 * h + LANES
            q_ref[:, a:b] = qa_ref[:, a:b].astype(q_ref.dtype)
            q_ref[:, b:b + LANES] = _rope(qa_ref[:, b:b + LANES], c_ref[...], sp_ref[...], sn_ref[...], w).astype(q_ref.dtype)
            k_ref[:, a:b] = kva_ref[:, a:b].astype(k_ref.dtype)
            k_ref[:, b:b + LANES] = kpe_ref[...]
            v_ref[:, LANES * h:LANES * (h + 1)] = kva_ref[:, b:b + LANES].astype(v_ref.dtype)

    tab = pl.BlockSpec((tr, LANES), lambda i: (i, 0))
    wide = pl.BlockSpec((tr, AH * 256), lambda i: (i, 0))
    return _pcall(body, name=name, grid=(S // tr,), in_specs=[wide, wide, tab, tab, tab, tab],
                  out_specs=(wide, wide, pl.BlockSpec((tr, AH * LANES), lambda i: (i, 0))),
                  out_shape=(jax.ShapeDtypeStruct((S, AH * 256), MM_DT), jax.ShapeDtypeStruct((S, AH * 256), MM_DT),
                             jax.ShapeDtypeStruct((S, AH * LANES), MM_DT)),
                  compiler_params=_row_params())(qa, kva, kpe, C, SP, SN)


def _mla_build_bwd(name, cfg, dq, dk, dv, tabs):
    S, AH = cfg.S, cfg.AH
    tr = _pick(S, 256, 16)
    (C, SP, SN), w = tabs

    def body(dq_ref, dk_ref, dv_ref, c_ref, sp_ref, sn_ref, dqa_ref, dkva_ref, dkpe_ref):
        dkpe = None
        for h in range(AH):
            a, b = 256 * h, 256 * h + LANES
            dqa_ref[:, a:b] = dq_ref[:, a:b].astype(dqa_ref.dtype)
            dqa_ref[:, b:b + LANES] = _rope_t(dq_ref[:, b:b + LANES], c_ref[...], sp_ref[...], sn_ref[...], w).astype(dqa_ref.dtype)
            dkva_ref[:, a:b] = dk_ref[:, a:b].astype(dkva_ref.dtype)
            dkva_ref[:, b:b + LANES] = dv_ref[:, LANES * h:LANES * (h + 1)].astype(dkva_ref.dtype)
            part = dk_ref[:, b:b + LANES]
            dkpe = part if dkpe is None else dkpe + part
        dkpe_ref[...] = dkpe

    tab = pl.BlockSpec((tr, LANES), lambda i: (i, 0))
    wide = pl.BlockSpec((tr, AH * 256), lambda i: (i, 0))
    return _pcall(body, name=name, grid=(S // tr,),
                  in_specs=[wide, wide, pl.BlockSpec((tr, AH * LANES), lambda i: (i, 0)), tab, tab, tab],
                  out_specs=(wide, wide, tab),
                  out_shape=(jax.ShapeDtypeStruct((S, AH * 256), MM_DT), jax.ShapeDtypeStruct((S, AH * 256), MM_DT),
                             jax.ShapeDtypeStruct((S, LANES), F32)),
                  compiler_params=_row_params())(dq, dk, dv, C, SP, SN)


def _outnorm_fwd(name, cfg, oa, ob, oc, g):
    S = cfg.S
    tr = _pick(S, 256, 16)
    widths = (cfg.AW, cfg.BW, cfg.CW)

    def body(a_ref, b_ref, c_ref, g_ref, o_ref):
        off = 0
        for ref, wd in zip((a_ref, b_ref, c_ref), widths):
            v = ref[...]
            r = lax.rsqrt(jnp.mean(v * v, axis=-1, keepdims=True) + EPS)
            o_ref[:, off:off + wd] = (v * r * g_ref[:, off:off + wd]).astype(o_ref.dtype)
            off += wd

    return _pcall(body, name=name, grid=(S // tr,),
                  in_specs=[pl.BlockSpec((tr, wd), lambda i: (i, 0)) for wd in widths] + [pl.BlockSpec((1, cfg.MIX), lambda i: (0, 0))],
                  out_specs=pl.BlockSpec((tr, cfg.MIX), lambda i: (i, 0)),
                  out_shape=jax.ShapeDtypeStruct((S, cfg.MIX), MM_DT), compiler_params=_row_params())(oa, ob, oc, g)


def _outnorm_bwd(name, cfg, dmix, oa, ob, oc, g):
    S = cfg.S
    tr = _pick(S, 256, 16)
    widths = (cfg.AW, cfg.BW, cfg.CW)

    def body(dm_ref, a_ref, b_ref, c_ref, g_ref, da_ref, db_ref, dc_ref, dg_ref):
        off = 0
        parts = []
        for ref, dref, wd in zip((a_ref, b_ref, c_ref), (da_ref, db_ref, dc_ref), widths):
            v = ref[...]
            r = lax.rsqrt(jnp.mean(v * v, axis=-1, keepdims=True) + EPS)
            n = v * r
            dm = dm_ref[:, off:off + wd]
            parts.append(jnp.sum(dm * n, axis=0, keepdims=True))
            dn = dm * g_ref[:, off:off + wd]
            dref[...] = r * (dn - n * jnp.mean(dn * n, axis=-1, keepdims=True))
            off += wd
        _acc_rows(dg_ref, jnp.concatenate(parts, axis=1), pl.program_id(0) == 0)

    segs = [pl.BlockSpec((tr, wd), lambda i: (i, 0)) for wd in widths]
    return _pcall(body, name=name, grid=(S // tr,),
                  in_specs=[pl.BlockSpec((tr, cfg.MIX), lambda i: (i, 0))] + segs + [pl.BlockSpec((1, cfg.MIX), lambda i: (0, 0))],
                  out_specs=tuple(segs) + (pl.BlockSpec((8, cfg.MIX), lambda i: (0, 0)),),
                  out_shape=tuple(jax.ShapeDtypeStruct((S, wd), F32) for wd in widths) + (jax.ShapeDtypeStruct((8, cfg.MIX), F32),),
                  compiler_params=_row_params())(dmix, oa, ob, oc, g)


def _band_bias(cfg):
    tq, tk, W = cfg.TB, cfg.TBK, cfg.W
    ns = 2 * W + 1
    shape = ((tk // tq) * ns, tq, tk)
    slab = lax.broadcasted_iota(jnp.int32, shape, 0)
    row = lax.broadcasted_iota(jnp.int32, shape, 1)
    col = lax.broadcasted_iota(jnp.int32, shape, 2)
    d = (slab // ns) * tq + (W - slab % ns) * tk + row - col
    ad = jnp.abs(d)
    m = jnp.zeros(d.shape, F32)
    for reach, dil in cfg.branches:
        ok = ad <= reach
        if dil > 1:
            ok = ok & ((d & (dil - 1)) == 0)
        m = m + ok.astype(F32)
    return jnp.where(m > 0, jnp.log(jnp.maximum(m, 1.0)), NEG)


def _attn_params():
    return pltpu.CompilerParams(dimension_semantics=("parallel", "parallel", "arbitrary"), vmem_limit_bytes=VMEM_LIMIT)


def _scores(q_ref, k_ref, scale, bias_ref):
    s = lax.dot_general(q_ref[...], k_ref[...], NT, preferred_element_type=F32) * scale
    return s if bias_ref is None else s + bias_ref[...]


def _flash_fwd(name, q, k, v, H, G, dk, dv, scale, bias=None, W=None):
    S = q.shape[0]
    band = bias is not None
    tq = bias.shape[1] if band else _pick(S, ATT_TQ, LANES)
    tk = bias.shape[2] if band else _pick(S, ATT_TK_FWD, LANES)
    n = S // tk
    nsteps = 2 * W + 1 if band else n
    R = tk // tq

    def kblock(qi, st):
        return jnp.clip(qi // R - W + st, 0, n - 1) if band else st

    def body(*refs):
        q_ref, k_ref, v_ref = refs[:3]
        bias_ref = refs[3] if band else None
        o_ref, lse_ref, m_sc, l_sc, acc_sc = refs[-5:]
        qi, st = pl.program_id(1), pl.program_id(2)

        @pl.when(st == 0)
        def _():
            m_sc[...] = jnp.full_like(m_sc, NEG)
            l_sc[...] = jnp.zeros_like(l_sc)
            acc_sc[...] = jnp.zeros_like(acc_sc)

        kj = qi // R - W + st if band else st

        def step():
            s = _scores(q_ref, k_ref, scale, bias_ref)
            m_prev = m_sc[...]
            m_new = jnp.maximum(m_prev, jnp.max(s, axis=-1, keepdims=True))
            alpha = jnp.exp(m_prev - m_new)
            p = jnp.exp(s - m_new)
            l_sc[...] = alpha * l_sc[...] + jnp.sum(p, axis=-1, keepdims=True)
            acc_sc[...] = alpha * acc_sc[...] + lax.dot_general(p.astype(MM_DT), v_ref[...], NN, preferred_element_type=F32)
            m_sc[...] = m_new

        if band:
            pl.when((kj >= 0) & (kj < n))(step)
        else:
            step()

        @pl.when(st == nsteps - 1)
        def _():
            l = l_sc[...]
            o_ref[...] = acc_sc[...] / l
            lse_ref[...] = jnp.broadcast_to(m_sc[...] + jnp.log(l), lse_ref.shape)

    in_specs = [pl.BlockSpec((tq, dk), lambda h, qi, st: (qi, h)),
                pl.BlockSpec((tk, dk), lambda h, qi, st: (kblock(qi, st), h // G)),
                pl.BlockSpec((tk, dv), lambda h, qi, st: (kblock(qi, st), h // G))]
    ops = [q, k, v]
    if band:
        in_specs.append(pl.BlockSpec((None, tq, tk), lambda h, qi, st: ((qi % R) * nsteps + st, 0, 0)))
        ops.append(bias)
    return _pcall(body, name=name, grid=(H, S // tq, nsteps), in_specs=in_specs,
                  out_specs=(pl.BlockSpec((tq, dv), lambda h, qi, st: (qi, h)),
                             pl.BlockSpec((None, tq, LANES), lambda h, qi, st: (h, qi, 0))),
                  out_shape=(jax.ShapeDtypeStruct((S, H * dv), F32), jax.ShapeDtypeStruct((H, S, LANES), F32)),
                  scratch_shapes=[pltpu.VMEM((tq, 1), F32), pltpu.VMEM((tq, 1), F32), pltpu.VMEM((tq, dv), F32)],
                  compiler_params=_attn_params())(*ops)


def _flash_dq(name, q, k, v, do, o, lse, H, G, dk, dv, scale, bias=None, W=None):
    S = q.shape[0]
    band = bias is not None
    tq = bias.shape[1] if band else _pick(S, ATT_TQ, LANES)
    tk = bias.shape[2] if band else _pick(S, ATT_TK, LANES)
    n = S // tk
    nsteps = 2 * W + 1 if band else n
    R = tk // tq

    def kblock(qi, st):
        return jnp.clip(qi // R - W + st, 0, n - 1) if band else st

    def body(*refs):
        q_ref, k_ref, v_ref, do_ref, o_ref, lse_ref = refs[:6]
        bias_ref = refs[6] if band else None
        dq_ref, delta_sc, acc_sc = refs[-3:]
        qi, st = pl.program_id(1), pl.program_id(2)

        @pl.when(st == 0)
        def _():
            delta_sc[...] = jnp.sum(do_ref[...] * o_ref[...], axis=-1, keepdims=True)
            acc_sc[...] = jnp.zeros_like(acc_sc)

        kj = qi // R - W + st if band else st

        def step():
            p = jnp.exp(_scores(q_ref, k_ref, scale, bias_ref) - lse_ref[:, 0:1])
            dp = lax.dot_general(do_ref[...].astype(MM_DT), v_ref[...], NT, preferred_element_type=F32)
            ds = p * (dp - delta_sc[...])
            acc_sc[...] += lax.dot_general(ds.astype(MM_DT), k_ref[...], NN, preferred_element_type=F32)

        if band:
            pl.when((kj >= 0) & (kj < n))(step)
        else:
            step()

        @pl.when(st == nsteps - 1)
        def _():
            dq_ref[...] = acc_sc[...] * scale

    qspec = lambda wd: pl.BlockSpec((tq, wd), lambda h, qi, st: (qi, h))
    in_specs = [qspec(dk),
                pl.BlockSpec((tk, dk), lambda h, qi, st: (kblock(qi, st), h // G)),
                pl.BlockSpec((tk, dv), lambda h, qi, st: (kblock(qi, st), h // G)),
                qspec(dv), qspec(dv),
                pl.BlockSpec((None, tq, LANES), lambda h, qi, st: (h, qi, 0))]
    ops = [q, k, v, do, o, lse]
    if band:
        in_specs.append(pl.BlockSpec((None, tq, tk), lambda h, qi, st: ((qi % R) * nsteps + st, 0, 0)))
        ops.append(bias)
    return _pcall(body, name=name, grid=(H, S // tq, nsteps), in_specs=in_specs,
                  out_specs=qspec(dk), out_shape=jax.ShapeDtypeStruct((S, H * dk), F32),
                  scratch_shapes=[pltpu.VMEM((tq, 1), F32), pltpu.VMEM((tq, dk), F32)],
                  compiler_params=_attn_params())(*ops)


def _flash_dkv(name, q, k, v, do, o, lse, H, G, dk, dv, scale, bias=None, W=None):
    S = q.shape[0]
    band = bias is not None
    tq = bias.shape[1] if band else _pick(S, ATT_TQ, LANES)
    tk = bias.shape[2] if band else _pick(S, ATT_TK_DKV, LANES)
    n = S // tq
    R = tk // tq
    nq = R * (2 * W + 1) if band else n
    nsteps = G * nq
    Hkv = H // G

    def qhead(hk, st):
        return hk * G + st // nq

    def qblock(kj, st):
        return jnp.clip(R * (kj - W) + st % nq, 0, n - 1) if band else st % nq

    def body(*refs):
        q_ref, k_ref, v_ref, do_ref, o_ref, lse_ref = refs[:6]
        bias_ref = refs[6] if band else None
        dk_ref, dv_ref, dk_sc, dv_sc = refs[-4:]
        kj, st = pl.program_id(1), pl.program_id(2)

        @pl.when(st == 0)
        def _():
            dk_sc[...] = jnp.zeros_like(dk_sc)
            dv_sc[...] = jnp.zeros_like(dv_sc)

        qi = R * (kj - W) + st % nq if band else st % nq

        def step():
            p = jnp.exp(_scores(q_ref, k_ref, scale, bias_ref) - lse_ref[:, 0:1])
            dof = do_ref[...]
            dob = dof.astype(MM_DT)
            dv_sc[...] += lax.dot_general(p.astype(MM_DT), dob, TN, preferred_element_type=F32)
            dp = lax.dot_general(dob, v_ref[...], NT, preferred_element_type=F32)
            delta = jnp.sum(dof * o_ref[...], axis=-1, keepdims=True)
            ds = p * (dp - delta)
            dk_sc[...] += lax.dot_general(ds.astype(MM_DT), q_ref[...], TN, preferred_element_type=F32)

        if band:
            pl.when((qi >= 0) & (qi < n))(step)
        else:
            step()

        @pl.when(st == nsteps - 1)
        def _():
            dk_ref[...] = dk_sc[...] * scale
            dv_ref[...] = dv_sc[...]

    qspec = lambda wd: pl.BlockSpec((tq, wd), lambda hk, kj, st: (qblock(kj, st), qhead(hk, st)))
    kspec = lambda wd: pl.BlockSpec((tk, wd), lambda hk, kj, st: (kj, hk))
    in_specs = [qspec(dk), kspec(dk), kspec(dv), qspec(dv), qspec(dv),
                pl.BlockSpec((None, tq, LANES), lambda hk, kj, st: (qhead(hk, st), qblock(kj, st), 0))]
    ops = [q, k, v, do, o, lse]
    if band:
        in_specs.append(pl.BlockSpec((None, tq, tk),
                                     lambda hk, kj, st: ((st % nq % R) * (2 * W + 1) + 2 * W - (st % nq) // R, 0, 0)))
        ops.append(bias)
    return _pcall(body, name=name, grid=(Hkv, S // tk, nsteps), in_specs=in_specs,
                  out_specs=(kspec(dk), kspec(dv)),
                  out_shape=(jax.ShapeDtypeStruct((S, Hkv * dk), F32), jax.ShapeDtypeStruct((S, Hkv * dv), F32)),
                  scratch_shapes=[pltpu.VMEM((tk, dk), F32), pltpu.VMEM((tk, dv), F32)],
                  compiler_params=_attn_params())(*ops)


def _rowtile(rows, cols):
    return _pick(rows, max(16, (512 * 1024) // cols // 16 * 16), 16)


def _cast_rows(name, w, dtype):
    R, C = w.shape
    tr = _rowtile(R, C)

    def body(w_ref, o_ref):
        o_ref[...] = w_ref[...].astype(o_ref.dtype)

    spec = pl.BlockSpec((tr, C), lambda i: (i, 0))
    return _pcall(body, name=name, grid=(R // tr,), in_specs=[spec], out_specs=spec,
                  out_shape=jax.ShapeDtypeStruct((R, C), dtype), compiler_params=_row_params())(w)


def _adamw_math(wv, gv, mv, vv):
    bc1 = 1.0 - ADAM_B1 ** ADAM_STEP
    bc2 = 1.0 - ADAM_B2 ** ADAM_STEP
    mn = ADAM_B1 * mv + (1.0 - ADAM_B1) * gv
    vn = ADAM_B2 * vv + (1.0 - ADAM_B2) * jnp.square(gv)
    m_hat = mn / bc1
    v_hat = vn / bc2
    return -ADAM_LR * (m_hat / (jnp.sqrt(v_hat) + ADAM_EPS) + ADAM_WD * wv), mn, vn


def _adamw_halves(name, w, g2, m, v, c_arr):
    depth, _, R, C = w.shape
    tr = _rowtile(R, C)

    def body(c_ref, w_ref, g_ref, m_ref, v_ref, go_ref, d_ref, nm_ref, nv_ref):
        gv = g_ref[...]
        go_ref[...] = gv
        d_ref[...], nm_ref[...], nv_ref[...] = _adamw_math(w_ref[...], gv, m_ref[...], v_ref[...])

    spec = pl.BlockSpec((None, None, tr, C), lambda l, h, r, c_ref: (l, h, r, 0))
    gspec = pl.BlockSpec((None, None, tr, C), lambda l, h, r, c_ref: (l, (h + c_ref[0]) % 2, r, 0))
    sds = jax.ShapeDtypeStruct(w.shape, F32)
    grid_spec = pltpu.PrefetchScalarGridSpec(num_scalar_prefetch=1, grid=(depth, 2, R // tr), in_specs=[spec, gspec, spec, spec],
                                             out_specs=(spec,) * 4)
    return _pcall(body, name=name, grid_spec=grid_spec, out_shape=(sds,) * 4,
                  compiler_params=pltpu.CompilerParams(dimension_semantics=("arbitrary",) * 3, vmem_limit_bytes=VMEM_LIMIT))(c_arr, w, g2, m, v)


def _adamw(name, w, g, m, v):
    R, C = w.shape
    tr = _rowtile(R, C)

    def body(w_ref, g_ref, m_ref, v_ref, d_ref, nm_ref, nv_ref):
        d_ref[...], nm_ref[...], nv_ref[...] = _adamw_math(w_ref[...], g_ref[...], m_ref[...], v_ref[...])

    spec = pl.BlockSpec((tr, C), lambda i: (i, 0))
    sds = jax.ShapeDtypeStruct((R, C), F32)
    return _pcall(body, name=name, grid=(R // tr,), in_specs=[spec] * 4, out_specs=(spec,) * 3,
                  out_shape=(sds, sds, sds), compiler_params=_row_params())(w, g, m, v)


HBM_SPEC = pl.BlockSpec(memory_space=pltpu.HBM)


def _place():
    x, y, c = lax.axis_index("x"), lax.axis_index("y"), lax.axis_index("c")
    chips = [(1 - x, y), (x, 1 - y), (1 - x, 1 - y)]
    return x, y, c, chips


def _allgather_body(ins, outs, send, recv, handshake):
    n = len(ins)
    x, y, c, chips = _place()
    me = 2 * x + y
    sib = (x, y, 1 - c)
    if handshake:
        barrier = pltpu.get_barrier_semaphore()
        for peer in [(chip[0], chip[1], c) for chip in chips] + [sib]:
            pl.semaphore_signal(barrier, inc=1, device_id=peer, device_id_type=MESH)
        pl.semaphore_wait(barrier, 4)

    def rcopy(src, dst, k, to):
        return pltpu.make_async_remote_copy(src_ref=src, dst_ref=dst, send_sem=send.at[k], recv_sem=recv.at[k],
                                            device_id=to, device_id_type=MESH)

    def rows(t, cc):
        hr = ins[t].shape[0] // 2
        return pl.ds(cc * hr, hr)

    sends = []
    for t in range(n):
        for j, chip in enumerate(chips):
            cp = rcopy(ins[t].at[rows(t, c)], outs[t].at[me, rows(t, c)], 7 * t + j, (chip[0], chip[1], c))
            cp.start()
            sends.append(cp)
    for t in range(n):
        cp = rcopy(ins[t], outs[t].at[me], 7 * t + 6, sib)
        cp.start()
        sends.append(cp)
    for t in range(n):
        for j, chip in enumerate(chips):
            slab = outs[t].at[2 * chip[0] + chip[1], rows(t, c)]
            rcopy(slab, slab, 7 * t + j, (chip[0], chip[1], c)).wait_recv()
            fw = rcopy(slab, slab, 7 * t + 3 + j, sib)
            fw.start()
            sends.append(fw)
    for t in range(n):
        rcopy(ins[t], outs[t].at[me], 7 * t + 6, sib).wait_recv()
        for j, chip in enumerate(chips):
            slab = outs[t].at[2 * chip[0] + chip[1], rows(t, 1 - c)]
            rcopy(slab, slab, 7 * t + 3 + j, sib).wait_recv()
    for cp in sends:
        cp.wait_send()


def _allgather_layer(name, shards):
    n = len(shards)

    def body(*refs):
        _allgather_body(refs[:n], refs[n:2 * n], refs[2 * n], refs[2 * n + 1], False)

    return _pcall(body, name=name, in_specs=[HBM_SPEC] * n, out_specs=tuple([HBM_SPEC] * n),
                  out_shape=tuple(jax.ShapeDtypeStruct((N_SHARD,) + s.shape, s.dtype) for s in shards),
                  scratch_shapes=[pltpu.SemaphoreType.DMA((7 * n,)), pltpu.SemaphoreType.DMA((7 * n,))])(*shards)


def _allgather_layer_async(name, shards, collective_id):
    n = len(shards)
    in_refs = [jax.new_ref(s, memory_space=pltpu.MemorySpace.HBM) for s in shards]
    out_refs = [jax.empty_ref(jax.ShapeDtypeStruct((N_SHARD,) + s.shape, s.dtype), memory_space=pltpu.MemorySpace.HBM)
                for s in shards]

    @pl.kernel(mesh=plsc.ScalarSubcoreMesh(axis_name="seq", num_cores=1), name=name,
               scratch_types=(pltpu.SemaphoreType.DMA((7 * n,)), pltpu.SemaphoreType.DMA((7 * n,))),
               compiler_params=pltpu.CompilerParams(collective_id=collective_id))
    def launch(send, recv):
        _allgather_body(in_refs, out_refs, send, recv, True)

    launch()
    return [r[...] for r in out_refs]


def _flip(x, y, c, r):
    return (1 - x if r & 4 else x, 1 - y if r & 2 else y, 1 - c if r & 1 else c)


def _grad_exchange_async(name, gls, collective_id):
    n = len(gls)
    in_refs = [jax.new_ref(g, memory_space=pltpu.MemorySpace.HBM) for g in gls]
    out_refs = [jax.empty_ref(jax.ShapeDtypeStruct((7, g.shape[1] // 2, g.shape[2]), g.dtype), memory_space=pltpu.MemorySpace.HBM)
                for g in gls]

    @pl.kernel(mesh=plsc.ScalarSubcoreMesh(axis_name="seq", num_cores=1), name=name,
               scratch_types=(pltpu.SemaphoreType.DMA((7 * n,)), pltpu.SemaphoreType.DMA((7 * n,))),
               compiler_params=pltpu.CompilerParams(collective_id=collective_id))
    def launch(send, recv):
        _grad_exchange_body(in_refs, out_refs, send, recv, True)

    launch()
    return [r[...] for r in out_refs]


def _grad_exchange_body(ins, outs, send, recv, handshake):
    x, y, c, _ = _place()
    peers = [_flip(x, y, c, r) for r in range(1, 8)]
    if handshake:
        barrier = pltpu.get_barrier_semaphore()
        for peer in peers:
            pl.semaphore_signal(barrier, inc=1, device_id=peer, device_id_type=MESH)
        pl.semaphore_wait(barrier, 7)
    cps = []
    for t in range(len(ins)):
        hr = ins[t].shape[1] // 2
        for j, (px, py, pc) in enumerate(peers):
            cp = pltpu.make_async_remote_copy(src_ref=ins[t].at[2 * px + py, pl.ds(pc * hr, hr)], dst_ref=outs[t].at[j],
                                              send_sem=send.at[7 * t + j], recv_sem=recv.at[7 * t + j],
                                              device_id=(px, py, pc), device_id_type=MESH)
            cp.start()
            cps.append(cp)
    for cp in cps:
        cp.wait()


def _add_eight(name, own, slots, dev_arr, l, buf):
    ns, Ks, Ns = own.shape
    hr = Ks // 2
    tr = _rowtile(hr, Ns)
    fresh = isinstance(buf, tuple)

    def body(pl_ref, own_ref, *refs):
        acc = own_ref[...].astype(F32)
        for s_ref in refs[:7]:
            acc = acc + s_ref[...].astype(F32)
        refs[-1][...] = acc

    in_specs = [pl.BlockSpec((None, tr, Ns), lambda r, pl_ref: (pl_ref[0], r, 0))]
    in_specs += [pl.BlockSpec((None, tr, Ns), functools.partial(lambda r, pl_ref, j: (j, r, 0), j=j)) for j in range(7)]
    ops = [dev_arr, own.reshape(ns * 2, hr, Ns)] + [slots] * 7
    if not fresh:
        in_specs.append(pl.BlockSpec(memory_space=pl.ANY))
        ops.append(buf)
    grid_spec = pltpu.PrefetchScalarGridSpec(num_scalar_prefetch=1, grid=(hr // tr,), in_specs=in_specs,
                                             out_specs=pl.BlockSpec((None, None, tr, Ns), lambda r, pl_ref: (l, 0, r, 0)))
    return _pcall(body, name=name, grid_spec=grid_spec, out_shape=jax.ShapeDtypeStruct(buf if fresh else buf.shape, F32),
                  input_output_aliases={} if fresh else {9: 0}, compiler_params=_row_params())(*ops)


def _share_reduced(gs):
    n = len(gs)

    def body(*refs):
        ins, outs = refs[:n], refs[n:2 * n]
        send, recv = refs[2 * n:]
        x, y, c, _ = _place()
        cps = []
        for t in range(n):
            cp = pltpu.make_async_remote_copy(src_ref=ins[t].at[:, 0], dst_ref=outs[t].at[:, 1], send_sem=send.at[t], recv_sem=recv.at[t],
                                              device_id=(x, y, 1 - c), device_id_type=MESH)
            cp.start()
            cps.append(cp)
        for cp in cps:
            cp.wait()

    return _pcall(body, name="rs_share_reduced", in_specs=[HBM_SPEC] * n, out_specs=tuple([HBM_SPEC] * n),
                  out_shape=tuple(jax.ShapeDtypeStruct(g.shape, g.dtype) for g in gs),
                  input_output_aliases={t: t for t in range(n)},
                  scratch_shapes=[pltpu.SemaphoreType.DMA((n,)), pltpu.SemaphoreType.DMA((n,))])(*gs)


def _allreduce_small(vec):
    R = vec.shape[0]

    def body(v_ref, o_ref, buf, send, recv):
        x, y, c, _ = _place()
        me = 4 * x + 2 * y + c
        buf[me] = v_ref[...]
        cps = []
        for r in range(1, 8):
            fx, fy, fc = (r >> 2) & 1, (r >> 1) & 1, r & 1
            to = (1 - x if fx else x, 1 - y if fy else y, 1 - c if fc else c)
            cp = pltpu.make_async_remote_copy(src_ref=v_ref, dst_ref=buf.at[me], send_sem=send.at[r - 1], recv_sem=recv.at[r - 1],
                                              device_id=to, device_id_type=MESH)
            cp.start()
            cps.append(cp)
        for r in range(1, 8):
            fx, fy, fc = (r >> 2) & 1, (r >> 1) & 1, r & 1
            frm = (1 - x if fx else x, 1 - y if fy else y, 1 - c if fc else c)
            src = 4 * frm[0] + 2 * frm[1] + frm[2]
            pltpu.make_async_remote_copy(src_ref=v_ref, dst_ref=buf.at[src], send_sem=send.at[r - 1], recv_sem=recv.at[r - 1],
                                         device_id=frm, device_id_type=MESH).wait_recv()
        for cp in cps:
            cp.wait_send()
        acc = buf[0]
        for i in range(1, 8):
            acc = acc + buf[i]
        o_ref[...] = acc

    vm = pl.BlockSpec(memory_space=pltpu.VMEM)
    return _pcall(body, name="allreduce_small", in_specs=[vm], out_specs=vm, out_shape=jax.ShapeDtypeStruct((R, LANES), F32),
                  scratch_shapes=[pltpu.VMEM((8, R, LANES), F32), pltpu.SemaphoreType.DMA((7,)), pltpu.SemaphoreType.DMA((7,))])(vec)


def _unshard_cols(wg):
    ns, depth, K, Ns = wg.shape
    return jnp.moveaxis(wg, 0, 2).reshape(depth, K, ns * Ns)


def _shard_cols(w):
    K, N = w.shape
    return jnp.moveaxis(w.reshape(K, N_SHARD, N // N_SHARD), 1, 0)


def _uq_padded(cfg, wuq_g):
    depth = wuq_g.shape[1]
    w = _unshard_cols(wuq_g).reshape(depth, cfg.QL, cfg.AH, 192)
    w = jnp.pad(w, ((0, 0), (0, cfg.QLP - cfg.QL), (0, 0), (0, 64)))
    return w.reshape(1, depth, cfg.QLP, cfg.AH * 256)


def _uq_grad_unpadded(cfg, dw):
    w = dw[:cfg.QL].reshape(cfg.QL, cfg.AH, 256)[:, :, :192].reshape(cfg.QL, cfg.UQ)
    return _shard_cols(w)


def _ukv_padded(cfg, wukv_g):
    w = _unshard_cols(wukv_g)
    w = jnp.pad(w, ((0, 0), (cfg.KOFF, cfg.KVW - cfg.KVL - cfg.KOFF), (0, 0)))
    return w[None]


def _ukv_grad_unpadded(cfg, dw):
    return _shard_cols(dw[cfg.KOFF:cfg.KOFF + cfg.KVL])


def _pad_lanes(v, lo, total):
    return jnp.pad(v, (lo, total - lo - v.shape[0]))[None]


def _layer_fwd(cfg, l, x, W, small, tabs):
    ln1, gq, gkv, gqn, gkn, gout, ln2 = small
    sc_a, sc_h = 1.0 / math.sqrt(192), 1.0 / math.sqrt(128)
    n = f"l{l}_"
    h = _rms_fwd(n + "ln1", x, ln1)
    proj = _mm_nn(n + "proj", h, (W["w_in"], "col", 0))
    cqn, ckvn, kpe = _mla_prep_fwd(n + "mla_prep", cfg, proj, gq, gkv, tabs["akr"])
    qa = _mm_nn(n + "uq", cqn, (W["uq_p"], "col", 0))
    kva = _mm_nn(n + "ukv", ckvn, (W["ukv_p"], "col", 0))
    q_a, k_a, v_a = _mla_build_fwd(n + "mla_build", cfg, qa, kva, kpe, tabs["aq"])
    o_a, lse_a = _flash_fwd(n + "attn_a", q_a, k_a, v_a, cfg.AH, 1, 256, 128, sc_a)
    q_b = _headprep_fwd(n + "bq", proj, cfg.o_bq, cfg.BH, gqn, tabs["b"])
    k_b = _headprep_fwd(n + "bk", proj, cfg.o_bk, cfg.BKV, gkn, tabs["b"])
    v_b = _headprep_fwd(n + "bv", proj, cfg.o_bv, cfg.BKV, None, None)
    o_b, lse_b = _flash_fwd(n + "attn_b", q_b, k_b, v_b, cfg.BH, cfg.G, 128, 128, sc_h)
    q_c = _headprep_fwd(n + "cq", proj, cfg.o_cq, cfg.CH, None, tabs["c"])
    k_c = _headprep_fwd(n + "ck", proj, cfg.o_ck, cfg.CH, None, tabs["c"])
    v_c = _headprep_fwd(n + "cv", proj, cfg.o_cv, cfg.CH, None, None)
    o_c, lse_c = _flash_fwd(n + "attn_c", q_c, k_c, v_c, cfg.CH, 1, 128, 128, sc_h, tabs["bias_c"], cfg.W)
    mixed = _outnorm_fwd(n + "outnorm", cfg, o_a, o_b, o_c, gout)
    x1 = _mm_nn(n + "out", mixed, (W["w_out"], "row", 0), epi=_epi_residual, extra=x)
    h2 = _rms_fwd(n + "ln2", x1, ln2)
    a, u = _mm_nn(n + "ff1", h2, (W["w_ff1"], "col", 0), epi=_epi_relu2, out_dtypes=(MM_DT, MM_DT))
    x2 = _mm_nn(n + "ff2", u, (W["w_ff2"], "row", 0), epi=_epi_residual, extra=x1)
    saved = dict(x=x, h=h, proj=proj, cqn=cqn, ckvn=ckvn, q_a=q_a, k_a=k_a, v_a=v_a, o_a=o_a, lse_a=lse_a,
                 q_b=q_b, k_b=k_b, v_b=v_b, o_b=o_b, lse_b=lse_b, q_c=q_c, k_c=k_c, v_c=v_c, o_c=o_c, lse_c=lse_c,
                 mixed=mixed, x1=x1, h2=h2, a=a, u=u)
    return x2, saved


def _layer_bwd(cfg, l, dx2, dx2_mm, sv, W, small, tabs, GW):
    ln1, gq, gkv, gqn, gkn, gout, ln2 = small
    sc_a, sc_h = 1.0 / math.sqrt(192), 1.0 / math.sqrt(128)
    n = f"l{l}_b_"
    S = cfg.S
    mats = {m[0]: m for m in cfg.mats}

    def dw(name, a, g, key):
        _, Rs, Cs, kind = mats[key]
        GW[key] = _mm_tn(n + name, a, g, kind, 0, Rs, Cs, (N_SHARD, 1, Rs, Cs), out_dtype=MM_DT).reshape(N_SHARD, Rs, Cs)

    da = _mm_nt(n + "ff2_dx", dx2_mm, (W["w_ff2"], "row", 0), epi=_epi_drelu2, out_dtype=MM_DT, extra=sv["a"])
    dw("ff2_dw", sv["u"], dx2_mm, "w_ff2")
    dh2 = _mm_nt(n + "ff1_dx", da, (W["w_ff1"], "col", 0))
    dw("ff1_dw", sv["h2"], da, "w_ff1")
    dx1, dx1_mm, dln2 = _rms_bwd(n + "ln2", sv["x1"], ln2, dh2, dx2)
    dmix = _mm_nt(n + "out_dx", dx1_mm, (W["w_out"], "row", 0))
    dw("out_dw", sv["mixed"], dx1_mm, "w_out")
    do_a, do_b, do_c, dgout = _outnorm_bwd(n + "outnorm", cfg, dmix, sv["o_a"], sv["o_b"], sv["o_c"], gout)
    dproj = jnp.zeros((S, cfg.IN), MM_DT)
    args_c = (sv["q_c"], sv["k_c"], sv["v_c"], do_c, sv["o_c"], sv["lse_c"], cfg.CH, 1, 128, 128, sc_h, tabs["bias_c"], cfg.W)
    dq_c = _flash_dq(n + "attn_c_dq", *args_c)
    dk_c, dv_c = _flash_dkv(n + "attn_c_dkv", *args_c)
    dproj, _ = _headprep_bwd(n + "cq", dq_c, sv["proj"], cfg.o_cq, cfg.CH, None, tabs["c"], dproj)
    dproj, _ = _headprep_bwd(n + "ck", dk_c, sv["proj"], cfg.o_ck, cfg.CH, None, tabs["c"], dproj)
    dproj, _ = _headprep_bwd(n + "cv", dv_c, sv["proj"], cfg.o_cv, cfg.CH, None, None, dproj)
    args_b = (sv["q_b"], sv["k_b"], sv["v_b"], do_b, sv["o_b"], sv["lse_b"], cfg.BH, cfg.G, 128, 128, sc_h)
    dq_b = _flash_dq(n + "attn_b_dq", *args_b)
    dk_b, dv_b = _flash_dkv(n + "attn_b_dkv", *args_b)
    dproj, dgqn = _headprep_bwd(n + "bq", dq_b, sv["proj"], cfg.o_bq, cfg.BH, gqn, tabs["b"], dproj)
    dproj, dgkn = _headprep_bwd(n + "bk", dk_b, sv["proj"], cfg.o_bk, cfg.BKV, gkn, tabs["b"], dproj)
    dproj, _ = _headprep_bwd(n + "bv", dv_b, sv["proj"], cfg.o_bv, cfg.BKV, None, None, dproj)
    args_a = (sv["q_a"], sv["k_a"], sv["v_a"], do_a, sv["o_a"], sv["lse_a"], cfg.AH, 1, 256, 128, sc_a)
    dq_a = _flash_dq(n + "attn_a_dq", *args_a)
    dk_a, dv_a = _flash_dkv(n + "attn_a_dkv", *args_a)
    dqa, dkva, dkpe = _mla_build_bwd(n + "mla_build", cfg, dq_a, dk_a, dv_a, tabs["aq"])
    dcq = _mm_nt(n + "uq_dx", dqa, (W["uq_p"], "col", 0))
    dwuq = _mm_tn(n + "uq_dw", sv["cqn"], dqa, "col", 0, cfg.QLP, cfg.AH * 256, (1, 1, cfg.QLP, cfg.AH * 256))
    dckv = _mm_nt(n + "ukv_dx", dkva, (W["ukv_p"], "col", 0))
    dwukv = _mm_tn(n + "ukv_dw", sv["ckvn"], dkva, "col", 0, cfg.KVW, cfg.AH * 256, (1, 1, cfg.KVW, cfg.AH * 256))
    dproj, dgq, dgkv = _mla_prep_bwd(n + "mla_prep", cfg, dcq, dckv, dkpe, sv["proj"], gq, gkv, tabs["akr"], dproj)
    dh = _mm_nt(n + "proj_dx", dproj, (W["w_in"], "col", 0))
    dw("proj_dw", sv["h"], dproj, "w_in")
    dx, dx_mm, dln1 = _rms_bwd(n + "ln1", sv["x"], ln1, dh, dx1)
    gains = dict(ln1_g=dln1[0], g_q_a=dgq[0, :cfg.QL], g_kv_a=dgkv[0, cfg.KOFF:cfg.KOFF + cfg.KVL], g_qn_b=dgqn[0],
                 g_kn_b=dgkn[0], g_out=dgout[0], ln2_g=dln2[0])
    GW["w_uq"] = _uq_grad_unpadded(cfg, dwuq[0, 0]).astype(MM_DT)
    GW["w_ukv"] = _ukv_grad_unpadded(cfg, dwukv[0, 0]).astype(MM_DT)
    return dx, dx_mm, gains


SMALL_NAMES = ("ln1_g", "g_q_a", "g_kv_a", "g_qn_b", "g_kn_b", "g_out", "ln2_g")
MAT_NAMES = ("w_in", "w_uq", "w_ukv", "w_out", "w_ff1", "w_ff2")


def _pack_small(cfg, per_layer, final, scalar=None):
    last = jnp.zeros((1,), F32) if scalar is None else scalar.reshape(1)
    flat = jnp.concatenate([per_layer[k].reshape(-1) for k in SMALL_NAMES] + [final.reshape(-1), last])
    total = flat.shape[0]
    rows = _rup(-(-total // LANES), 8)
    return jnp.pad(flat, (0, rows * LANES - total)).reshape(rows, LANES)


def _unpack_small(cfg, packed, shapes):
    flat = packed.reshape(-1)
    out, off = {}, 0
    for k in SMALL_NAMES + ("ln_f_g",):
        n = math.prod(shapes[k])
        out[k] = flat[off:off + n].reshape(shapes[k])
        off += n
    return out, flat[off]


def _step(cfg, w, m, v, x, tgt):
    DEPTH, hd = cfg.DEPTH, cfg.HD
    c = lax.axis_index("c")
    me_chip = 2 * lax.axis_index("x") + lax.axis_index("y")
    c_arr = jnp.reshape(c, (1,)).astype(jnp.int32)
    dev_arr = jnp.reshape(2 * me_chip + c, (1,)).astype(jnp.int32)
    mats = {mt[0]: mt for mt in cfg.mats}

    shards = []
    for name in MAT_NAMES:
        _, Ks, Ns, _ = mats[name]
        shards.append(_cast_rows("cast_" + name, w[name].reshape(DEPTH * Ks, Ns), MM_DT).reshape(DEPTH, Ks, Ns))
    n_first = 3

    def gather(l, after):
        mine = [s[l] for s in shards]
        if l == 0:
            first = _allgather_layer("allgather_l0_first", mine[:n_first])
            rest = list(lax.optimization_barrier((tuple(mine[n_first:]), tuple(first)))[0])
            got = list(first) + list(_allgather_layer_async("allgather_l0_rest", rest, collective_id=3 * DEPTH))
        else:
            mine = list(lax.optimization_barrier((tuple(mine), after))[0])
            got = _allgather_layer_async(f"allgather_l{l}", mine, collective_id=l)
        g = {name: a[:, None] for name, a in zip(MAT_NAMES, got)}
        return dict(w_in=g["w_in"], w_out=g["w_out"], w_ff1=g["w_ff1"], w_ff2=g["w_ff2"],
                    uq_p=_uq_padded(cfg, g["w_uq"]), ukv_p=_ukv_padded(cfg, g["w_ukv"]))

    tabs = _all_tables(cfg)
    tabs["bias_c"] = _band_bias(cfg)

    def small_of(l):
        return (w["ln1_g"][l][None], _pad_lanes(w["g_q_a"][l], 0, cfg.QLP), _pad_lanes(w["g_kv_a"][l], cfg.KOFF, cfg.KVW),
                w["g_qn_b"][l][None], w["g_kn_b"][l][None], w["g_out"][l][None], w["ln2_g"][l][None])

    saved, W_layers = [], []
    xc = x
    for l in range(DEPTH):
        W_layers.append(gather(l, saved[l - 1]["mixed"] if l else None))
        xc, sv = _layer_fwd(cfg, l, xc, W_layers[l], small_of(l), tabs)
        saved.append(sv)
    dx, dlnf, loss_rows = _final_loss("final_loss", xc, w["ln_f_g"][None], tgt)
    dx_mm = _cast_rows("cast_dx", dx, MM_DT)
    loss = loss_rows[0, 0]
    gain_rows, own, slots = [None] * DEPTH, [None] * DEPTH, [None] * DEPTH
    for l in reversed(range(DEPTH)):
        GW = {}
        dx, dx_mm, gain_rows[l] = _layer_bwd(cfg, l, dx, dx_mm, saved[l], W_layers[l], small_of(l), tabs, GW)
        own[l] = [GW[name] for name in MAT_NAMES]
        early = _grad_exchange_async(f"grad_exchange_l{l}_early", own[l][n_first:], collective_id=DEPTH + 2 * l)
        late = _grad_exchange_async(f"grad_exchange_l{l}_late", own[l][:n_first], collective_id=DEPTH + 2 * l + 1)
        slots[l] = late + early

    reduced = []
    for t, name in enumerate(MAT_NAMES):
        _, Ks, Ns, _ = mats[name]
        buf = (DEPTH, 2, Ks // 2, Ns)
        for l in reversed(range(DEPTH)):
            buf = _add_eight(f"rs_add_l{l}_" + name, own[l][t], slots[l][t], dev_arr, l, buf)
        reduced.append(buf)
    full = _share_reduced(reduced)
    grad, delta, new_m, new_v = {}, {}, {}, {}
    for name, g2 in zip(MAT_NAMES, full):
        _, Ks, Ns, _ = mats[name]
        halves, shp = (DEPTH, 2, Ks // 2, Ns), (DEPTH, Ks, Ns)
        res = _adamw_halves("adamw_" + name, w[name].reshape(halves), g2, m[name].reshape(halves), v[name].reshape(halves), c_arr)
        grad[name], delta[name], new_m[name], new_v[name] = (r.reshape(shp) for r in res)

    per_layer = {k: jnp.stack([gain_rows[l][k] for l in range(DEPTH)]) for k in SMALL_NAMES}
    shapes = {k: w[k].shape for k in SMALL_NAMES + ("ln_f_g",)}
    gsum = _allreduce_small(_pack_small(cfg, per_layer, dlnf[0], loss))
    pk = lambda d: _pack_small(cfg, {k: d[k] for k in SMALL_NAMES}, d["ln_f_g"])
    d_s, m_s, v_s = _adamw("adamw_small", pk(w), gsum, pk(m), pk(v))
    for res, packed in ((grad, gsum), (delta, d_s), (new_m, m_s), (new_v, v_s)):
        res.update(_unpack_small(cfg, packed, shapes)[0])
    loss_total = _unpack_small(cfg, gsum, shapes)[1]
    return loss_total, dx, grad, delta, new_m, new_v


WEIGHT_NAMES = ("ln1_g", "w_in", "g_q_a", "w_uq", "g_kv_a", "w_ukv", "g_qn_b", "g_kn_b", "g_out", "w_out", "ln2_g",
                "w_ff1", "w_ff2", "ln_f_g")


def _run(cfg, args):
    nw = len(WEIGHT_NAMES)
    x, tgt = args[0], args[1 + nw]
    w = dict(zip(WEIGHT_NAMES, args[1:1 + nw]))
    m = dict(zip(WEIGHT_NAMES, args[2 + nw:2 + 2 * nw]))
    v = dict(zip(WEIGHT_NAMES, args[2 + 2 * nw:2 + 3 * nw]))
    loss, dx, grad, delta, new_m, new_v = _step(cfg, w, m, v, x.reshape(cfg.S, cfg.D), tgt.reshape(cfg.S, cfg.D))
    return (loss, dx.reshape(x.shape), *[grad[k] for k in WEIGHT_NAMES], *[delta[k] for k in WEIGHT_NAMES],
            *[new_m[k] for k in WEIGHT_NAMES], *[new_v[k] for k in WEIGHT_NAMES])


def kernel(x, ln1_g, w_in, g_q_a, w_uq, g_kv_a, w_ukv, g_qn_b, g_kn_b, g_out, w_out, ln2_g, w_ff1, w_ff2, ln_f_g, loss_target, m_ln1_g, m_w_in, m_g_q_a, m_w_uq, m_g_kv_a, m_w_ukv, m_g_qn_b, m_g_kn_b, m_g_out, m_w_out, m_ln2_g, m_w_ff1, m_w_ff2, m_ln_f_g, v_ln1_g, v_w_in, v_g_q_a, v_w_uq, v_g_kv_a, v_w_ukv, v_g_qn_b, v_g_kn_b, v_g_out, v_w_out, v_ln2_g, v_w_ff1, v_w_ff2, v_ln_f_g):
    return _run(Cfg(), (x, ln1_g, w_in, g_q_a, w_uq, g_kv_a, w_ukv, g_qn_b, g_kn_b, g_out, w_out, ln2_g, w_ff1, w_ff2, ln_f_g, loss_target, m_ln1_g, m_w_in, m_g_q_a, m_w_uq, m_g_kv_a, m_w_ukv, m_g_qn_b, m_g_kn_b, m_g_out, m_w_out, m_ln2_g, m_w_ff1, m_w_ff2, m_ln_f_g, v_ln1_g, v_w_in, v_g_q_a, v_w_uq, v_g_kv_a, v_w_ukv, v_g_qn_b, v_g_kn_b, v_g_out, v_w_out, v_ln2_g, v_w_ff1, v_w_ff2, v_ln_f_g))
```
